```python
import math
import jax, jax.numpy as jnp
from jax import lax
import numpy as np

D_MODEL = 1024
BATCH = 2
SEQ = 8192
DEPTH = 1
DEC_BATCH = 128
DEC_SEQ = 4
PAST_LEN = 16384
PAGE_SIZE = 128

HEAD_DIM = 64
MIX_WIDTH = D_MODEL
N_SWA_HEADS = MIX_WIDTH // 2 // HEAD_DIM
N_SWA_KV = 2
SWA_GROUP = N_SWA_HEADS // N_SWA_KV
WINDOW = 128
N_GDN_HEADS = MIX_WIDTH // 4 // HEAD_DIM
GDN_DK = 64
GDN_DV = 64
GDN_CONV = 4
GDN_CHUNK = 64
N_MEM_HEADS = MIX_WIDTH // 4 // HEAD_DIM
N_MEM = 256
N_EXPERTS = 32
TOP_K = 4
D_FF = D_MODEL
SWIGLU_ALPHA = 1.702
SWIGLU_LIMIT = 7.0
MOE_BLOCK = 128
EPS = 1e-6
ATTN_SCALE = HEAD_DIM ** -0.5

SWA_Q_COLS = N_SWA_HEADS * HEAD_DIM
SWA_KV_COLS = N_SWA_KV * HEAD_DIM
GDN_QK_COLS = N_GDN_HEADS * GDN_DK
GDN_V_COLS = N_GDN_HEADS * GDN_DV
GDN_CONV_CH = 2 * GDN_QK_COLS + GDN_V_COLS
MEM_Q_COLS = N_MEM_HEADS * HEAD_DIM
IN_SPLITS = (SWA_Q_COLS, SWA_KV_COLS, SWA_KV_COLS, GDN_CONV_CH, N_GDN_HEADS, N_GDN_HEADS, GDN_V_COLS, MEM_Q_COLS)
IN_COLS = SWA_Q_COLS + 2 * SWA_KV_COLS + GDN_CONV_CH + 2 * N_GDN_HEADS + GDN_V_COLS + MEM_Q_COLS

kernel_name = 'hymba_swa_gdn_mem_moe_step'


def _rmsnorm(x, g):
    xf = x.astype(jnp.float32)
    y = xf * lax.rsqrt(jnp.mean(xf * xf, axis=-1, keepdims=True) + EPS)
    return (y * g.astype(jnp.float32)).astype(x.dtype)


def _l2norm(x):
    return x * lax.rsqrt(jnp.sum(x * x, axis=-1, keepdims=True) + EPS)


def _split_cols(z):
    idx = np.cumsum(IN_SPLITS)[:-1].tolist()
    return jnp.split(z, idx, axis=-1)


def _alibi_slopes():
    h = jnp.arange(1, N_SWA_HEADS + 1, dtype=jnp.float32)
    return jnp.exp2(-(8.0 / N_SWA_HEADS) * h).reshape(N_SWA_KV, SWA_GROUP, 1, 1)


def _alibi_sink_probs(scores, dist, valid, sinks):
    s = scores.astype(jnp.float32) * ATTN_SCALE - _alibi_slopes() * dist.astype(jnp.float32)
    s = jnp.where(valid, s, -jnp.inf)
    sink = sinks.astype(jnp.float32).reshape(N_SWA_KV, SWA_GROUP, 1, 1)
    m = jnp.maximum(jnp.max(s, axis=-1, keepdims=True), sink)
    p = jnp.exp(s - m)
    return p / (jnp.sum(p, axis=-1, keepdims=True) + jnp.exp(sink - m))


def _swa_prompt(q, k, v, q_g, k_g, sinks):
    B, S, _ = q.shape
    NB = S // WINDOW
    q = _rmsnorm(q.reshape(B, S, N_SWA_KV, SWA_GROUP, HEAD_DIM), q_g)
    k = _rmsnorm(k.reshape(B, S, N_SWA_KV, HEAD_DIM), k_g)
    v = v.reshape(B, S, N_SWA_KV, HEAD_DIM)
    qb = q.reshape(B, NB, WINDOW, N_SWA_KV, SWA_GROUP, HEAD_DIM)
    kb = k.reshape(B, NB, WINDOW, N_SWA_KV, HEAD_DIM)
    vb = v.reshape(B, NB, WINDOW, N_SWA_KV, HEAD_DIM)
    pad = jnp.zeros_like(kb[:, :1])
    k2 = jnp.concatenate([jnp.concatenate([pad, kb[:, :-1]], axis=1), kb], axis=2)
    v2 = jnp.concatenate([jnp.concatenate([pad, vb[:, :-1]], axis=1), vb], axis=2)
    i = jnp.arange(WINDOW)[:, None]
    j = jnp.arange(2 * WINDOW)[None, :]
    dist = i + WINDOW - j
    band = (dist >= 0) & (dist < WINDOW)
    has_prev = (jnp.arange(NB)[:, None, None] > 0) | (j[None] >= WINDOW)
    valid = (band[None] & has_prev)[:, None, None]
    scores = jnp.einsum('bnqkgd,bnskd->bnkgqs', qb, k2)
    p = _alibi_sink_probs(scores, dist, valid, sinks)
    o = jnp.einsum('bnkgqs,bnskd->bnqkgd', p, v2)
    return o.reshape(B, S, SWA_Q_COLS), k[:, -WINDOW:], v[:, -WINDOW:]


def _swa_sample(q, k, v, buf_k, buf_v, q_g, k_g, sinks):
    B, L, _ = q.shape
    Wb = buf_k.shape[1]
    q = _rmsnorm(q.reshape(B, L, N_SWA_KV, SWA_GROUP, HEAD_DIM), q_g)
    k = _rmsnorm(k.reshape(B, L, N_SWA_KV, HEAD_DIM), k_g)
    v = v.reshape(B, L, N_SWA_KV, HEAD_DIM)
    kc = jnp.concatenate([buf_k.astype(k.dtype), k], axis=1)
    vc = jnp.concatenate([buf_v.astype(v.dtype), v], axis=1)
    dist = Wb + jnp.arange(L)[:, None] - jnp.arange(Wb + L)[None, :]
    valid = (dist >= 0) & (dist < WINDOW)
    scores = jnp.einsum('bqkgd,bskd->bkgqs', q, kc)
    p = _alibi_sink_probs(scores, dist, valid, sinks)
    o = jnp.einsum('bkgqs,bskd->bqkgd', p, vc)
    return o.reshape(B, L, SWA_Q_COLS), kc[:, -Wb:], vc[:, -Wb:]


def _causal_conv(u, buf, w):
    L = u.shape[1]
    up = jnp.concatenate([buf.astype(u.dtype), u], axis=1)
    y = up[:, 0:L] * w[0]
    for i in range(1, GDN_CONV):
        y = y + up[:, i:i + L] * w[i]
    return jax.nn.silu(y), up[:, -(GDN_CONV - 1):]


def _gated_delta_chunked(q, k, v, g, beta, s0):
    B, L, H, DK = q.shape
    DV = v.shape[-1]
    C = math.gcd(GDN_CHUNK, L)
    N = L // C

    def blocks(t):
        return jnp.moveaxis(t.reshape((B, N, C, H) + t.shape[3:]), 3, 1)

    qc, kc, vc, gc, bc = blocks(q), blocks(k), blocks(v), blocks(g), blocks(beta)
    gcum = jnp.cumsum(gc, axis=-1)
    incl = jnp.tril(jnp.ones((C, C), bool))
    strict = jnp.tril(jnp.ones((C, C), bool), -1)
    decay = jnp.exp(jnp.where(incl, gcum[..., :, None] - gcum[..., None, :], -jnp.inf))
    a_mat = jnp.where(strict, bc[..., :, None] * jnp.einsum('bhnid,bhnjd->bhnij', kc, kc) * decay, 0.0)
    rhs = jnp.concatenate([vc * bc[..., None], kc * (bc * jnp.exp(gcum))[..., None]], axis=-1)
    sol = lax.linalg.triangular_solve(a_mat, rhs, left_side=True, lower=True, unit_diagonal=True)
    u, w = sol[..., :DV], sol[..., DV:]
    qk = jnp.einsum('bhnid,bhnjd->bhnij', qc, kc) * decay
    q_dec = qc * jnp.exp(gcum)[..., None]
    k_dec = kc * jnp.exp(gcum[..., -1:] - gcum)[..., None]
    g_last = jnp.exp(gcum[..., -1])
    xs = tuple(jnp.moveaxis(t, 2, 0) for t in (u, w, qk, q_dec, k_dec, g_last))

    def step(S, inp):
        u_n, w_n, qk_n, qd_n, kd_n, gl_n = inp
        v_new = u_n - jnp.einsum('bhck,bhkv->bhcv', w_n, S)
        o_n = jnp.einsum('bhck,bhkv->bhcv', qd_n, S) + jnp.einsum('bhij,bhjv->bhiv', qk_n, v_new)
        S = S * gl_n[..., None, None] + jnp.einsum('bhck,bhcv->bhkv', kd_n, v_new)
        return S, o_n

    s_fin, o = lax.scan(step, s0, xs)
    o = jnp.transpose(o, (1, 0, 3, 2, 4)).reshape(B, L, H, DV)
    return o, s_fin


def _gdn(qkv_raw, a_raw, b_raw, gate, conv_buf, s0, conv_w, a_log, dt_bias, norm_g):
    B, L, _ = qkv_raw.shape
    qkv, new_buf = _causal_conv(qkv_raw, conv_buf, conv_w)
    qkv = qkv.astype(jnp.float32)
    q, k, v = jnp.split(qkv, [GDN_QK_COLS, 2 * GDN_QK_COLS], axis=-1)
    q = _l2norm(q.reshape(B, L, N_GDN_HEADS, GDN_DK)) * (GDN_DK ** -0.5)
    k = _l2norm(k.reshape(B, L, N_GDN_HEADS, GDN_DK))
    v = v.reshape(B, L, N_GDN_HEADS, GDN_DV)
    beta = jax.nn.sigmoid(b_raw.astype(jnp.float32))
    g = -jnp.exp(a_log.astype(jnp.float32)) * jax.nn.softplus(a_raw.astype(jnp.float32) + dt_bias.astype(jnp.float32))
    o, s_new = _gated_delta_chunked(q, k, v, g, beta, s0.astype(jnp.float32))
    o = _rmsnorm(o, norm_g) * jax.nn.silu(gate.reshape(B, L, N_GDN_HEADS, GDN_DV).astype(jnp.float32))
    return o.reshape(B, L, GDN_V_COLS), s_new, new_buf


def _mem_kv(mem, ln_g, w_kv, k_g):
    B, M, _ = mem.shape
    kv = _rmsnorm(mem, ln_g) @ w_kv
    k, v = jnp.split(kv, 2, axis=-1)
    k = _rmsnorm(k.reshape(B, M, N_MEM_HEADS, HEAD_DIM), k_g)
    return k, v.reshape(B, M, N_MEM_HEADS, HEAD_DIM)


def _mem_attend(q, mem_k, mem_v, q_g):
    B, L, _ = q.shape
    q = _rmsnorm(q.reshape(B, L, N_MEM_HEADS, HEAD_DIM), q_g)
    s = jnp.einsum('bqhd,bmhd->bhqm', q, mem_k.astype(q.dtype)).astype(jnp.float32) * ATTN_SCALE
    p = jax.nn.softmax(s, axis=-1)
    o = jnp.einsum('bhqm,bmhd->bqhd', p, mem_v)
    return o.reshape(B, L, MEM_Q_COLS)


def _moe(x, router_w, router_b, w1, b1, w2, b2):
    B, L, D = x.shape
    T = B * L
    xt = x.reshape(T, D)
    logits = (xt @ router_w).astype(jnp.float32) + router_b.astype(jnp.float32)
    top_v, top_e = lax.top_k(logits, TOP_K)
    gates = jax.nn.softmax(top_v, axis=-1)
    flat_e = top_e.reshape(-1)
    flat_tok = jnp.repeat(jnp.arange(T, dtype=jnp.int32), TOP_K)
    flat_gate = gates.reshape(-1)
    order = jnp.argsort(flat_e)
    sorted_e = flat_e[order]
    counts = jnp.bincount(flat_e, length=N_EXPERTS)
    padded = (counts + MOE_BLOCK - 1) // MOE_BLOCK * MOE_BLOCK
    start = jnp.cumsum(counts) - counts
    pad_end = jnp.cumsum(padded)
    pad_start = pad_end - padded
    rank = jnp.arange(T * TOP_K, dtype=jnp.int32) - start[sorted_e]
    dest = pad_start[sorted_e] + rank
    n_blocks = -(-(T * TOP_K) // MOE_BLOCK) + N_EXPERTS
    n_rows = n_blocks * MOE_BLOCK
    row_tok = jnp.full((n_rows,), T, jnp.int32).at[dest].set(flat_tok[order])
    row_gate = jnp.zeros((n_rows,), jnp.float32).at[dest].set(flat_gate[order])
    block_e = jnp.minimum(jnp.searchsorted(pad_end, jnp.arange(n_blocks) * MOE_BLOCK, side='right'), N_EXPERTS - 1)
    xp = jnp.concatenate([xt, jnp.zeros((1, D), xt.dtype)], axis=0)
    xb = xp[row_tok].reshape(n_blocks, MOE_BLOCK, D)

    def expert_block(args):
        xblk, e = args
        h = xblk @ w1[e] + b1[e]
        glu, lin = jnp.split(h, 2, axis=-1)
        glu = jnp.minimum(glu, SWIGLU_LIMIT)
        lin = jnp.clip(lin, -SWIGLU_LIMIT, SWIGLU_LIMIT)
        act = glu * jax.nn.sigmoid(SWIGLU_ALPHA * glu) * (lin + 1.0)
        return act @ w2[e] + b2[e]

    yb = lax.map(expert_block, (xb, block_e)).reshape(n_rows, D)
    y = jax.ops.segment_sum(yb * row_gate[:, None], row_tok, num_segments=T + 1)
    return y[:T].reshape(B, L, D)


def _layer(x, lp, mem_k, mem_v, gdn_s, conv_buf, swa_k_buf=None, swa_v_buf=None):
    n = _rmsnorm(x, lp['ln1'])
    q_s, k_s, v_s, qkv_d, a_d, b_d, gate_d, q_m = _split_cols(n @ lp['w_in'])
    if swa_k_buf is None:
        o_s, nk, nv = _swa_prompt(q_s, k_s, v_s, lp['q_norm'], lp['k_norm'], lp['sinks'])
    else:
        o_s, nk, nv = _swa_sample(q_s, k_s, v_s, swa_k_buf, swa_v_buf, lp['q_norm'], lp['k_norm'], lp['sinks'])
    o_d, ns, nc = _gdn(qkv_d, a_d, b_d, gate_d, conv_buf, gdn_s, lp['conv_w'], lp['a_log'], lp['dt_bias'], lp['gdn_norm'])
    o_m = _mem_attend(q_m, mem_k, mem_v, lp['mem_q_norm'])
    o = jnp.concatenate([o_s, o_d, o_m], axis=-1).astype(x.dtype)
    h = x + o @ lp['w_o']
    y = h + _moe(_rmsnorm(h, lp['ln2']), lp['router_w'], lp['router_b'], lp['w1'], lp['b1'], lp['w2'], lp['b2'])
    return y.astype(x.dtype), nk, nv, ns, nc


def setup_inputs(seed: int = 0) -> dict:
    key = jax.random.key(seed)
    ks = iter(jax.random.split(key, 40))

    def nrm(shape, s=1.0):
        return s * jax.random.normal(next(ks), shape, jnp.float32)

    def gain(shape):
        return 1.0 + nrm(shape, 0.02)

    swa_buf = min(WINDOW, PAST_LEN)
    dt = jnp.exp(jax.random.uniform(next(ks), (DEPTH, N_GDN_HEADS), jnp.float32, math.log(1e-3), math.log(1e-1)))
    return {
        'x_prompt': nrm((BATCH, SEQ, D_MODEL)),
        'x_sample': nrm((DEC_BATCH, DEC_SEQ, D_MODEL)),
        'cache_swa_k': nrm((DEPTH, DEC_BATCH, swa_buf, N_SWA_KV, HEAD_DIM)),
        'cache_swa_v': nrm((DEPTH, DEC_BATCH, swa_buf, N_SWA_KV, HEAD_DIM)),
        'state_gdn': nrm((DEPTH, DEC_BATCH, N_GDN_HEADS, GDN_DK, GDN_DV), 0.3),
        'state_gdn_conv': nrm((DEPTH, DEC_BATCH, GDN_CONV - 1, GDN_CONV_CH)),
        'cache_mem_k': nrm((DEPTH, DEC_BATCH, N_MEM, N_MEM_HEADS, HEAD_DIM)),
        'cache_mem_v': nrm((DEPTH, DEC_BATCH, N_MEM, N_MEM_HEADS, HEAD_DIM)),
        'mem_prompt': nrm((BATCH, N_MEM, D_MODEL)),
        'ln1_g': gain((DEPTH, D_MODEL)),
        'w_in': nrm((DEPTH, D_MODEL, IN_COLS), D_MODEL ** -0.5),
        'swa_q_norm': gain((DEPTH, HEAD_DIM)),
        'swa_k_norm': gain((DEPTH, HEAD_DIM)),
        'swa_sinks': nrm((DEPTH, N_SWA_HEADS)),
        'gdn_conv_w': nrm((DEPTH, GDN_CONV, GDN_CONV_CH), GDN_CONV ** -0.5),
        'gdn_a_log': jnp.log(jax.random.uniform(next(ks), (DEPTH, N_GDN_HEADS), jnp.float32, 1.0, 16.0)),
        'gdn_dt_bias': dt + jnp.log(-jnp.expm1(-dt)),
        'gdn_norm_g': gain((DEPTH, GDN_DV)),
        'mem_ln_g': gain((DEPTH, D_MODEL)),
        'w_mem_kv': nrm((DEPTH, D_MODEL, 2 * MEM_Q_COLS), D_MODEL ** -0.5),
        'mem_q_norm': gain((DEPTH, HEAD_DIM)),
        'mem_k_norm': gain((DEPTH, HEAD_DIM)),
        'w_o': nrm((DEPTH, MIX_WIDTH, D_MODEL), MIX_WIDTH ** -0.5),
        'ln2_g': gain((DEPTH, D_MODEL)),
        'router_w': nrm((DEPTH, D_MODEL, N_EXPERTS), D_MODEL ** -0.5),
        'router_b': nrm((DEPTH, N_EXPERTS), 0.01),
        'moe_w1': nrm((DEPTH, N_EXPERTS, D_MODEL, 2 * D_FF), D_MODEL ** -0.5),
        'moe_b1': nrm((DEPTH, N_EXPERTS, 2 * D_FF), 0.01),
        'moe_w2': nrm((DEPTH, N_EXPERTS, D_FF, D_MODEL), D_FF ** -0.5),
        'moe_b2': nrm((DEPTH, N_EXPERTS, D_MODEL), 0.01),
    }


def reference(x_prompt, x_sample, cache_swa_k, cache_swa_v, state_gdn, state_gdn_conv, cache_mem_k, cache_mem_v,
              mem_prompt, ln1_g, w_in, swa_q_norm, swa_k_norm, swa_sinks, gdn_conv_w, gdn_a_log, gdn_dt_bias,
              gdn_norm_g, mem_ln_g, w_mem_kv, mem_q_norm, mem_k_norm, w_o, ln2_g, router_w, router_b,
              moe_w1, moe_b1, moe_w2, moe_b2):
    B = x_prompt.shape[0]
    y_p, y_s = x_prompt, x_sample
    pk_l, pv_l, ps_l, pc_l, mk_l, mv_l = [], [], [], [], [], []
    sk_l, sv_l, ss_l, sc_l = [], [], [], []
    for l in range(DEPTH):
        lp = {'ln1': ln1_g[l], 'w_in': w_in[l], 'q_norm': swa_q_norm[l], 'k_norm': swa_k_norm[l],
              'sinks': swa_sinks[l], 'conv_w': gdn_conv_w[l], 'a_log': gdn_a_log[l], 'dt_bias': gdn_dt_bias[l],
              'gdn_norm': gdn_norm_g[l], 'mem_q_norm': mem_q_norm[l], 'w_o': w_o[l], 'ln2': ln2_g[l],
              'router_w': router_w[l], 'router_b': router_b[l], 'w1': moe_w1[l], 'b1': moe_b1[l],
              'w2': moe_w2[l], 'b2': moe_b2[l]}
        mk, mv = _mem_kv(mem_prompt, mem_ln_g[l], w_mem_kv[l], mem_k_norm[l])
        s_zero = jnp.zeros((B, N_GDN_HEADS, GDN_DK, GDN_DV), jnp.float32)
        c_zero = jnp.zeros((B, GDN_CONV - 1, GDN_CONV_CH), x_prompt.dtype)
        y_p, pk, pv, ps, pc = _layer(y_p, lp, mk, mv, s_zero, c_zero)
        y_s, sk, sv, ss, sc = _layer(y_s, lp, cache_mem_k[l], cache_mem_v[l], state_gdn[l], state_gdn_conv[l],
                                     cache_swa_k[l], cache_swa_v[l])
        pk_l.append(pk); pv_l.append(pv); ps_l.append(ps); pc_l.append(pc); mk_l.append(mk); mv_l.append(mv)
        sk_l.append(sk); sv_l.append(sv); ss_l.append(ss); sc_l.append(sc)
    new_swa_k_prompt = jnp.stack(pk_l, 0)
    new_swa_v_prompt = jnp.stack(pv_l, 0)
    new_state_gdn_prompt = jnp.stack(ps_l, 0)
    new_state_gdn_conv_prompt = jnp.stack(pc_l, 0)
    new_mem_k_prompt = jnp.stack(mk_l, 0)
    new_mem_v_prompt = jnp.stack(mv_l, 0)
    new_swa_k_sample = jnp.stack(sk_l, 0)
    new_swa_v_sample = jnp.stack(sv_l, 0)
    new_state_gdn_sample = jnp.stack(ss_l, 0)
    new_state_gdn_conv_sample = jnp.stack(sc_l, 0)
    return (y_p, y_s, new_swa_k_prompt, new_swa_v_prompt, new_state_gdn_prompt, new_state_gdn_conv_prompt,
            new_mem_k_prompt, new_mem_v_prompt, new_swa_k_sample, new_swa_v_sample, new_state_gdn_sample,
            new_state_gdn_conv_sample)
```

```python
import functools

import jax
import jax.numpy as jnp
from jax import lax
from jax.experimental import pallas as pl
from jax.experimental.pallas import tpu as pltpu

F32 = jnp.float32
BF16 = jnp.bfloat16
I32 = jnp.int32

HEAD_DIM = 64
N_SWA_HEADS = 8
N_SWA_KV = 2
SWA_GROUP = N_SWA_HEADS // N_SWA_KV
WINDOW = 128
N_GDN_HEADS = 4
GDN_DK = 64
GDN_DV = 64
GDN_CONV = 4
GDN_CHUNK = 64
N_MEM_HEADS = 4
N_EXPERTS = 32
TOP_K = 4
SWIGLU_ALPHA = 1.702
SWIGLU_LIMIT = 7.0
EPS = 1e-6
ATTN_SCALE = HEAD_DIM ** -0.5

SWA_Q_COLS = N_SWA_HEADS * HEAD_DIM
SWA_KV_COLS = N_SWA_KV * HEAD_DIM
GDN_QK_COLS = N_GDN_HEADS * GDN_DK
GDN_V_COLS = N_GDN_HEADS * GDN_DV
GDN_CONV_CH = 2 * GDN_QK_COLS + GDN_V_COLS
MEM_Q_COLS = N_MEM_HEADS * HEAD_DIM

LANES = 128
SUBLANES = 8
VMEM_LIMIT = 56 * 1024 * 1024

Z_Q = 0
Z_K = Z_Q + SWA_Q_COLS
Z_V = Z_K + SWA_KV_COLS
Z_GDN = Z_V + SWA_KV_COLS
Z_GATE = Z_GDN + GDN_CONV_CH
Z_QM = Z_GATE + GDN_V_COLS
Z_AB = Z_QM + MEM_Q_COLS
Z_COLS = Z_AB + LANES

ROW_TILE = 512
MOE_TILE = 256
ROUTE_TILE = 256
DISPATCH_TOKENS = 256


def _cparams(*sem):
    return pltpu.CompilerParams(dimension_semantics=sem, vmem_limit_bytes=VMEM_LIMIT)


def _bdot(a, b):
    return jnp.dot(a.astype(BF16), b.astype(BF16), preferred_element_type=F32)


def _bdot_nt(a, b):
    return lax.dot_general(a.astype(BF16), b.astype(BF16), (((1,), (1,)), ((), ())),
                           preferred_element_type=F32)


def _bdot_tn(a, b):
    return lax.dot_general(a.astype(BF16), b.astype(BF16), (((0,), (0,)), ((), ())),
                           preferred_element_type=F32)


def _split2(x):
    hi = x.astype(BF16)
    lo = (x - hi.astype(F32)).astype(BF16)
    return hi, lo


def _split3(x):
    hi = x.astype(BF16)
    r = x - hi.astype(F32)
    mid = r.astype(BF16)
    lo = (r - mid.astype(F32)).astype(BF16)
    return hi, mid, lo


def _rms_rows(x, g):
    ms = jnp.mean(x * x, axis=-1, keepdims=True)
    return x * lax.rsqrt(ms + EPS) * g


def _inproj_kernel(x_ref, g_ref, w_ref, z_ref):
    n = _rms_rows(x_ref[...], g_ref[...])
    z_ref[...] = jnp.dot(n.astype(BF16), w_ref[...], preferred_element_type=F32)


def _inproj(x2d, ln_g, w_z):
    t, d = x2d.shape
    tm = min(ROW_TILE, t)
    return pl.pallas_call(
        _inproj_kernel,
        grid=(t // tm,),
        in_specs=[pl.BlockSpec((tm, d), lambda i: (i, 0)),
                  pl.BlockSpec((1, d), lambda i: (0, 0)),
                  pl.BlockSpec((d, Z_COLS), lambda i: (0, 0))],
        out_specs=pl.BlockSpec((tm, Z_COLS), lambda i: (i, 0)),
        out_shape=jax.ShapeDtypeStruct((t, Z_COLS), F32),
        compiler_params=_cparams("parallel"),
        name="inproj",
    )(x2d, ln_g.reshape(1, d), w_z)


def _outproj_kernel(x_ref, os_ref, od_ref, om_ref, wo_ref, g_ref, rwh_ref, rwl_ref, rb_ref,
                    *refs, n_own):
    h_ref, hn_ref, lg_ref = refs[-3:]
    i = pl.program_id(0)

    @pl.when(i < n_own)
    def _():
        n_s = os_ref.shape[1]
        n_d = od_ref.shape[1]
        h = x_ref[...]
        h = h + jnp.dot(os_ref[...].astype(BF16), wo_ref[0:n_s, :], preferred_element_type=F32)
        h = h + jnp.dot(od_ref[...].astype(BF16), wo_ref[n_s:n_s + n_d, :], preferred_element_type=F32)
        h = h + jnp.dot(om_ref[...].astype(BF16), wo_ref[n_s + n_d:, :], preferred_element_type=F32)
        h_ref[...] = h
        hn = _rms_rows(h, g_ref[...])
        hn_ref[...] = hn
        hi, lo = _split2(hn)
        lg = (jnp.dot(hi, rwh_ref[...], preferred_element_type=F32)
              + jnp.dot(lo, rwh_ref[...], preferred_element_type=F32)
              + jnp.dot(hi, rwl_ref[...], preferred_element_type=F32))
        lg_ref[...] = lg + rb_ref[...]

    @pl.when(i >= n_own)
    def _():
        hn_ref[...] = jnp.zeros_like(hn_ref)
        lg_ref[...] = jnp.zeros_like(lg_ref)


def _outproj(x2d, o_s, o_d, o_m, w_o, ln_g, rw_hi, rw_lo, rb, t_all, row0, prev=None):
    t, d = x2d.shape
    tm = min(ROW_TILE, t)
    blk0 = row0 // tm
    n_own = t // tm
    n_steps = n_own if prev is not None else t_all // tm
    row = lambda i: (jnp.minimum(i, n_own - 1), 0)
    row_off = lambda i: (i + blk0, 0)
    const = lambda i: (0, 0)
    in_specs = [pl.BlockSpec((tm, d), row),
                pl.BlockSpec((tm, o_s.shape[1]), row),
                pl.BlockSpec((tm, o_d.shape[1]), row),
                pl.BlockSpec((tm, o_m.shape[1]), row),
                pl.BlockSpec((d, d), const),
                pl.BlockSpec((1, d), const),
                pl.BlockSpec((d, LANES), const),
                pl.BlockSpec((d, LANES), const),
                pl.BlockSpec((1, LANES), const)]
    args = [x2d, o_s, o_d, o_m, w_o, ln_g.reshape(1, d), rw_hi, rw_lo, rb]
    aliases = {}
    if prev is not None:
        in_specs += [pl.BlockSpec(memory_space=pl.ANY), pl.BlockSpec(memory_space=pl.ANY)]
        aliases = {len(args): 1, len(args) + 1: 2}
        args += list(prev)
    return pl.pallas_call(
        functools.partial(_outproj_kernel, n_own=n_own),
        grid=(n_steps,),
        in_specs=in_specs,
        out_specs=[pl.BlockSpec((tm, d), row),
                   pl.BlockSpec((tm, d), row_off),
                   pl.BlockSpec((tm, LANES), row_off)],
        out_shape=[jax.ShapeDtypeStruct((t, d), F32),
                   jax.ShapeDtypeStruct((t_all, d), F32),
                   jax.ShapeDtypeStruct((t_all, LANES), F32)],
        input_output_aliases=aliases,
        compiler_params=_cparams("arbitrary"),
        name="outproj_router",
    )(*args)


def _route_kernel(lg_ref, e_ref, r_ref, g_ref, cnt_ref, base_ref):
    i = pl.program_id(0)
    tm = lg_ref.shape[0]

    @pl.when(i == 0)
    def _():
        base_ref[...] = jnp.zeros_like(base_ref)

    lane = lax.broadcasted_iota(I32, (tm, LANES), 1).astype(F32)
    l = jnp.where(lane < N_EXPERTS, lg_ref[...], -jnp.inf)
    vals, idxs = [], []
    for _k in range(TOP_K):
        m = jnp.max(l, axis=-1, keepdims=True)
        idx = jnp.min(jnp.where(l == m, lane, float(LANES)), axis=-1, keepdims=True)
        l = jnp.where(lane == idx, -jnp.inf, l)
        vals.append(m)
        idxs.append(idx)
    ex = [jnp.exp(v - vals[0]) for v in vals]
    den = ex[0] + ex[1] + ex[2] + ex[3]
    member = jnp.zeros((tm, LANES), F32)
    for idx in idxs:
        member = member + jnp.where(lane == idx, 1.0, 0.0)
    ri = lax.broadcasted_iota(I32, (tm, tm), 0)
    ci = lax.broadcasted_iota(I32, (tm, tm), 1)
    strict = jnp.where(ci < ri, 1.0, 0.0).astype(BF16)
    prefix = jnp.dot(strict, member.astype(BF16), preferred_element_type=F32) + base_ref[...]
    e_out = jnp.zeros((tm, LANES), F32)
    r_out = jnp.zeros((tm, LANES), F32)
    g_out = jnp.zeros((tm, LANES), F32)
    for k in range(TOP_K):
        rank = jnp.sum(jnp.where(lane == idxs[k], prefix, 0.0), axis=-1, keepdims=True)
        e_out = jnp.where(lane == float(k), idxs[k], e_out)
        r_out = jnp.where(lane == float(k), rank, r_out)
        g_out = jnp.where(lane == float(k), ex[k] / den, g_out)
    e_ref[...] = e_out[:, :TOP_K].astype(I32)
    r_ref[...] = r_out[:, :TOP_K].astype(I32)
    g_ref[...] = g_out[:, :TOP_K]
    base_ref[...] = base_ref[...] + jnp.sum(member, axis=0, keepdims=True)
    cnt_ref[...] = base_ref[...]


def _route(logits):
    t = logits.shape[0]
    tm = ROUTE_TILE
    return pl.pallas_call(
        _route_kernel,
        grid=(t // tm,),
        in_specs=[pl.BlockSpec((tm, LANES), lambda i: (i, 0))],
        out_specs=[pl.BlockSpec((tm, TOP_K), lambda i: (i, 0)),
                   pl.BlockSpec((tm, TOP_K), lambda i: (i, 0)),
                   pl.BlockSpec((tm, TOP_K), lambda i: (i, 0)),
                   pl.BlockSpec((1, LANES), lambda i: (0, 0))],
        out_shape=[jax.ShapeDtypeStruct((t, TOP_K), I32),
                   jax.ShapeDtypeStruct((t, TOP_K), I32),
                   jax.ShapeDtypeStruct((t, TOP_K), F32),
                   jax.ShapeDtypeStruct((1, LANES), F32)],
        scratch_shapes=[pltpu.VMEM((1, LANES), F32)],
        compiler_params=_cparams("arbitrary"),
        name="route",
    )(logits)


def _row_copy(src, s, dst, d, sem):
    return pltpu.make_async_copy(src.at[pl.ds(s, 1)], dst.at[pl.ds(d, 1)], sem)


def _dispatch_kernel(start_ref, cnt_ref, e_ref, r_ref, hn_hbm, xs_hbm, zero_ref, sem, zsem):
    i = pl.program_id(0)
    n_slots = e_ref.shape[0]
    tok0 = i * (n_slots // TOP_K)

    def issue(j, c):
        d = start_ref[e_ref[j]] + r_ref[j]
        _row_copy(hn_hbm, tok0 + j // TOP_K, xs_hbm, d, sem).start()
        return c

    lax.fori_loop(0, n_slots, issue, 0)

    @pl.when(i == 0)
    def _():
        zero_ref[...] = jnp.zeros_like(zero_ref)

        def fill(pos, rows):
            cp = pltpu.make_async_copy(zero_ref.at[pl.ds(0, rows)], xs_hbm.at[pl.ds(pos, rows)], zsem)
            cp.start()
            cp.wait()

        for e in range(N_EXPERTS):
            cnt = cnt_ref[e]
            pad = (MOE_TILE - cnt % MOE_TILE) % MOE_TILE
            pos = start_ref[e] + cnt
            head = (SUBLANES - cnt % SUBLANES) % SUBLANES
            for r in range(SUBLANES - 1):
                pl.when(r < head)(functools.partial(fill, pos + r, 1))
            pos = pos + head
            rest = pad - head
            bit = MOE_TILE // 2
            while bit >= SUBLANES:
                take = (rest & bit) != 0
                pl.when(take)(functools.partial(fill, pl.multiple_of(pos, SUBLANES), bit))
                pos = pos + jnp.where(take, bit, 0)
                bit //= 2

        used = start_ref[N_EXPERTS - 1] + cnt_ref[N_EXPERTS - 1]
        first_free = (used + MOE_TILE - 1) // MOE_TILE

        def fill_tile(ti, c):
            for half in range(2):
                fill(pl.multiple_of(ti * MOE_TILE + half * (MOE_TILE // 2), MOE_TILE // 2), MOE_TILE // 2)
            return c

        lax.fori_loop(first_free, xs_hbm.shape[0] // MOE_TILE, fill_tile, 0)

    def drain(j, c):
        _row_copy(hn_hbm, 0, xs_hbm, 0, sem).wait()
        return c

    lax.fori_loop(0, n_slots, drain, 0)


def _dispatch(hn, e_flat, r_flat, start, counts, n_rows):
    t, d = hn.shape
    n_slots = DISPATCH_TOKENS * TOP_K
    grid_spec = pltpu.PrefetchScalarGridSpec(
        num_scalar_prefetch=2,
        grid=(t // DISPATCH_TOKENS,),
        in_specs=[pl.BlockSpec((n_slots,), lambda i, *_: (i,), memory_space=pltpu.SMEM),
                  pl.BlockSpec((n_slots,), lambda i, *_: (i,), memory_space=pltpu.SMEM),
                  pl.BlockSpec(memory_space=pl.ANY)],
        out_specs=pl.BlockSpec(memory_space=pl.ANY),
        scratch_shapes=[pltpu.VMEM((MOE_TILE // 2, d), F32),
                        pltpu.SemaphoreType.DMA(()),
                        pltpu.SemaphoreType.DMA(())],
    )
    return pl.pallas_call(
        _dispatch_kernel,
        grid_spec=grid_spec,
        out_shape=jax.ShapeDtypeStruct((n_rows, d), F32),
        compiler_params=_cparams("arbitrary"),
        name="dispatch",
    )(start, counts, e_flat, r_flat, hn)


def _expert_kernel(te_ref, nu_ref, x_ref, w1_ref, b1_ref, w2_ref, b2_ref, y_ref):
    i = pl.program_id(0)

    @pl.when(i < nu_ref[0])
    def _():
        f = w2_ref.shape[1]
        h = jnp.dot(x_ref[...].astype(BF16), w1_ref[0], preferred_element_type=F32) + b1_ref[0]
        glu = jnp.minimum(h[:, :f], SWIGLU_LIMIT)
        lin = jnp.clip(h[:, f:], -SWIGLU_LIMIT, SWIGLU_LIMIT)
        act = glu * jax.nn.sigmoid(SWIGLU_ALPHA * glu) * (lin + 1.0)
        y_ref[...] = jnp.dot(act.astype(BF16), w2_ref[0], preferred_element_type=F32) + b2_ref[0]

    @pl.when(i >= nu_ref[0])
    def _():
        y_ref[...] = jnp.zeros_like(y_ref)


def _experts(xs, tile_expert, n_used, w1, b1, w2, b2):
    n_rows, d = xs.shape
    f2 = w1.shape[2]
    f = w2.shape[1]
    n_tiles = n_rows // MOE_TILE
    live = lambda i, te, nu: (jnp.minimum(i, nu[0] - 1), 0)
    every = lambda i, te, nu: (i, 0)
    wsel = lambda i, te, nu: (te[i], 0, 0)
    grid_spec = pltpu.PrefetchScalarGridSpec(
        num_scalar_prefetch=2,
        grid=(n_tiles,),
        in_specs=[pl.BlockSpec((MOE_TILE, d), live),
                  pl.BlockSpec((1, d, f2), wsel),
                  pl.BlockSpec((1, 1, f2), wsel),
                  pl.BlockSpec((1, f, d), wsel),
                  pl.BlockSpec((1, 1, d), wsel)],
        out_specs=pl.BlockSpec((MOE_TILE, d), every),
    )
    return pl.pallas_call(
        _expert_kernel,
        grid_spec=grid_spec,
        out_shape=jax.ShapeDtypeStruct((n_rows, d), F32),
        compiler_params=_cparams("arbitrary"),
        name="experts",
    )(tile_expert, n_used, xs, w1, b1, w2, b2)


def _combine_kernel(start_ref, e_ref, r_ref, h_ref, g_ref, yb_hbm, y_ref, buf_ref, sem):
    n_slots = e_ref.shape[0]

    def issue(j, c):
        d = start_ref[e_ref[j]] + r_ref[j]
        pltpu.make_async_copy(yb_hbm.at[pl.ds(d, 1)],
                              buf_ref.at[j % TOP_K, pl.ds(j // TOP_K, 1)], sem).start()
        return c

    lax.fori_loop(0, n_slots, issue, 0)

    def drain(j, c):
        pltpu.make_async_copy(yb_hbm.at[pl.ds(0, 1)], buf_ref.at[0, pl.ds(0, 1)], sem).wait()
        return c

    lax.fori_loop(0, n_slots, drain, 0)
    g = g_ref[...]
    y = h_ref[...]
    for k in range(TOP_K):
        y = y + g[:, k:k + 1] * buf_ref[k]
    y_ref[...] = y


def _combine(h, e_flat, r_flat, gates, start, yb):
    t, d = h.shape
    tb = min(DISPATCH_TOKENS, t)
    n_slots = tb * TOP_K
    grid_spec = pltpu.PrefetchScalarGridSpec(
        num_scalar_prefetch=1,
        grid=(t // tb,),
        in_specs=[pl.BlockSpec((n_slots,), lambda i, *_: (i,), memory_space=pltpu.SMEM),
                  pl.BlockSpec((n_slots,), lambda i, *_: (i,), memory_space=pltpu.SMEM),
                  pl.BlockSpec((tb, d), lambda i, *_: (i, 0)),
                  pl.BlockSpec((tb, TOP_K), lambda i, *_: (i, 0)),
                  pl.BlockSpec(memory_space=pl.ANY)],
        out_specs=pl.BlockSpec((tb, d), lambda i, *_: (i, 0)),
        scratch_shapes=[pltpu.VMEM((TOP_K, tb, d), F32),
                        pltpu.SemaphoreType.DMA(())],
    )
    return pl.pallas_call(
        _combine_kernel,
        grid_spec=grid_spec,
        out_shape=jax.ShapeDtypeStruct((t, d), F32),
        compiler_params=_cparams("arbitrary"),
        name="combine",
    )(start, e_flat, r_flat, h, gates, yb)


def _moe_rows(t_all):
    return (-(-(t_all * TOP_K) // MOE_TILE) + N_EXPERTS) * MOE_TILE


def _moe(hn_all, logits_all, h_parts, w1, b1, w2, b2):
    t_all = hn_all.shape[0]
    e_idx, rank, gates, counts_f = _route(logits_all)
    counts = counts_f[0, :N_EXPERTS].astype(I32)
    padded = (counts + MOE_TILE - 1) // MOE_TILE * MOE_TILE
    pad_end = jnp.cumsum(padded)
    start = (pad_end - padded).astype(I32)
    n_rows = _moe_rows(t_all)
    n_tiles = n_rows // MOE_TILE
    n_used = (pad_end[-1:] // MOE_TILE).astype(I32)
    tile_expert = jnp.minimum(
        jnp.searchsorted(pad_end, jnp.arange(n_tiles, dtype=I32) * MOE_TILE, side="right"),
        N_EXPERTS - 1).astype(I32)
    e_flat = e_idx.reshape(-1)
    r_flat = rank.reshape(-1)
    xs = _dispatch(hn_all, e_flat, r_flat, start, counts, n_rows)
    yb = _experts(xs, tile_expert, n_used, w1, b1, w2, b2)
    outs = []
    row = 0
    for h in h_parts:
        t = h.shape[0]
        outs.append(_combine(h, e_flat[row * TOP_K:(row + t) * TOP_K], r_flat[row * TOP_K:(row + t) * TOP_K],
                             gates[row:row + t], start, yb))
        row += t
    return outs


import math
import numpy as np


def _j_rmsnorm(x, g):
    return x * lax.rsqrt(jnp.mean(x * x, axis=-1, keepdims=True) + EPS) * g


def _j_alibi_sink_probs(scores, dist, valid, sinks):
    hh = jnp.arange(1, N_SWA_HEADS + 1, dtype=F32)
    slopes = jnp.exp2(-(8.0 / N_SWA_HEADS) * hh).reshape(N_SWA_KV, SWA_GROUP, 1, 1)
    s = scores * ATTN_SCALE - slopes * dist.astype(F32)
    s = jnp.where(valid, s, -jnp.inf)
    sink = sinks.reshape(N_SWA_KV, SWA_GROUP, 1, 1)
    m = jnp.maximum(jnp.max(s, axis=-1, keepdims=True), sink)
    p = jnp.exp(s - m)
    return p / (jnp.sum(p, axis=-1, keepdims=True) + jnp.exp(sink - m))


def _j_swa_prompt(q, k, v, q_g, k_g, sinks):
    B, S, _ = q.shape
    NB = S // WINDOW
    q = _j_rmsnorm(q.reshape(B, S, N_SWA_KV, SWA_GROUP, HEAD_DIM), q_g)
    k = _j_rmsnorm(k.reshape(B, S, N_SWA_KV, HEAD_DIM), k_g)
    v = v.reshape(B, S, N_SWA_KV, HEAD_DIM)
    qb = q.reshape(B, NB, WINDOW, N_SWA_KV, SWA_GROUP, HEAD_DIM)
    kb = k.reshape(B, NB, WINDOW, N_SWA_KV, HEAD_DIM)
    vb = v.reshape(B, NB, WINDOW, N_SWA_KV, HEAD_DIM)
    pad = jnp.zeros_like(kb[:, :1])
    k2 = jnp.concatenate([jnp.concatenate([pad, kb[:, :-1]], axis=1), kb], axis=2)
    v2 = jnp.concatenate([jnp.concatenate([pad, vb[:, :-1]], axis=1), vb], axis=2)
    i = jnp.arange(WINDOW)[:, None]
    j = jnp.arange(2 * WINDOW)[None, :]
    dist = i + WINDOW - j
    band = (dist >= 0) & (dist < WINDOW)
    has_prev = (jnp.arange(NB)[:, None, None] > 0) | (j[None] >= WINDOW)
    valid = (band[None] & has_prev)[:, None, None]
    scores = jnp.einsum('bnqkgd,bnskd->bnkgqs', qb, k2)
    p = _j_alibi_sink_probs(scores, dist, valid, sinks)
    o = jnp.einsum('bnkgqs,bnskd->bnqkgd', p, v2)
    return o.reshape(B, S, SWA_Q_COLS), k[:, -WINDOW:], v[:, -WINDOW:]


def _j_swa_sample(q, k, v, buf_k, buf_v, q_g, k_g, sinks):
    B, L, _ = q.shape
    Wb = buf_k.shape[1]
    q = _j_rmsnorm(q.reshape(B, L, N_SWA_KV, SWA_GROUP, HEAD_DIM), q_g)
    k = _j_rmsnorm(k.reshape(B, L, N_SWA_KV, HEAD_DIM), k_g)
    v = v.reshape(B, L, N_SWA_KV, HEAD_DIM)
    kc = jnp.concatenate([buf_k, k], axis=1)
    vc = jnp.concatenate([buf_v, v], axis=1)
    dist = Wb + jnp.arange(L)[:, None] - jnp.arange(Wb + L)[None, :]
    valid = (dist >= 0) & (dist < WINDOW)
    scores = jnp.einsum('bqkgd,bskd->bkgqs', q, kc)
    p = _j_alibi_sink_probs(scores, dist, valid, sinks)
    o = jnp.einsum('bkgqs,bskd->bqkgd', p, vc)
    return o.reshape(B, L, SWA_Q_COLS), kc[:, -Wb:], vc[:, -Wb:]


def _j_causal_conv(u, buf, w):
    L = u.shape[1]
    up = jnp.concatenate([buf, u], axis=1)
    y = up[:, 0:L] * w[0]
    for i in range(1, GDN_CONV):
        y = y + up[:, i:i + L] * w[i]
    return jax.nn.silu(y), up[:, -(GDN_CONV - 1):]


def _j_gated_delta_chunked(q, k, v, g, beta, s0):
    B, L, H, DK = q.shape
    DV = v.shape[-1]
    C = math.gcd(GDN_CHUNK, L)
    N = L // C

    def blocks(t):
        return jnp.moveaxis(t.reshape((B, N, C, H) + t.shape[3:]), 3, 1)

    qc, kc, vc, gc, bc = blocks(q), blocks(k), blocks(v), blocks(g), blocks(beta)
    gcum = jnp.cumsum(gc, axis=-1)
    incl = jnp.tril(jnp.ones((C, C), bool))
    strict = jnp.tril(jnp.ones((C, C), bool), -1)
    decay = jnp.exp(jnp.where(incl, gcum[..., :, None] - gcum[..., None, :], -jnp.inf))
    a_mat = jnp.where(strict, bc[..., :, None] * jnp.einsum('bhnid,bhnjd->bhnij', kc, kc) * decay, 0.0)
    rhs = jnp.concatenate([vc * bc[..., None], kc * (bc * jnp.exp(gcum))[..., None]], axis=-1)
    sol = lax.linalg.triangular_solve(a_mat, rhs, left_side=True, lower=True, unit_diagonal=True)
    u, w = sol[..., :DV], sol[..., DV:]
    qk = jnp.einsum('bhnid,bhnjd->bhnij', qc, kc) * decay
    q_dec = qc * jnp.exp(gcum)[..., None]
    k_dec = kc * jnp.exp(gcum[..., -1:] - gcum)[..., None]
    g_last = jnp.exp(gcum[..., -1])
    xs = tuple(jnp.moveaxis(t, 2, 0) for t in (u, w, qk, q_dec, k_dec, g_last))

    def step(S, inp):
        u_n, w_n, qk_n, qd_n, kd_n, gl_n = inp
        v_new = u_n - jnp.einsum('bhck,bhkv->bhcv', w_n, S)
        o_n = jnp.einsum('bhck,bhkv->bhcv', qd_n, S) + jnp.einsum('bhij,bhjv->bhiv', qk_n, v_new)
        S = S * gl_n[..., None, None] + jnp.einsum('bhck,bhcv->bhkv', kd_n, v_new)
        return S, o_n

    s_fin, o = lax.scan(step, s0, xs)
    o = jnp.transpose(o, (1, 0, 3, 2, 4)).reshape(B, L, H, DV)
    return o, s_fin


def _j_gdn(qkv_raw, a_raw, b_raw, gate, conv_buf, s0, conv_w, a_log, dt_bias, norm_g):
    B, L, _ = qkv_raw.shape
    qkv, new_buf = _j_causal_conv(qkv_raw, conv_buf, conv_w)
    q, k, v = jnp.split(qkv, [GDN_QK_COLS, 2 * GDN_QK_COLS], axis=-1)
    l2 = lambda x: x * lax.rsqrt(jnp.sum(x * x, axis=-1, keepdims=True) + EPS)
    q = l2(q.reshape(B, L, N_GDN_HEADS, GDN_DK)) * (GDN_DK ** -0.5)
    k = l2(k.reshape(B, L, N_GDN_HEADS, GDN_DK))
    v = v.reshape(B, L, N_GDN_HEADS, GDN_DV)
    beta = jax.nn.sigmoid(b_raw)
    g = -jnp.exp(a_log) * jax.nn.softplus(a_raw + dt_bias)
    o, s_new = _j_gated_delta_chunked(q, k, v, g, beta, s0)
    o = _j_rmsnorm(o, norm_g) * jax.nn.silu(gate.reshape(B, L, N_GDN_HEADS, GDN_DV))
    return o.reshape(B, L, GDN_V_COLS), s_new, new_buf


def _j_mem_kv(mem, ln_g, w_kv, k_g):
    B, M, _ = mem.shape
    kv = _j_rmsnorm(mem, ln_g) @ w_kv
    k, v = jnp.split(kv, 2, axis=-1)
    k = _j_rmsnorm(k.reshape(B, M, N_MEM_HEADS, HEAD_DIM), k_g)
    return k, v.reshape(B, M, N_MEM_HEADS, HEAD_DIM)


def _j_mem_attend(q, mem_k, mem_v, q_g):
    B, L, _ = q.shape
    q = _j_rmsnorm(q.reshape(B, L, N_MEM_HEADS, HEAD_DIM), q_g)
    s = jnp.einsum('bqhd,bmhd->bhqm', q, mem_k) * ATTN_SCALE
    p = jax.nn.softmax(s, axis=-1)
    o = jnp.einsum('bhqm,bmhd->bqhd', p, mem_v)
    return o.reshape(B, L, MEM_Q_COLS)


def _mixers(z, B, L, p, mem_k, mem_v, gdn_s, conv_buf, swa_k_buf=None, swa_v_buf=None):
    z3 = z.reshape(B, L, Z_COLS)
    q_s, k_s, v_s = z3[..., Z_Q:Z_K], z3[..., Z_K:Z_V], z3[..., Z_V:Z_GDN]
    qkv_d, gate_d, q_m = z3[..., Z_GDN:Z_GATE], z3[..., Z_GATE:Z_QM], z3[..., Z_QM:Z_AB]
    a_d, b_d = z3[..., Z_AB:Z_AB + N_GDN_HEADS], z3[..., Z_AB + N_GDN_HEADS:Z_AB + 2 * N_GDN_HEADS]
    if swa_k_buf is None:
        o_s, nk, nv = _j_swa_prompt(q_s, k_s, v_s, p['q_norm'], p['k_norm'], p['sinks'])
    else:
        o_s, nk, nv = _j_swa_sample(q_s, k_s, v_s, swa_k_buf, swa_v_buf, p['q_norm'], p['k_norm'], p['sinks'])
    o_d, ns, nc = _j_gdn(qkv_d, a_d, b_d, gate_d, conv_buf, gdn_s, p['conv_w'], p['a_log'], p['dt_bias'], p['gdn_norm'])
    o_m = _j_mem_attend(q_m, mem_k, mem_v, p['mem_q_norm'])
    T = B * L
    return o_s.reshape(T, -1), o_d.reshape(T, -1), o_m.reshape(T, -1), nk, nv, ns, nc


def kernel(x_prompt, x_sample, cache_swa_k, cache_swa_v, state_gdn, state_gdn_conv, cache_mem_k, cache_mem_v,
           mem_prompt, ln1_g, w_in, swa_q_norm, swa_k_norm, swa_sinks, gdn_conv_w, gdn_a_log, gdn_dt_bias,
           gdn_norm_g, mem_ln_g, w_mem_kv, mem_q_norm, mem_k_norm, w_o, ln2_g, router_w, router_b,
           moe_w1, moe_b1, moe_w2, moe_b2):
    B, S, D = x_prompt.shape
    DB, DL, _ = x_sample.shape
    depth = ln1_g.shape[0]
    assert depth == 1
    l = 0
    tp, ts = B * S, DB * DL
    t_all = tp + ts
    n_ab = 2 * N_GDN_HEADS
    c_ab = SWA_Q_COLS + 2 * SWA_KV_COLS + GDN_CONV_CH
    w = w_in[l]
    w_z = jnp.concatenate([w[:, :c_ab], w[:, c_ab + n_ab:], w[:, c_ab:c_ab + n_ab],
                           jnp.zeros((D, LANES - n_ab), F32)], axis=1).astype(BF16)
    rw = jnp.pad(router_w[l], ((0, 0), (0, LANES - N_EXPERTS)))
    rw_hi = rw.astype(BF16)
    rw_lo = (rw - rw_hi.astype(F32)).astype(BF16)
    rb = jnp.pad(router_b[l], (0, LANES - N_EXPERTS)).reshape(1, LANES)
    wo = w_o[l].astype(BF16)
    w1 = moe_w1[l].astype(BF16)
    w2 = moe_w2[l].astype(BF16)
    b1 = moe_b1[l].reshape(N_EXPERTS, 1, -1)
    b2 = moe_b2[l].reshape(N_EXPERTS, 1, -1)
    p = {'q_norm': swa_q_norm[l], 'k_norm': swa_k_norm[l], 'sinks': swa_sinks[l], 'conv_w': gdn_conv_w[l],
         'a_log': gdn_a_log[l], 'dt_bias': gdn_dt_bias[l], 'gdn_norm': gdn_norm_g[l], 'mem_q_norm': mem_q_norm[l]}

    xp = x_prompt.reshape(tp, D)
    xs = x_sample.reshape(ts, D)
    z_p = _inproj(xp, ln1_g[l], w_z)
    z_s = _inproj(xs, ln1_g[l], w_z)

    mk, mv = _j_mem_kv(mem_prompt, mem_ln_g[l], w_mem_kv[l], mem_k_norm[l])
    s_zero = jnp.zeros((B, N_GDN_HEADS, GDN_DK, GDN_DV), F32)
    c_zero = jnp.zeros((B, GDN_CONV - 1, GDN_CONV_CH), F32)
    os_p, od_p, om_p, pk, pv, ps, pc = _mixers(z_p, B, S, p, mk, mv, s_zero, c_zero)
    os_s, od_s, om_s, sk, sv, ss, sc = _mixers(z_s, DB, DL, p, cache_mem_k[l], cache_mem_v[l], state_gdn[l],
                                               state_gdn_conv[l], cache_swa_k[l], cache_swa_v[l])

    h_p, hn_all, lg_all = _outproj(xp, os_p, od_p, om_p, wo, ln2_g[l], rw_hi, rw_lo, rb, t_all, 0)
    h_s, hn_all, lg_all = _outproj(xs, os_s, od_s, om_s, wo, ln2_g[l], rw_hi, rw_lo, rb, t_all, tp,
                                   prev=(hn_all, lg_all))
    y_p, y_s = _moe(hn_all, lg_all, [h_p, h_s], w1, b1, w2, b2)
    return (y_p.reshape(B, S, D), y_s.reshape(DB, DL, D), pk[None], pv[None], ps[None], pc[None], mk[None],
            mv[None], sk[None], sv[None], ss[None], sc[None])
```

```python
import functools

import jax
import jax.numpy as jnp
from jax import lax
from jax.experimental import pallas as pl
from jax.experimental.pallas import tpu as pltpu

F32 = jnp.float32
BF16 = jnp.bfloat16
I32 = jnp.int32

HEAD_DIM = 64
N_SWA_HEADS = 8
N_SWA_KV = 2
SWA_GROUP = N_SWA_HEADS // N_SWA_KV
WINDOW = 128
N_GDN_HEADS = 4
GDN_DK = 64
GDN_DV = 64
GDN_CONV = 4
GDN_CHUNK = 64
N_MEM_HEADS = 4
N_EXPERTS = 32
TOP_K = 4
SWIGLU_ALPHA = 1.702
SWIGLU_LIMIT = 7.0
EPS = 1e-6
ATTN_SCALE = HEAD_DIM ** -0.5

SWA_Q_COLS = N_SWA_HEADS * HEAD_DIM
SWA_KV_COLS = N_SWA_KV * HEAD_DIM
GDN_QK_COLS = N_GDN_HEADS * GDN_DK
GDN_V_COLS = N_GDN_HEADS * GDN_DV
GDN_CONV_CH = 2 * GDN_QK_COLS + GDN_V_COLS
MEM_Q_COLS = N_MEM_HEADS * HEAD_DIM

LANES = 128
SUBLANES = 8
VMEM_LIMIT = 56 * 1024 * 1024

Z_Q = 0
Z_K = Z_Q + SWA_Q_COLS
Z_V = Z_K + SWA_KV_COLS
Z_GDN = Z_V + SWA_KV_COLS
Z_GATE = Z_GDN + GDN_CONV_CH
Z_QM = Z_GATE + GDN_V_COLS
Z_AB = Z_QM + MEM_Q_COLS
Z_COLS = Z_AB + LANES

ROW_TILE = 512
MOE_TILE = 256
MOE_BLK = 512
BLK_ROWS = -(-(MOE_BLK * TOP_K + N_EXPERTS * (SUBLANES - 1)) // MOE_TILE) * MOE_TILE
U32 = jnp.uint32


def _cparams(*sem):
    return pltpu.CompilerParams(dimension_semantics=sem, vmem_limit_bytes=VMEM_LIMIT)


def _bdot(a, b):
    return jnp.dot(a.astype(BF16), b.astype(BF16), preferred_element_type=F32)


def _bdot_nt(a, b):
    return lax.dot_general(a.astype(BF16), b.astype(BF16), (((1,), (1,)), ((), ())),
                           preferred_element_type=F32)


def _bdot_tn(a, b):
    return lax.dot_general(a.astype(BF16), b.astype(BF16), (((0,), (0,)), ((), ())),
                           preferred_element_type=F32)


def _split2(x):
    hi = x.astype(BF16)
    lo = (x - hi.astype(F32)).astype(BF16)
    return hi, lo


def _split3(x):
    hi = x.astype(BF16)
    r = x - hi.astype(F32)
    mid = r.astype(BF16)
    lo = (r - mid.astype(F32)).astype(BF16)
    return hi, mid, lo


def _rms_rows(x, g):
    ms = jnp.mean(x * x, axis=-1, keepdims=True)
    return x * lax.rsqrt(ms + EPS) * g


def _inproj_kernel(x_ref, g_ref, w_ref, z_ref):
    n = _rms_rows(x_ref[...], g_ref[...])
    z_ref[...] = jnp.dot(n.astype(BF16), w_ref[...], preferred_element_type=F32)


def _inproj(x2d, ln_g, w_z):
    t, d = x2d.shape
    tm = min(ROW_TILE, t)
    return pl.pallas_call(
        _inproj_kernel,
        grid=(t // tm,),
        in_specs=[pl.BlockSpec((tm, d), lambda i: (i, 0)),
                  pl.BlockSpec((1, d), lambda i: (0, 0)),
                  pl.BlockSpec((d, Z_COLS), lambda i: (0, 0))],
        out_specs=pl.BlockSpec((tm, Z_COLS), lambda i: (i, 0)),
        out_shape=jax.ShapeDtypeStruct((t, Z_COLS), F32),
        compiler_params=_cparams("parallel"),
        name="inproj",
    )(x2d, ln_g.reshape(1, d), w_z)


def _outproj_kernel(x_ref, os_ref, od_ref, om_ref, wo_ref, g_ref, rwh_ref, rwl_ref, rb_ref,
                    *refs, n_own):
    h_ref, hn_ref, lg_ref = refs[-3:]
    i = pl.program_id(0)

    @pl.when(i < n_own)
    def _():
        n_s = os_ref.shape[1]
        n_d = od_ref.shape[1]
        h = x_ref[...]
        h = h + jnp.dot(os_ref[...].astype(BF16), wo_ref[0:n_s, :], preferred_element_type=F32)
        h = h + jnp.dot(od_ref[...].astype(BF16), wo_ref[n_s:n_s + n_d, :], preferred_element_type=F32)
        h = h + jnp.dot(om_ref[...].astype(BF16), wo_ref[n_s + n_d:, :], preferred_element_type=F32)
        h_ref[...] = h
        hn = _rms_rows(h, g_ref[...])
        hn_ref[...] = hn
        hi, lo = _split2(hn)
        lg = (jnp.dot(hi, rwh_ref[...], preferred_element_type=F32)
              + jnp.dot(lo, rwh_ref[...], preferred_element_type=F32)
              + jnp.dot(hi, rwl_ref[...], preferred_element_type=F32))
        lg_ref[...] = lg + rb_ref[...]

    @pl.when(i >= n_own)
    def _():
        hn_ref[...] = jnp.zeros_like(hn_ref)
        lg_ref[...] = jnp.zeros_like(lg_ref)


def _outproj(x2d, o_s, o_d, o_m, w_o, ln_g, rw_hi, rw_lo, rb, t_all, row0, prev=None):
    t, d = x2d.shape
    tm = min(ROW_TILE, t)
    blk0 = row0 // tm
    n_own = t // tm
    n_steps = n_own if prev is not None else t_all // tm
    row = lambda i: (jnp.minimum(i, n_own - 1), 0)
    row_off = lambda i: (i + blk0, 0)
    const = lambda i: (0, 0)
    in_specs = [pl.BlockSpec((tm, d), row),
                pl.BlockSpec((tm, o_s.shape[1]), row),
                pl.BlockSpec((tm, o_d.shape[1]), row),
                pl.BlockSpec((tm, o_m.shape[1]), row),
                pl.BlockSpec((d, d), const),
                pl.BlockSpec((1, d), const),
                pl.BlockSpec((d, LANES), const),
                pl.BlockSpec((d, LANES), const),
                pl.BlockSpec((1, LANES), const)]
    args = [x2d, o_s, o_d, o_m, w_o, ln_g.reshape(1, d), rw_hi, rw_lo, rb]
    aliases = {}
    if prev is not None:
        in_specs += [pl.BlockSpec(memory_space=pl.ANY), pl.BlockSpec(memory_space=pl.ANY)]
        aliases = {len(args): 1, len(args) + 1: 2}
        args += list(prev)
    return pl.pallas_call(
        functools.partial(_outproj_kernel, n_own=n_own),
        grid=(n_steps,),
        in_specs=in_specs,
        out_specs=[pl.BlockSpec((tm, d), row),
                   pl.BlockSpec((tm, d), row_off),
                   pl.BlockSpec((tm, LANES), row_off)],
        out_shape=[jax.ShapeDtypeStruct((t, d), F32),
                   jax.ShapeDtypeStruct((t_all, d), F32),
                   jax.ShapeDtypeStruct((t_all, LANES), F32)],
        input_output_aliases=aliases,
        compiler_params=_cparams("arbitrary"),
        name="outproj_router",
    )(*args)


def _route_kernel(lg_ref, pos_ref, post_ref, g_ref, cnt_ref):
    tm = lg_ref.shape[0]
    lane = lax.broadcasted_iota(I32, (tm, LANES), 1).astype(F32)
    l = jnp.where(lane < N_EXPERTS, lg_ref[...], -jnp.inf)
    vals, idxs = [], []
    for _k in range(TOP_K):
        m = jnp.max(l, axis=-1, keepdims=True)
        idx = jnp.min(jnp.where(l == m, lane, float(LANES)), axis=-1, keepdims=True)
        l = jnp.where(lane == idx, -jnp.inf, l)
        vals.append(m)
        idxs.append(idx)
    ex = [jnp.exp(v - vals[0]) for v in vals]
    den = ex[0] + ex[1] + ex[2] + ex[3]
    member = jnp.zeros((tm, LANES), F32)
    for idx in idxs:
        member = member + jnp.where(lane == idx, 1.0, 0.0)
    ri = lax.broadcasted_iota(I32, (tm, tm), 0)
    ci = lax.broadcasted_iota(I32, (tm, tm), 1)
    strict = jnp.where(ci < ri, 1.0, 0.0).astype(BF16)
    prefix = jnp.dot(strict, member.astype(BF16), preferred_element_type=F32)
    cnt = jnp.sum(member, axis=0, keepdims=True)
    cpad = jnp.ceil(cnt * (1.0 / SUBLANES)) * float(SUBLANES)
    c_hi = jnp.floor(cpad * (1.0 / 256.0))
    c_lo = cpad - 256.0 * c_hi
    ej = lax.broadcasted_iota(I32, (LANES, LANES), 0)
    ee = lax.broadcasted_iota(I32, (LANES, LANES), 1)
    before = jnp.where(ej < ee, 1.0, 0.0).astype(BF16)
    bcast = lambda v: jnp.broadcast_to(v, (SUBLANES, LANES)).astype(BF16)
    off = (256.0 * jnp.dot(bcast(c_hi), before, preferred_element_type=F32)
           + jnp.dot(bcast(c_lo), before, preferred_element_type=F32))[0:1]
    where_in_run = prefix + off
    p_out = jnp.zeros((tm, LANES), F32)
    g_out = jnp.zeros((tm, LANES), F32)
    for k in range(TOP_K):
        pos = jnp.sum(jnp.where(lane == idxs[k], where_in_run, 0.0), axis=-1, keepdims=True)
        p_out = jnp.where(lane == float(k), pos, p_out)
        g_out = jnp.where(lane == float(k), ex[k] / den, g_out)
    pos_ref[...] = p_out[:, :TOP_K]
    post_ref[...] = p_out.T[:SUBLANES, :]
    g_ref[...] = g_out[:, :TOP_K]
    cnt_ref[0] = cnt


def _route(logits):
    t = logits.shape[0]
    tm = MOE_BLK
    nb = t // tm
    return pl.pallas_call(
        _route_kernel,
        grid=(nb,),
        in_specs=[pl.BlockSpec((tm, LANES), lambda i: (i, 0))],
        out_specs=[pl.BlockSpec((tm, TOP_K), lambda i: (i, 0)),
                   pl.BlockSpec((SUBLANES, tm), lambda i: (0, i)),
                   pl.BlockSpec((tm, TOP_K), lambda i: (i, 0)),
                   pl.BlockSpec((1, 1, LANES), lambda i: (i, 0, 0))],
        out_shape=[jax.ShapeDtypeStruct((t, TOP_K), F32),
                   jax.ShapeDtypeStruct((SUBLANES, t), F32),
                   jax.ShapeDtypeStruct((t, TOP_K), F32),
                   jax.ShapeDtypeStruct((nb, 1, LANES), F32)],
        compiler_params=_cparams("parallel"),
        name="route",
    )(logits)


def _pack_halves(x):
    c = x.shape[1] // 2
    lo = lax.bitcast_convert_type(x[:, :c].astype(BF16).astype(F32), U32)
    hi = lax.bitcast_convert_type(x[:, c:].astype(BF16).astype(F32), U32)
    return (lo >> 16) | (hi & jnp.uint32(0xFFFF0000))


def _unpack_halves(w):
    lo = lax.bitcast_convert_type(w << 16, F32).astype(BF16)
    hi = lax.bitcast_convert_type(w & jnp.uint32(0xFFFF0000), F32).astype(BF16)
    return lo, hi


def _run_copies(n, max_rows, src_ref, src0, dst_ref, dst0, sem, wait):
    pos = 0
    bit = max_rows
    while bit >= SUBLANES:
        take = (n & bit) != 0

        def go(pos=pos, bit=bit):
            cp = pltpu.make_async_copy(src_ref.at[pl.ds(pl.multiple_of(src0 + pos, SUBLANES), bit)],
                                       dst_ref.at[pl.ds(pl.multiple_of(dst0 + pos, SUBLANES), bit)], sem)
            cp.wait() if wait else cp.start()

        pl.when(take)(go)
        pos = pos + jnp.where(take, bit, 0)
        bit //= 2


def _dispatch_kernel(dst_ref, len_ref, off_ref, estart_ref, elen_ref, nused_ref,
                     hn_ref, post_ref, xs_hbm, buf_ref, zero_ref, sem, zsem):
    b = pl.program_id(0)
    tm = hn_ref.shape[0]
    x = hn_ref[...].astype(BF16)
    post = post_ref[...]
    for c in range(BLK_ROWS // MOE_TILE):
        r = (lax.broadcasted_iota(I32, (MOE_TILE, tm), 0) + c * MOE_TILE).astype(F32)
        sel = jnp.zeros((MOE_TILE, tm), F32)
        for k in range(TOP_K):
            sel = sel + jnp.where(r == post[k:k + 1, :], 1.0, 0.0)
        rows = jnp.dot(sel.astype(BF16), x, preferred_element_type=F32)
        buf_ref[c * MOE_TILE:(c + 1) * MOE_TILE, :] = _pack_halves(rows)

    def runs(wait):
        def body(e, c):
            j = b * N_EXPERTS + e
            _run_copies(len_ref[j], MOE_BLK, buf_ref, off_ref[j], xs_hbm, dst_ref[j], sem, wait)
            return c
        lax.fori_loop(0, N_EXPERTS, body, 0)

    runs(False)

    @pl.when(b == 0)
    def _():
        zero_ref[...] = jnp.zeros_like(zero_ref)

        def tail(wait):
            def body(e, c):
                n = (MOE_TILE - elen_ref[e] % MOE_TILE) % MOE_TILE
                _run_copies(n, MOE_TILE // 2, zero_ref, 0, xs_hbm, estart_ref[e] + elen_ref[e], zsem, wait)
                return c
            lax.fori_loop(0, N_EXPERTS, body, 0)

            def free_tile(ti, c):
                for half in range(2):
                    _run_copies(MOE_TILE // 2, MOE_TILE // 2, zero_ref, 0, xs_hbm,
                                ti * MOE_TILE + half * (MOE_TILE // 2), zsem, wait)
                return c
            lax.fori_loop(nused_ref[0], xs_hbm.shape[0] // MOE_TILE, free_tile, 0)

        tail(False)
        tail(True)

    runs(True)


def _dispatch(hn, post, seg_dst, seg_len, seg_off, e_start, e_len, n_used, n_rows):
    t, d = hn.shape
    grid_spec = pltpu.PrefetchScalarGridSpec(
        num_scalar_prefetch=6,
        grid=(t // MOE_BLK,),
        in_specs=[pl.BlockSpec((MOE_BLK, d), lambda i, *_: (i, 0)),
                  pl.BlockSpec((SUBLANES, MOE_BLK), lambda i, *_: (0, i))],
        out_specs=pl.BlockSpec(memory_space=pl.ANY),
        scratch_shapes=[pltpu.VMEM((BLK_ROWS, d // 2), U32),
                        pltpu.VMEM((MOE_TILE, d // 2), U32),
                        pltpu.SemaphoreType.DMA(()),
                        pltpu.SemaphoreType.DMA(())],
    )
    return pl.pallas_call(
        _dispatch_kernel,
        grid_spec=grid_spec,
        out_shape=jax.ShapeDtypeStruct((n_rows, d // 2), U32),
        compiler_params=_cparams("arbitrary"),
        name="dispatch",
    )(seg_dst, seg_len, seg_off, e_start, e_len, n_used, hn, post)


def _expert_kernel(te_ref, nu_ref, x_ref, w1_ref, b1_ref, w2_ref, b2_ref, y_ref):
    i = pl.program_id(0)

    @pl.when(i < nu_ref[0])
    def _():
        f = w2_ref.shape[1]
        half = x_ref.shape[1]
        x_lo, x_hi = _unpack_halves(x_ref[...])
        h = (jnp.dot(x_lo, w1_ref[0, :half, :], preferred_element_type=F32)
             + jnp.dot(x_hi, w1_ref[0, half:, :], preferred_element_type=F32) + b1_ref[0])
        glu = jnp.minimum(h[:, :f], SWIGLU_LIMIT)
        lin = jnp.clip(h[:, f:], -SWIGLU_LIMIT, SWIGLU_LIMIT)
        act = glu * jax.nn.sigmoid(SWIGLU_ALPHA * glu) * (lin + 1.0)
        y = jnp.dot(act.astype(BF16), w2_ref[0], preferred_element_type=F32) + b2_ref[0]
        y_ref[...] = _pack_halves(y)

    @pl.when(i >= nu_ref[0])
    def _():
        y_ref[...] = jnp.zeros_like(y_ref)


def _experts(xs, tile_expert, n_used, w1, b1, w2, b2):
    n_rows, half = xs.shape
    d = 2 * half
    f2 = w1.shape[2]
    f = w2.shape[1]
    n_tiles = n_rows // MOE_TILE
    live = lambda i, te, nu: (jnp.minimum(i, nu[0] - 1), 0)
    every = lambda i, te, nu: (i, 0)
    wsel = lambda i, te, nu: (te[i], 0, 0)
    grid_spec = pltpu.PrefetchScalarGridSpec(
        num_scalar_prefetch=2,
        grid=(n_tiles,),
        in_specs=[pl.BlockSpec((MOE_TILE, half), live),
                  pl.BlockSpec((1, d, f2), wsel),
                  pl.BlockSpec((1, 1, f2), wsel),
                  pl.BlockSpec((1, f, d), wsel),
                  pl.BlockSpec((1, 1, d), wsel)],
        out_specs=pl.BlockSpec((MOE_TILE, half), every),
    )
    return pl.pallas_call(
        _expert_kernel,
        grid_spec=grid_spec,
        out_shape=jax.ShapeDtypeStruct((n_rows, half), U32),
        compiler_params=_cparams("arbitrary"),
        name="experts",
    )(tile_expert, n_used, xs, w1, b1, w2, b2)


def _combine_kernel(dst_ref, len_ref, off_ref, h_ref, pos_ref, g_ref, yb_hbm, y_ref, buf_ref, sem, *, blk0):
    i = pl.program_id(0)
    b = i + blk0
    tm, d = h_ref.shape
    half = d // 2

    @pl.when(i == 0)
    def _():
        buf_ref[...] = jnp.zeros_like(buf_ref)

    def runs(wait):
        def body(e, c):
            j = b * N_EXPERTS + e
            _run_copies(len_ref[j], MOE_BLK, yb_hbm, dst_ref[j], buf_ref, off_ref[j], sem, wait)
            return c
        lax.fori_loop(0, N_EXPERTS, body, 0)

    runs(False)
    pos = pos_ref[...]
    g = g_ref[...]
    runs(True)
    y_lo = h_ref[:, :half]
    y_hi = h_ref[:, half:]
    for c in range(BLK_ROWS // MOE_TILE):
        col = (lax.broadcasted_iota(I32, (tm, MOE_TILE), 1) + c * MOE_TILE).astype(F32)
        wgt = jnp.zeros((tm, MOE_TILE), F32)
        for k in range(TOP_K):
            wgt = wgt + jnp.where(col == pos[:, k:k + 1], g[:, k:k + 1], 0.0)
        e_lo, e_hi = _unpack_halves(buf_ref[c * MOE_TILE:(c + 1) * MOE_TILE, :])
        wgt = wgt.astype(BF16)
        y_lo = y_lo + jnp.dot(wgt, e_lo, preferred_element_type=F32)
        y_hi = y_hi + jnp.dot(wgt, e_hi, preferred_element_type=F32)
    y_ref[:, :half] = y_lo
    y_ref[:, half:] = y_hi


def _combine(h, pos, gates, seg_dst, seg_len, seg_off, yb, blk0):
    t, d = h.shape
    grid_spec = pltpu.PrefetchScalarGridSpec(
        num_scalar_prefetch=3,
        grid=(t // MOE_BLK,),
        in_specs=[pl.BlockSpec((MOE_BLK, d), lambda i, *_: (i, 0)),
                  pl.BlockSpec((MOE_BLK, TOP_K), lambda i, *_: (i + blk0, 0)),
                  pl.BlockSpec((MOE_BLK, TOP_K), lambda i, *_: (i + blk0, 0)),
                  pl.BlockSpec(memory_space=pl.ANY)],
        out_specs=pl.BlockSpec((MOE_BLK, d), lambda i, *_: (i, 0)),
        scratch_shapes=[pltpu.VMEM((BLK_ROWS, d // 2), U32),
                        pltpu.SemaphoreType.DMA(())],
    )
    return pl.pallas_call(
        functools.partial(_combine_kernel, blk0=blk0),
        grid_spec=grid_spec,
        out_shape=jax.ShapeDtypeStruct((t, d), F32),
        compiler_params=_cparams("arbitrary"),
        name="combine",
    )(seg_dst, seg_len, seg_off, h, pos, gates, yb)


def _moe(hn_all, logits_all, h_parts, w1, b1, w2, b2):
    t_all = hn_all.shape[0]
    nb = t_all // MOE_BLK
    pos, post, gates, counts_f = _route(logits_all)
    cnt = counts_f.reshape(nb, LANES)[:, :N_EXPERTS].astype(I32)
    seg_len = (cnt + SUBLANES - 1) // SUBLANES * SUBLANES
    seg_off = jnp.cumsum(seg_len, axis=1) - seg_len
    e_len = jnp.sum(seg_len, axis=0)
    e_tiles = (e_len + MOE_TILE - 1) // MOE_TILE
    tile_end = jnp.cumsum(e_tiles)
    e_start = (tile_end - e_tiles) * MOE_TILE
    seg_dst = e_start[None, :] + jnp.cumsum(seg_len, axis=0) - seg_len
    max_rows = t_all * TOP_K + nb * N_EXPERTS * (SUBLANES - 1) + N_EXPERTS * (MOE_TILE - SUBLANES)
    n_tiles = -(-max_rows // MOE_TILE)
    n_rows = n_tiles * MOE_TILE
    n_used = tile_end[-1:].astype(I32)
    tile_expert = jnp.minimum(jnp.searchsorted(tile_end, jnp.arange(n_tiles, dtype=I32), side="right"),
                              N_EXPERTS - 1).astype(I32)
    flat = lambda a: a.reshape(-1).astype(I32)
    xs = _dispatch(hn_all, post, flat(seg_dst), flat(seg_len), flat(seg_off), flat(e_start), flat(e_len),
                   n_used, n_rows)
    yb = _experts(xs, tile_expert, n_used, w1, b1, w2, b2)
    outs = []
    row = 0
    for h in h_parts:
        outs.append(_combine(h, pos, gates, flat(seg_dst), flat(seg_len), flat(seg_off), yb, row // MOE_BLK))
        row += h.shape[0]
    return outs


import math
import numpy as np


def _j_rmsnorm(x, g):
    return x * lax.rsqrt(jnp.mean(x * x, axis=-1, keepdims=True) + EPS) * g


def _j_alibi_sink_probs(scores, dist, valid, sinks):
    hh = jnp.arange(1, N_SWA_HEADS + 1, dtype=F32)
    slopes = jnp.exp2(-(8.0 / N_SWA_HEADS) * hh).reshape(N_SWA_KV, SWA_GROUP, 1, 1)
    s = scores * ATTN_SCALE - slopes * dist.astype(F32)
    s = jnp.where(valid, s, -jnp.inf)
    sink = sinks.reshape(N_SWA_KV, SWA_GROUP, 1, 1)
    m = jnp.maximum(jnp.max(s, axis=-1, keepdims=True), sink)
    p = jnp.exp(s - m)
    return p / (jnp.sum(p, axis=-1, keepdims=True) + jnp.exp(sink - m))


def _j_swa_prompt(q, k, v, q_g, k_g, sinks):
    B, S, _ = q.shape
    NB = S // WINDOW
    q = _j_rmsnorm(q.reshape(B, S, N_SWA_KV, SWA_GROUP, HEAD_DIM), q_g)
    k = _j_rmsnorm(k.reshape(B, S, N_SWA_KV, HEAD_DIM), k_g)
    v = v.reshape(B, S, N_SWA_KV, HEAD_DIM)
    qb = q.reshape(B, NB, WINDOW, N_SWA_KV, SWA_GROUP, HEAD_DIM)
    kb = k.reshape(B, NB, WINDOW, N_SWA_KV, HEAD_DIM)
    vb = v.reshape(B, NB, WINDOW, N_SWA_KV, HEAD_DIM)
    pad = jnp.zeros_like(kb[:, :1])
    k2 = jnp.concatenate([jnp.concatenate([pad, kb[:, :-1]], axis=1), kb], axis=2)
    v2 = jnp.concatenate([jnp.concatenate([pad, vb[:, :-1]], axis=1), vb], axis=2)
    i = jnp.arange(WINDOW)[:, None]
    j = jnp.arange(2 * WINDOW)[None, :]
    dist = i + WINDOW - j
    band = (dist >= 0) & (dist < WINDOW)
    has_prev = (jnp.arange(NB)[:, None, None] > 0) | (j[None] >= WINDOW)
    valid = (band[None] & has_prev)[:, None, None]
    scores = jnp.einsum('bnqkgd,bnskd->bnkgqs', qb, k2)
    p = _j_alibi_sink_probs(scores, dist, valid, sinks)
    o = jnp.einsum('bnkgqs,bnskd->bnqkgd', p, v2)
    return o.reshape(B, S, SWA_Q_COLS), k[:, -WINDOW:], v[:, -WINDOW:]


def _j_swa_sample(q, k, v, buf_k, buf_v, q_g, k_g, sinks):
    B, L, _ = q.shape
    Wb = buf_k.shape[1]
    q = _j_rmsnorm(q.reshape(B, L, N_SWA_KV, SWA_GROUP, HEAD_DIM), q_g)
    k = _j_rmsnorm(k.reshape(B, L, N_SWA_KV, HEAD_DIM), k_g)
    v = v.reshape(B, L, N_SWA_KV, HEAD_DIM)
    kc = jnp.concatenate([buf_k, k], axis=1)
    vc = jnp.concatenate([buf_v, v], axis=1)
    dist = Wb + jnp.arange(L)[:, None] - jnp.arange(Wb + L)[None, :]
    valid = (dist >= 0) & (dist < WINDOW)
    scores = jnp.einsum('bqkgd,bskd->bkgqs', q, kc)
    p = _j_alibi_sink_probs(scores, dist, valid, sinks)
    o = jnp.einsum('bkgqs,bskd->bqkgd', p, vc)
    return o.reshape(B, L, SWA_Q_COLS), kc[:, -Wb:], vc[:, -Wb:]


def _j_causal_conv(u, buf, w):
    L = u.shape[1]
    up = jnp.concatenate([buf, u], axis=1)
    y = up[:, 0:L] * w[0]
    for i in range(1, GDN_CONV):
        y = y + up[:, i:i + L] * w[i]
    return jax.nn.silu(y), up[:, -(GDN_CONV - 1):]


def _j_gated_delta_chunked(q, k, v, g, beta, s0):
    B, L, H, DK = q.shape
    DV = v.shape[-1]
    C = math.gcd(GDN_CHUNK, L)
    N = L // C

    def blocks(t):
        return jnp.moveaxis(t.reshape((B, N, C, H) + t.shape[3:]), 3, 1)

    qc, kc, vc, gc, bc = blocks(q), blocks(k), blocks(v), blocks(g), blocks(beta)
    gcum = jnp.cumsum(gc, axis=-1)
    incl = jnp.tril(jnp.ones((C, C), bool))
    strict = jnp.tril(jnp.ones((C, C), bool), -1)
    decay = jnp.exp(jnp.where(incl, gcum[..., :, None] - gcum[..., None, :], -jnp.inf))
    a_mat = jnp.where(strict, bc[..., :, None] * jnp.einsum('bhnid,bhnjd->bhnij', kc, kc) * decay, 0.0)
    rhs = jnp.concatenate([vc * bc[..., None], kc * (bc * jnp.exp(gcum))[..., None]], axis=-1)
    sol = lax.linalg.triangular_solve(a_mat, rhs, left_side=True, lower=True, unit_diagonal=True)
    u, w = sol[..., :DV], sol[..., DV:]
    qk = jnp.einsum('bhnid,bhnjd->bhnij', qc, kc) * decay
    q_dec = qc * jnp.exp(gcum)[..., None]
    k_dec = kc * jnp.exp(gcum[..., -1:] - gcum)[..., None]
    g_last = jnp.exp(gcum[..., -1])
    xs = tuple(jnp.moveaxis(t, 2, 0) for t in (u, w, qk, q_dec, k_dec, g_last))

    def step(S, inp):
        u_n, w_n, qk_n, qd_n, kd_n, gl_n = inp
        v_new = u_n - jnp.einsum('bhck,bhkv->bhcv', w_n, S)
        o_n = jnp.einsum('bhck,bhkv->bhcv', qd_n, S) + jnp.einsum('bhij,bhjv->bhiv', qk_n, v_new)
        S = S * gl_n[..., None, None] + jnp.einsum('bhck,bhcv->bhkv', kd_n, v_new)
        return S, o_n

    s_fin, o = lax.scan(step, s0, xs)
    o = jnp.transpose(o, (1, 0, 3, 2, 4)).reshape(B, L, H, DV)
    return o, s_fin


def _j_gdn(qkv_raw, a_raw, b_raw, gate, conv_buf, s0, conv_w, a_log, dt_bias, norm_g):
    B, L, _ = qkv_raw.shape
    qkv, new_buf = _j_causal_conv(qkv_raw, conv_buf, conv_w)
    q, k, v = jnp.split(qkv, [GDN_QK_COLS, 2 * GDN_QK_COLS], axis=-1)
    l2 = lambda x: x * lax.rsqrt(jnp.sum(x * x, axis=-1, keepdims=True) + EPS)
    q = l2(q.reshape(B, L, N_GDN_HEADS, GDN_DK)) * (GDN_DK ** -0.5)
    k = l2(k.reshape(B, L, N_GDN_HEADS, GDN_DK))
    v = v.reshape(B, L, N_GDN_HEADS, GDN_DV)
    beta = jax.nn.sigmoid(b_raw)
    g = -jnp.exp(a_log) * jax.nn.softplus(a_raw + dt_bias)
    o, s_new = _j_gated_delta_chunked(q, k, v, g, beta, s0)
    o = _j_rmsnorm(o, norm_g) * jax.nn.silu(gate.reshape(B, L, N_GDN_HEADS, GDN_DV))
    return o.reshape(B, L, GDN_V_COLS), s_new, new_buf


def _j_mem_kv(mem, ln_g, w_kv, k_g):
    B, M, _ = mem.shape
    kv = _j_rmsnorm(mem, ln_g) @ w_kv
    k, v = jnp.split(kv, 2, axis=-1)
    k = _j_rmsnorm(k.reshape(B, M, N_MEM_HEADS, HEAD_DIM), k_g)
    return k, v.reshape(B, M, N_MEM_HEADS, HEAD_DIM)


def _j_mem_attend(q, mem_k, mem_v, q_g):
    B, L, _ = q.shape
    q = _j_rmsnorm(q.reshape(B, L, N_MEM_HEADS, HEAD_DIM), q_g)
    s = jnp.einsum('bqhd,bmhd->bhqm', q, mem_k) * ATTN_SCALE
    p = jax.nn.softmax(s, axis=-1)
    o = jnp.einsum('bhqm,bmhd->bqhd', p, mem_v)
    return o.reshape(B, L, MEM_Q_COLS)


def _mixers(z, B, L, p, mem_k, mem_v, gdn_s, conv_buf, swa_k_buf=None, swa_v_buf=None):
    z3 = z.reshape(B, L, Z_COLS)
    q_s, k_s, v_s = z3[..., Z_Q:Z_K], z3[..., Z_K:Z_V], z3[..., Z_V:Z_GDN]
    qkv_d, gate_d, q_m = z3[..., Z_GDN:Z_GATE], z3[..., Z_GATE:Z_QM], z3[..., Z_QM:Z_AB]
    a_d, b_d = z3[..., Z_AB:Z_AB + N_GDN_HEADS], z3[..., Z_AB + N_GDN_HEADS:Z_AB + 2 * N_GDN_HEADS]
    if swa_k_buf is None:
        o_s, nk, nv = _j_swa_prompt(q_s, k_s, v_s, p['q_norm'], p['k_norm'], p['sinks'])
    else:
        o_s, nk, nv = _j_swa_sample(q_s, k_s, v_s, swa_k_buf, swa_v_buf, p['q_norm'], p['k_norm'], p['sinks'])
    o_d, ns, nc = _j_gdn(qkv_d, a_d, b_d, gate_d, conv_buf, gdn_s, p['conv_w'], p['a_log'], p['dt_bias'], p['gdn_norm'])
    o_m = _j_mem_attend(q_m, mem_k, mem_v, p['mem_q_norm'])
    T = B * L
    return o_s.reshape(T, -1), o_d.reshape(T, -1), o_m.reshape(T, -1), nk, nv, ns, nc


def kernel(x_prompt, x_sample, cache_swa_k, cache_swa_v, state_gdn, state_gdn_conv, cache_mem_k, cache_mem_v,
           mem_prompt, ln1_g, w_in, swa_q_norm, swa_k_norm, swa_sinks, gdn_conv_w, gdn_a_log, gdn_dt_bias,
           gdn_norm_g, mem_ln_g, w_mem_kv, mem_q_norm, mem_k_norm, w_o, ln2_g, router_w, router_b,
           moe_w1, moe_b1, moe_w2, moe_b2):
    B, S, D = x_prompt.shape
    DB, DL, _ = x_sample.shape
    depth = ln1_g.shape[0]
    assert depth == 1
    l = 0
    tp, ts = B * S, DB * DL
    t_all = tp + ts
    n_ab = 2 * N_GDN_HEADS
    c_ab = SWA_Q_COLS + 2 * SWA_KV_COLS + GDN_CONV_CH
    w = w_in[l]
    w_z = jnp.concatenate([w[:, :c_ab], w[:, c_ab + n_ab:], w[:, c_ab:c_ab + n_ab],
                           jnp.zeros((D, LANES - n_ab), F32)], axis=1).astype(BF16)
    rw = jnp.pad(router_w[l], ((0, 0), (0, LANES - N_EXPERTS)))
    rw_hi = rw.astype(BF16)
    rw_lo = (rw - rw_hi.astype(F32)).astype(BF16)
    rb = jnp.pad(router_b[l], (0, LANES - N_EXPERTS)).reshape(1, LANES)
    wo = w_o[l].astype(BF16)
    w1 = moe_w1[l].astype(BF16)
    w2 = moe_w2[l].astype(BF16)
    b1 = moe_b1[l].reshape(N_EXPERTS, 1, -1)
    b2 = moe_b2[l].reshape(N_EXPERTS, 1, -1)
    p = {'q_norm': swa_q_norm[l], 'k_norm': swa_k_norm[l], 'sinks': swa_sinks[l], 'conv_w': gdn_conv_w[l],
         'a_log': gdn_a_log[l], 'dt_bias': gdn_dt_bias[l], 'gdn_norm': gdn_norm_g[l], 'mem_q_norm': mem_q_norm[l]}

    xp = x_prompt.reshape(tp, D)
    xs = x_sample.reshape(ts, D)
    z_p = _inproj(xp, ln1_g[l], w_z)
    z_s = _inproj(xs, ln1_g[l], w_z)

    mk, mv = _j_mem_kv(mem_prompt, mem_ln_g[l], w_mem_kv[l], mem_k_norm[l])
    s_zero = jnp.zeros((B, N_GDN_HEADS, GDN_DK, GDN_DV), F32)
    c_zero = jnp.zeros((B, GDN_CONV - 1, GDN_CONV_CH), F32)
    os_p, od_p, om_p, pk, pv, ps, pc = _mixers(z_p, B, S, p, mk, mv, s_zero, c_zero)
    os_s, od_s, om_s, sk, sv, ss, sc = _mixers(z_s, DB, DL, p, cache_mem_k[l], cache_mem_v[l], state_gdn[l],
                                               state_gdn_conv[l], cache_swa_k[l], cache_swa_v[l])

    h_p, hn_all, lg_all = _outproj(xp, os_p, od_p, om_p, wo, ln2_g[l], rw_hi, rw_lo, rb, t_all, 0)
    h_s, hn_all, lg_all = _outproj(xs, os_s, od_s, om_s, wo, ln2_g[l], rw_hi, rw_lo, rb, t_all, tp,
                                   prev=(hn_all, lg_all))
    y_p, y_s = _moe(hn_all, lg_all, [h_p, h_s], w1, b1, w2, b2)
    return (y_p.reshape(B, S, D), y_s.reshape(DB, DL, D), pk[None], pv[None], ps[None], pc[None], mk[None],
            mv[None], sk[None], sv[None], ss[None], sc[None])
```

```python
import functools

import jax
import jax.numpy as jnp
from jax import lax
from jax.experimental import pallas as pl
from jax.experimental.pallas import tpu as pltpu

F32 = jnp.float32
BF16 = jnp.bfloat16
I32 = jnp.int32

HEAD_DIM = 64
N_SWA_HEADS = 8
N_SWA_KV = 2
SWA_GROUP = N_SWA_HEADS // N_SWA_KV
WINDOW = 128
N_GDN_HEADS = 4
GDN_DK = 64
GDN_DV = 64
GDN_CONV = 4
GDN_CHUNK = 64
N_MEM_HEADS = 4
N_EXPERTS = 32
TOP_K = 4
SWIGLU_ALPHA = 1.702
SWIGLU_LIMIT = 7.0
EPS = 1e-6
ATTN_SCALE = HEAD_DIM ** -0.5

SWA_Q_COLS = N_SWA_HEADS * HEAD_DIM
SWA_KV_COLS = N_SWA_KV * HEAD_DIM
GDN_QK_COLS = N_GDN_HEADS * GDN_DK
GDN_V_COLS = N_GDN_HEADS * GDN_DV
GDN_CONV_CH = 2 * GDN_QK_COLS + GDN_V_COLS
MEM_Q_COLS = N_MEM_HEADS * HEAD_DIM

LANES = 128
SUBLANES = 8
VMEM_LIMIT = 56 * 1024 * 1024

Z_Q = 0
Z_K = Z_Q + SWA_Q_COLS
Z_V = Z_K + SWA_KV_COLS
Z_GDN = Z_V + SWA_KV_COLS
Z_GATE = Z_GDN + GDN_CONV_CH
Z_QM = Z_GATE + GDN_V_COLS
Z_AB = Z_QM + MEM_Q_COLS
Z_COLS = Z_AB + LANES

ROW_TILE = 512
MOE_TILE = 256
MOE_BLK = 512
BLK_ROWS = -(-(MOE_BLK * TOP_K + N_EXPERTS * (SUBLANES - 1)) // MOE_TILE) * MOE_TILE
U32 = jnp.uint32


def _cparams(*sem):
    return pltpu.CompilerParams(dimension_semantics=sem, vmem_limit_bytes=VMEM_LIMIT)


def _bdot(a, b):
    return jnp.dot(a.astype(BF16), b.astype(BF16), preferred_element_type=F32)


def _bdot_nt(a, b):
    return lax.dot_general(a.astype(BF16), b.astype(BF16), (((1,), (1,)), ((), ())),
                           preferred_element_type=F32)


def _bdot_tn(a, b):
    return lax.dot_general(a.astype(BF16), b.astype(BF16), (((0,), (0,)), ((), ())),
                           preferred_element_type=F32)


def _split2(x):
    hi = x.astype(BF16)
    lo = (x - hi.astype(F32)).astype(BF16)
    return hi, lo


def _split3(x):
    hi = x.astype(BF16)
    r = x - hi.astype(F32)
    mid = r.astype(BF16)
    lo = (r - mid.astype(F32)).astype(BF16)
    return hi, mid, lo


def _rms_rows(x, g):
    ms = jnp.mean(x * x, axis=-1, keepdims=True)
    return x * lax.rsqrt(ms + EPS) * g


def _inproj_kernel(x_ref, g_ref, w_ref, z_ref):
    n = _rms_rows(x_ref[...], g_ref[...])
    z_ref[...] = jnp.dot(n.astype(BF16), w_ref[...], preferred_element_type=F32)


def _inproj(x2d, ln_g, w_z):
    t, d = x2d.shape
    tm = min(ROW_TILE, t)
    return pl.pallas_call(
        _inproj_kernel,
        grid=(t // tm,),
        in_specs=[pl.BlockSpec((tm, d), lambda i: (i, 0)),
                  pl.BlockSpec((1, d), lambda i: (0, 0)),
                  pl.BlockSpec((d, Z_COLS), lambda i: (0, 0))],
        out_specs=pl.BlockSpec((tm, Z_COLS), lambda i: (i, 0)),
        out_shape=jax.ShapeDtypeStruct((t, Z_COLS), F32),
        compiler_params=_cparams("parallel"),
        name="inproj",
    )(x2d, ln_g.reshape(1, d), w_z)


def _outproj_kernel(x_ref, os_ref, od_ref, om_ref, wo_ref, g_ref, rwh_ref, rwl_ref, rb_ref,
                    *refs, n_own):
    h_ref, hn_ref, lg_ref = refs[-3:]
    i = pl.program_id(0)

    @pl.when(i < n_own)
    def _():
        n_s = os_ref.shape[1]
        n_d = od_ref.shape[1]
        h = x_ref[...]
        h = h + jnp.dot(os_ref[...].astype(BF16), wo_ref[0:n_s, :], preferred_element_type=F32)
        h = h + jnp.dot(od_ref[...].astype(BF16), wo_ref[n_s:n_s + n_d, :], preferred_element_type=F32)
        h = h + jnp.dot(om_ref[...].astype(BF16), wo_ref[n_s + n_d:, :], preferred_element_type=F32)
        h_ref[...] = h
        hn = _rms_rows(h, g_ref[...])
        hn_ref[...] = hn
        hi, lo = _split2(hn)
        lg = (jnp.dot(hi, rwh_ref[...], preferred_element_type=F32)
              + jnp.dot(lo, rwh_ref[...], preferred_element_type=F32)
              + jnp.dot(hi, rwl_ref[...], preferred_element_type=F32))
        lg_ref[...] = lg + rb_ref[...]

    @pl.when(i >= n_own)
    def _():
        hn_ref[...] = jnp.zeros_like(hn_ref)
        lg_ref[...] = jnp.zeros_like(lg_ref)


def _outproj(x2d, o_s, o_d, o_m, w_o, ln_g, rw_hi, rw_lo, rb, t_all, row0, prev=None):
    t, d = x2d.shape
    tm = min(ROW_TILE, t)
    blk0 = row0 // tm
    n_own = t // tm
    n_steps = n_own if prev is not None else t_all // tm
    row = lambda i: (jnp.minimum(i, n_own - 1), 0)
    row_off = lambda i: (i + blk0, 0)
    const = lambda i: (0, 0)
    in_specs = [pl.BlockSpec((tm, d), row),
                pl.BlockSpec((tm, o_s.shape[1]), row),
                pl.BlockSpec((tm, o_d.shape[1]), row),
                pl.BlockSpec((tm, o_m.shape[1]), row),
                pl.BlockSpec((d, d), const),
                pl.BlockSpec((1, d), const),
                pl.BlockSpec((d, LANES), const),
                pl.BlockSpec((d, LANES), const),
                pl.BlockSpec((1, LANES), const)]
    args = [x2d, o_s, o_d, o_m, w_o, ln_g.reshape(1, d), rw_hi, rw_lo, rb]
    aliases = {}
    if prev is not None:
        in_specs += [pl.BlockSpec(memory_space=pl.ANY), pl.BlockSpec(memory_space=pl.ANY)]
        aliases = {len(args): 1, len(args) + 1: 2}
        args += list(prev)
    return pl.pallas_call(
        functools.partial(_outproj_kernel, n_own=n_own),
        grid=(n_steps,),
        in_specs=in_specs,
        out_specs=[pl.BlockSpec((tm, d), row),
                   pl.BlockSpec((tm, d), row_off),
                   pl.BlockSpec((tm, LANES), row_off)],
        out_shape=[jax.ShapeDtypeStruct((t, d), F32),
                   jax.ShapeDtypeStruct((t_all, d), F32),
                   jax.ShapeDtypeStruct((t_all, LANES), F32)],
        input_output_aliases=aliases,
        compiler_params=_cparams("arbitrary"),
        name="outproj_router",
    )(*args)


def _route_kernel(lg_ref, pos_ref, post_ref, g_ref, cnt_ref):
    tm = lg_ref.shape[0]
    lane = lax.broadcasted_iota(I32, (tm, LANES), 1).astype(F32)
    l = jnp.where(lane < N_EXPERTS, lg_ref[...], -jnp.inf)
    vals, idxs = [], []
    for _k in range(TOP_K):
        m = jnp.max(l, axis=-1, keepdims=True)
        idx = jnp.min(jnp.where(l == m, lane, float(LANES)), axis=-1, keepdims=True)
        l = jnp.where(lane == idx, -jnp.inf, l)
        vals.append(m)
        idxs.append(idx)
    ex = [jnp.exp(v - vals[0]) for v in vals]
    den = ex[0] + ex[1] + ex[2] + ex[3]
    member = jnp.zeros((tm, LANES), F32)
    for idx in idxs:
        member = member + jnp.where(lane == idx, 1.0, 0.0)
    ri = lax.broadcasted_iota(I32, (tm, tm), 0)
    ci = lax.broadcasted_iota(I32, (tm, tm), 1)
    strict = jnp.where(ci < ri, 1.0, 0.0).astype(BF16)
    prefix = jnp.dot(strict, member.astype(BF16), preferred_element_type=F32)
    cnt = jnp.sum(member, axis=0, keepdims=True)
    cpad = jnp.ceil(cnt * (1.0 / SUBLANES)) * float(SUBLANES)
    c_hi = jnp.floor(cpad * (1.0 / 256.0))
    c_lo = cpad - 256.0 * c_hi
    ej = lax.broadcasted_iota(I32, (LANES, LANES), 0)
    ee = lax.broadcasted_iota(I32, (LANES, LANES), 1)
    before = jnp.where(ej < ee, 1.0, 0.0).astype(BF16)
    bcast = lambda v: jnp.broadcast_to(v, (SUBLANES, LANES)).astype(BF16)
    off = (256.0 * jnp.dot(bcast(c_hi), before, preferred_element_type=F32)
           + jnp.dot(bcast(c_lo), before, preferred_element_type=F32))[0:1]
    where_in_run = prefix + off
    p_out = jnp.zeros((tm, LANES), F32)
    g_out = jnp.zeros((tm, LANES), F32)
    for k in range(TOP_K):
        pos = jnp.sum(jnp.where(lane == idxs[k], where_in_run, 0.0), axis=-1, keepdims=True)
        p_out = jnp.where(lane == float(k), pos, p_out)
        g_out = jnp.where(lane == float(k), ex[k] / den, g_out)
    pos_ref[...] = p_out[:, :TOP_K]
    post_ref[...] = p_out.T[:SUBLANES, :]
    g_ref[...] = g_out[:, :TOP_K]
    cnt_ref[0] = cnt


def _route(logits):
    t = logits.shape[0]
    tm = MOE_BLK
    nb = t // tm
    return pl.pallas_call(
        _route_kernel,
        grid=(nb,),
        in_specs=[pl.BlockSpec((tm, LANES), lambda i: (i, 0))],
        out_specs=[pl.BlockSpec((tm, TOP_K), lambda i: (i, 0)),
                   pl.BlockSpec((SUBLANES, tm), lambda i: (0, i)),
                   pl.BlockSpec((tm, TOP_K), lambda i: (i, 0)),
                   pl.BlockSpec((1, 1, LANES), lambda i: (i, 0, 0))],
        out_shape=[jax.ShapeDtypeStruct((t, TOP_K), F32),
                   jax.ShapeDtypeStruct((SUBLANES, t), F32),
                   jax.ShapeDtypeStruct((t, TOP_K), F32),
                   jax.ShapeDtypeStruct((nb, 1, LANES), F32)],
        compiler_params=_cparams("parallel"),
        name="route",
    )(logits)


def _pack_halves(x):
    c = x.shape[1] // 2
    lo = lax.bitcast_convert_type(x[:, :c].astype(BF16).astype(F32), U32)
    hi = lax.bitcast_convert_type(x[:, c:].astype(BF16).astype(F32), U32)
    return (lo >> 16) | (hi & jnp.uint32(0xFFFF0000))


def _unpack_halves(w):
    lo = lax.bitcast_convert_type(w << 16, F32).astype(BF16)
    hi = lax.bitcast_convert_type(w & jnp.uint32(0xFFFF0000), F32).astype(BF16)
    return lo, hi


def _run_copies(n, max_rows, src_ref, src0, dst_ref, dst0, sem, wait):
    pos = 0
    bit = max_rows
    while bit >= SUBLANES:
        take = (n & bit) != 0

        def go(pos=pos, bit=bit):
            cp = pltpu.make_async_copy(src_ref.at[pl.ds(pl.multiple_of(src0 + pos, SUBLANES), bit)],
                                       dst_ref.at[pl.ds(pl.multiple_of(dst0 + pos, SUBLANES), bit)], sem)
            cp.wait() if wait else cp.start()

        pl.when(take)(go)
        pos = pos + jnp.where(take, bit, 0)
        bit //= 2


def _dispatch_kernel(dst_ref, len_ref, off_ref, estart_ref, elen_ref, nused_ref,
                     hn_ref, post_ref, xs_hbm, buf_ref, zero_ref, sem, zsem):
    b = pl.program_id(0)
    tm = hn_ref.shape[0]
    x = hn_ref[...].astype(BF16)
    post = post_ref[...]
    for c in range(BLK_ROWS // MOE_TILE):
        r = (lax.broadcasted_iota(I32, (MOE_TILE, tm), 0) + c * MOE_TILE).astype(F32)
        sel = jnp.zeros((MOE_TILE, tm), F32)
        for k in range(TOP_K):
            sel = sel + jnp.where(r == post[k:k + 1, :], 1.0, 0.0)
        rows = jnp.dot(sel.astype(BF16), x, preferred_element_type=F32)
        buf_ref[c * MOE_TILE:(c + 1) * MOE_TILE, :] = _pack_halves(rows)

    def runs(wait):
        def body(e, c):
            j = b * N_EXPERTS + e
            _run_copies(len_ref[j], MOE_BLK, buf_ref, off_ref[j], xs_hbm, dst_ref[j], sem, wait)
            return c
        lax.fori_loop(0, N_EXPERTS, body, 0)

    runs(False)

    @pl.when(b == 0)
    def _():
        zero_ref[...] = jnp.zeros_like(zero_ref)

        def tail(wait):
            def body(e, c):
                n = (MOE_TILE - elen_ref[e] % MOE_TILE) % MOE_TILE
                _run_copies(n, MOE_TILE // 2, zero_ref, 0, xs_hbm, estart_ref[e] + elen_ref[e], zsem, wait)
                return c
            lax.fori_loop(0, N_EXPERTS, body, 0)

            def free_tile(ti, c):
                for half in range(2):
                    _run_copies(MOE_TILE // 2, MOE_TILE // 2, zero_ref, 0, xs_hbm,
                                ti * MOE_TILE + half * (MOE_TILE // 2), zsem, wait)
                return c
            lax.fori_loop(nused_ref[0], xs_hbm.shape[0] // MOE_TILE, free_tile, 0)

        tail(False)
        tail(True)

    runs(True)


def _dispatch(hn, post, seg_dst, seg_len, seg_off, e_start, e_len, n_used, n_rows):
    t, d = hn.shape
    grid_spec = pltpu.PrefetchScalarGridSpec(
        num_scalar_prefetch=6,
        grid=(t // MOE_BLK,),
        in_specs=[pl.BlockSpec((MOE_BLK, d), lambda i, *_: (i, 0)),
                  pl.BlockSpec((SUBLANES, MOE_BLK), lambda i, *_: (0, i))],
        out_specs=pl.BlockSpec(memory_space=pl.ANY),
        scratch_shapes=[pltpu.VMEM((BLK_ROWS, d // 2), U32),
                        pltpu.VMEM((MOE_TILE, d // 2), U32),
                        pltpu.SemaphoreType.DMA(()),
                        pltpu.SemaphoreType.DMA(())],
    )
    return pl.pallas_call(
        _dispatch_kernel,
        grid_spec=grid_spec,
        out_shape=jax.ShapeDtypeStruct((n_rows, d // 2), U32),
        compiler_params=_cparams("arbitrary"),
        name="dispatch",
    )(seg_dst, seg_len, seg_off, e_start, e_len, n_used, hn, post)


def _expert_kernel(te_ref, nu_ref, x_ref, w1_ref, b1_ref, w2_ref, b2_ref, y_ref):
    i = pl.program_id(0)

    @pl.when(i < nu_ref[0])
    def _():
        f = w2_ref.shape[1]
        half = x_ref.shape[1]
        x_lo, x_hi = _unpack_halves(x_ref[...])
        h = (jnp.dot(x_lo, w1_ref[0, :half, :], preferred_element_type=F32)
             + jnp.dot(x_hi, w1_ref[0, half:, :], preferred_element_type=F32) + b1_ref[0])
        glu = jnp.minimum(h[:, :f], SWIGLU_LIMIT)
        lin = jnp.clip(h[:, f:], -SWIGLU_LIMIT, SWIGLU_LIMIT)
        act = glu * jax.nn.sigmoid(SWIGLU_ALPHA * glu) * (lin + 1.0)
        y = jnp.dot(act.astype(BF16), w2_ref[0], preferred_element_type=F32) + b2_ref[0]
        y_ref[...] = _pack_halves(y)

    @pl.when(i >= nu_ref[0])
    def _():
        y_ref[...] = jnp.zeros_like(y_ref)


def _experts(xs, tile_expert, n_used, w1, b1, w2, b2):
    n_rows, half = xs.shape
    d = 2 * half
    f2 = w1.shape[2]
    f = w2.shape[1]
    n_tiles = n_rows // MOE_TILE
    live = lambda i, te, nu: (jnp.minimum(i, nu[0] - 1), 0)
    every = lambda i, te, nu: (i, 0)
    wsel = lambda i, te, nu: (te[i], 0, 0)
    grid_spec = pltpu.PrefetchScalarGridSpec(
        num_scalar_prefetch=2,
        grid=(n_tiles,),
        in_specs=[pl.BlockSpec((MOE_TILE, half), live),
                  pl.BlockSpec((1, d, f2), wsel),
                  pl.BlockSpec((1, 1, f2), wsel),
                  pl.BlockSpec((1, f, d), wsel),
                  pl.BlockSpec((1, 1, d), wsel)],
        out_specs=pl.BlockSpec((MOE_TILE, half), every),
    )
    return pl.pallas_call(
        _expert_kernel,
        grid_spec=grid_spec,
        out_shape=jax.ShapeDtypeStruct((n_rows, half), U32),
        compiler_params=_cparams("arbitrary"),
        name="experts",
    )(tile_expert, n_used, xs, w1, b1, w2, b2)


def _combine_kernel(dst_ref, len_ref, off_ref, h_ref, pos_ref, g_ref, yb_hbm, y_ref, buf_ref, sem, *, blk0):
    i = pl.program_id(0)
    b = i + blk0
    tm, d = h_ref.shape
    half = d // 2

    @pl.when(i == 0)
    def _():
        buf_ref[...] = jnp.zeros_like(buf_ref)

    def runs(wait):
        def body(e, c):
            j = b * N_EXPERTS + e
            _run_copies(len_ref[j], MOE_BLK, yb_hbm, dst_ref[j], buf_ref, off_ref[j], sem, wait)
            return c
        lax.fori_loop(0, N_EXPERTS, body, 0)

    runs(False)
    pos = pos_ref[...]
    g = g_ref[...]
    runs(True)
    y_lo = h_ref[:, :half]
    y_hi = h_ref[:, half:]
    for c in range(BLK_ROWS // MOE_TILE):
        col = (lax.broadcasted_iota(I32, (tm, MOE_TILE), 1) + c * MOE_TILE).astype(F32)
        wgt = jnp.zeros((tm, MOE_TILE), F32)
        for k in range(TOP_K):
            wgt = wgt + jnp.where(col == pos[:, k:k + 1], g[:, k:k + 1], 0.0)
        e_lo, e_hi = _unpack_halves(buf_ref[c * MOE_TILE:(c + 1) * MOE_TILE, :])
        wgt = wgt.astype(BF16)
        y_lo = y_lo + jnp.dot(wgt, e_lo, preferred_element_type=F32)
        y_hi = y_hi + jnp.dot(wgt, e_hi, preferred_element_type=F32)
    y_ref[:, :half] = y_lo
    y_ref[:, half:] = y_hi


def _combine(h, pos, gates, seg_dst, seg_len, seg_off, yb, blk0):
    t, d = h.shape
    grid_spec = pltpu.PrefetchScalarGridSpec(
        num_scalar_prefetch=3,
        grid=(t // MOE_BLK,),
        in_specs=[pl.BlockSpec((MOE_BLK, d), lambda i, *_: (i, 0)),
                  pl.BlockSpec((MOE_BLK, TOP_K), lambda i, *_: (i + blk0, 0)),
                  pl.BlockSpec((MOE_BLK, TOP_K), lambda i, *_: (i + blk0, 0)),
                  pl.BlockSpec(memory_space=pl.ANY)],
        out_specs=pl.BlockSpec((MOE_BLK, d), lambda i, *_: (i, 0)),
        scratch_shapes=[pltpu.VMEM((BLK_ROWS, d // 2), U32),
                        pltpu.SemaphoreType.DMA(())],
    )
    return pl.pallas_call(
        functools.partial(_combine_kernel, blk0=blk0),
        grid_spec=grid_spec,
        out_shape=jax.ShapeDtypeStruct((t, d), F32),
        compiler_params=_cparams("arbitrary"),
        name="combine",
    )(seg_dst, seg_len, seg_off, h, pos, gates, yb)


def _moe(hn_all, logits_all, h_parts, w1, b1, w2, b2):
    t_all = hn_all.shape[0]
    nb = t_all // MOE_BLK
    pos, post, gates, counts_f = _route(logits_all)
    cnt = counts_f.reshape(nb, LANES)[:, :N_EXPERTS].astype(I32)
    seg_len = (cnt + SUBLANES - 1) // SUBLANES * SUBLANES
    seg_off = jnp.cumsum(seg_len, axis=1) - seg_len
    e_len = jnp.sum(seg_len, axis=0)
    e_tiles = (e_len + MOE_TILE - 1) // MOE_TILE
    tile_end = jnp.cumsum(e_tiles)
    e_start = (tile_end - e_tiles) * MOE_TILE
    seg_dst = e_start[None, :] + jnp.cumsum(seg_len, axis=0) - seg_len
    max_rows = t_all * TOP_K + nb * N_EXPERTS * (SUBLANES - 1) + N_EXPERTS * (MOE_TILE - SUBLANES)
    n_tiles = -(-max_rows // MOE_TILE)
    n_rows = n_tiles * MOE_TILE
    n_used = tile_end[-1:].astype(I32)
    tile_expert = jnp.minimum(jnp.searchsorted(tile_end, jnp.arange(n_tiles, dtype=I32), side="right"),
                              N_EXPERTS - 1).astype(I32)
    flat = lambda a: a.reshape(-1).astype(I32)
    xs = _dispatch(hn_all, post, flat(seg_dst), flat(seg_len), flat(seg_off), flat(e_start), flat(e_len),
                   n_used, n_rows)
    yb = _experts(xs, tile_expert, n_used, w1, b1, w2, b2)
    outs = []
    row = 0
    for h in h_parts:
        outs.append(_combine(h, pos, gates, flat(seg_dst), flat(seg_len), flat(seg_off), yb, row // MOE_BLK))
        row += h.shape[0]
    return outs


GDN_ROWS = 4 * GDN_CHUNK
CONV_HALO = SUBLANES


def _softplus(x):
    return jnp.maximum(x, 0.0) + jnp.log1p(jnp.exp(-jnp.abs(x)))


def _gdn_prompt_kernel(u_ref, ab_ref, gate_ref, cw_ref, alog_ref, dtb_ref, ng_ref, o_ref, s_ref, ubuf_ref):
    step = pl.program_id(1)
    R = GDN_ROWS
    C = GDN_CHUNK

    @pl.when(step == 0)
    def _():
        ubuf_ref[0:CONV_HALO, :] = jnp.zeros((CONV_HALO, ubuf_ref.shape[1]), F32)
        s_ref[...] = jnp.zeros_like(s_ref)

    u = u_ref[0]
    ubuf_ref[CONV_HALO:CONV_HALO + R, :] = u
    cw = cw_ref[...]
    y = u * cw[GDN_CONV - 1:GDN_CONV, :]
    for j in range(1, GDN_CONV):
        y = y + ubuf_ref[CONV_HALO - j:CONV_HALO - j + R, :] * cw[GDN_CONV - 1 - j:GDN_CONV - j, :]
    ubuf_ref[0:CONV_HALO, :] = u[R - CONV_HALO:, :]
    qkv = y * jax.nn.sigmoid(y)

    ab = ab_ref[0]
    g_t = -jnp.exp(alog_ref[...]) * _softplus(ab + dtb_ref[...])
    beta_t = jax.nn.sigmoid(ab)

    ri = lax.broadcasted_iota(I32, (R, R), 0)
    ci = lax.broadcasted_iota(I32, (R, R), 1)
    same = (ri // C) == (ci // C)
    incl = same & (ci <= ri)
    strict = same & (ci < ri)
    tri = jnp.where(incl, 1.0, 0.0).astype(BF16)
    blk = jnp.where(same, 1.0, 0.0).astype(BF16)
    parts = _split3(g_t)
    gcum = sum(jnp.dot(tri, p, preferred_element_type=F32) for p in parts)
    gtot = sum(jnp.dot(blk, p, preferred_element_type=F32) for p in parts)
    gcum_t = gcum.T
    ng = ng_ref[...]
    gate = gate_ref[0]

    for h in range(N_GDN_HEADS):
        q = qkv[:, h * GDN_DK:(h + 1) * GDN_DK]
        k = qkv[:, GDN_QK_COLS + h * GDN_DK:GDN_QK_COLS + (h + 1) * GDN_DK]
        v = qkv[:, 2 * GDN_QK_COLS + h * GDN_DV:2 * GDN_QK_COLS + (h + 1) * GDN_DV]
        q = q * lax.rsqrt(jnp.sum(q * q, axis=-1, keepdims=True) + EPS) * (GDN_DK ** -0.5)
        k = k * lax.rsqrt(jnp.sum(k * k, axis=-1, keepdims=True) + EPS)
        gc = gcum[:, h:h + 1]
        gt = gtot[:, h:h + 1]
        beta = beta_t[:, N_GDN_HEADS + h:N_GDN_HEADS + h + 1]
        decay = jnp.exp(jnp.where(incl, gc - gcum_t[h:h + 1, :], -jnp.inf))
        a_mat = jnp.where(strict, beta * _bdot_nt(k, k) * decay, 0.0)
        qk = _bdot_nt(q, k) * decay
        r = jnp.concatenate([v * beta, k * (beta * jnp.exp(gc))], axis=-1)
        pw = a_mat.astype(BF16)
        powers = [pw]
        sq = 2
        while sq < C:
            pw = jnp.dot(pw, pw, preferred_element_type=F32).astype(BF16)
            powers.append(pw)
            sq *= 2
        for pk in reversed(powers[1:]):
            r = r + jnp.dot(pk, r.astype(BF16), preferred_element_type=F32)
        r = r - jnp.dot(powers[0], r.astype(BF16), preferred_element_type=F32)
        u_s = r[:, :GDN_DV]
        w_s = r[:, GDN_DV:]
        q_dec = q * jnp.exp(gc)
        k_dec = k * jnp.exp(gt - gc)
        g_last = jnp.exp(gt)
        S = s_ref[0, h]
        outs = []
        for c in range(R // C):
            sl = slice(c * C, (c + 1) * C)
            v_new = u_s[sl] - _bdot(w_s[sl], S)
            outs.append(_bdot(q_dec[sl], S) + _bdot(qk[sl, sl], v_new))
            S = S * g_last[c * C:c * C + 1, :] + _bdot_tn(k_dec[sl], v_new)
        s_ref[0, h] = S
        o = jnp.concatenate(outs, axis=0)
        o = o * lax.rsqrt(jnp.mean(o * o, axis=-1, keepdims=True) + EPS) * ng
        gh = gate[:, h * GDN_DV:(h + 1) * GDN_DV]
        o_ref[0, :, h * GDN_DV:(h + 1) * GDN_DV] = o * (gh * jax.nn.sigmoid(gh))


def _gdn_prompt(z3, conv_w, a_log, dt_bias, norm_g):
    B, S, _ = z3.shape
    R = GDN_ROWS
    lanes4 = lambda a: jnp.pad(a, (0, LANES - a.shape[0])).reshape(1, LANES)
    return pl.pallas_call(
        _gdn_prompt_kernel,
        grid=(B, S // R),
        in_specs=[pl.BlockSpec((1, R, GDN_CONV_CH), lambda b, s: (b, s, Z_GDN // GDN_CONV_CH)),
                  pl.BlockSpec((1, R, LANES), lambda b, s: (b, s, Z_AB // LANES)),
                  pl.BlockSpec((1, R, GDN_V_COLS), lambda b, s: (b, s, Z_GATE // GDN_V_COLS)),
                  pl.BlockSpec((GDN_CONV, GDN_CONV_CH), lambda b, s: (0, 0)),
                  pl.BlockSpec((1, LANES), lambda b, s: (0, 0)),
                  pl.BlockSpec((1, LANES), lambda b, s: (0, 0)),
                  pl.BlockSpec((1, GDN_DV), lambda b, s: (0, 0))],
        out_specs=[pl.BlockSpec((1, R, GDN_V_COLS), lambda b, s: (b, s, 0)),
                   pl.BlockSpec((1, N_GDN_HEADS, GDN_DK, GDN_DV), lambda b, s: (b, 0, 0, 0))],
        out_shape=[jax.ShapeDtypeStruct((B, S, GDN_V_COLS), F32),
                   jax.ShapeDtypeStruct((B, N_GDN_HEADS, GDN_DK, GDN_DV), F32)],
        scratch_shapes=[pltpu.VMEM((CONV_HALO + R, GDN_CONV_CH), F32)],
        compiler_params=_cparams("arbitrary", "arbitrary"),
        name="gdn_prompt",
    )(z3, z3, z3, conv_w, lanes4(a_log), lanes4(dt_bias), norm_g.reshape(1, GDN_DV))


def _pair_rms(x, g):
    li = lax.broadcasted_iota(I32, (LANES, LANES), 0) // HEAD_DIM
    lj = lax.broadcasted_iota(I32, (LANES, LANES), 1) // HEAD_DIM
    same = jnp.where(li == lj, 1.0, 0.0).astype(BF16)
    hi, lo = _split2(x * x)
    ms = (jnp.dot(hi, same, preferred_element_type=F32)
          + jnp.dot(lo, same, preferred_element_type=F32)) * (1.0 / HEAD_DIM)
    return x * lax.rsqrt(ms + EPS) * g


def _first_half(shape):
    return lax.broadcasted_iota(I32, shape, 1) < HEAD_DIM


def _swa_prompt_kernel(sink_ref, q_ref, kc_ref, kp_ref, vc_ref, vp_ref, qg_ref, kg_ref, o_ref, kn_ref):
    n = pl.program_id(1)
    W = WINDOW
    kg = kg_ref[...]
    qg = qg_ref[...]
    kc = _pair_rms(kc_ref[0], kg)
    kn_ref[0] = kc
    k2 = jnp.concatenate([_pair_rms(kp_ref[0], kg), kc], axis=0)
    v2 = jnp.concatenate([vp_ref[0], vc_ref[0]], axis=0)
    fh = _first_half(k2.shape)
    k2r = pltpu.roll(k2, HEAD_DIM, 1)
    v2r = pltpu.roll(v2, HEAD_DIM, 1)
    kdup = (jnp.where(fh, k2, k2r), jnp.where(fh, k2r, k2))
    vdup = (jnp.where(fh, v2, v2r), jnp.where(fh, v2r, v2))
    qi = lax.broadcasted_iota(I32, (W, 2 * W), 0)
    kj = lax.broadcasted_iota(I32, (W, 2 * W), 1)
    dist = qi + W - kj
    valid = (dist >= 0) & (dist < W) & ((n > 0) | (kj >= W))
    bias = jnp.where(valid, 0.0, -jnp.inf)
    distf = dist.astype(F32)
    fq = _first_half((W, LANES))
    for t in range(SWA_Q_COLS // LANES):
        kv = t // (SWA_GROUP // 2)
        qt = _pair_rms(q_ref[0, :, t * LANES:(t + 1) * LANES], qg) * ATTN_SCALE
        halves = []
        for half in range(2):
            head = 2 * t + half
            slope = 2.0 ** (-(8.0 / N_SWA_HEADS) * (head + 1))
            qm = jnp.where(fq == (half == 0), qt, 0.0)
            s = _bdot_nt(qm, kdup[kv]) - slope * distf + bias
            sink = sink_ref[head]
            m = jnp.maximum(jnp.max(s, axis=-1, keepdims=True), sink)
            p = jnp.exp(s - m)
            den = jnp.sum(p, axis=-1, keepdims=True) + jnp.exp(sink - m)
            halves.append(_bdot(p, vdup[kv]) / den)
        o_ref[0, :, t * LANES:(t + 1) * LANES] = jnp.where(fq, halves[0], halves[1])


def _swa_prompt(z3, q_g, k_g, sinks):
    B, S, _ = z3.shape
    W = WINDOW
    twice = lambda g: jnp.concatenate([g, g]).reshape(1, LANES)
    kcol, vcol = Z_K // LANES, Z_V // LANES
    grid_spec = pltpu.PrefetchScalarGridSpec(
        num_scalar_prefetch=0,
        grid=(B, S // W),
        in_specs=[pl.BlockSpec(memory_space=pltpu.SMEM),
                  pl.BlockSpec((1, W, SWA_Q_COLS), lambda b, n: (b, n, 0)),
                  pl.BlockSpec((1, W, LANES), lambda b, n: (b, n, kcol)),
                  pl.BlockSpec((1, W, LANES), lambda b, n: (b, jnp.maximum(n - 1, 0), kcol)),
                  pl.BlockSpec((1, W, LANES), lambda b, n: (b, n, vcol)),
                  pl.BlockSpec((1, W, LANES), lambda b, n: (b, jnp.maximum(n - 1, 0), vcol)),
                  pl.BlockSpec((1, LANES), lambda b, n: (0, 0)),
                  pl.BlockSpec((1, LANES), lambda b, n: (0, 0))],
        out_specs=[pl.BlockSpec((1, W, SWA_Q_COLS), lambda b, n: (b, n, 0)),
                   pl.BlockSpec((1, W, LANES), lambda b, n: (b, 0, 0))],
    )
    return pl.pallas_call(
        _swa_prompt_kernel,
        grid_spec=grid_spec,
        out_shape=[jax.ShapeDtypeStruct((B, S, SWA_Q_COLS), F32),
                   jax.ShapeDtypeStruct((B, W, LANES), F32)],
        compiler_params=_cparams("arbitrary", "arbitrary"),
        name="swa_prompt",
    )(sinks, z3, z3, z3, z3, z3, twice(q_g), twice(k_g))


def _mem_kv_kernel(m_ref, g_ref, w_ref, kg_ref, k_ref, v_ref):
    n = _rms_rows(m_ref[...], g_ref[...])
    kv = jnp.dot(n.astype(BF16), w_ref[...], preferred_element_type=F32)
    kg = kg_ref[...]
    for t in range(MEM_Q_COLS // LANES):
        k_ref[:, t * LANES:(t + 1) * LANES] = _pair_rms(kv[:, t * LANES:(t + 1) * LANES], kg)
    v_ref[...] = kv[:, MEM_Q_COLS:]


def _mem_kv(mem2d, ln_g, w_kv, k_g):
    r, d = mem2d.shape
    twice = jnp.concatenate([k_g, k_g]).reshape(1, LANES)
    full = lambda shape: pl.BlockSpec(shape, lambda i: (0,) * len(shape))
    return pl.pallas_call(
        _mem_kv_kernel,
        grid=(1,),
        in_specs=[full((r, d)), full((1, d)), full((d, 2 * MEM_Q_COLS)), full((1, LANES))],
        out_specs=[full((r, MEM_Q_COLS)), full((r, MEM_Q_COLS))],
        out_shape=[jax.ShapeDtypeStruct((r, MEM_Q_COLS), F32), jax.ShapeDtypeStruct((r, MEM_Q_COLS), F32)],
        compiler_params=_cparams("arbitrary"),
        name="mem_kv",
    )(mem2d, ln_g.reshape(1, d), w_kv.astype(BF16), twice)


def _mem_attn_kernel(q_ref, k_ref, v_ref, qg_ref, o_ref):
    qg = qg_ref[...]
    rows = q_ref.shape[1]
    fq = _first_half((rows, LANES))
    for t in range(MEM_Q_COLS // LANES):
        cols = slice(t * LANES, (t + 1) * LANES)
        qt = _pair_rms(q_ref[0, :, cols], qg) * ATTN_SCALE
        kt = k_ref[0, :, cols]
        vt = v_ref[0, :, cols]
        halves = []
        for half in range(2):
            qm = jnp.where(fq == (half == 0), qt, 0.0)
            s = _bdot_nt(qm, kt)
            p = jnp.exp(s - jnp.max(s, axis=-1, keepdims=True))
            halves.append(_bdot(p, vt) / jnp.sum(p, axis=-1, keepdims=True))
        o_ref[0, :, cols] = jnp.where(fq, halves[0], halves[1])


MEM_Q_TILE = 256


def _mem_attn_prompt(z3, mem_k, mem_v, q_g):
    B, S, _ = z3.shape
    M = mem_k.shape[1]
    tq = MEM_Q_TILE
    twice = jnp.concatenate([q_g, q_g]).reshape(1, LANES)
    return pl.pallas_call(
        _mem_attn_kernel,
        grid=(B, S // tq),
        in_specs=[pl.BlockSpec((1, tq, MEM_Q_COLS), lambda b, i: (b, i, Z_QM // MEM_Q_COLS)),
                  pl.BlockSpec((1, M, MEM_Q_COLS), lambda b, i: (b, 0, 0)),
                  pl.BlockSpec((1, M, MEM_Q_COLS), lambda b, i: (b, 0, 0)),
                  pl.BlockSpec((1, LANES), lambda b, i: (0, 0))],
        out_specs=pl.BlockSpec((1, tq, MEM_Q_COLS), lambda b, i: (b, i, 0)),
        out_shape=jax.ShapeDtypeStruct((B, S, MEM_Q_COLS), F32),
        compiler_params=_cparams("parallel", "parallel"),
        name="mem_attn_prompt",
    )(z3, mem_k, mem_v, twice)


import math
import numpy as np


def _j_rmsnorm(x, g):
    return x * lax.rsqrt(jnp.mean(x * x, axis=-1, keepdims=True) + EPS) * g


def _j_alibi_sink_probs(scores, dist, valid, sinks):
    hh = jnp.arange(1, N_SWA_HEADS + 1, dtype=F32)
    slopes = jnp.exp2(-(8.0 / N_SWA_HEADS) * hh).reshape(N_SWA_KV, SWA_GROUP, 1, 1)
    s = scores * ATTN_SCALE - slopes * dist.astype(F32)
    s = jnp.where(valid, s, -jnp.inf)
    sink = sinks.reshape(N_SWA_KV, SWA_GROUP, 1, 1)
    m = jnp.maximum(jnp.max(s, axis=-1, keepdims=True), sink)
    p = jnp.exp(s - m)
    return p / (jnp.sum(p, axis=-1, keepdims=True) + jnp.exp(sink - m))


def _j_swa_prompt(q, k, v, q_g, k_g, sinks):
    B, S, _ = q.shape
    NB = S // WINDOW
    q = _j_rmsnorm(q.reshape(B, S, N_SWA_KV, SWA_GROUP, HEAD_DIM), q_g)
    k = _j_rmsnorm(k.reshape(B, S, N_SWA_KV, HEAD_DIM), k_g)
    v = v.reshape(B, S, N_SWA_KV, HEAD_DIM)
    qb = q.reshape(B, NB, WINDOW, N_SWA_KV, SWA_GROUP, HEAD_DIM)
    kb = k.reshape(B, NB, WINDOW, N_SWA_KV, HEAD_DIM)
    vb = v.reshape(B, NB, WINDOW, N_SWA_KV, HEAD_DIM)
    pad = jnp.zeros_like(kb[:, :1])
    k2 = jnp.concatenate([jnp.concatenate([pad, kb[:, :-1]], axis=1), kb], axis=2)
    v2 = jnp.concatenate([jnp.concatenate([pad, vb[:, :-1]], axis=1), vb], axis=2)
    i = jnp.arange(WINDOW)[:, None]
    j = jnp.arange(2 * WINDOW)[None, :]
    dist = i + WINDOW - j
    band = (dist >= 0) & (dist < WINDOW)
    has_prev = (jnp.arange(NB)[:, None, None] > 0) | (j[None] >= WINDOW)
    valid = (band[None] & has_prev)[:, None, None]
    scores = jnp.einsum('bnqkgd,bnskd->bnkgqs', qb, k2)
    p = _j_alibi_sink_probs(scores, dist, valid, sinks)
    o = jnp.einsum('bnkgqs,bnskd->bnqkgd', p, v2)
    return o.reshape(B, S, SWA_Q_COLS), k[:, -WINDOW:], v[:, -WINDOW:]


def _j_swa_sample(q, k, v, buf_k, buf_v, q_g, k_g, sinks):
    B, L, _ = q.shape
    Wb = buf_k.shape[1]
    q = _j_rmsnorm(q.reshape(B, L, N_SWA_KV, SWA_GROUP, HEAD_DIM), q_g)
    k = _j_rmsnorm(k.reshape(B, L, N_SWA_KV, HEAD_DIM), k_g)
    v = v.reshape(B, L, N_SWA_KV, HEAD_DIM)
    kc = jnp.concatenate([buf_k, k], axis=1)
    vc = jnp.concatenate([buf_v, v], axis=1)
    dist = Wb + jnp.arange(L)[:, None] - jnp.arange(Wb + L)[None, :]
    valid = (dist >= 0) & (dist < WINDOW)
    scores = jnp.einsum('bqkgd,bskd->bkgqs', q, kc)
    p = _j_alibi_sink_probs(scores, dist, valid, sinks)
    o = jnp.einsum('bkgqs,bskd->bqkgd', p, vc)
    return o.reshape(B, L, SWA_Q_COLS), kc[:, -Wb:], vc[:, -Wb:]


def _j_causal_conv(u, buf, w):
    L = u.shape[1]
    up = jnp.concatenate([buf, u], axis=1)
    y = up[:, 0:L] * w[0]
    for i in range(1, GDN_CONV):
        y = y + up[:, i:i + L] * w[i]
    return jax.nn.silu(y), up[:, -(GDN_CONV - 1):]


def _j_gated_delta_chunked(q, k, v, g, beta, s0):
    B, L, H, DK = q.shape
    DV = v.shape[-1]
    C = math.gcd(GDN_CHUNK, L)
    N = L // C

    def blocks(t):
        return jnp.moveaxis(t.reshape((B, N, C, H) + t.shape[3:]), 3, 1)

    qc, kc, vc, gc, bc = blocks(q), blocks(k), blocks(v), blocks(g), blocks(beta)
    gcum = jnp.cumsum(gc, axis=-1)
    incl = jnp.tril(jnp.ones((C, C), bool))
    strict = jnp.tril(jnp.ones((C, C), bool), -1)
    decay = jnp.exp(jnp.where(incl, gcum[..., :, None] - gcum[..., None, :], -jnp.inf))
    a_mat = jnp.where(strict, bc[..., :, None] * jnp.einsum('bhnid,bhnjd->bhnij', kc, kc) * decay, 0.0)
    rhs = jnp.concatenate([vc * bc[..., None], kc * (bc * jnp.exp(gcum))[..., None]], axis=-1)
    sol = lax.linalg.triangular_solve(a_mat, rhs, left_side=True, lower=True, unit_diagonal=True)
    u, w = sol[..., :DV], sol[..., DV:]
    qk = jnp.einsum('bhnid,bhnjd->bhnij', qc, kc) * decay
    q_dec = qc * jnp.exp(gcum)[..., None]
    k_dec = kc * jnp.exp(gcum[..., -1:] - gcum)[..., None]
    g_last = jnp.exp(gcum[..., -1])
    xs = tuple(jnp.moveaxis(t, 2, 0) for t in (u, w, qk, q_dec, k_dec, g_last))

    def step(S, inp):
        u_n, w_n, qk_n, qd_n, kd_n, gl_n = inp
        v_new = u_n - jnp.einsum('bhck,bhkv->bhcv', w_n, S)
        o_n = jnp.einsum('bhck,bhkv->bhcv', qd_n, S) + jnp.einsum('bhij,bhjv->bhiv', qk_n, v_new)
        S = S * gl_n[..., None, None] + jnp.einsum('bhck,bhcv->bhkv', kd_n, v_new)
        return S, o_n

    s_fin, o = lax.scan(step, s0, xs)
    o = jnp.transpose(o, (1, 0, 3, 2, 4)).reshape(B, L, H, DV)
    return o, s_fin


def _j_gdn(qkv_raw, a_raw, b_raw, gate, conv_buf, s0, conv_w, a_log, dt_bias, norm_g):
    B, L, _ = qkv_raw.shape
    qkv, new_buf = _j_causal_conv(qkv_raw, conv_buf, conv_w)
    q, k, v = jnp.split(qkv, [GDN_QK_COLS, 2 * GDN_QK_COLS], axis=-1)
    l2 = lambda x: x * lax.rsqrt(jnp.sum(x * x, axis=-1, keepdims=True) + EPS)
    q = l2(q.reshape(B, L, N_GDN_HEADS, GDN_DK)) * (GDN_DK ** -0.5)
    k = l2(k.reshape(B, L, N_GDN_HEADS, GDN_DK))
    v = v.reshape(B, L, N_GDN_HEADS, GDN_DV)
    beta = jax.nn.sigmoid(b_raw)
    g = -jnp.exp(a_log) * jax.nn.softplus(a_raw + dt_bias)
    o, s_new = _j_gated_delta_chunked(q, k, v, g, beta, s0)
    o = _j_rmsnorm(o, norm_g) * jax.nn.silu(gate.reshape(B, L, N_GDN_HEADS, GDN_DV))
    return o.reshape(B, L, GDN_V_COLS), s_new, new_buf


def _j_mem_kv(mem, ln_g, w_kv, k_g):
    B, M, _ = mem.shape
    kv = _j_rmsnorm(mem, ln_g) @ w_kv
    k, v = jnp.split(kv, 2, axis=-1)
    k = _j_rmsnorm(k.reshape(B, M, N_MEM_HEADS, HEAD_DIM), k_g)
    return k, v.reshape(B, M, N_MEM_HEADS, HEAD_DIM)


def _j_mem_attend(q, mem_k, mem_v, q_g):
    B, L, _ = q.shape
    q = _j_rmsnorm(q.reshape(B, L, N_MEM_HEADS, HEAD_DIM), q_g)
    s = jnp.einsum('bqhd,bmhd->bhqm', q, mem_k) * ATTN_SCALE
    p = jax.nn.softmax(s, axis=-1)
    o = jnp.einsum('bhqm,bmhd->bqhd', p, mem_v)
    return o.reshape(B, L, MEM_Q_COLS)


def _mixers(z, B, L, p, mem_k, mem_v, gdn_s, conv_buf, swa_k_buf=None, swa_v_buf=None):
    z3 = z.reshape(B, L, Z_COLS)
    q_s, k_s, v_s = z3[..., Z_Q:Z_K], z3[..., Z_K:Z_V], z3[..., Z_V:Z_GDN]
    qkv_d, gate_d, q_m = z3[..., Z_GDN:Z_GATE], z3[..., Z_GATE:Z_QM], z3[..., Z_QM:Z_AB]
    a_d, b_d = z3[..., Z_AB:Z_AB + N_GDN_HEADS], z3[..., Z_AB + N_GDN_HEADS:Z_AB + 2 * N_GDN_HEADS]
    if swa_k_buf is None:
        o_s, nk, nv = _j_swa_prompt(q_s, k_s, v_s, p['q_norm'], p['k_norm'], p['sinks'])
    else:
        o_s, nk, nv = _j_swa_sample(q_s, k_s, v_s, swa_k_buf, swa_v_buf, p['q_norm'], p['k_norm'], p['sinks'])
    o_d, ns, nc = _j_gdn(qkv_d, a_d, b_d, gate_d, conv_buf, gdn_s, p['conv_w'], p['a_log'], p['dt_bias'], p['gdn_norm'])
    o_m = _j_mem_attend(q_m, mem_k, mem_v, p['mem_q_norm'])
    T = B * L
    return o_s.reshape(T, -1), o_d.reshape(T, -1), o_m.reshape(T, -1), nk, nv, ns, nc


def kernel(x_prompt, x_sample, cache_swa_k, cache_swa_v, state_gdn, state_gdn_conv, cache_mem_k, cache_mem_v,
           mem_prompt, ln1_g, w_in, swa_q_norm, swa_k_norm, swa_sinks, gdn_conv_w, gdn_a_log, gdn_dt_bias,
           gdn_norm_g, mem_ln_g, w_mem_kv, mem_q_norm, mem_k_norm, w_o, ln2_g, router_w, router_b,
           moe_w1, moe_b1, moe_w2, moe_b2):
    B, S, D = x_prompt.shape
    DB, DL, _ = x_sample.shape
    depth = ln1_g.shape[0]
    assert depth == 1
    l = 0
    tp, ts = B * S, DB * DL
    t_all = tp + ts
    n_ab = 2 * N_GDN_HEADS
    c_ab = SWA_Q_COLS + 2 * SWA_KV_COLS + GDN_CONV_CH
    w = w_in[l]
    w_z = jnp.concatenate([w[:, :c_ab], w[:, c_ab + n_ab:], w[:, c_ab:c_ab + n_ab],
                           jnp.zeros((D, LANES - n_ab), F32)], axis=1).astype(BF16)
    rw = jnp.pad(router_w[l], ((0, 0), (0, LANES - N_EXPERTS)))
    rw_hi = rw.astype(BF16)
    rw_lo = (rw - rw_hi.astype(F32)).astype(BF16)
    rb = jnp.pad(router_b[l], (0, LANES - N_EXPERTS)).reshape(1, LANES)
    wo = w_o[l].astype(BF16)
    w1 = moe_w1[l].astype(BF16)
    w2 = moe_w2[l].astype(BF16)
    b1 = moe_b1[l].reshape(N_EXPERTS, 1, -1)
    b2 = moe_b2[l].reshape(N_EXPERTS, 1, -1)
    p = {'q_norm': swa_q_norm[l], 'k_norm': swa_k_norm[l], 'sinks': swa_sinks[l], 'conv_w': gdn_conv_w[l],
         'a_log': gdn_a_log[l], 'dt_bias': gdn_dt_bias[l], 'gdn_norm': gdn_norm_g[l], 'mem_q_norm': mem_q_norm[l]}

    xp = x_prompt.reshape(tp, D)
    xs = x_sample.reshape(ts, D)
    z_p = _inproj(xp, ln1_g[l], w_z)
    z_s = _inproj(xs, ln1_g[l], w_z)

    M = mem_prompt.shape[1]
    z_p3 = z_p.reshape(B, S, Z_COLS)
    mk2, mv2 = _mem_kv(mem_prompt.reshape(B * M, D), mem_ln_g[l], w_mem_kv[l], mem_k_norm[l])
    mk = mk2.reshape(B, M, N_MEM_HEADS, HEAD_DIM)
    mv = mv2.reshape(B, M, N_MEM_HEADS, HEAD_DIM)
    os_p, pk = _swa_prompt(z_p3, p['q_norm'], p['k_norm'], p['sinks'])
    od_p, ps = _gdn_prompt(z_p3, p['conv_w'], p['a_log'], p['dt_bias'], p['gdn_norm'])
    om_p = _mem_attn_prompt(z_p3, mk2.reshape(B, M, MEM_Q_COLS), mv2.reshape(B, M, MEM_Q_COLS), p['mem_q_norm'])
    os_p, od_p, om_p = os_p.reshape(tp, -1), od_p.reshape(tp, -1), om_p.reshape(tp, -1)
    pk = pk.reshape(B, WINDOW, N_SWA_KV, HEAD_DIM)
    pv = z_p3[:, S - WINDOW:, Z_V:Z_GDN].reshape(B, WINDOW, N_SWA_KV, HEAD_DIM)
    pc = z_p3[:, S - (GDN_CONV - 1):, Z_GDN:Z_GATE]
    os_s, od_s, om_s, sk, sv, ss, sc = _mixers(z_s, DB, DL, p, cache_mem_k[l], cache_mem_v[l], state_gdn[l],
                                               state_gdn_conv[l], cache_swa_k[l], cache_swa_v[l])

    h_p, hn_all, lg_all = _outproj(xp, os_p, od_p, om_p, wo, ln2_g[l], rw_hi, rw_lo, rb, t_all, 0)
    h_s, hn_all, lg_all = _outproj(xs, os_s, od_s, om_s, wo, ln2_g[l], rw_hi, rw_lo, rb, t_all, tp,
                                   prev=(hn_all, lg_all))
    y_p, y_s = _moe(hn_all, lg_all, [h_p, h_s], w1, b1, w2, b2)
    return (y_p.reshape(B, S, D), y_s.reshape(DB, DL, D), pk[None], pv[None], ps[None], pc[None], mk[None],
            mv[None], sk[None], sv[None], ss[None], sc[None])
```

```python
import functools

import jax
import jax.numpy as jnp
from jax import lax
from jax.experimental import pallas as pl
from jax.experimental.pallas import tpu as pltpu

F32 = jnp.float32
BF16 = jnp.bfloat16
I32 = jnp.int32

HEAD_DIM = 64
N_SWA_HEADS = 8
N_SWA_KV = 2
SWA_GROUP = N_SWA_HEADS // N_SWA_KV
WINDOW = 128
N_GDN_HEADS = 4
GDN_DK = 64
GDN_DV = 64
GDN_CONV = 4
GDN_CHUNK = 64
N_MEM_HEADS = 4
N_EXPERTS = 32
TOP_K = 4
SWIGLU_ALPHA = 1.702
SWIGLU_LIMIT = 7.0
EPS = 1e-6
ATTN_SCALE = HEAD_DIM ** -0.5

SWA_Q_COLS = N_SWA_HEADS * HEAD_DIM
SWA_KV_COLS = N_SWA_KV * HEAD_DIM
GDN_QK_COLS = N_GDN_HEADS * GDN_DK
GDN_V_COLS = N_GDN_HEADS * GDN_DV
GDN_CONV_CH = 2 * GDN_QK_COLS + GDN_V_COLS
MEM_Q_COLS = N_MEM_HEADS * HEAD_DIM

LANES = 128
SUBLANES = 8
VMEM_LIMIT = 56 * 1024 * 1024

Z_Q = 0
Z_K = Z_Q + SWA_Q_COLS
Z_V = Z_K + SWA_KV_COLS
Z_GDN = Z_V + SWA_KV_COLS
Z_GATE = Z_GDN + GDN_CONV_CH
Z_QM = Z_GATE + GDN_V_COLS
Z_AB = Z_QM + MEM_Q_COLS
Z_COLS = Z_AB + LANES

ROW_TILE = 512
MOE_TILE = 256
MOE_BLK = 512
BLK_ROWS = -(-(MOE_BLK * TOP_K + N_EXPERTS * (SUBLANES - 1)) // MOE_TILE) * MOE_TILE
U32 = jnp.uint32


def _cparams(*sem):
    return pltpu.CompilerParams(dimension_semantics=sem, vmem_limit_bytes=VMEM_LIMIT)


def _bdot(a, b):
    return jnp.dot(a.astype(BF16), b.astype(BF16), preferred_element_type=F32)


def _bdot_nt(a, b):
    return lax.dot_general(a.astype(BF16), b.astype(BF16), (((1,), (1,)), ((), ())),
                           preferred_element_type=F32)


def _bdot_tn(a, b):
    return lax.dot_general(a.astype(BF16), b.astype(BF16), (((0,), (0,)), ((), ())),
                           preferred_element_type=F32)


def _split2(x):
    hi = x.astype(BF16)
    lo = (x - hi.astype(F32)).astype(BF16)
    return hi, lo


def _split3(x):
    hi = x.astype(BF16)
    r = x - hi.astype(F32)
    mid = r.astype(BF16)
    lo = (r - mid.astype(F32)).astype(BF16)
    return hi, mid, lo


def _rms_rows(x, g):
    ms = jnp.mean(x * x, axis=-1, keepdims=True)
    return x * lax.rsqrt(ms + EPS) * g


def _inproj_kernel(x_ref, g_ref, w_ref, z_ref):
    n = _rms_rows(x_ref[...], g_ref[...])
    z_ref[...] = jnp.dot(n.astype(BF16), w_ref[...], preferred_element_type=F32)


def _inproj(x2d, ln_g, w_z):
    t, d = x2d.shape
    tm = min(ROW_TILE, t)
    return pl.pallas_call(
        _inproj_kernel,
        grid=(t // tm,),
        in_specs=[pl.BlockSpec((tm, d), lambda i: (i, 0)),
                  pl.BlockSpec((1, d), lambda i: (0, 0)),
                  pl.BlockSpec((d, Z_COLS), lambda i: (0, 0))],
        out_specs=pl.BlockSpec((tm, Z_COLS), lambda i: (i, 0)),
        out_shape=jax.ShapeDtypeStruct((t, Z_COLS), F32),
        compiler_params=_cparams("parallel"),
        name="inproj",
    )(x2d, ln_g.reshape(1, d), w_z)


def _outproj_kernel(x_ref, os_ref, od_ref, om_ref, wo_ref, g_ref, rwh_ref, rwl_ref, rb_ref,
                    *refs, n_own):
    h_ref, hn_ref, lg_ref = refs[-3:]
    i = pl.program_id(0)

    @pl.when(i < n_own)
    def _():
        n_s = os_ref.shape[1]
        n_d = od_ref.shape[1]
        h = x_ref[...]
        h = h + jnp.dot(os_ref[...].astype(BF16), wo_ref[0:n_s, :], preferred_element_type=F32)
        h = h + jnp.dot(od_ref[...].astype(BF16), wo_ref[n_s:n_s + n_d, :], preferred_element_type=F32)
        h = h + jnp.dot(om_ref[...].astype(BF16), wo_ref[n_s + n_d:, :], preferred_element_type=F32)
        h_ref[...] = h
        hn = _rms_rows(h, g_ref[...])
        hn_ref[...] = hn
        hi, lo = _split2(hn)
        lg = (jnp.dot(hi, rwh_ref[...], preferred_element_type=F32)
              + jnp.dot(lo, rwh_ref[...], preferred_element_type=F32)
              + jnp.dot(hi, rwl_ref[...], preferred_element_type=F32))
        lg_ref[...] = lg + rb_ref[...]

    @pl.when(i >= n_own)
    def _():
        hn_ref[...] = jnp.zeros_like(hn_ref)
        lg_ref[...] = jnp.zeros_like(lg_ref)


def _outproj(x2d, o_s, o_d, o_m, w_o, ln_g, rw_hi, rw_lo, rb, t_all, row0, prev=None):
    t, d = x2d.shape
    tm = min(ROW_TILE, t)
    blk0 = row0 // tm
    n_own = t // tm
    n_steps = n_own if prev is not None else t_all // tm
    row = lambda i: (jnp.minimum(i, n_own - 1), 0)
    row_off = lambda i: (i + blk0, 0)
    const = lambda i: (0, 0)
    in_specs = [pl.BlockSpec((tm, d), row),
                pl.BlockSpec((tm, o_s.shape[1]), row),
                pl.BlockSpec((tm, o_d.shape[1]), row),
                pl.BlockSpec((tm, o_m.shape[1]), row),
                pl.BlockSpec((d, d), const),
                pl.BlockSpec((1, d), const),
                pl.BlockSpec((d, LANES), const),
                pl.BlockSpec((d, LANES), const),
                pl.BlockSpec((1, LANES), const)]
    args = [x2d, o_s, o_d, o_m, w_o, ln_g.reshape(1, d), rw_hi, rw_lo, rb]
    aliases = {}
    if prev is not None:
        in_specs += [pl.BlockSpec(memory_space=pl.ANY), pl.BlockSpec(memory_space=pl.ANY)]
        aliases = {len(args): 1, len(args) + 1: 2}
        args += list(prev)
    return pl.pallas_call(
        functools.partial(_outproj_kernel, n_own=n_own),
        grid=(n_steps,),
        in_specs=in_specs,
        out_specs=[pl.BlockSpec((tm, d), row),
                   pl.BlockSpec((tm, d), row_off),
                   pl.BlockSpec((tm, LANES), row_off)],
        out_shape=[jax.ShapeDtypeStruct((t, d), F32),
                   jax.ShapeDtypeStruct((t_all, d), F32),
                   jax.ShapeDtypeStruct((t_all, LANES), F32)],
        input_output_aliases=aliases,
        compiler_params=_cparams("arbitrary"),
        name="outproj_router",
    )(*args)


def _route_kernel(lg_ref, pos_ref, post_ref, g_ref, cnt_ref):
    tm = lg_ref.shape[0]
    lane = lax.broadcasted_iota(I32, (tm, LANES), 1).astype(F32)
    l = jnp.where(lane < N_EXPERTS, lg_ref[...], -jnp.inf)
    vals, idxs = [], []
    for _k in range(TOP_K):
        m = jnp.max(l, axis=-1, keepdims=True)
        idx = jnp.min(jnp.where(l == m, lane, float(LANES)), axis=-1, keepdims=True)
        l = jnp.where(lane == idx, -jnp.inf, l)
        vals.append(m)
        idxs.append(idx)
    ex = [jnp.exp(v - vals[0]) for v in vals]
    den = ex[0] + ex[1] + ex[2] + ex[3]
    member = jnp.zeros((tm, LANES), F32)
    for idx in idxs:
        member = member + jnp.where(lane == idx, 1.0, 0.0)
    ri = lax.broadcasted_iota(I32, (tm, tm), 0)
    ci = lax.broadcasted_iota(I32, (tm, tm), 1)
    strict = jnp.where(ci < ri, 1.0, 0.0).astype(BF16)
    prefix = jnp.dot(strict, member.astype(BF16), preferred_element_type=F32)
    cnt = jnp.sum(member, axis=0, keepdims=True)
    cpad = jnp.ceil(cnt * (1.0 / SUBLANES)) * float(SUBLANES)
    c_hi = jnp.floor(cpad * (1.0 / 256.0))
    c_lo = cpad - 256.0 * c_hi
    ej = lax.broadcasted_iota(I32, (LANES, LANES), 0)
    ee = lax.broadcasted_iota(I32, (LANES, LANES), 1)
    before = jnp.where(ej < ee, 1.0, 0.0).astype(BF16)
    bcast = lambda v: jnp.broadcast_to(v, (SUBLANES, LANES)).astype(BF16)
    off = (256.0 * jnp.dot(bcast(c_hi), before, preferred_element_type=F32)
           + jnp.dot(bcast(c_lo), before, preferred_element_type=F32))[0:1]
    where_in_run = prefix + off
    p_out = jnp.zeros((tm, LANES), F32)
    g_out = jnp.zeros((tm, LANES), F32)
    for k in range(TOP_K):
        pos = jnp.sum(jnp.where(lane == idxs[k], where_in_run, 0.0), axis=-1, keepdims=True)
        p_out = jnp.where(lane == float(k), pos, p_out)
        g_out = jnp.where(lane == float(k), ex[k] / den, g_out)
    pos_ref[...] = p_out[:, :TOP_K]
    post_ref[...] = p_out.T[:SUBLANES, :]
    g_ref[...] = g_out[:, :TOP_K]
    cnt_ref[0] = cnt


def _route(logits):
    t = logits.shape[0]
    tm = MOE_BLK
    nb = t // tm
    return pl.pallas_call(
        _route_kernel,
        grid=(nb,),
        in_specs=[pl.BlockSpec((tm, LANES), lambda i: (i, 0))],
        out_specs=[pl.BlockSpec((tm, TOP_K), lambda i: (i, 0)),
                   pl.BlockSpec((SUBLANES, tm), lambda i: (0, i)),
                   pl.BlockSpec((tm, TOP_K), lambda i: (i, 0)),
                   pl.BlockSpec((1, 1, LANES), lambda i: (i, 0, 0))],
        out_shape=[jax.ShapeDtypeStruct((t, TOP_K), F32),
                   jax.ShapeDtypeStruct((SUBLANES, t), F32),
                   jax.ShapeDtypeStruct((t, TOP_K), F32),
                   jax.ShapeDtypeStruct((nb, 1, LANES), F32)],
        compiler_params=_cparams("parallel"),
        name="route",
    )(logits)


def _pack_halves(x):
    c = x.shape[1] // 2
    lo = lax.bitcast_convert_type(x[:, :c].astype(BF16).astype(F32), U32)
    hi = lax.bitcast_convert_type(x[:, c:].astype(BF16).astype(F32), U32)
    return (lo >> 16) | (hi & jnp.uint32(0xFFFF0000))


def _unpack_halves(w):
    lo = lax.bitcast_convert_type(w << 16, F32).astype(BF16)
    hi = lax.bitcast_convert_type(w & jnp.uint32(0xFFFF0000), F32).astype(BF16)
    return lo, hi


def _run_copies(n, max_rows, src_ref, src0, dst_ref, dst0, sem, wait):
    pos = 0
    bit = max_rows
    while bit >= SUBLANES:
        take = (n & bit) != 0

        def go(pos=pos, bit=bit):
            cp = pltpu.make_async_copy(src_ref.at[pl.ds(pl.multiple_of(src0 + pos, SUBLANES), bit)],
                                       dst_ref.at[pl.ds(pl.multiple_of(dst0 + pos, SUBLANES), bit)], sem)
            cp.wait() if wait else cp.start()

        pl.when(take)(go)
        pos = pos + jnp.where(take, bit, 0)
        bit //= 2


def _dispatch_kernel(dst_ref, len_ref, off_ref, estart_ref, elen_ref, nused_ref,
                     hn_ref, post_ref, xs_hbm, buf_ref, zero_ref, sem, zsem):
    b = pl.program_id(0)
    tm = hn_ref.shape[0]
    x = hn_ref[...].astype(BF16)
    post = post_ref[...]
    for c in range(BLK_ROWS // MOE_TILE):
        r = (lax.broadcasted_iota(I32, (MOE_TILE, tm), 0) + c * MOE_TILE).astype(F32)
        sel = jnp.zeros((MOE_TILE, tm), F32)
        for k in range(TOP_K):
            sel = sel + jnp.where(r == post[k:k + 1, :], 1.0, 0.0)
        rows = jnp.dot(sel.astype(BF16), x, preferred_element_type=F32)
        buf_ref[c * MOE_TILE:(c + 1) * MOE_TILE, :] = _pack_halves(rows)

    def runs(wait):
        def body(e, c):
            j = b * N_EXPERTS + e
            _run_copies(len_ref[j], MOE_BLK, buf_ref, off_ref[j], xs_hbm, dst_ref[j], sem, wait)
            return c
        lax.fori_loop(0, N_EXPERTS, body, 0)

    runs(False)

    @pl.when(b == 0)
    def _():
        zero_ref[...] = jnp.zeros_like(zero_ref)

        def tail(wait):
            def body(e, c):
                n = (MOE_TILE - elen_ref[e] % MOE_TILE) % MOE_TILE
                _run_copies(n, MOE_TILE // 2, zero_ref, 0, xs_hbm, estart_ref[e] + elen_ref[e], zsem, wait)
                return c
            lax.fori_loop(0, N_EXPERTS, body, 0)

            def free_tile(ti, c):
                for half in range(2):
                    _run_copies(MOE_TILE // 2, MOE_TILE // 2, zero_ref, 0, xs_hbm,
                                ti * MOE_TILE + half * (MOE_TILE // 2), zsem, wait)
                return c
            lax.fori_loop(nused_ref[0], xs_hbm.shape[0] // MOE_TILE, free_tile, 0)

        tail(False)
        tail(True)

    runs(True)


def _dispatch(hn, post, seg_dst, seg_len, seg_off, e_start, e_len, n_used, n_rows):
    t, d = hn.shape
    grid_spec = pltpu.PrefetchScalarGridSpec(
        num_scalar_prefetch=6,
        grid=(t // MOE_BLK,),
        in_specs=[pl.BlockSpec((MOE_BLK, d), lambda i, *_: (i, 0)),
                  pl.BlockSpec((SUBLANES, MOE_BLK), lambda i, *_: (0, i))],
        out_specs=pl.BlockSpec(memory_space=pl.ANY),
        scratch_shapes=[pltpu.VMEM((BLK_ROWS, d // 2), U32),
                        pltpu.VMEM((MOE_TILE, d // 2), U32),
                        pltpu.SemaphoreType.DMA(()),
                        pltpu.SemaphoreType.DMA(())],
    )
    return pl.pallas_call(
        _dispatch_kernel,
        grid_spec=grid_spec,
        out_shape=jax.ShapeDtypeStruct((n_rows, d // 2), U32),
        compiler_params=_cparams("arbitrary"),
        name="dispatch",
    )(seg_dst, seg_len, seg_off, e_start, e_len, n_used, hn, post)


def _expert_kernel(te_ref, nu_ref, x_ref, w1_ref, b1_ref, w2_ref, b2_ref, y_ref, w1b_ref, w2b_ref):
    i = pl.program_id(0)
    live = i < nu_ref[0]

    @pl.when(live & ((i == 0) | (te_ref[i] != te_ref[jnp.maximum(i - 1, 0)])))
    def _():
        w1b_ref[...] = w1_ref[0].astype(BF16)
        w2b_ref[...] = w2_ref[0].astype(BF16)

    @pl.when(live)
    def _():
        f = w2_ref.shape[1]
        half = x_ref.shape[1]
        x_lo, x_hi = _unpack_halves(x_ref[...])
        h = (jnp.dot(x_lo, w1b_ref[:half, :], preferred_element_type=F32)
             + jnp.dot(x_hi, w1b_ref[half:, :], preferred_element_type=F32) + b1_ref[0])
        glu = jnp.minimum(h[:, :f], SWIGLU_LIMIT)
        lin = jnp.clip(h[:, f:], -SWIGLU_LIMIT, SWIGLU_LIMIT)
        act = glu * jax.nn.sigmoid(SWIGLU_ALPHA * glu) * (lin + 1.0)
        y = jnp.dot(act.astype(BF16), w2b_ref[...], preferred_element_type=F32) + b2_ref[0]
        y_ref[...] = _pack_halves(y)

    @pl.when(i >= nu_ref[0])
    def _():
        y_ref[...] = jnp.zeros_like(y_ref)


def _experts(xs, tile_expert, n_used, w1, b1, w2, b2):
    n_rows, half = xs.shape
    d = 2 * half
    f2 = w1.shape[2]
    f = w2.shape[1]
    n_tiles = n_rows // MOE_TILE
    live = lambda i, te, nu: (jnp.minimum(i, nu[0] - 1), 0)
    every = lambda i, te, nu: (i, 0)
    wsel = lambda i, te, nu: (te[i], 0, 0)
    grid_spec = pltpu.PrefetchScalarGridSpec(
        num_scalar_prefetch=2,
        grid=(n_tiles,),
        in_specs=[pl.BlockSpec((MOE_TILE, half), live),
                  pl.BlockSpec((1, d, f2), wsel),
                  pl.BlockSpec((1, 1, f2), wsel),
                  pl.BlockSpec((1, f, d), wsel),
                  pl.BlockSpec((1, 1, d), wsel)],
        out_specs=pl.BlockSpec((MOE_TILE, half), every),
        scratch_shapes=[pltpu.VMEM((d, f2), BF16), pltpu.VMEM((f, d), BF16)],
    )
    return pl.pallas_call(
        _expert_kernel,
        grid_spec=grid_spec,
        out_shape=jax.ShapeDtypeStruct((n_rows, half), U32),
        compiler_params=_cparams("arbitrary"),
        name="experts",
    )(tile_expert, n_used, xs, w1, b1, w2, b2)


def _combine_kernel(dst_ref, len_ref, off_ref, h_ref, pos_ref, g_ref, yb_hbm, y_ref, buf_ref, sem, *, blk0):
    i = pl.program_id(0)
    b = i + blk0
    tm, d = h_ref.shape
    half = d // 2

    @pl.when(i == 0)
    def _():
        buf_ref[...] = jnp.zeros_like(buf_ref)

    def runs(wait):
        def body(e, c):
            j = b * N_EXPERTS + e
            _run_copies(len_ref[j], MOE_BLK, yb_hbm, dst_ref[j], buf_ref, off_ref[j], sem, wait)
            return c
        lax.fori_loop(0, N_EXPERTS, body, 0)

    runs(False)
    pos = pos_ref[...]
    g = g_ref[...]
    runs(True)
    y_lo = h_ref[:, :half]
    y_hi = h_ref[:, half:]
    for c in range(BLK_ROWS // MOE_TILE):
        col = (lax.broadcasted_iota(I32, (tm, MOE_TILE), 1) + c * MOE_TILE).astype(F32)
        wgt = jnp.zeros((tm, MOE_TILE), F32)
        for k in range(TOP_K):
            wgt = wgt + jnp.where(col == pos[:, k:k + 1], g[:, k:k + 1], 0.0)
        e_lo, e_hi = _unpack_halves(buf_ref[c * MOE_TILE:(c + 1) * MOE_TILE, :])
        wgt = wgt.astype(BF16)
        y_lo = y_lo + jnp.dot(wgt, e_lo, preferred_element_type=F32)
        y_hi = y_hi + jnp.dot(wgt, e_hi, preferred_element_type=F32)
    y_ref[:, :half] = y_lo
    y_ref[:, half:] = y_hi


def _combine(h, pos, gates, seg_dst, seg_len, seg_off, yb, blk0):
    t, d = h.shape
    grid_spec = pltpu.PrefetchScalarGridSpec(
        num_scalar_prefetch=3,
        grid=(t // MOE_BLK,),
        in_specs=[pl.BlockSpec((MOE_BLK, d), lambda i, *_: (i, 0)),
                  pl.BlockSpec((MOE_BLK, TOP_K), lambda i, *_: (i + blk0, 0)),
                  pl.BlockSpec((MOE_BLK, TOP_K), lambda i, *_: (i + blk0, 0)),
                  pl.BlockSpec(memory_space=pl.ANY)],
        out_specs=pl.BlockSpec((MOE_BLK, d), lambda i, *_: (i, 0)),
        scratch_shapes=[pltpu.VMEM((BLK_ROWS, d // 2), U32),
                        pltpu.SemaphoreType.DMA(())],
    )
    return pl.pallas_call(
        functools.partial(_combine_kernel, blk0=blk0),
        grid_spec=grid_spec,
        out_shape=jax.ShapeDtypeStruct((t, d), F32),
        compiler_params=_cparams("arbitrary"),
        name="combine",
    )(seg_dst, seg_len, seg_off, h, pos, gates, yb)


def _moe(hn_all, logits_all, h_parts, w1, b1, w2, b2):
    t_all = hn_all.shape[0]
    nb = t_all // MOE_BLK
    pos, post, gates, counts_f = _route(logits_all)
    cnt = counts_f.reshape(nb, LANES)[:, :N_EXPERTS].astype(I32)
    seg_len = (cnt + SUBLANES - 1) // SUBLANES * SUBLANES
    seg_off = jnp.cumsum(seg_len, axis=1) - seg_len
    e_len = jnp.sum(seg_len, axis=0)
    e_tiles = (e_len + MOE_TILE - 1) // MOE_TILE
    tile_end = jnp.cumsum(e_tiles)
    e_start = (tile_end - e_tiles) * MOE_TILE
    seg_dst = e_start[None, :] + jnp.cumsum(seg_len, axis=0) - seg_len
    max_rows = t_all * TOP_K + nb * N_EXPERTS * (SUBLANES - 1) + N_EXPERTS * (MOE_TILE - SUBLANES)
    n_tiles = -(-max_rows // MOE_TILE)
    n_rows = n_tiles * MOE_TILE
    n_used = tile_end[-1:].astype(I32)
    tile_expert = jnp.minimum(jnp.searchsorted(tile_end, jnp.arange(n_tiles, dtype=I32), side="right"),
                              N_EXPERTS - 1).astype(I32)
    flat = lambda a: a.reshape(-1).astype(I32)
    xs = _dispatch(hn_all, post, flat(seg_dst), flat(seg_len), flat(seg_off), flat(e_start), flat(e_len),
                   n_used, n_rows)
    yb = _experts(xs, tile_expert, n_used, w1, b1, w2, b2)
    outs = []
    row = 0
    for h in h_parts:
        outs.append(_combine(h, pos, gates, flat(seg_dst), flat(seg_len), flat(seg_off), yb, row // MOE_BLK))
        row += h.shape[0]
    return outs


GDN_ROWS = 4 * GDN_CHUNK
CONV_HALO = SUBLANES
NEUMANN_SPLIT = 2


def _softplus(x):
    return jnp.maximum(x, 0.0) + jnp.log1p(jnp.exp(-jnp.abs(x)))


def _gdn_prompt_kernel(u_ref, ab_ref, gate_ref, cw_ref, alog_ref, dtb_ref, ng_ref, o_ref, s_ref, ubuf_ref):
    step = pl.program_id(0)
    NB = u_ref.shape[0]
    R = GDN_ROWS
    C = GDN_CHUNK
    NC = R // C

    @pl.when(step == 0)
    def _():
        ubuf_ref[:, 0:CONV_HALO, :] = jnp.zeros((NB, CONV_HALO, ubuf_ref.shape[2]), F32)
        s_ref[...] = jnp.zeros_like(s_ref)

    ri = lax.broadcasted_iota(I32, (R, R), 0)
    ci = lax.broadcasted_iota(I32, (R, R), 1)
    shift = C.bit_length() - 1
    same = lax.shift_right_logical(ri, shift) == lax.shift_right_logical(ci, shift)
    incl = same & (ci <= ri)
    strict = same & (ci < ri)
    tri = jnp.where(incl, 1.0, 0.0).astype(BF16)
    blk = jnp.where(same, 1.0, 0.0).astype(BF16)
    cw = cw_ref[...]
    ng = ng_ref[...]

    chains = []
    for b in range(NB):
        u = u_ref[b]
        ubuf_ref[b, CONV_HALO:CONV_HALO + R, :] = u
        y = u * cw[GDN_CONV - 1:GDN_CONV, :]
        for j in range(1, GDN_CONV):
            y = y + ubuf_ref[b, CONV_HALO - j:CONV_HALO - j + R, :] * cw[GDN_CONV - 1 - j:GDN_CONV - j, :]
        ubuf_ref[b, 0:CONV_HALO, :] = u[R - CONV_HALO:, :]
        qkv = y * jax.nn.sigmoid(y)
        ab = ab_ref[b]
        g_t = -jnp.exp(alog_ref[...]) * _softplus(ab + dtb_ref[...])
        beta_t = jax.nn.sigmoid(ab)
        parts = _split3(g_t)
        gcum = sum(jnp.dot(tri, p, preferred_element_type=F32) for p in parts)
        gtot = sum(jnp.dot(blk, p, preferred_element_type=F32) for p in parts)
        gcum_t = gcum.T
        qk_n = []
        for t in range(2 * GDN_QK_COLS // LANES):
            x = qkv[:, t * LANES:(t + 1) * LANES]
            x = x * lax.rsqrt(_pair_sumsq(x) + EPS)
            qk_n.append(x * (GDN_DK ** -0.5) if t < GDN_QK_COLS // LANES else x)
        qk_n = jnp.concatenate(qk_n, axis=-1)
        for h in range(N_GDN_HEADS):
            q = qk_n[:, h * GDN_DK:(h + 1) * GDN_DK]
            k = qk_n[:, GDN_QK_COLS + h * GDN_DK:GDN_QK_COLS + (h + 1) * GDN_DK]
            v = qkv[:, 2 * GDN_QK_COLS + h * GDN_DV:2 * GDN_QK_COLS + (h + 1) * GDN_DV]
            gc = gcum[:, h:h + 1]
            gt = gtot[:, h:h + 1]
            beta = beta_t[:, N_GDN_HEADS + h:N_GDN_HEADS + h + 1]
            decay = jnp.exp(jnp.where(incl, gc - gcum_t[h:h + 1, :], -jnp.inf))
            chains.append(dict(
                b=b, h=h,
                a=jnp.where(strict, beta * _bdot_nt(k, k) * decay, 0.0),
                qk=_bdot_nt(q, k) * decay,
                r=jnp.concatenate([v * beta, k * (beta * jnp.exp(gc))], axis=-1),
                q_dec=q * jnp.exp(gc), k_dec=k * jnp.exp(gt - gc), g_last=jnp.exp(gt)))

    dot = lambda x, y: jnp.dot(x, y, preferred_element_type=F32)
    level, j = 1, 0
    while level < C:
        last = 2 * level >= C
        for ch in chains:
            if j < NEUMANN_SPLIT:
                a_hi, a_lo = _split2(ch['a'])
                r_hi, r_lo = _split2(ch['r'])
                upd = dot(a_hi, r_hi) + dot(a_hi, r_lo) + dot(a_lo, r_hi)
            else:
                a_hi = ch['a'].astype(BF16)
                upd = dot(a_hi, ch['r'].astype(BF16))
            ch['r'] = ch['r'] - upd if level == 1 else ch['r'] + upd
            if not last:
                sq = dot(a_hi, a_hi)
                if j + 1 < NEUMANN_SPLIT:
                    sq = sq + dot(a_hi, a_lo) + dot(a_lo, a_hi)
                ch['a'] = sq
        level *= 2
        j += 1

    for ch in chains:
        ch['u'], ch['w'] = ch['r'][:, :GDN_DV], ch['r'][:, GDN_DV:]
        ch['S'] = s_ref[ch['b'], ch['h']]
        ch['k_dec_t'] = ch['k_dec'].T
        ch['outs'] = []
    for c in range(NC):
        sl = slice(c * C, (c + 1) * C)
        for ch in chains:
            S = ch['S']
            v_new = ch['u'][sl] - _bdot(ch['w'][sl], S)
            ch['outs'].append(_bdot(ch['q_dec'][sl], S) + _bdot(ch['qk'][sl, sl], v_new))
            ch['S'] = S * ch['g_last'][c * C:c * C + 1, :] + _bdot(ch['k_dec_t'][:, sl], v_new)
    for ch in chains:
        b, h = ch['b'], ch['h']
        s_ref[b, h] = ch['S']
        o = jnp.concatenate(ch['outs'], axis=0)
        o = o * lax.rsqrt(jnp.mean(o * o, axis=-1, keepdims=True) + EPS) * ng
        gh = gate_ref[b, :, h * GDN_DV:(h + 1) * GDN_DV]
        o_ref[b, :, h * GDN_DV:(h + 1) * GDN_DV] = o * (gh * jax.nn.sigmoid(gh))


def _gdn_prompt(z3, conv_w, a_log, dt_bias, norm_g):
    B, S, _ = z3.shape
    R = GDN_ROWS
    lanes4 = lambda a: jnp.pad(a, (0, LANES - a.shape[0])).reshape(1, LANES)
    return pl.pallas_call(
        _gdn_prompt_kernel,
        grid=(S // R,),
        in_specs=[pl.BlockSpec((B, R, GDN_CONV_CH), lambda s: (0, s, Z_GDN // GDN_CONV_CH)),
                  pl.BlockSpec((B, R, LANES), lambda s: (0, s, Z_AB // LANES)),
                  pl.BlockSpec((B, R, GDN_V_COLS), lambda s: (0, s, Z_GATE // GDN_V_COLS)),
                  pl.BlockSpec((GDN_CONV, GDN_CONV_CH), lambda s: (0, 0)),
                  pl.BlockSpec((1, LANES), lambda s: (0, 0)),
                  pl.BlockSpec((1, LANES), lambda s: (0, 0)),
                  pl.BlockSpec((1, GDN_DV), lambda s: (0, 0))],
        out_specs=[pl.BlockSpec((B, R, GDN_V_COLS), lambda s: (0, s, 0)),
                   pl.BlockSpec((B, N_GDN_HEADS, GDN_DK, GDN_DV), lambda s: (0, 0, 0, 0))],
        out_shape=[jax.ShapeDtypeStruct((B, S, GDN_V_COLS), F32),
                   jax.ShapeDtypeStruct((B, N_GDN_HEADS, GDN_DK, GDN_DV), F32)],
        scratch_shapes=[pltpu.VMEM((B, CONV_HALO + R, GDN_CONV_CH), F32)],
        compiler_params=_cparams("arbitrary"),
        name="gdn_prompt",
    )(z3, z3, z3, conv_w, lanes4(a_log), lanes4(dt_bias), norm_g.reshape(1, GDN_DV))


def _pair_sumsq(x):
    li = lax.broadcasted_iota(I32, (LANES, LANES), 0) // HEAD_DIM
    lj = lax.broadcasted_iota(I32, (LANES, LANES), 1) // HEAD_DIM
    same = jnp.where(li == lj, 1.0, 0.0).astype(BF16)
    hi, lo = _split2(x * x)
    return jnp.dot(hi, same, preferred_element_type=F32) + jnp.dot(lo, same, preferred_element_type=F32)


def _pair_rms(x, g):
    return x * lax.rsqrt(_pair_sumsq(x) * (1.0 / HEAD_DIM) + EPS) * g


def _first_half(shape):
    return lax.broadcasted_iota(I32, shape, 1) < HEAD_DIM


def _swa_prompt_kernel(sink_ref, q_ref, kc_ref, kp_ref, vc_ref, vp_ref, qg_ref, kg_ref, o_ref, kn_ref):
    n = pl.program_id(1)
    W = WINDOW
    kg = kg_ref[...]
    qg = qg_ref[...]
    kc = _pair_rms(kc_ref[0], kg)
    kn_ref[0] = kc
    k2 = jnp.concatenate([_pair_rms(kp_ref[0], kg), kc], axis=0)
    v2 = jnp.concatenate([vp_ref[0], vc_ref[0]], axis=0)
    fh = _first_half(k2.shape)
    k2r = pltpu.roll(k2, HEAD_DIM, 1)
    v2r = pltpu.roll(v2, HEAD_DIM, 1)
    kdup = (jnp.where(fh, k2, k2r), jnp.where(fh, k2r, k2))
    vdup = (jnp.where(fh, v2, v2r), jnp.where(fh, v2r, v2))
    qi = lax.broadcasted_iota(I32, (W, 2 * W), 0)
    kj = lax.broadcasted_iota(I32, (W, 2 * W), 1)
    dist = qi + W - kj
    valid = (dist >= 0) & (dist < W) & ((n > 0) | (kj >= W))
    bias = jnp.where(valid, 0.0, -jnp.inf)
    distf = dist.astype(F32)
    fq = _first_half((W, LANES))
    for t in range(SWA_Q_COLS // LANES):
        kv = t // (SWA_GROUP // 2)
        qt = _pair_rms(q_ref[0, :, t * LANES:(t + 1) * LANES], qg) * ATTN_SCALE
        halves = []
        for half in range(2):
            head = 2 * t + half
            slope = 2.0 ** (-(8.0 / N_SWA_HEADS) * (head + 1))
            qm = jnp.where(fq == (half == 0), qt, 0.0)
            s = _bdot_nt(qm, kdup[kv]) - slope * distf + bias
            sink = sink_ref[head]
            m = jnp.maximum(jnp.max(s, axis=-1, keepdims=True), sink)
            p = jnp.exp(s - m)
            den = jnp.sum(p, axis=-1, keepdims=True) + jnp.exp(sink - m)
            halves.append(_bdot(p, vdup[kv]) / den)
        o_ref[0, :, t * LANES:(t + 1) * LANES] = jnp.where(fq, halves[0], halves[1])


def _swa_prompt(z3, q_g, k_g, sinks):
    B, S, _ = z3.shape
    W = WINDOW
    twice = lambda g: jnp.concatenate([g, g]).reshape(1, LANES)
    kcol, vcol = Z_K // LANES, Z_V // LANES
    grid_spec = pltpu.PrefetchScalarGridSpec(
        num_scalar_prefetch=0,
        grid=(B, S // W),
        in_specs=[pl.BlockSpec(memory_space=pltpu.SMEM),
                  pl.BlockSpec((1, W, SWA_Q_COLS), lambda b, n: (b, n, 0)),
                  pl.BlockSpec((1, W, LANES), lambda b, n: (b, n, kcol)),
                  pl.BlockSpec((1, W, LANES), lambda b, n: (b, jnp.maximum(n - 1, 0), kcol)),
                  pl.BlockSpec((1, W, LANES), lambda b, n: (b, n, vcol)),
                  pl.BlockSpec((1, W, LANES), lambda b, n: (b, jnp.maximum(n - 1, 0), vcol)),
                  pl.BlockSpec((1, LANES), lambda b, n: (0, 0)),
                  pl.BlockSpec((1, LANES), lambda b, n: (0, 0))],
        out_specs=[pl.BlockSpec((1, W, SWA_Q_COLS), lambda b, n: (b, n, 0)),
                   pl.BlockSpec((1, W, LANES), lambda b, n: (b, 0, 0))],
    )
    return pl.pallas_call(
        _swa_prompt_kernel,
        grid_spec=grid_spec,
        out_shape=[jax.ShapeDtypeStruct((B, S, SWA_Q_COLS), F32),
                   jax.ShapeDtypeStruct((B, W, LANES), F32)],
        compiler_params=_cparams("arbitrary", "arbitrary"),
        name="swa_prompt",
    )(sinks, z3, z3, z3, z3, z3, twice(q_g), twice(k_g))


def _mem_kv_kernel(m_ref, g_ref, w_ref, kg_ref, k_ref, v_ref):
    n = _rms_rows(m_ref[...], g_ref[...])
    kv = jnp.dot(n.astype(BF16), w_ref[...], preferred_element_type=F32)
    kg = kg_ref[...]
    for t in range(MEM_Q_COLS // LANES):
        k_ref[:, t * LANES:(t + 1) * LANES] = _pair_rms(kv[:, t * LANES:(t + 1) * LANES], kg)
    v_ref[...] = kv[:, MEM_Q_COLS:]


def _mem_kv(mem2d, ln_g, w_kv, k_g):
    r, d = mem2d.shape
    twice = jnp.concatenate([k_g, k_g]).reshape(1, LANES)
    full = lambda shape: pl.BlockSpec(shape, lambda i: (0,) * len(shape))
    return pl.pallas_call(
        _mem_kv_kernel,
        grid=(1,),
        in_specs=[full((r, d)), full((1, d)), full((d, 2 * MEM_Q_COLS)), full((1, LANES))],
        out_specs=[full((r, MEM_Q_COLS)), full((r, MEM_Q_COLS))],
        out_shape=[jax.ShapeDtypeStruct((r, MEM_Q_COLS), F32), jax.ShapeDtypeStruct((r, MEM_Q_COLS), F32)],
        compiler_params=_cparams("arbitrary"),
        name="mem_kv",
    )(mem2d, ln_g.reshape(1, d), w_kv.astype(BF16), twice)


def _mem_attn_kernel(q_ref, k_ref, v_ref, qg_ref, o_ref):
    qg = qg_ref[...]
    rows = q_ref.shape[1]
    fq = _first_half((rows, LANES))
    for t in range(MEM_Q_COLS // LANES):
        cols = slice(t * LANES, (t + 1) * LANES)
        qt = _pair_rms(q_ref[0, :, cols], qg) * ATTN_SCALE
        kt = k_ref[0, :, cols]
        vt = v_ref[0, :, cols]
        halves = []
        for half in range(2):
            qm = jnp.where(fq == (half == 0), qt, 0.0)
            s = _bdot_nt(qm, kt)
            p = jnp.exp(s - jnp.max(s, axis=-1, keepdims=True))
            halves.append(_bdot(p, vt) / jnp.sum(p, axis=-1, keepdims=True))
        o_ref[0, :, cols] = jnp.where(fq, halves[0], halves[1])


MEM_Q_TILE = 256


def _mem_attn_prompt(z3, mem_k, mem_v, q_g):
    B, S, _ = z3.shape
    M = mem_k.shape[1]
    tq = MEM_Q_TILE
    twice = jnp.concatenate([q_g, q_g]).reshape(1, LANES)
    return pl.pallas_call(
        _mem_attn_kernel,
        grid=(B, S // tq),
        in_specs=[pl.BlockSpec((1, tq, MEM_Q_COLS), lambda b, i: (b, i, Z_QM // MEM_Q_COLS)),
                  pl.BlockSpec((1, M, MEM_Q_COLS), lambda b, i: (b, 0, 0)),
                  pl.BlockSpec((1, M, MEM_Q_COLS), lambda b, i: (b, 0, 0)),
                  pl.BlockSpec((1, LANES), lambda b, i: (0, 0))],
        out_specs=pl.BlockSpec((1, tq, MEM_Q_COLS), lambda b, i: (b, i, 0)),
        out_shape=jax.ShapeDtypeStruct((B, S, MEM_Q_COLS), F32),
        compiler_params=_cparams("parallel", "parallel"),
        name="mem_attn_prompt",
    )(z3, mem_k, mem_v, twice)


PAIR = 2


def _swa_sample_kernel(sink_ref, q_ref, k_ref, v_ref, ck_ref, cv_ref, qg_ref, kg_ref, o_ref, kn_ref, *, L):
    n_seq = ck_ref.shape[0]
    Wb = ck_ref.shape[1]
    rows8 = SUBLANES
    nh = N_SWA_HEADS
    kn = _pair_rms(k_ref[...], kg_ref[...])
    kn_ref[...] = kn
    qg = qg_ref[...]
    R = nh * rows8
    row = lax.broadcasted_iota(I32, (R, 1), 0)
    head = row // rows8
    seq_in_pair = (row % rows8) // L
    step = (row % L).astype(F32)
    slope = jnp.exp2(-(8.0 / N_SWA_HEADS) * (head.astype(F32) + 1.0))
    sink = jnp.zeros((R, 1), F32)
    for h in range(nh):
        sink = jnp.where(head == h, sink_ref[h], sink)
    key = lax.broadcasted_iota(I32, (R, Wb), 1).astype(F32)
    dist_c = float(Wb) + step - key
    bias_c = jnp.where(dist_c < float(WINDOW), 0.0, -jnp.inf)
    col = lax.broadcasted_iota(I32, (R, rows8), 1)
    dist_n = step - (col % L).astype(F32)
    bias_n = jnp.where((dist_n >= 0.0) & ((col // L) == seq_in_pair), 0.0, -jnp.inf)
    fh8 = _first_half((rows8, LANES))
    fhR = _first_half((R, LANES))
    kv_first = head < SWA_GROUP
    for pr in range(n_seq // PAIR):
        r0 = pr * rows8
        pieces = []
        for t in range(SWA_Q_COLS // LANES):
            qt = _pair_rms(q_ref[r0:r0 + rows8, t * LANES:(t + 1) * LANES], qg) * ATTN_SCALE
            qr = pltpu.roll(qt, HEAD_DIM, 1)
            kv = t // (SWA_GROUP // 2)
            for half in range(2):
                src = qt if half == kv else qr
                pieces.append(jnp.where(fh8 == (kv == 0), src, 0.0))
        qs = jnp.concatenate(pieces, axis=0)
        k_new = kn[r0:r0 + rows8]
        v_new = v_ref[r0:r0 + rows8, :]
        s_c = [_bdot_nt(qs, ck_ref[pr * PAIR + j]) for j in range(PAIR)]
        s_c = jnp.where(seq_in_pair == 0, s_c[0], s_c[1]) - slope * dist_c + bias_c
        s_n = _bdot_nt(qs, k_new) - slope * dist_n + bias_n
        m = jnp.maximum(jnp.maximum(jnp.max(s_c, axis=-1, keepdims=True),
                                    jnp.max(s_n, axis=-1, keepdims=True)), sink)
        p_c = jnp.exp(s_c - m)
        p_n = jnp.exp(s_n - m)
        den = jnp.sum(p_c, axis=-1, keepdims=True) + jnp.sum(p_n, axis=-1, keepdims=True) + jnp.exp(sink - m)
        o = _bdot(p_n, v_new)
        for j in range(PAIR):
            o = o + _bdot(jnp.where(seq_in_pair == j, p_c, 0.0), cv_ref[pr * PAIR + j])
        o = o / den
        o = jnp.where(fhR == kv_first, o, 0.0)
        o_r = pltpu.roll(o, HEAD_DIM, 1)
        for t in range(SWA_Q_COLS // LANES):
            kv = t // (SWA_GROUP // 2)
            halves = []
            for half in range(2):
                h = 2 * t + half
                src = o if half == kv else o_r
                halves.append(src[h * rows8:(h + 1) * rows8])
            o_ref[r0:r0 + rows8, t * LANES:(t + 1) * LANES] = jnp.where(fh8, halves[0], halves[1])


SAMPLE_SEQS = 8


def _swa_sample(z_s, cache_k, cache_v, q_g, k_g, sinks, L):
    t = z_s.shape[0]
    DB, Wb, _ = cache_k.shape
    ns = SAMPLE_SEQS
    rows = ns * L
    twice = lambda g: jnp.concatenate([g, g]).reshape(1, LANES)
    return pl.pallas_call(
        functools.partial(_swa_sample_kernel, L=L),
        grid=(DB // ns,),
        in_specs=[pl.BlockSpec(memory_space=pltpu.SMEM),
                  pl.BlockSpec((rows, SWA_Q_COLS), lambda i: (i, 0)),
                  pl.BlockSpec((rows, LANES), lambda i: (i, Z_K // LANES)),
                  pl.BlockSpec((rows, LANES), lambda i: (i, Z_V // LANES)),
                  pl.BlockSpec((ns, Wb, LANES), lambda i: (i, 0, 0)),
                  pl.BlockSpec((ns, Wb, LANES), lambda i: (i, 0, 0)),
                  pl.BlockSpec((1, LANES), lambda i: (0, 0)),
                  pl.BlockSpec((1, LANES), lambda i: (0, 0))],
        out_specs=[pl.BlockSpec((rows, SWA_Q_COLS), lambda i: (i, 0)),
                   pl.BlockSpec((rows, LANES), lambda i: (i, 0))],
        out_shape=[jax.ShapeDtypeStruct((t, SWA_Q_COLS), F32),
                   jax.ShapeDtypeStruct((t, LANES), F32)],
        compiler_params=_cparams("parallel"),
        name="swa_sample",
    )(sinks, z_s, z_s, z_s, cache_k, cache_v, twice(q_g), twice(k_g))


def _mem_sample_kernel(q_ref, mk_ref, mv_ref, qg_ref, o_ref, *, L):
    n_seq = mk_ref.shape[0]
    rows8 = SUBLANES
    qg = qg_ref[...]
    R = 2 * rows8
    row = lax.broadcasted_iota(I32, (R, 1), 0)
    seq_in_pair = (row % rows8) // L
    fh8 = _first_half((rows8, LANES))
    for pr in range(n_seq // PAIR):
        r0 = pr * rows8
        for t in range(MEM_Q_COLS // LANES):
            cols = slice(t * LANES, (t + 1) * LANES)
            qt = _pair_rms(q_ref[r0:r0 + rows8, cols], qg) * ATTN_SCALE
            qs = jnp.concatenate([jnp.where(fh8, qt, 0.0), jnp.where(fh8, 0.0, qt)], axis=0)
            s = [_bdot_nt(qs, mk_ref[pr * PAIR + j, :, cols]) for j in range(PAIR)]
            s = jnp.where(seq_in_pair == 0, s[0], s[1])
            p = jnp.exp(s - jnp.max(s, axis=-1, keepdims=True))
            den = jnp.sum(p, axis=-1, keepdims=True)
            o = jnp.zeros((R, LANES), F32)
            for j in range(PAIR):
                o = o + _bdot(jnp.where(seq_in_pair == j, p, 0.0), mv_ref[pr * PAIR + j, :, cols])
            o = o / den
            o_ref[r0:r0 + rows8, cols] = jnp.where(fh8, o[:rows8], o[rows8:])


def _mem_attn_sample(z_s, mem_k, mem_v, q_g, L):
    t = z_s.shape[0]
    DB, M, _ = mem_k.shape
    ns = SAMPLE_SEQS
    rows = ns * L
    twice = jnp.concatenate([q_g, q_g]).reshape(1, LANES)
    return pl.pallas_call(
        functools.partial(_mem_sample_kernel, L=L),
        grid=(DB // ns,),
        in_specs=[pl.BlockSpec((rows, MEM_Q_COLS), lambda i: (i, Z_QM // MEM_Q_COLS)),
                  pl.BlockSpec((ns, M, MEM_Q_COLS), lambda i: (i, 0, 0)),
                  pl.BlockSpec((ns, M, MEM_Q_COLS), lambda i: (i, 0, 0)),
                  pl.BlockSpec((1, LANES), lambda i: (0, 0))],
        out_specs=pl.BlockSpec((rows, MEM_Q_COLS), lambda i: (i, 0)),
        out_shape=jax.ShapeDtypeStruct((t, MEM_Q_COLS), F32),
        compiler_params=_cparams("parallel"),
        name="mem_attn_sample",
    )(z_s, mem_k, mem_v, twice)


def _gdn_sample_kernel(uq_ref, uk_ref, uv_ref, bq_ref, bk_ref, bv_ref, wq_ref, wk_ref, wv_ref,
                       ab_ref, gate_ref, alog_ref, dtb_ref, ng_ref, s_in_ref, o_ref, s_ref, kq_ref):
    h = pl.program_id(0)
    L = uq_ref.shape[0]
    nbuf = bq_ref.shape[0]
    DK = GDN_DK

    def conv(u_ref, b_ref, w_ref, t):
        up = [b_ref[i] for i in range(nbuf)] + [u_ref[i] for i in range(L)]
        y = up[t] * w_ref[0]
        for i in range(1, GDN_CONV):
            y = y + up[t + i] * w_ref[i]
        return y * jax.nn.sigmoid(y)

    s_ref[...] = s_in_ref[...]
    ng = ng_ref[...]
    hsel = lax.broadcasted_iota(I32, (SUBLANES, 1), 0)
    pick = lambda m, r: jnp.sum(jnp.where(hsel == r, m, 0.0), axis=0, keepdims=True)
    alog = pick(alog_ref[...], h)
    dtb = pick(dtb_ref[...], h)
    for t in range(L):
        q = conv(uq_ref, bq_ref, wq_ref, t)
        k = conv(uk_ref, bk_ref, wk_ref, t)
        v = conv(uv_ref, bv_ref, wv_ref, t)
        q = q * lax.rsqrt(jnp.sum(q * q, axis=0, keepdims=True) + EPS) * (GDN_DK ** -0.5)
        k = k * lax.rsqrt(jnp.sum(k * k, axis=0, keepdims=True) + EPS)
        ab = ab_ref[t]
        a = pick(ab, h)
        bb = pick(ab, h + N_GDN_HEADS)
        decay = jnp.exp(-jnp.exp(alog) * _softplus(a + dtb))
        beta = jax.nn.sigmoid(bb)
        kq_ref[0] = k
        kq_ref[1] = q

        def decay_and_project(dk, acc):
            s = s_ref[0, dk] * decay
            s_ref[0, dk] = s
            return acc + s * kq_ref[0, pl.ds(dk, 1), :]

        sk = lax.fori_loop(0, DK, decay_and_project, jnp.zeros_like(v), unroll=8)
        u = beta * (v - sk)

        def update_and_read(dk, acc):
            s = s_ref[0, dk] + kq_ref[0, pl.ds(dk, 1), :] * u
            s_ref[0, dk] = s
            return acc + s * kq_ref[1, pl.ds(dk, 1), :]

        o = lax.fori_loop(0, DK, update_and_read, jnp.zeros_like(v), unroll=8)
        o = o * lax.rsqrt(jnp.mean(o * o, axis=0, keepdims=True) + EPS) * ng
        g = gate_ref[t]
        o_ref[t] = o * (g * jax.nn.sigmoid(g))


def _gdn_sample(z_s, conv_buf, state, conv_w, a_log, dt_bias, norm_g, DB, L):
    H = N_GDN_HEADS
    z3 = z_s.reshape(DB, L, Z_COLS)
    u_t = jnp.transpose(z3[:, :, Z_GDN:Z_GATE], (1, 2, 0))
    gate_t = jnp.transpose(z3[:, :, Z_GATE:Z_QM], (1, 2, 0))
    ab_t = jnp.transpose(z3[:, :, Z_AB:Z_AB + SUBLANES], (1, 2, 0))
    buf_t = jnp.transpose(conv_buf, (1, 2, 0))
    s_t = jnp.transpose(state, (1, 2, 3, 0))
    w_col = conv_w.reshape(GDN_CONV, GDN_CONV_CH, 1)
    col8 = lambda a: jnp.pad(a, (0, SUBLANES - a.shape[0])).reshape(SUBLANES, 1)
    nbuf = conv_buf.shape[1]
    part = lambda n, j: pl.BlockSpec((n, GDN_DK, DB), lambda h: (0, j * H + h, 0))
    wpart = lambda j: pl.BlockSpec((GDN_CONV, GDN_DK, 1), lambda h: (0, j * H + h, 0))
    whole = lambda shape: pl.BlockSpec(shape, lambda h: (0,) * len(shape))
    o_t, s_new = pl.pallas_call(
        _gdn_sample_kernel,
        grid=(H,),
        in_specs=[part(L, 0), part(L, 1), part(L, 2), part(nbuf, 0), part(nbuf, 1), part(nbuf, 2),
                  wpart(0), wpart(1), wpart(2),
                  whole((L, SUBLANES, DB)),
                  pl.BlockSpec((L, GDN_DV, DB), lambda h: (0, h, 0)),
                  whole((SUBLANES, 1)), whole((SUBLANES, 1)), whole((GDN_DV, 1)),
                  pl.BlockSpec((1, GDN_DK, GDN_DV, DB), lambda h: (h, 0, 0, 0))],
        out_specs=[pl.BlockSpec((L, GDN_DV, DB), lambda h: (0, h, 0)),
                   pl.BlockSpec((1, GDN_DK, GDN_DV, DB), lambda h: (h, 0, 0, 0))],
        out_shape=[jax.ShapeDtypeStruct((L, H * GDN_DV, DB), F32),
                   jax.ShapeDtypeStruct((H, GDN_DK, GDN_DV, DB), F32)],
        scratch_shapes=[pltpu.VMEM((2, GDN_DK, DB), F32)],
        compiler_params=_cparams("parallel"),
        name="gdn_sample",
    )(u_t, u_t, u_t, buf_t, buf_t, buf_t, w_col, w_col, w_col, ab_t, gate_t,
      col8(a_log), col8(dt_bias), norm_g.reshape(GDN_DV, 1), s_t)
    o = jnp.transpose(o_t, (2, 0, 1)).reshape(DB * L, H * GDN_DV)
    return o, jnp.transpose(s_new, (3, 0, 1, 2))


def kernel(x_prompt, x_sample, cache_swa_k, cache_swa_v, state_gdn, state_gdn_conv, cache_mem_k, cache_mem_v,
           mem_prompt, ln1_g, w_in, swa_q_norm, swa_k_norm, swa_sinks, gdn_conv_w, gdn_a_log, gdn_dt_bias,
           gdn_norm_g, mem_ln_g, w_mem_kv, mem_q_norm, mem_k_norm, w_o, ln2_g, router_w, router_b,
           moe_w1, moe_b1, moe_w2, moe_b2):
    B, S, D = x_prompt.shape
    DB, DL, _ = x_sample.shape
    depth = ln1_g.shape[0]
    assert depth == 1
    l = 0
    tp, ts = B * S, DB * DL
    t_all = tp + ts
    n_ab = 2 * N_GDN_HEADS
    c_ab = SWA_Q_COLS + 2 * SWA_KV_COLS + GDN_CONV_CH
    w = w_in[l]
    w_z = jnp.concatenate([w[:, :c_ab], w[:, c_ab + n_ab:], w[:, c_ab:c_ab + n_ab],
                           jnp.zeros((D, LANES - n_ab), F32)], axis=1).astype(BF16)
    rw = jnp.pad(router_w[l], ((0, 0), (0, LANES - N_EXPERTS)))
    rw_hi = rw.astype(BF16)
    rw_lo = (rw - rw_hi.astype(F32)).astype(BF16)
    rb = jnp.pad(router_b[l], (0, LANES - N_EXPERTS)).reshape(1, LANES)
    wo = w_o[l].astype(BF16)
    w1 = moe_w1[l]
    w2 = moe_w2[l]
    b1 = moe_b1[l].reshape(N_EXPERTS, 1, -1)
    b2 = moe_b2[l].reshape(N_EXPERTS, 1, -1)
    p = {'q_norm': swa_q_norm[l], 'k_norm': swa_k_norm[l], 'sinks': swa_sinks[l], 'conv_w': gdn_conv_w[l],
         'a_log': gdn_a_log[l], 'dt_bias': gdn_dt_bias[l], 'gdn_norm': gdn_norm_g[l], 'mem_q_norm': mem_q_norm[l]}

    xp = x_prompt.reshape(tp, D)
    xs = x_sample.reshape(ts, D)
    z_p = _inproj(xp, ln1_g[l], w_z)
    z_s = _inproj(xs, ln1_g[l], w_z)

    M = mem_prompt.shape[1]
    z_p3 = z_p.reshape(B, S, Z_COLS)
    mk2, mv2 = _mem_kv(mem_prompt.reshape(B * M, D), mem_ln_g[l], w_mem_kv[l], mem_k_norm[l])
    mk = mk2.reshape(B, M, N_MEM_HEADS, HEAD_DIM)
    mv = mv2.reshape(B, M, N_MEM_HEADS, HEAD_DIM)
    os_p, pk = _swa_prompt(z_p3, p['q_norm'], p['k_norm'], p['sinks'])
    od_p, ps = _gdn_prompt(z_p3, p['conv_w'], p['a_log'], p['dt_bias'], p['gdn_norm'])
    om_p = _mem_attn_prompt(z_p3, mk2.reshape(B, M, MEM_Q_COLS), mv2.reshape(B, M, MEM_Q_COLS), p['mem_q_norm'])
    os_p, od_p, om_p = os_p.reshape(tp, -1), od_p.reshape(tp, -1), om_p.reshape(tp, -1)
    pk = pk.reshape(B, WINDOW, N_SWA_KV, HEAD_DIM)
    pv = z_p3[:, S - WINDOW:, Z_V:Z_GDN].reshape(B, WINDOW, N_SWA_KV, HEAD_DIM)
    pc = z_p3[:, S - (GDN_CONV - 1):, Z_GDN:Z_GATE]
    Wb = cache_swa_k.shape[2]
    ck = cache_swa_k[l].reshape(DB, Wb, SWA_KV_COLS)
    cv = cache_swa_v[l].reshape(DB, Wb, SWA_KV_COLS)
    os_s, k_new = _swa_sample(z_s, ck, cv, p['q_norm'], p['k_norm'], p['sinks'], DL)
    od_s, ss = _gdn_sample(z_s, state_gdn_conv[l], state_gdn[l], p['conv_w'], p['a_log'], p['dt_bias'],
                           p['gdn_norm'], DB, DL)
    om_s = _mem_attn_sample(z_s, cache_mem_k[l].reshape(DB, -1, MEM_Q_COLS),
                            cache_mem_v[l].reshape(DB, -1, MEM_Q_COLS), p['mem_q_norm'], DL)
    z_s3 = z_s.reshape(DB, DL, Z_COLS)
    sk = jnp.concatenate([ck, k_new.reshape(DB, DL, SWA_KV_COLS)], axis=1)[:, DL:]
    sv = jnp.concatenate([cv, z_s3[:, :, Z_V:Z_GDN]], axis=1)[:, DL:]
    sk = sk.reshape(DB, Wb, N_SWA_KV, HEAD_DIM)
    sv = sv.reshape(DB, Wb, N_SWA_KV, HEAD_DIM)
    sc = jnp.concatenate([state_gdn_conv[l], z_s3[:, :, Z_GDN:Z_GATE]], axis=1)[:, DL:]

    h_p, hn_all, lg_all = _outproj(xp, os_p, od_p, om_p, wo, ln2_g[l], rw_hi, rw_lo, rb, t_all, 0)
    h_s, hn_all, lg_all = _outproj(xs, os_s, od_s, om_s, wo, ln2_g[l], rw_hi, rw_lo, rb, t_all, tp,
                                   prev=(hn_all, lg_all))
    y_p, y_s = _moe(hn_all, lg_all, [h_p, h_s], w1, b1, w2, b2)
    return (y_p.reshape(B, S, D), y_s.reshape(DB, DL, D), pk[None], pv[None], ps[None], pc[None], mk[None],
            mv[None], sk[None], sv[None], ss[None], sc[None])
```

```python
import functools

import jax
import jax.numpy as jnp
from jax import lax
from jax.experimental import pallas as pl
from jax.experimental.pallas import tpu as pltpu

F32 = jnp.float32
BF16 = jnp.bfloat16
I32 = jnp.int32

HEAD_DIM = 64
N_SWA_HEADS = 8
N_SWA_KV = 2
SWA_GROUP = N_SWA_HEADS // N_SWA_KV
WINDOW = 128
N_GDN_HEADS = 4
GDN_DK = 64
GDN_DV = 64
GDN_CONV = 4
GDN_CHUNK = 64
N_MEM_HEADS = 4
N_EXPERTS = 32
TOP_K = 4
SWIGLU_ALPHA = 1.702
SWIGLU_LIMIT = 7.0
EPS = 1e-6
ATTN_SCALE = HEAD_DIM ** -0.5

SWA_Q_COLS = N_SWA_HEADS * HEAD_DIM
SWA_KV_COLS = N_SWA_KV * HEAD_DIM
GDN_QK_COLS = N_GDN_HEADS * GDN_DK
GDN_V_COLS = N_GDN_HEADS * GDN_DV
GDN_CONV_CH = 2 * GDN_QK_COLS + GDN_V_COLS
MEM_Q_COLS = N_MEM_HEADS * HEAD_DIM

LANES = 128
SUBLANES = 8
VMEM_LIMIT = 56 * 1024 * 1024

Z_Q = 0
Z_K = Z_Q + SWA_Q_COLS
Z_V = Z_K + SWA_KV_COLS
Z_GDN = Z_V + SWA_KV_COLS
Z_GATE = Z_GDN + GDN_CONV_CH
Z_QM = Z_GATE + GDN_V_COLS
Z_AB = Z_QM + MEM_Q_COLS
Z_COLS = Z_AB + LANES

ROW_TILE = 512
MOE_TILE = 512
MOE_BLK = 512
PERM_CHUNK = 256
RUN_ALIGN = 16
BLK_ROWS = -(-(MOE_BLK * TOP_K + N_EXPERTS * (RUN_ALIGN - 1)) // PERM_CHUNK) * PERM_CHUNK


def _cparams(*sem):
    return pltpu.CompilerParams(dimension_semantics=sem, vmem_limit_bytes=VMEM_LIMIT)


def _bdot(a, b):
    return jnp.dot(a.astype(BF16), b.astype(BF16), preferred_element_type=F32)


def _bdot_nt(a, b):
    return lax.dot_general(a.astype(BF16), b.astype(BF16), (((1,), (1,)), ((), ())),
                           preferred_element_type=F32)


def _bdot_tn(a, b):
    return lax.dot_general(a.astype(BF16), b.astype(BF16), (((0,), (0,)), ((), ())),
                           preferred_element_type=F32)


def _split2(x):
    hi = x.astype(BF16)
    lo = (x - hi.astype(F32)).astype(BF16)
    return hi, lo


def _split3(x):
    hi = x.astype(BF16)
    r = x - hi.astype(F32)
    mid = r.astype(BF16)
    lo = (r - mid.astype(F32)).astype(BF16)
    return hi, mid, lo


def _rms_rows(x, g):
    ms = jnp.mean(x * x, axis=-1, keepdims=True)
    return x * lax.rsqrt(ms + EPS) * g


def _inproj_kernel(x_ref, g_ref, w_ref, z_ref):
    n = _rms_rows(x_ref[...], g_ref[...])
    z_ref[...] = jnp.dot(n.astype(BF16), w_ref[...], preferred_element_type=F32)


def _inproj(x2d, ln_g, w_z):
    t, d = x2d.shape
    tm = min(ROW_TILE, t)
    return pl.pallas_call(
        _inproj_kernel,
        grid=(t // tm,),
        in_specs=[pl.BlockSpec((tm, d), lambda i: (i, 0)),
                  pl.BlockSpec((1, d), lambda i: (0, 0)),
                  pl.BlockSpec((d, Z_COLS), lambda i: (0, 0))],
        out_specs=pl.BlockSpec((tm, Z_COLS), lambda i: (i, 0)),
        out_shape=jax.ShapeDtypeStruct((t, Z_COLS), F32),
        compiler_params=_cparams("parallel"),
        name="inproj",
    )(x2d, ln_g.reshape(1, d), w_z)


def _outproj_kernel(x_ref, os_ref, od_ref, om_ref, wo_ref, g_ref, rwh_ref, rwl_ref, rb_ref,
                    *refs, n_own):
    h_ref, hn_ref, lg_ref = refs[-3:]
    i = pl.program_id(0)

    @pl.when(i < n_own)
    def _():
        n_s = os_ref.shape[1]
        n_d = od_ref.shape[1]
        h = x_ref[...]
        h = h + jnp.dot(os_ref[...].astype(BF16), wo_ref[0:n_s, :], preferred_element_type=F32)
        h = h + jnp.dot(od_ref[...].astype(BF16), wo_ref[n_s:n_s + n_d, :], preferred_element_type=F32)
        h = h + jnp.dot(om_ref[...].astype(BF16), wo_ref[n_s + n_d:, :], preferred_element_type=F32)
        h_ref[...] = h
        hn = _rms_rows(h, g_ref[...])
        hn_ref[...] = hn.astype(BF16)
        hi, lo = _split2(hn)
        lg = (jnp.dot(hi, rwh_ref[...], preferred_element_type=F32)
              + jnp.dot(lo, rwh_ref[...], preferred_element_type=F32)
              + jnp.dot(hi, rwl_ref[...], preferred_element_type=F32))
        lg_ref[...] = lg + rb_ref[...]

    @pl.when(i >= n_own)
    def _():
        hn_ref[...] = jnp.zeros_like(hn_ref)
        lg_ref[...] = jnp.zeros_like(lg_ref)


def _outproj(x2d, o_s, o_d, o_m, w_o, ln_g, rw_hi, rw_lo, rb, t_all, row0, prev=None):
    t, d = x2d.shape
    tm = min(ROW_TILE, t)
    blk0 = row0 // tm
    n_own = t // tm
    n_steps = n_own if prev is not None else t_all // tm
    row = lambda i: (jnp.minimum(i, n_own - 1), 0)
    row_off = lambda i: (i + blk0, 0)
    const = lambda i: (0, 0)
    in_specs = [pl.BlockSpec((tm, d), row),
                pl.BlockSpec((tm, o_s.shape[1]), row),
                pl.BlockSpec((tm, o_d.shape[1]), row),
                pl.BlockSpec((tm, o_m.shape[1]), row),
                pl.BlockSpec((d, d), const),
                pl.BlockSpec((1, d), const),
                pl.BlockSpec((d, LANES), const),
                pl.BlockSpec((d, LANES), const),
                pl.BlockSpec((1, LANES), const)]
    args = [x2d, o_s, o_d, o_m, w_o, ln_g.reshape(1, d), rw_hi, rw_lo, rb]
    aliases = {}
    if prev is not None:
        in_specs += [pl.BlockSpec(memory_space=pl.ANY), pl.BlockSpec(memory_space=pl.ANY)]
        aliases = {len(args): 1, len(args) + 1: 2}
        args += list(prev)
    return pl.pallas_call(
        functools.partial(_outproj_kernel, n_own=n_own),
        grid=(n_steps,),
        in_specs=in_specs,
        out_specs=[pl.BlockSpec((tm, d), row),
                   pl.BlockSpec((tm, d), row_off),
                   pl.BlockSpec((tm, LANES), row_off)],
        out_shape=[jax.ShapeDtypeStruct((t, d), F32),
                   jax.ShapeDtypeStruct((t_all, d), BF16),
                   jax.ShapeDtypeStruct((t_all, LANES), F32)],
        input_output_aliases=aliases,
        compiler_params=_cparams("arbitrary"),
        name="outproj_router",
    )(*args)


def _route_kernel(lg_ref, pos_ref, post_ref, g_ref, cnt_ref):
    tm = lg_ref.shape[0]
    lane = lax.broadcasted_iota(I32, (tm, LANES), 1).astype(F32)
    l = jnp.where(lane < N_EXPERTS, lg_ref[...], -jnp.inf)
    vals, idxs = [], []
    for _k in range(TOP_K):
        m = jnp.max(l, axis=-1, keepdims=True)
        idx = jnp.min(jnp.where(l == m, lane, float(LANES)), axis=-1, keepdims=True)
        l = jnp.where(lane == idx, -jnp.inf, l)
        vals.append(m)
        idxs.append(idx)
    ex = [jnp.exp(v - vals[0]) for v in vals]
    den = ex[0] + ex[1] + ex[2] + ex[3]
    member = jnp.zeros((tm, LANES), F32)
    for idx in idxs:
        member = member + jnp.where(lane == idx, 1.0, 0.0)
    ri = lax.broadcasted_iota(I32, (tm, tm), 0)
    ci = lax.broadcasted_iota(I32, (tm, tm), 1)
    strict = jnp.where(ci < ri, 1.0, 0.0).astype(BF16)
    prefix = jnp.dot(strict, member.astype(BF16), preferred_element_type=F32)
    cnt = jnp.sum(member, axis=0, keepdims=True)
    cpad = jnp.ceil(cnt * (1.0 / RUN_ALIGN)) * float(RUN_ALIGN)
    c_hi = jnp.floor(cpad * (1.0 / 256.0))
    c_lo = cpad - 256.0 * c_hi
    ej = lax.broadcasted_iota(I32, (LANES, LANES), 0)
    ee = lax.broadcasted_iota(I32, (LANES, LANES), 1)
    before = jnp.where(ej < ee, 1.0, 0.0).astype(BF16)
    bcast = lambda v: jnp.broadcast_to(v, (SUBLANES, LANES)).astype(BF16)
    off = (256.0 * jnp.dot(bcast(c_hi), before, preferred_element_type=F32)
           + jnp.dot(bcast(c_lo), before, preferred_element_type=F32))[0:1]
    where_in_run = prefix + off
    p_out = jnp.zeros((tm, LANES), F32)
    g_out = jnp.zeros((tm, LANES), F32)
    for k in range(TOP_K):
        pos = jnp.sum(jnp.where(lane == idxs[k], where_in_run, 0.0), axis=-1, keepdims=True)
        p_out = jnp.where(lane == float(k), pos, p_out)
        g_out = jnp.where(lane == float(k), ex[k] / den, g_out)
    pos_ref[...] = p_out[:, :TOP_K]
    post_ref[...] = p_out.T[:SUBLANES, :]
    g_ref[...] = g_out[:, :TOP_K]
    cnt_ref[0] = cnt


def _route(logits):
    t = logits.shape[0]
    tm = MOE_BLK
    nb = t // tm
    return pl.pallas_call(
        _route_kernel,
        grid=(nb,),
        in_specs=[pl.BlockSpec((tm, LANES), lambda i: (i, 0))],
        out_specs=[pl.BlockSpec((tm, TOP_K), lambda i: (i, 0)),
                   pl.BlockSpec((SUBLANES, tm), lambda i: (0, i)),
                   pl.BlockSpec((tm, TOP_K), lambda i: (i, 0)),
                   pl.BlockSpec((1, 1, LANES), lambda i: (i, 0, 0))],
        out_shape=[jax.ShapeDtypeStruct((t, TOP_K), F32),
                   jax.ShapeDtypeStruct((SUBLANES, t), F32),
                   jax.ShapeDtypeStruct((t, TOP_K), F32),
                   jax.ShapeDtypeStruct((nb, 1, LANES), F32)],
        compiler_params=_cparams("parallel"),
        name="route",
    )(logits)


def _run_copies(n, max_rows, src_ref, src0, dst_ref, dst0, sem, wait):
    pos = 0
    bit = max_rows
    while bit >= RUN_ALIGN:
        take = (n & bit) != 0

        def go(pos=pos, bit=bit):
            cp = pltpu.make_async_copy(src_ref.at[pl.ds(pl.multiple_of(src0 + pos, RUN_ALIGN), bit)],
                                       dst_ref.at[pl.ds(pl.multiple_of(dst0 + pos, RUN_ALIGN), bit)], sem)
            cp.wait() if wait else cp.start()

        pl.when(take)(go)
        pos = pos + jnp.where(take, bit, 0)
        bit //= 2


RUN_SIZES = tuple(MOE_BLK >> i for i in range((MOE_BLK // RUN_ALIGN).bit_length()))


def _piece_copies(b, cnt_ref, loc_ref, glob_ref, local_ref, global_hbm, sem, to_global, wait):
    for c, rows in enumerate(RUN_SIZES):
        base = b * len(RUN_SIZES) + c

        def body(s, carry, rows=rows, base=base):
            j = base * N_EXPERTS + s
            loc = local_ref.at[pl.ds(pl.multiple_of(loc_ref[j], RUN_ALIGN), rows)]
            glob = global_hbm.at[pl.ds(pl.multiple_of(glob_ref[j], RUN_ALIGN), rows)]
            cp = pltpu.make_async_copy(loc, glob, sem) if to_global else pltpu.make_async_copy(glob, loc, sem)
            cp.wait() if wait else cp.start()
            return carry

        lax.fori_loop(0, cnt_ref[base], body, 0)


def _dispatch_kernel(cnt_ref, loc_ref, glob_ref, estart_ref, elen_ref, nused_ref,
                     hn_ref, post_ref, xs_hbm, buf_ref, zero_ref, sem, zsem):
    b = pl.program_id(0)
    tm = hn_ref.shape[0]
    x = hn_ref[...].astype(BF16)
    post = post_ref[...]
    P = PERM_CHUNK
    for c in range(BLK_ROWS // P):
        r = (lax.broadcasted_iota(I32, (P, tm), 0) + c * P).astype(F32)
        sel = jnp.zeros((P, tm), F32)
        for k in range(TOP_K):
            sel = jnp.where(r == post[k:k + 1, :], 1.0, sel)
        buf_ref[c * P:(c + 1) * P, :] = jnp.dot(sel.astype(BF16), x, preferred_element_type=F32).astype(BF16)

    runs = functools.partial(_piece_copies, b, cnt_ref, loc_ref, glob_ref, buf_ref, xs_hbm, sem, True)
    runs(False)

    @pl.when(b == 0)
    def _():
        zero_ref[...] = jnp.zeros_like(zero_ref)

        def tail(wait):
            def body(e, c):
                n = (MOE_TILE - elen_ref[e] % MOE_TILE) % MOE_TILE
                _run_copies(n, MOE_TILE // 2, zero_ref, 0, xs_hbm, estart_ref[e] + elen_ref[e], zsem, wait)
                return c
            lax.fori_loop(0, N_EXPERTS, body, 0)

            def free_tile(ti, c):
                for half in range(2):
                    _run_copies(MOE_TILE // 2, MOE_TILE // 2, zero_ref, 0, xs_hbm,
                                ti * MOE_TILE + half * (MOE_TILE // 2), zsem, wait)
                return c
            lax.fori_loop(nused_ref[0], xs_hbm.shape[0] // MOE_TILE, free_tile, 0)

        tail(False)
        tail(True)

    runs(True)


def _dispatch(hn, post, piece_cnt, piece_loc, piece_glob, e_start, e_len, n_used, n_rows):
    t, d = hn.shape
    grid_spec = pltpu.PrefetchScalarGridSpec(
        num_scalar_prefetch=6,
        grid=(t // MOE_BLK,),
        in_specs=[pl.BlockSpec((MOE_BLK, d), lambda i, *_: (i, 0)),
                  pl.BlockSpec((SUBLANES, MOE_BLK), lambda i, *_: (0, i))],
        out_specs=pl.BlockSpec(memory_space=pl.ANY),
        scratch_shapes=[pltpu.VMEM((BLK_ROWS, d), BF16),
                        pltpu.VMEM((MOE_TILE, d), BF16),
                        pltpu.SemaphoreType.DMA(()),
                        pltpu.SemaphoreType.DMA(())],
    )
    return pl.pallas_call(
        _dispatch_kernel,
        grid_spec=grid_spec,
        out_shape=jax.ShapeDtypeStruct((n_rows, d), BF16),
        compiler_params=_cparams("arbitrary"),
        name="dispatch",
    )(piece_cnt, piece_loc, piece_glob, e_start, e_len, n_used, hn, post)


def _expert_kernel(te_ref, nu_ref, x_ref, w1_ref, b1_ref, w2_ref, b2_ref, y_ref, w1b_ref, w2b_ref):
    i = pl.program_id(0)
    live = i < nu_ref[0]

    @pl.when(live & ((i == 0) | (te_ref[i] != te_ref[jnp.maximum(i - 1, 0)])))
    def _():
        w1b_ref[...] = w1_ref[0].astype(BF16)
        w2b_ref[...] = w2_ref[0].astype(BF16)

    @pl.when(live)
    def _():
        f = w2_ref.shape[1]
        h = jnp.dot(x_ref[...], w1b_ref[...], preferred_element_type=F32) + b1_ref[0]
        glu = jnp.minimum(h[:, :f], SWIGLU_LIMIT)
        lin = jnp.clip(h[:, f:], -SWIGLU_LIMIT, SWIGLU_LIMIT)
        act = glu * jax.nn.sigmoid(SWIGLU_ALPHA * glu) * (lin + 1.0)
        y = jnp.dot(act.astype(BF16), w2b_ref[...], preferred_element_type=F32) + b2_ref[0]
        y_ref[...] = y.astype(BF16)

    @pl.when(i >= nu_ref[0])
    def _():
        y_ref[...] = jnp.zeros_like(y_ref)


def _experts(xs, tile_expert, n_used, w1, b1, w2, b2):
    n_rows, d = xs.shape
    f2 = w1.shape[2]
    f = w2.shape[1]
    n_tiles = n_rows // MOE_TILE
    live = lambda i, te, nu: (jnp.minimum(i, nu[0] - 1), 0)
    every = lambda i, te, nu: (i, 0)
    wsel = lambda i, te, nu: (te[i], 0, 0)
    grid_spec = pltpu.PrefetchScalarGridSpec(
        num_scalar_prefetch=2,
        grid=(n_tiles,),
        in_specs=[pl.BlockSpec((MOE_TILE, d), live),
                  pl.BlockSpec((1, d, f2), wsel),
                  pl.BlockSpec((1, 1, f2), wsel),
                  pl.BlockSpec((1, f, d), wsel),
                  pl.BlockSpec((1, 1, d), wsel)],
        out_specs=pl.BlockSpec((MOE_TILE, d), every),
        scratch_shapes=[pltpu.VMEM((d, f2), BF16), pltpu.VMEM((f, d), BF16)],
    )
    return pl.pallas_call(
        _expert_kernel,
        grid_spec=grid_spec,
        out_shape=jax.ShapeDtypeStruct((n_rows, d), BF16),
        compiler_params=_cparams("arbitrary"),
        name="experts",
    )(tile_expert, n_used, xs, w1, b1, w2, b2)


def _combine_kernel(cnt_ref, loc_ref, glob_ref, h_ref, pos_ref, g_ref, yb_hbm, y_ref, buf_ref, sem, *, blk0):
    i = pl.program_id(0)
    b = i + blk0
    tm, d = h_ref.shape

    @pl.when(i == 0)
    def _():
        buf_ref[...] = jnp.zeros_like(buf_ref)

    runs = functools.partial(_piece_copies, b, cnt_ref, loc_ref, glob_ref, buf_ref, yb_hbm, sem, False)
    runs(False)
    pos = pos_ref[...]
    g = g_ref[...]
    runs(True)
    y = h_ref[...]
    P = PERM_CHUNK
    for c in range(BLK_ROWS // P):
        col = (lax.broadcasted_iota(I32, (tm, P), 1) + c * P).astype(F32)
        wgt = jnp.zeros((tm, P), F32)
        for k in range(TOP_K):
            wgt = jnp.where(col == pos[:, k:k + 1], g[:, k:k + 1], wgt)
        y = y + jnp.dot(wgt.astype(BF16), buf_ref[c * P:(c + 1) * P, :], preferred_element_type=F32)
    y_ref[...] = y


def _combine(h, pos, gates, piece_cnt, piece_loc, piece_glob, yb, blk0):
    t, d = h.shape
    grid_spec = pltpu.PrefetchScalarGridSpec(
        num_scalar_prefetch=3,
        grid=(t // MOE_BLK,),
        in_specs=[pl.BlockSpec((MOE_BLK, d), lambda i, *_: (i, 0)),
                  pl.BlockSpec((MOE_BLK, TOP_K), lambda i, *_: (i + blk0, 0)),
                  pl.BlockSpec((MOE_BLK, TOP_K), lambda i, *_: (i + blk0, 0)),
                  pl.BlockSpec(memory_space=pl.ANY)],
        out_specs=pl.BlockSpec((MOE_BLK, d), lambda i, *_: (i, 0)),
        scratch_shapes=[pltpu.VMEM((BLK_ROWS, d), BF16),
                        pltpu.SemaphoreType.DMA(())],
    )
    return pl.pallas_call(
        functools.partial(_combine_kernel, blk0=blk0),
        grid_spec=grid_spec,
        out_shape=jax.ShapeDtypeStruct((t, d), F32),
        compiler_params=_cparams("arbitrary"),
        name="combine",
    )(piece_cnt, piece_loc, piece_glob, h, pos, gates, yb)


def _moe(hn_all, logits_all, h_parts, w1, b1, w2, b2):
    t_all = hn_all.shape[0]
    nb = t_all // MOE_BLK
    pos, post, gates, counts_f = _route(logits_all)
    cnt = counts_f.reshape(nb, LANES)[:, :N_EXPERTS].astype(I32)
    seg_len = (cnt + RUN_ALIGN - 1) // RUN_ALIGN * RUN_ALIGN
    before_e = jnp.arange(N_EXPERTS)[:, None] < jnp.arange(N_EXPERTS)[None, :]
    before_b = jnp.arange(nb)[None, :] < jnp.arange(nb)[:, None]
    seg_off = jnp.sum(jnp.where(before_e[None], seg_len[:, :, None], 0), axis=1)
    e_len = jnp.sum(seg_len, axis=0)
    e_tiles = (e_len + MOE_TILE - 1) // MOE_TILE
    tile_start = jnp.sum(jnp.where(before_e, e_tiles[:, None], 0), axis=0)
    tile_end = tile_start + e_tiles
    e_start = tile_start * MOE_TILE
    seg_dst = e_start[None, :] + jnp.sum(jnp.where(before_b[:, :, None], seg_len[None], 0), axis=1)
    max_rows = t_all * TOP_K + nb * N_EXPERTS * (RUN_ALIGN - 1) + N_EXPERTS * (MOE_TILE - RUN_ALIGN)
    n_tiles = -(-max_rows // MOE_TILE)
    n_rows = n_tiles * MOE_TILE
    n_used = tile_end[-1:].astype(I32)
    tile_expert = jnp.minimum(jnp.sum(tile_end[None, :] <= jnp.arange(n_tiles, dtype=I32)[:, None], axis=1),
                              N_EXPERTS - 1).astype(I32)
    sizes = jnp.array(RUN_SIZES, I32)[None, :, None]
    n_run = seg_len[:, None, :]
    has = (n_run & sizes) != 0
    piece_at = n_run & ~(2 * sizes - 1)
    rank = jnp.sum(jnp.where(before_e[None, None], has[:, :, :, None], False), axis=2)
    slot = jnp.arange(N_EXPERTS)
    put = has[..., None] & (rank[..., None] == slot)
    listed = lambda v: jnp.sum(jnp.where(put, v[..., None], 0), axis=2)
    piece_loc = listed(seg_off[:, None, :] + piece_at)
    piece_glob = listed(seg_dst[:, None, :] + piece_at)
    piece_cnt = jnp.sum(has, axis=2)
    flat = lambda a: a.reshape(-1).astype(I32)
    tables = (flat(piece_cnt), flat(piece_loc), flat(piece_glob))
    xs = _dispatch(hn_all, post, *tables, flat(e_start), flat(e_len), n_used, n_rows)
    yb = _experts(xs, tile_expert, n_used, w1, b1, w2, b2)
    outs = []
    row = 0
    for h in h_parts:
        outs.append(_combine(h, pos, gates, *tables, yb, row // MOE_BLK))
        row += h.shape[0]
    return outs


GDN_ROWS = 4 * GDN_CHUNK
CONV_HALO = SUBLANES
NEUMANN_SPLIT = 2


def _softplus(x):
    return jnp.maximum(x, 0.0) + jnp.log1p(jnp.exp(-jnp.abs(x)))


def _gdn_prompt_kernel(u_ref, ab_ref, gate_ref, cw_ref, alog_ref, dtb_ref, ng_ref, o_ref, s_ref, ubuf_ref):
    step = pl.program_id(0)
    NB = u_ref.shape[0]
    R = GDN_ROWS
    C = GDN_CHUNK
    NC = R // C

    @pl.when(step == 0)
    def _():
        ubuf_ref[:, 0:CONV_HALO, :] = jnp.zeros((NB, CONV_HALO, ubuf_ref.shape[2]), F32)
        s_ref[...] = jnp.zeros_like(s_ref)

    ri = lax.broadcasted_iota(I32, (R, R), 0)
    ci = lax.broadcasted_iota(I32, (R, R), 1)
    shift = C.bit_length() - 1
    same = lax.shift_right_logical(ri, shift) == lax.shift_right_logical(ci, shift)
    incl = same & (ci <= ri)
    strict = same & (ci < ri)
    tri = jnp.where(incl, 1.0, 0.0).astype(BF16)
    blk = jnp.where(same, 1.0, 0.0).astype(BF16)
    cw = cw_ref[...]
    ng = ng_ref[...]

    chains = []
    for b in range(NB):
        u = u_ref[b]
        ubuf_ref[b, CONV_HALO:CONV_HALO + R, :] = u
        y = u * cw[GDN_CONV - 1:GDN_CONV, :]
        for j in range(1, GDN_CONV):
            y = y + ubuf_ref[b, CONV_HALO - j:CONV_HALO - j + R, :] * cw[GDN_CONV - 1 - j:GDN_CONV - j, :]
        ubuf_ref[b, 0:CONV_HALO, :] = u[R - CONV_HALO:, :]
        qkv = y * jax.nn.sigmoid(y)
        ab = ab_ref[b]
        g_t = -jnp.exp(alog_ref[...]) * _softplus(ab + dtb_ref[...])
        beta_t = jax.nn.sigmoid(ab)
        parts = _split3(g_t)
        gcum = sum(jnp.dot(tri, p, preferred_element_type=F32) for p in parts)
        gtot = sum(jnp.dot(blk, p, preferred_element_type=F32) for p in parts)
        gcum_t = gcum.T
        qk_n = []
        for t in range(2 * GDN_QK_COLS // LANES):
            x = qkv[:, t * LANES:(t + 1) * LANES]
            x = x * lax.rsqrt(_pair_sumsq(x) + EPS)
            qk_n.append(x * (GDN_DK ** -0.5) if t < GDN_QK_COLS // LANES else x)
        qk_n = jnp.concatenate(qk_n, axis=-1)
        for h in range(N_GDN_HEADS):
            q = qk_n[:, h * GDN_DK:(h + 1) * GDN_DK]
            k = qk_n[:, GDN_QK_COLS + h * GDN_DK:GDN_QK_COLS + (h + 1) * GDN_DK]
            v = qkv[:, 2 * GDN_QK_COLS + h * GDN_DV:2 * GDN_QK_COLS + (h + 1) * GDN_DV]
            gc = gcum[:, h:h + 1]
            gt = gtot[:, h:h + 1]
            beta = beta_t[:, N_GDN_HEADS + h:N_GDN_HEADS + h + 1]
            decay = jnp.exp(jnp.where(incl, gc - gcum_t[h:h + 1, :], -jnp.inf))
            chains.append(dict(
                b=b, h=h,
                a=jnp.where(strict, beta * _bdot_nt(k, k) * decay, 0.0),
                qk=_bdot_nt(q, k) * decay,
                r=jnp.concatenate([v * beta, k * (beta * jnp.exp(gc))], axis=-1),
                q_dec=q * jnp.exp(gc), k_dec=k * jnp.exp(gt - gc), g_last=jnp.exp(gt)))

    dot = lambda x, y: jnp.dot(x, y, preferred_element_type=F32)
    level, j = 1, 0
    while level < C:
        last = 2 * level >= C
        for ch in chains:
            if j < NEUMANN_SPLIT:
                a_hi, a_lo = _split2(ch['a'])
                r_hi, r_lo = _split2(ch['r'])
                upd = dot(a_hi, r_hi) + dot(a_hi, r_lo) + dot(a_lo, r_hi)
            else:
                a_hi = ch['a'].astype(BF16)
                upd = dot(a_hi, ch['r'].astype(BF16))
            ch['r'] = ch['r'] - upd if level == 1 else ch['r'] + upd
            if not last:
                sq = dot(a_hi, a_hi)
                if j + 1 < NEUMANN_SPLIT:
                    sq = sq + dot(a_hi, a_lo) + dot(a_lo, a_hi)
                ch['a'] = sq
        level *= 2
        j += 1

    for ch in chains:
        ch['u'], ch['w'] = ch['r'][:, :GDN_DV], ch['r'][:, GDN_DV:]
        ch['S'] = s_ref[ch['b'], ch['h']]
        ch['k_dec_t'] = ch['k_dec'].T
        ch['outs'] = []
    for c in range(NC):
        sl = slice(c * C, (c + 1) * C)
        for ch in chains:
            S = ch['S']
            v_new = ch['u'][sl] - _bdot(ch['w'][sl], S)
            ch['outs'].append(_bdot(ch['q_dec'][sl], S) + _bdot(ch['qk'][sl, sl], v_new))
            ch['S'] = S * ch['g_last'][c * C:c * C + 1, :] + _bdot(ch['k_dec_t'][:, sl], v_new)
    for ch in chains:
        b, h = ch['b'], ch['h']
        s_ref[b, h] = ch['S']
        o = jnp.concatenate(ch['outs'], axis=0)
        o = o * lax.rsqrt(jnp.mean(o * o, axis=-1, keepdims=True) + EPS) * ng
        gh = gate_ref[b, :, h * GDN_DV:(h + 1) * GDN_DV]
        o_ref[b, :, h * GDN_DV:(h + 1) * GDN_DV] = o * (gh * jax.nn.sigmoid(gh))


def _gdn_prompt(z3, conv_w, a_log, dt_bias, norm_g):
    B, S, _ = z3.shape
    R = GDN_ROWS
    lanes4 = lambda a: jnp.pad(a, (0, LANES - a.shape[0])).reshape(1, LANES)
    return pl.pallas_call(
        _gdn_prompt_kernel,
        grid=(S // R,),
        in_specs=[pl.BlockSpec((B, R, GDN_CONV_CH), lambda s: (0, s, Z_GDN // GDN_CONV_CH)),
                  pl.BlockSpec((B, R, LANES), lambda s: (0, s, Z_AB // LANES)),
                  pl.BlockSpec((B, R, GDN_V_COLS), lambda s: (0, s, Z_GATE // GDN_V_COLS)),
                  pl.BlockSpec((GDN_CONV, GDN_CONV_CH), lambda s: (0, 0)),
                  pl.BlockSpec((1, LANES), lambda s: (0, 0)),
                  pl.BlockSpec((1, LANES), lambda s: (0, 0)),
                  pl.BlockSpec((1, GDN_DV), lambda s: (0, 0))],
        out_specs=[pl.BlockSpec((B, R, GDN_V_COLS), lambda s: (0, s, 0)),
                   pl.BlockSpec((B, N_GDN_HEADS, GDN_DK, GDN_DV), lambda s: (0, 0, 0, 0))],
        out_shape=[jax.ShapeDtypeStruct((B, S, GDN_V_COLS), F32),
                   jax.ShapeDtypeStruct((B, N_GDN_HEADS, GDN_DK, GDN_DV), F32)],
        scratch_shapes=[pltpu.VMEM((B, CONV_HALO + R, GDN_CONV_CH), F32)],
        compiler_params=_cparams("arbitrary"),
        name="gdn_prompt",
    )(z3, z3, z3, conv_w, lanes4(a_log), lanes4(dt_bias), norm_g.reshape(1, GDN_DV))


def _pair_sumsq(x):
    li = lax.broadcasted_iota(I32, (LANES, LANES), 0) // HEAD_DIM
    lj = lax.broadcasted_iota(I32, (LANES, LANES), 1) // HEAD_DIM
    same = jnp.where(li == lj, 1.0, 0.0).astype(BF16)
    hi, lo = _split2(x * x)
    return jnp.dot(hi, same, preferred_element_type=F32) + jnp.dot(lo, same, preferred_element_type=F32)


def _pair_rms(x, g):
    return x * lax.rsqrt(_pair_sumsq(x) * (1.0 / HEAD_DIM) + EPS) * g


def _first_half(shape):
    return lax.broadcasted_iota(I32, shape, 1) < HEAD_DIM


LOG2E = 1.4426950408889634


def _swa_prompt_kernel(sink_ref, q_ref, kc_ref, kp_ref, vc_ref, vp_ref, qg_ref, kg_ref, o_ref, kn_ref,
                       bias_ref):
    n = pl.program_id(1)
    W = WINDOW

    @pl.when(n <= 1)
    def _():
        qi = lax.broadcasted_iota(I32, (W, 2 * W), 0)
        kj = lax.broadcasted_iota(I32, (W, 2 * W), 1)
        dist = qi + W - kj
        valid = (dist >= 0) & (dist < W) & ((n > 0) | (kj >= W))
        mask = jnp.where(valid, 0.0, -jnp.inf)
        distf = dist.astype(F32)
        for head in range(N_SWA_HEADS):
            slope = 2.0 ** (-(8.0 / N_SWA_HEADS) * (head + 1))
            bias_ref[head] = mask - (slope * LOG2E) * distf

    kg = kg_ref[...]
    qg = qg_ref[...]
    kc = _pair_rms(kc_ref[0], kg)
    kn_ref[0] = kc
    k2 = jnp.concatenate([_pair_rms(kp_ref[0], kg), kc], axis=0)
    v2 = jnp.concatenate([vp_ref[0], vc_ref[0]], axis=0)
    fh = _first_half(k2.shape)
    k2r = pltpu.roll(k2, HEAD_DIM, 1)
    v2r = pltpu.roll(v2, HEAD_DIM, 1)
    kdup = (jnp.where(fh, k2, k2r).astype(BF16), jnp.where(fh, k2r, k2).astype(BF16))
    vdup = (jnp.where(fh, v2, v2r).astype(BF16), jnp.where(fh, v2r, v2).astype(BF16))
    fq = _first_half((W, LANES))
    heads = range(N_SWA_HEADS)
    kv_of = lambda head: head // SWA_GROUP
    qts = [_pair_rms(q_ref[0, :, t * LANES:(t + 1) * LANES], qg) * (ATTN_SCALE * LOG2E)
           for t in range(SWA_Q_COLS // LANES)]
    qms = [jnp.where(fq == (head % 2 == 0), qts[head // 2], 0.0).astype(BF16) for head in heads]
    ss = [_bdot_nt(qms[head], kdup[kv_of(head)]) + bias_ref[head] for head in heads]
    sinks = [sink_ref[head] * LOG2E for head in heads]
    ms = [jnp.maximum(jnp.max(ss[head], axis=-1, keepdims=True), sinks[head]) for head in heads]
    ps = [jnp.exp2(ss[head] - ms[head]) for head in heads]
    dens = [jnp.sum(ps[head], axis=-1, keepdims=True) + jnp.exp2(sinks[head] - ms[head]) for head in heads]
    outs = [_bdot(ps[head], vdup[kv_of(head)]) / dens[head] for head in heads]
    for t in range(SWA_Q_COLS // LANES):
        o_ref[0, :, t * LANES:(t + 1) * LANES] = jnp.where(fq, outs[2 * t], outs[2 * t + 1])


def _swa_prompt(z3, q_g, k_g, sinks):
    B, S, _ = z3.shape
    W = WINDOW
    twice = lambda g: jnp.concatenate([g, g]).reshape(1, LANES)
    kcol, vcol = Z_K // LANES, Z_V // LANES
    grid_spec = pltpu.PrefetchScalarGridSpec(
        num_scalar_prefetch=0,
        grid=(B, S // W),
        in_specs=[pl.BlockSpec(memory_space=pltpu.SMEM),
                  pl.BlockSpec((1, W, SWA_Q_COLS), lambda b, n: (b, n, 0)),
                  pl.BlockSpec((1, W, LANES), lambda b, n: (b, n, kcol)),
                  pl.BlockSpec((1, W, LANES), lambda b, n: (b, jnp.maximum(n - 1, 0), kcol)),
                  pl.BlockSpec((1, W, LANES), lambda b, n: (b, n, vcol)),
                  pl.BlockSpec((1, W, LANES), lambda b, n: (b, jnp.maximum(n - 1, 0), vcol)),
                  pl.BlockSpec((1, LANES), lambda b, n: (0, 0)),
                  pl.BlockSpec((1, LANES), lambda b, n: (0, 0))],
        out_specs=[pl.BlockSpec((1, W, SWA_Q_COLS), lambda b, n: (b, n, 0)),
                   pl.BlockSpec((1, W, LANES), lambda b, n: (b, 0, 0))],
        scratch_shapes=[pltpu.VMEM((N_SWA_HEADS, W, 2 * W), F32)],
    )
    return pl.pallas_call(
        _swa_prompt_kernel,
        grid_spec=grid_spec,
        out_shape=[jax.ShapeDtypeStruct((B, S, SWA_Q_COLS), F32),
                   jax.ShapeDtypeStruct((B, W, LANES), F32)],
        compiler_params=_cparams("arbitrary", "arbitrary"),
        name="swa_prompt",
    )(sinks, z3, z3, z3, z3, z3, twice(q_g), twice(k_g))


def _mem_kv_kernel(m_ref, g_ref, w_ref, kg_ref, k_ref, v_ref):
    n = _rms_rows(m_ref[...], g_ref[...])
    kv = jnp.dot(n.astype(BF16), w_ref[...], preferred_element_type=F32)
    kg = kg_ref[...]
    for t in range(MEM_Q_COLS // LANES):
        k_ref[:, t * LANES:(t + 1) * LANES] = _pair_rms(kv[:, t * LANES:(t + 1) * LANES], kg)
    v_ref[...] = kv[:, MEM_Q_COLS:]


def _mem_kv(mem2d, ln_g, w_kv, k_g):
    r, d = mem2d.shape
    twice = jnp.concatenate([k_g, k_g]).reshape(1, LANES)
    full = lambda shape: pl.BlockSpec(shape, lambda i: (0,) * len(shape))
    return pl.pallas_call(
        _mem_kv_kernel,
        grid=(1,),
        in_specs=[full((r, d)), full((1, d)), full((d, 2 * MEM_Q_COLS)), full((1, LANES))],
        out_specs=[full((r, MEM_Q_COLS)), full((r, MEM_Q_COLS))],
        out_shape=[jax.ShapeDtypeStruct((r, MEM_Q_COLS), F32), jax.ShapeDtypeStruct((r, MEM_Q_COLS), F32)],
        compiler_params=_cparams("arbitrary"),
        name="mem_kv",
    )(mem2d, ln_g.reshape(1, d), w_kv.astype(BF16), twice)


def _mem_attn_kernel(q_ref, k_ref, v_ref, qg_ref, o_ref):
    qg = qg_ref[...]
    rows = q_ref.shape[1]
    fq = _first_half((rows, LANES))
    heads = range(N_MEM_HEADS)
    tile = lambda t: slice(t * LANES, (t + 1) * LANES)
    qts = [_pair_rms(q_ref[0, :, tile(t)], qg) * (ATTN_SCALE * LOG2E) for t in range(MEM_Q_COLS // LANES)]
    kts = [k_ref[0, :, tile(t)].astype(BF16) for t in range(MEM_Q_COLS // LANES)]
    vts = [v_ref[0, :, tile(t)].astype(BF16) for t in range(MEM_Q_COLS // LANES)]
    ss = [_bdot_nt(jnp.where(fq == (h % 2 == 0), qts[h // 2], 0.0), kts[h // 2]) for h in heads]
    ps = [jnp.exp2(s - jnp.max(s, axis=-1, keepdims=True)) for s in ss]
    outs = [_bdot(ps[h], vts[h // 2]) / jnp.sum(ps[h], axis=-1, keepdims=True) for h in heads]
    for t in range(MEM_Q_COLS // LANES):
        o_ref[0, :, tile(t)] = jnp.where(fq, outs[2 * t], outs[2 * t + 1])


MEM_Q_TILE = 256


def _mem_attn_prompt(z3, mem_k, mem_v, q_g):
    B, S, _ = z3.shape
    M = mem_k.shape[1]
    tq = MEM_Q_TILE
    twice = jnp.concatenate([q_g, q_g]).reshape(1, LANES)
    return pl.pallas_call(
        _mem_attn_kernel,
        grid=(B, S // tq),
        in_specs=[pl.BlockSpec((1, tq, MEM_Q_COLS), lambda b, i: (b, i, Z_QM // MEM_Q_COLS)),
                  pl.BlockSpec((1, M, MEM_Q_COLS), lambda b, i: (b, 0, 0)),
                  pl.BlockSpec((1, M, MEM_Q_COLS), lambda b, i: (b, 0, 0)),
                  pl.BlockSpec((1, LANES), lambda b, i: (0, 0))],
        out_specs=pl.BlockSpec((1, tq, MEM_Q_COLS), lambda b, i: (b, i, 0)),
        out_shape=jax.ShapeDtypeStruct((B, S, MEM_Q_COLS), F32),
        compiler_params=_cparams("parallel", "parallel"),
        name="mem_attn_prompt",
    )(z3, mem_k, mem_v, twice)


PAIR = 2


def _swa_sample_kernel(sink_ref, q_ref, k_ref, v_ref, ck_ref, cv_ref, qg_ref, kg_ref, o_ref, kn_ref, *, L):
    n_seq = ck_ref.shape[0]
    Wb = ck_ref.shape[1]
    rows8 = SUBLANES
    nh = N_SWA_HEADS
    kn = _pair_rms(k_ref[...], kg_ref[...])
    kn_ref[...] = kn
    qg = qg_ref[...]
    R = nh * rows8
    row = lax.broadcasted_iota(I32, (R, 1), 0)
    head = row // rows8
    seq_in_pair = (row % rows8) // L
    step = (row % L).astype(F32)
    slope = jnp.exp2(-(8.0 / N_SWA_HEADS) * (head.astype(F32) + 1.0))
    sink = jnp.zeros((R, 1), F32)
    for h in range(nh):
        sink = jnp.where(head == h, sink_ref[h], sink)
    key = lax.broadcasted_iota(I32, (R, Wb), 1).astype(F32)
    dist_c = float(Wb) + step - key
    bias_c = jnp.where(dist_c < float(WINDOW), 0.0, -jnp.inf)
    col = lax.broadcasted_iota(I32, (R, rows8), 1)
    dist_n = step - (col % L).astype(F32)
    bias_n = jnp.where((dist_n >= 0.0) & ((col // L) == seq_in_pair), 0.0, -jnp.inf)
    fh8 = _first_half((rows8, LANES))
    fhR = _first_half((R, LANES))
    kv_first = head < SWA_GROUP
    pairs = range(n_seq // PAIR)
    bias_c = bias_c - slope * dist_c
    bias_n = bias_n - slope * dist_n

    def stacked_queries(pr):
        r0 = pr * rows8
        pieces = []
        for t in range(SWA_Q_COLS // LANES):
            qt = _pair_rms(q_ref[r0:r0 + rows8, t * LANES:(t + 1) * LANES], qg) * ATTN_SCALE
            qr = pltpu.roll(qt, HEAD_DIM, 1)
            kv = t // (SWA_GROUP // 2)
            for half in range(2):
                src = qt if half == kv else qr
                pieces.append(jnp.where(fh8 == (kv == 0), src, 0.0))
        return jnp.concatenate(pieces, axis=0).astype(BF16)

    qs = [stacked_queries(pr) for pr in pairs]
    s_c = [jnp.where(seq_in_pair == 0, _bdot_nt(qs[pr], ck_ref[pr * PAIR]), _bdot_nt(qs[pr], ck_ref[pr * PAIR + 1]))
           + bias_c for pr in pairs]
    s_n = [_bdot_nt(qs[pr], kn[pr * rows8:(pr + 1) * rows8]) + bias_n for pr in pairs]
    m = [jnp.maximum(jnp.maximum(jnp.max(s_c[pr], axis=-1, keepdims=True),
                                 jnp.max(s_n[pr], axis=-1, keepdims=True)), sink) for pr in pairs]
    p_c = [jnp.exp(s_c[pr] - m[pr]) for pr in pairs]
    p_n = [jnp.exp(s_n[pr] - m[pr]) for pr in pairs]
    den = [jnp.sum(p_c[pr], axis=-1, keepdims=True) + jnp.sum(p_n[pr], axis=-1, keepdims=True)
           + jnp.exp(sink - m[pr]) for pr in pairs]
    outs = [(_bdot(p_n[pr], v_ref[pr * rows8:(pr + 1) * rows8, :])
             + _bdot(jnp.where(seq_in_pair == 0, p_c[pr], 0.0), cv_ref[pr * PAIR])
             + _bdot(jnp.where(seq_in_pair == 1, p_c[pr], 0.0), cv_ref[pr * PAIR + 1])) / den[pr] for pr in pairs]
    for pr in pairs:
        r0 = pr * rows8
        o = jnp.where(fhR == kv_first, outs[pr], 0.0)
        o_r = pltpu.roll(o, HEAD_DIM, 1)
        for t in range(SWA_Q_COLS // LANES):
            kv = t // (SWA_GROUP // 2)
            halves = []
            for half in range(2):
                h = 2 * t + half
                src = o if half == kv else o_r
                halves.append(src[h * rows8:(h + 1) * rows8])
            o_ref[r0:r0 + rows8, t * LANES:(t + 1) * LANES] = jnp.where(fh8, halves[0], halves[1])


SAMPLE_SEQS = 8


def _swa_sample(z_s, cache_k, cache_v, q_g, k_g, sinks, L):
    t = z_s.shape[0]
    DB, Wb, _ = cache_k.shape
    ns = SAMPLE_SEQS
    rows = ns * L
    twice = lambda g: jnp.concatenate([g, g]).reshape(1, LANES)
    return pl.pallas_call(
        functools.partial(_swa_sample_kernel, L=L),
        grid=(DB // ns,),
        in_specs=[pl.BlockSpec(memory_space=pltpu.SMEM),
                  pl.BlockSpec((rows, SWA_Q_COLS), lambda i: (i, 0)),
                  pl.BlockSpec((rows, LANES), lambda i: (i, Z_K // LANES)),
                  pl.BlockSpec((rows, LANES), lambda i: (i, Z_V // LANES)),
                  pl.BlockSpec((ns, Wb, LANES), lambda i: (i, 0, 0)),
                  pl.BlockSpec((ns, Wb, LANES), lambda i: (i, 0, 0)),
                  pl.BlockSpec((1, LANES), lambda i: (0, 0)),
                  pl.BlockSpec((1, LANES), lambda i: (0, 0))],
        out_specs=[pl.BlockSpec((rows, SWA_Q_COLS), lambda i: (i, 0)),
                   pl.BlockSpec((rows, LANES), lambda i: (i, 0))],
        out_shape=[jax.ShapeDtypeStruct((t, SWA_Q_COLS), F32),
                   jax.ShapeDtypeStruct((t, LANES), F32)],
        compiler_params=_cparams("parallel"),
        name="swa_sample",
    )(sinks, z_s, z_s, z_s, cache_k, cache_v, twice(q_g), twice(k_g))


def _mem_sample_kernel(q_ref, mk_ref, mv_ref, qg_ref, o_ref, *, L):
    n_seq = mk_ref.shape[0]
    rows8 = SUBLANES
    qg = qg_ref[...]
    R = 2 * rows8
    row = lax.broadcasted_iota(I32, (R, 1), 0)
    seq_in_pair = (row % rows8) // L
    fh8 = _first_half((rows8, LANES))
    probs = [(pr, t) for pr in range(n_seq // PAIR) for t in range(MEM_Q_COLS // LANES)]
    cols = lambda t: slice(t * LANES, (t + 1) * LANES)

    def stacked_queries(pr, t):
        qt = _pair_rms(q_ref[pr * rows8:(pr + 1) * rows8, cols(t)], qg) * ATTN_SCALE
        return jnp.concatenate([jnp.where(fh8, qt, 0.0), jnp.where(fh8, 0.0, qt)], axis=0).astype(BF16)

    qs = [stacked_queries(pr, t) for pr, t in probs]
    ss = [jnp.where(seq_in_pair == 0, _bdot_nt(q, mk_ref[pr * PAIR, :, cols(t)]),
                    _bdot_nt(q, mk_ref[pr * PAIR + 1, :, cols(t)])) for q, (pr, t) in zip(qs, probs)]
    ps = [jnp.exp(s - jnp.max(s, axis=-1, keepdims=True)) for s in ss]
    outs = [(_bdot(jnp.where(seq_in_pair == 0, p, 0.0), mv_ref[pr * PAIR, :, cols(t)])
             + _bdot(jnp.where(seq_in_pair == 1, p, 0.0), mv_ref[pr * PAIR + 1, :, cols(t)]))
            / jnp.sum(p, axis=-1, keepdims=True) for p, (pr, t) in zip(ps, probs)]
    for o, (pr, t) in zip(outs, probs):
        o_ref[pr * rows8:(pr + 1) * rows8, cols(t)] = jnp.where(fh8, o[:rows8], o[rows8:])


def _mem_attn_sample(z_s, mem_k, mem_v, q_g, L):
    t = z_s.shape[0]
    DB, M, _ = mem_k.shape
    ns = SAMPLE_SEQS
    rows = ns * L
    twice = jnp.concatenate([q_g, q_g]).reshape(1, LANES)
    return pl.pallas_call(
        functools.partial(_mem_sample_kernel, L=L),
        grid=(DB // ns,),
        in_specs=[pl.BlockSpec((rows, MEM_Q_COLS), lambda i: (i, Z_QM // MEM_Q_COLS)),
                  pl.BlockSpec((ns, M, MEM_Q_COLS), lambda i: (i, 0, 0)),
                  pl.BlockSpec((ns, M, MEM_Q_COLS), lambda i: (i, 0, 0)),
                  pl.BlockSpec((1, LANES), lambda i: (0, 0))],
        out_specs=pl.BlockSpec((rows, MEM_Q_COLS), lambda i: (i, 0)),
        out_shape=jax.ShapeDtypeStruct((t, MEM_Q_COLS), F32),
        compiler_params=_cparams("parallel"),
        name="mem_attn_sample",
    )(z_s, mem_k, mem_v, twice)


def _gdn_sample_kernel(uq_ref, uk_ref, uv_ref, bq_ref, bk_ref, bv_ref, wq_ref, wk_ref, wv_ref,
                       ab_ref, gate_ref, alog_ref, dtb_ref, ng_ref, s_in_ref, o_ref, s_ref, kq_ref):
    h = pl.program_id(0)
    L = uq_ref.shape[0]
    nbuf = bq_ref.shape[0]
    DK = GDN_DK

    def conv(u_ref, b_ref, w_ref, t):
        up = [b_ref[i] for i in range(nbuf)] + [u_ref[i] for i in range(L)]
        y = up[t] * w_ref[0]
        for i in range(1, GDN_CONV):
            y = y + up[t + i] * w_ref[i]
        return y * jax.nn.sigmoid(y)

    s_ref[...] = s_in_ref[...]
    ng = ng_ref[...]
    hsel = lax.broadcasted_iota(I32, (SUBLANES, 1), 0)
    pick = lambda m, r: jnp.sum(jnp.where(hsel == r, m, 0.0), axis=0, keepdims=True)
    alog = pick(alog_ref[...], h)
    dtb = pick(dtb_ref[...], h)
    for t in range(L):
        q = conv(uq_ref, bq_ref, wq_ref, t)
        k = conv(uk_ref, bk_ref, wk_ref, t)
        v = conv(uv_ref, bv_ref, wv_ref, t)
        q = q * lax.rsqrt(jnp.sum(q * q, axis=0, keepdims=True) + EPS) * (GDN_DK ** -0.5)
        k = k * lax.rsqrt(jnp.sum(k * k, axis=0, keepdims=True) + EPS)
        ab = ab_ref[t]
        a = pick(ab, h)
        bb = pick(ab, h + N_GDN_HEADS)
        decay = jnp.exp(-jnp.exp(alog) * _softplus(a + dtb))
        beta = jax.nn.sigmoid(bb)
        kq_ref[0] = k
        kq_ref[1] = q

        def decay_and_project(dk, acc):
            s = s_ref[0, dk] * decay
            s_ref[0, dk] = s
            return acc + s * kq_ref[0, pl.ds(dk, 1), :]

        sk = lax.fori_loop(0, DK, decay_and_project, jnp.zeros_like(v), unroll=8)
        u = beta * (v - sk)

        def update_and_read(dk, acc):
            s = s_ref[0, dk] + kq_ref[0, pl.ds(dk, 1), :] * u
            s_ref[0, dk] = s
            return acc + s * kq_ref[1, pl.ds(dk, 1), :]

        o = lax.fori_loop(0, DK, update_and_read, jnp.zeros_like(v), unroll=8)
        o = o * lax.rsqrt(jnp.mean(o * o, axis=0, keepdims=True) + EPS) * ng
        g = gate_ref[t]
        o_ref[t] = o * (g * jax.nn.sigmoid(g))


def _gdn_sample(z_s, conv_buf, state, conv_w, a_log, dt_bias, norm_g, DB, L):
    H = N_GDN_HEADS
    z3 = z_s.reshape(DB, L, Z_COLS)
    u_t = jnp.transpose(z3[:, :, Z_GDN:Z_GATE], (1, 2, 0))
    gate_t = jnp.transpose(z3[:, :, Z_GATE:Z_QM], (1, 2, 0))
    ab_t = jnp.transpose(z3[:, :, Z_AB:Z_AB + SUBLANES], (1, 2, 0))
    buf_t = jnp.transpose(conv_buf, (1, 2, 0))
    s_t = jnp.transpose(state, (1, 2, 3, 0))
    w_col = conv_w.reshape(GDN_CONV, GDN_CONV_CH, 1)
    col8 = lambda a: jnp.pad(a, (0, SUBLANES - a.shape[0])).reshape(SUBLANES, 1)
    nbuf = conv_buf.shape[1]
    part = lambda n, j: pl.BlockSpec((n, GDN_DK, DB), lambda h: (0, j * H + h, 0))
    wpart = lambda j: pl.BlockSpec((GDN_CONV, GDN_DK, 1), lambda h: (0, j * H + h, 0))
    whole = lambda shape: pl.BlockSpec(shape, lambda h: (0,) * len(shape))
    o_t, s_new = pl.pallas_call(
        _gdn_sample_kernel,
        grid=(H,),
        in_specs=[part(L, 0), part(L, 1), part(L, 2), part(nbuf, 0), part(nbuf, 1), part(nbuf, 2),
                  wpart(0), wpart(1), wpart(2),
                  whole((L, SUBLANES, DB)),
                  pl.BlockSpec((L, GDN_DV, DB), lambda h: (0, h, 0)),
                  whole((SUBLANES, 1)), whole((SUBLANES, 1)), whole((GDN_DV, 1)),
                  pl.BlockSpec((1, GDN_DK, GDN_DV, DB), lambda h: (h, 0, 0, 0))],
        out_specs=[pl.BlockSpec((L, GDN_DV, DB), lambda h: (0, h, 0)),
                   pl.BlockSpec((1, GDN_DK, GDN_DV, DB), lambda h: (h, 0, 0, 0))],
        out_shape=[jax.ShapeDtypeStruct((L, H * GDN_DV, DB), F32),
                   jax.ShapeDtypeStruct((H, GDN_DK, GDN_DV, DB), F32)],
        scratch_shapes=[pltpu.VMEM((2, GDN_DK, DB), F32)],
        compiler_params=_cparams("parallel"),
        name="gdn_sample",
    )(u_t, u_t, u_t, buf_t, buf_t, buf_t, w_col, w_col, w_col, ab_t, gate_t,
      col8(a_log), col8(dt_bias), norm_g.reshape(GDN_DV, 1), s_t)
    o = jnp.transpose(o_t, (2, 0, 1)).reshape(DB * L, H * GDN_DV)
    return o, jnp.transpose(s_new, (3, 0, 1, 2))


def kernel(x_prompt, x_sample, cache_swa_k, cache_swa_v, state_gdn, state_gdn_conv, cache_mem_k, cache_mem_v,
           mem_prompt, ln1_g, w_in, swa_q_norm, swa_k_norm, swa_sinks, gdn_conv_w, gdn_a_log, gdn_dt_bias,
           gdn_norm_g, mem_ln_g, w_mem_kv, mem_q_norm, mem_k_norm, w_o, ln2_g, router_w, router_b,
           moe_w1, moe_b1, moe_w2, moe_b2):
    B, S, D = x_prompt.shape
    DB, DL, _ = x_sample.shape
    depth = ln1_g.shape[0]
    assert depth == 1
    l = 0
    tp, ts = B * S, DB * DL
    t_all = tp + ts
    n_ab = 2 * N_GDN_HEADS
    c_ab = SWA_Q_COLS + 2 * SWA_KV_COLS + GDN_CONV_CH
    w = w_in[l]
    w_z = jnp.concatenate([w[:, :c_ab], w[:, c_ab + n_ab:], w[:, c_ab:c_ab + n_ab],
                           jnp.zeros((D, LANES - n_ab), F32)], axis=1).astype(BF16)
    rw = jnp.pad(router_w[l], ((0, 0), (0, LANES - N_EXPERTS)))
    rw_hi = rw.astype(BF16)
    rw_lo = (rw - rw_hi.astype(F32)).astype(BF16)
    rb = jnp.pad(router_b[l], (0, LANES - N_EXPERTS)).reshape(1, LANES)
    wo = w_o[l].astype(BF16)
    w1 = moe_w1[l]
    w2 = moe_w2[l]
    b1 = moe_b1[l].reshape(N_EXPERTS, 1, -1)
    b2 = moe_b2[l].reshape(N_EXPERTS, 1, -1)
    p = {'q_norm': swa_q_norm[l], 'k_norm': swa_k_norm[l], 'sinks': swa_sinks[l], 'conv_w': gdn_conv_w[l],
         'a_log': gdn_a_log[l], 'dt_bias': gdn_dt_bias[l], 'gdn_norm': gdn_norm_g[l], 'mem_q_norm': mem_q_norm[l]}

    xp = x_prompt.reshape(tp, D)
    xs = x_sample.reshape(ts, D)
    z_p = _inproj(xp, ln1_g[l], w_z)
    z_s = _inproj(xs, ln1_g[l], w_z)

    M = mem_prompt.shape[1]
    z_p3 = z_p.reshape(B, S, Z_COLS)
    mk2, mv2 = _mem_kv(mem_prompt.reshape(B * M, D), mem_ln_g[l], w_mem_kv[l], mem_k_norm[l])
    mk = mk2.reshape(B, M, N_MEM_HEADS, HEAD_DIM)
    mv = mv2.reshape(B, M, N_MEM_HEADS, HEAD_DIM)
    os_p, pk = _swa_prompt(z_p3, p['q_norm'], p['k_norm'], p['sinks'])
    od_p, ps = _gdn_prompt(z_p3, p['conv_w'], p['a_log'], p['dt_bias'], p['gdn_norm'])
    om_p = _mem_attn_prompt(z_p3, mk2.reshape(B, M, MEM_Q_COLS), mv2.reshape(B, M, MEM_Q_COLS), p['mem_q_norm'])
    os_p, od_p, om_p = os_p.reshape(tp, -1), od_p.reshape(tp, -1), om_p.reshape(tp, -1)
    pk = pk.reshape(B, WINDOW, N_SWA_KV, HEAD_DIM)
    pv = z_p3[:, S - WINDOW:, Z_V:Z_GDN].reshape(B, WINDOW, N_SWA_KV, HEAD_DIM)
    pc = z_p3[:, S - (GDN_CONV - 1):, Z_GDN:Z_GATE]
    Wb = cache_swa_k.shape[2]
    ck = cache_swa_k[l].reshape(DB, Wb, SWA_KV_COLS)
    cv = cache_swa_v[l].reshape(DB, Wb, SWA_KV_COLS)
    os_s, k_new = _swa_sample(z_s, ck, cv, p['q_norm'], p['k_norm'], p['sinks'], DL)
    od_s, ss = _gdn_sample(z_s, state_gdn_conv[l], state_gdn[l], p['conv_w'], p['a_log'], p['dt_bias'],
                           p['gdn_norm'], DB, DL)
    om_s = _mem_attn_sample(z_s, cache_mem_k[l].reshape(DB, -1, MEM_Q_COLS),
                            cache_mem_v[l].reshape(DB, -1, MEM_Q_COLS), p['mem_q_norm'], DL)
    z_s3 = z_s.reshape(DB, DL, Z_COLS)
    sk = jnp.concatenate([ck, k_new.reshape(DB, DL, SWA_KV_COLS)], axis=1)[:, DL:]
    sv = jnp.concatenate([cv, z_s3[:, :, Z_V:Z_GDN]], axis=1)[:, DL:]
    sk = sk.reshape(DB, Wb, N_SWA_KV, HEAD_DIM)
    sv = sv.reshape(DB, Wb, N_SWA_KV, HEAD_DIM)
    sc = jnp.concatenate([state_gdn_conv[l], z_s3[:, :, Z_GDN:Z_GATE]], axis=1)[:, DL:]

    h_p, hn_all, lg_all = _outproj(xp, os_p, od_p, om_p, wo, ln2_g[l], rw_hi, rw_lo, rb, t_all, 0)
    h_s, hn_all, lg_all = _outproj(xs, os_s, od_s, om_s, wo, ln2_g[l], rw_hi, rw_lo, rb, t_all, tp,
                                   prev=(hn_all, lg_all))
    y_p, y_s = _moe(hn_all, lg_all, [h_p, h_s], w1, b1, w2, b2)
    return (y_p.reshape(B, S, D), y_s.reshape(DB, DL, D), pk[None], pv[None], ps[None], pc[None], mk[None],
            mv[None], sk[None], sv[None], ss[None], sc[None])
```

```python
import functools

import jax
import jax.numpy as jnp
from jax import lax
from jax.experimental import pallas as pl
from jax.experimental.pallas import tpu as pltpu

F32 = jnp.float32
BF16 = jnp.bfloat16
I32 = jnp.int32

HEAD_DIM = 64
N_SWA_HEADS = 8
N_SWA_KV = 2
SWA_GROUP = N_SWA_HEADS // N_SWA_KV
WINDOW = 128
N_GDN_HEADS = 4
GDN_DK = 64
GDN_DV = 64
GDN_CONV = 4
GDN_CHUNK = 64
N_MEM_HEADS = 4
N_EXPERTS = 32
TOP_K = 4
SWIGLU_ALPHA = 1.702
SWIGLU_LIMIT = 7.0
EPS = 1e-6
ATTN_SCALE = HEAD_DIM ** -0.5

SWA_Q_COLS = N_SWA_HEADS * HEAD_DIM
SWA_KV_COLS = N_SWA_KV * HEAD_DIM
GDN_QK_COLS = N_GDN_HEADS * GDN_DK
GDN_V_COLS = N_GDN_HEADS * GDN_DV
GDN_CONV_CH = 2 * GDN_QK_COLS + GDN_V_COLS
MEM_Q_COLS = N_MEM_HEADS * HEAD_DIM

LANES = 128
SUBLANES = 8
VMEM_LIMIT = 56 * 1024 * 1024

Z_Q = 0
Z_K = Z_Q + SWA_Q_COLS
Z_V = Z_K + SWA_KV_COLS
Z_GDN = Z_V + SWA_KV_COLS
Z_GATE = Z_GDN + GDN_CONV_CH
Z_QM = Z_GATE + GDN_V_COLS
Z_AB = Z_QM + MEM_Q_COLS
Z_COLS = Z_AB + LANES

ROW_TILE = 512
MOE_TILE = 512
MOE_BLK = 512
PERM_CHUNK = 256
RUN_ALIGN = 16
BLK_ROWS = -(-(MOE_BLK * TOP_K + N_EXPERTS * (RUN_ALIGN - 1)) // PERM_CHUNK) * PERM_CHUNK


def _cparams(*sem):
    return pltpu.CompilerParams(dimension_semantics=sem, vmem_limit_bytes=VMEM_LIMIT)


def _bdot(a, b):
    return jnp.dot(a.astype(BF16), b.astype(BF16), preferred_element_type=F32)


def _bdot_nt(a, b):
    return lax.dot_general(a.astype(BF16), b.astype(BF16), (((1,), (1,)), ((), ())),
                           preferred_element_type=F32)


def _bdot_tn(a, b):
    return lax.dot_general(a.astype(BF16), b.astype(BF16), (((0,), (0,)), ((), ())),
                           preferred_element_type=F32)


def _split2(x):
    hi = x.astype(BF16)
    lo = (x - hi.astype(F32)).astype(BF16)
    return hi, lo


def _split3(x):
    hi = x.astype(BF16)
    r = x - hi.astype(F32)
    mid = r.astype(BF16)
    lo = (r - mid.astype(F32)).astype(BF16)
    return hi, mid, lo


def _rms_rows(x, g):
    ms = jnp.mean(x * x, axis=-1, keepdims=True)
    return x * lax.rsqrt(ms + EPS) * g


def _inproj_kernel(x_ref, g_ref, w_ref, z_ref):
    n = _rms_rows(x_ref[...], g_ref[...])
    z_ref[...] = jnp.dot(n.astype(BF16), w_ref[...], preferred_element_type=F32)


def _inproj(x2d, ln_g, w_z):
    t, d = x2d.shape
    tm = min(ROW_TILE, t)
    return pl.pallas_call(
        _inproj_kernel,
        grid=(t // tm,),
        in_specs=[pl.BlockSpec((tm, d), lambda i: (i, 0)),
                  pl.BlockSpec((1, d), lambda i: (0, 0)),
                  pl.BlockSpec((d, Z_COLS), lambda i: (0, 0))],
        out_specs=pl.BlockSpec((tm, Z_COLS), lambda i: (i, 0)),
        out_shape=jax.ShapeDtypeStruct((t, Z_COLS), F32),
        compiler_params=_cparams("parallel"),
        name="inproj",
    )(x2d, ln_g.reshape(1, d), w_z)


def _outproj_kernel(x_ref, os_ref, od_ref, om_ref, wo_ref, g_ref, rwh_ref, rwl_ref, rb_ref,
                    *refs, n_own):
    h_ref, hn_ref, lg_ref = refs[-3:]
    i = pl.program_id(0)

    @pl.when(i < n_own)
    def _():
        n_s = os_ref.shape[1]
        n_d = od_ref.shape[1]
        h = x_ref[...]
        h = h + jnp.dot(os_ref[...].astype(BF16), wo_ref[0:n_s, :], preferred_element_type=F32)
        h = h + jnp.dot(od_ref[...].astype(BF16), wo_ref[n_s:n_s + n_d, :], preferred_element_type=F32)
        h = h + jnp.dot(om_ref[...].astype(BF16), wo_ref[n_s + n_d:, :], preferred_element_type=F32)
        h_ref[...] = h
        hn = _rms_rows(h, g_ref[...])
        hn_ref[...] = hn.astype(BF16)
        hi, lo = _split2(hn)
        lg = (jnp.dot(hi, rwh_ref[...], preferred_element_type=F32)
              + jnp.dot(lo, rwh_ref[...], preferred_element_type=F32)
              + jnp.dot(hi, rwl_ref[...], preferred_element_type=F32))
        lg_ref[...] = lg + rb_ref[...]

    @pl.when(i >= n_own)
    def _():
        hn_ref[...] = jnp.zeros_like(hn_ref)
        lg_ref[...] = jnp.zeros_like(lg_ref)


def _outproj(x2d, o_s, o_d, o_m, w_o, ln_g, rw_hi, rw_lo, rb, t_all, row0, prev=None):
    t, d = x2d.shape
    tm = min(ROW_TILE, t)
    blk0 = row0 // tm
    n_own = t // tm
    n_steps = n_own if prev is not None else t_all // tm
    row = lambda i: (jnp.minimum(i, n_own - 1), 0)
    row_off = lambda i: (i + blk0, 0)
    const = lambda i: (0, 0)
    in_specs = [pl.BlockSpec((tm, d), row),
                pl.BlockSpec((tm, o_s.shape[1]), row),
                pl.BlockSpec((tm, o_d.shape[1]), row),
                pl.BlockSpec((tm, o_m.shape[1]), row),
                pl.BlockSpec((d, d), const),
                pl.BlockSpec((1, d), const),
                pl.BlockSpec((d, LANES), const),
                pl.BlockSpec((d, LANES), const),
                pl.BlockSpec((1, LANES), const)]
    args = [x2d, o_s, o_d, o_m, w_o, ln_g.reshape(1, d), rw_hi, rw_lo, rb]
    aliases = {}
    if prev is not None:
        in_specs += [pl.BlockSpec(memory_space=pl.ANY), pl.BlockSpec(memory_space=pl.ANY)]
        aliases = {len(args): 1, len(args) + 1: 2}
        args += list(prev)
    return pl.pallas_call(
        functools.partial(_outproj_kernel, n_own=n_own),
        grid=(n_steps,),
        in_specs=in_specs,
        out_specs=[pl.BlockSpec((tm, d), row),
                   pl.BlockSpec((tm, d), row_off),
                   pl.BlockSpec((tm, LANES), row_off)],
        out_shape=[jax.ShapeDtypeStruct((t, d), F32),
                   jax.ShapeDtypeStruct((t_all, d), BF16),
                   jax.ShapeDtypeStruct((t_all, LANES), F32)],
        input_output_aliases=aliases,
        compiler_params=_cparams("arbitrary"),
        name="outproj_router",
    )(*args)


def _route_kernel(lg_ref, pos_ref, post_ref, g_ref, cnt_ref):
    tm = lg_ref.shape[0]
    lane = lax.broadcasted_iota(I32, (tm, LANES), 1).astype(F32)
    l = jnp.where(lane < N_EXPERTS, lg_ref[...], -jnp.inf)
    vals, idxs = [], []
    for _k in range(TOP_K):
        m = jnp.max(l, axis=-1, keepdims=True)
        idx = jnp.min(jnp.where(l == m, lane, float(LANES)), axis=-1, keepdims=True)
        l = jnp.where(lane == idx, -jnp.inf, l)
        vals.append(m)
        idxs.append(idx)
    ex = [jnp.exp(v - vals[0]) for v in vals]
    den = ex[0] + ex[1] + ex[2] + ex[3]
    member = jnp.zeros((tm, LANES), F32)
    for idx in idxs:
        member = member + jnp.where(lane == idx, 1.0, 0.0)
    ri = lax.broadcasted_iota(I32, (tm, tm), 0)
    ci = lax.broadcasted_iota(I32, (tm, tm), 1)
    strict = jnp.where(ci < ri, 1.0, 0.0).astype(BF16)
    prefix = jnp.dot(strict, member.astype(BF16), preferred_element_type=F32)
    cnt = jnp.sum(member, axis=0, keepdims=True)
    cpad = jnp.ceil(cnt * (1.0 / RUN_ALIGN)) * float(RUN_ALIGN)
    c_hi = jnp.floor(cpad * (1.0 / 256.0))
    c_lo = cpad - 256.0 * c_hi
    ej = lax.broadcasted_iota(I32, (LANES, LANES), 0)
    ee = lax.broadcasted_iota(I32, (LANES, LANES), 1)
    before = jnp.where(ej < ee, 1.0, 0.0).astype(BF16)
    bcast = lambda v: jnp.broadcast_to(v, (SUBLANES, LANES)).astype(BF16)
    off = (256.0 * jnp.dot(bcast(c_hi), before, preferred_element_type=F32)
           + jnp.dot(bcast(c_lo), before, preferred_element_type=F32))[0:1]
    where_in_run = prefix + off
    p_out = jnp.zeros((tm, LANES), F32)
    g_out = jnp.zeros((tm, LANES), F32)
    for k in range(TOP_K):
        pos = jnp.sum(jnp.where(lane == idxs[k], where_in_run, 0.0), axis=-1, keepdims=True)
        p_out = jnp.where(lane == float(k), pos, p_out)
        g_out = jnp.where(lane == float(k), ex[k] / den, g_out)
    pos_ref[...] = p_out[:, :TOP_K]
    post_ref[...] = p_out.T[:SUBLANES, :]
    g_ref[...] = g_out[:, :TOP_K]
    cnt_ref[0] = cnt


def _route(logits):
    t = logits.shape[0]
    tm = MOE_BLK
    nb = t // tm
    return pl.pallas_call(
        _route_kernel,
        grid=(nb,),
        in_specs=[pl.BlockSpec((tm, LANES), lambda i: (i, 0))],
        out_specs=[pl.BlockSpec((tm, TOP_K), lambda i: (i, 0)),
                   pl.BlockSpec((SUBLANES, tm), lambda i: (0, i)),
                   pl.BlockSpec((tm, TOP_K), lambda i: (i, 0)),
                   pl.BlockSpec((1, 1, LANES), lambda i: (i, 0, 0))],
        out_shape=[jax.ShapeDtypeStruct((t, TOP_K), F32),
                   jax.ShapeDtypeStruct((SUBLANES, t), F32),
                   jax.ShapeDtypeStruct((t, TOP_K), F32),
                   jax.ShapeDtypeStruct((nb, 1, LANES), F32)],
        compiler_params=_cparams("parallel"),
        name="route",
    )(logits)


def _run_copies(n, max_rows, src_ref, src0, dst_ref, dst0, sem, wait):
    pos = 0
    bit = max_rows
    while bit >= RUN_ALIGN:
        take = (n & bit) != 0

        def go(pos=pos, bit=bit):
            cp = pltpu.make_async_copy(src_ref.at[pl.ds(pl.multiple_of(src0 + pos, RUN_ALIGN), bit)],
                                       dst_ref.at[pl.ds(pl.multiple_of(dst0 + pos, RUN_ALIGN), bit)], sem)
            cp.wait() if wait else cp.start()

        pl.when(take)(go)
        pos = pos + jnp.where(take, bit, 0)
        bit //= 2


RUN_SIZES = tuple(MOE_BLK >> i for i in range((MOE_BLK // RUN_ALIGN).bit_length()))


def _piece_copies(b, cnt_ref, loc_ref, glob_ref, local_ref, global_hbm, sem, to_global, wait):
    for c, rows in enumerate(RUN_SIZES):
        base = b * len(RUN_SIZES) + c

        def body(s, carry, rows=rows, base=base):
            j = base * N_EXPERTS + s
            loc = local_ref.at[pl.ds(pl.multiple_of(loc_ref[j], RUN_ALIGN), rows)]
            glob = global_hbm.at[pl.ds(pl.multiple_of(glob_ref[j], RUN_ALIGN), rows)]
            cp = pltpu.make_async_copy(loc, glob, sem) if to_global else pltpu.make_async_copy(glob, loc, sem)
            cp.wait() if wait else cp.start()
            return carry

        lax.fori_loop(0, cnt_ref[base], body, 0)


def _dispatch_kernel(cnt_ref, loc_ref, glob_ref, estart_ref, elen_ref, nused_ref,
                     hn_ref, post_ref, xs_hbm, buf_ref, zero_ref, sem, zsem):
    b = pl.program_id(0)
    nb = pl.num_programs(0)
    slot = b % 2
    tm = hn_ref.shape[0]
    x = hn_ref[...].astype(BF16)
    post = post_ref[...]
    P = PERM_CHUNK
    for c in range(BLK_ROWS // P):
        r = (lax.broadcasted_iota(I32, (P, tm), 0) + c * P).astype(F32)
        sel = jnp.zeros((P, tm), F32)
        for k in range(TOP_K):
            sel = jnp.where(r == post[k:k + 1, :], 1.0, sel)
        buf_ref[slot, c * P:(c + 1) * P, :] = jnp.dot(sel.astype(BF16), x,
                                                      preferred_element_type=F32).astype(BF16)

    def runs(blk, s, wait):
        _piece_copies(blk, cnt_ref, loc_ref, glob_ref, buf_ref.at[s], xs_hbm, sem.at[s], True, wait)

    runs(b, slot, False)

    @pl.when(b == 0)
    def _():
        zero_ref[...] = jnp.zeros_like(zero_ref)

        def tail(wait):
            def body(e, c):
                n = (MOE_TILE - elen_ref[e] % MOE_TILE) % MOE_TILE
                _run_copies(n, MOE_TILE // 2, zero_ref, 0, xs_hbm, estart_ref[e] + elen_ref[e], zsem, wait)
                return c
            lax.fori_loop(0, N_EXPERTS, body, 0)

            def free_tile(ti, c):
                for half in range(2):
                    _run_copies(MOE_TILE // 2, MOE_TILE // 2, zero_ref, 0, xs_hbm,
                                ti * MOE_TILE + half * (MOE_TILE // 2), zsem, wait)
                return c
            lax.fori_loop(nused_ref[0], xs_hbm.shape[0] // MOE_TILE, free_tile, 0)

        tail(False)
        tail(True)

    pl.when(b > 0)(lambda: runs(b - 1, 1 - slot, True))
    pl.when(b == nb - 1)(lambda: runs(b, slot, True))


def _dispatch(hn, post, piece_cnt, piece_loc, piece_glob, e_start, e_len, n_used, n_rows):
    t, d = hn.shape
    grid_spec = pltpu.PrefetchScalarGridSpec(
        num_scalar_prefetch=6,
        grid=(t // MOE_BLK,),
        in_specs=[pl.BlockSpec((MOE_BLK, d), lambda i, *_: (i, 0)),
                  pl.BlockSpec((SUBLANES, MOE_BLK), lambda i, *_: (0, i))],
        out_specs=pl.BlockSpec(memory_space=pl.ANY),
        scratch_shapes=[pltpu.VMEM((2, BLK_ROWS, d), BF16),
                        pltpu.VMEM((MOE_TILE, d), BF16),
                        pltpu.SemaphoreType.DMA((2,)),
                        pltpu.SemaphoreType.DMA(())],
    )
    return pl.pallas_call(
        _dispatch_kernel,
        grid_spec=grid_spec,
        out_shape=jax.ShapeDtypeStruct((n_rows, d), BF16),
        compiler_params=_cparams("arbitrary"),
        name="dispatch",
    )(piece_cnt, piece_loc, piece_glob, e_start, e_len, n_used, hn, post)


def _expert_kernel(te_ref, nu_ref, x_ref, w1_ref, b1_ref, w2_ref, b2_ref, y_ref, w1b_ref, w2b_ref):
    i = pl.program_id(0)
    live = i < nu_ref[0]

    @pl.when(live & ((i == 0) | (te_ref[i] != te_ref[jnp.maximum(i - 1, 0)])))
    def _():
        w1b_ref[...] = w1_ref[0].astype(BF16)
        w2b_ref[...] = w2_ref[0].astype(BF16)

    @pl.when(live)
    def _():
        f = w2_ref.shape[1]
        h = jnp.dot(x_ref[...], w1b_ref[...], preferred_element_type=F32) + b1_ref[0]
        glu = jnp.minimum(h[:, :f], SWIGLU_LIMIT)
        lin = jnp.clip(h[:, f:], -SWIGLU_LIMIT, SWIGLU_LIMIT)
        act = glu * jax.nn.sigmoid(SWIGLU_ALPHA * glu) * (lin + 1.0)
        y = jnp.dot(act.astype(BF16), w2b_ref[...], preferred_element_type=F32) + b2_ref[0]
        y_ref[...] = y.astype(BF16)

    @pl.when(i >= nu_ref[0])
    def _():
        y_ref[...] = jnp.zeros_like(y_ref)


def _experts(xs, tile_expert, n_used, w1, b1, w2, b2):
    n_rows, d = xs.shape
    f2 = w1.shape[2]
    f = w2.shape[1]
    n_tiles = n_rows // MOE_TILE
    live = lambda i, te, nu: (jnp.minimum(i, nu[0] - 1), 0)
    every = lambda i, te, nu: (i, 0)
    wsel = lambda i, te, nu: (te[i], 0, 0)
    grid_spec = pltpu.PrefetchScalarGridSpec(
        num_scalar_prefetch=2,
        grid=(n_tiles,),
        in_specs=[pl.BlockSpec((MOE_TILE, d), live),
                  pl.BlockSpec((1, d, f2), wsel),
                  pl.BlockSpec((1, 1, f2), wsel),
                  pl.BlockSpec((1, f, d), wsel),
                  pl.BlockSpec((1, 1, d), wsel)],
        out_specs=pl.BlockSpec((MOE_TILE, d), every),
        scratch_shapes=[pltpu.VMEM((d, f2), BF16), pltpu.VMEM((f, d), BF16)],
    )
    return pl.pallas_call(
        _expert_kernel,
        grid_spec=grid_spec,
        out_shape=jax.ShapeDtypeStruct((n_rows, d), BF16),
        compiler_params=_cparams("arbitrary"),
        name="experts",
    )(tile_expert, n_used, xs, w1, b1, w2, b2)


def _combine_kernel(cnt_ref, loc_ref, glob_ref, h_ref, pos_ref, g_ref, yb_hbm, y_ref, buf_ref, sem, *, blk0):
    i = pl.program_id(0)
    n_steps = pl.num_programs(0)
    b = i + blk0
    slot = i % 2
    tm, d = h_ref.shape

    def runs(blk, s, wait):
        _piece_copies(blk, cnt_ref, loc_ref, glob_ref, buf_ref.at[s], yb_hbm, sem.at[s], False, wait)

    @pl.when(i == 0)
    def _():
        buf_ref[...] = jnp.zeros_like(buf_ref)
        runs(b, slot, False)

    pl.when(i + 1 < n_steps)(lambda: runs(b + 1, 1 - slot, False))
    runs(b, slot, True)
    pos = pos_ref[...]
    g = g_ref[...]
    y = h_ref[...]
    P = PERM_CHUNK
    for c in range(BLK_ROWS // P):
        col = (lax.broadcasted_iota(I32, (tm, P), 1) + c * P).astype(F32)
        wgt = jnp.zeros((tm, P), F32)
        for k in range(TOP_K):
            wgt = jnp.where(col == pos[:, k:k + 1], g[:, k:k + 1], wgt)
        y = y + jnp.dot(wgt.astype(BF16), buf_ref[slot, c * P:(c + 1) * P, :], preferred_element_type=F32)
    y_ref[...] = y


def _combine(h, pos, gates, piece_cnt, piece_loc, piece_glob, yb, blk0):
    t, d = h.shape
    grid_spec = pltpu.PrefetchScalarGridSpec(
        num_scalar_prefetch=3,
        grid=(t // MOE_BLK,),
        in_specs=[pl.BlockSpec((MOE_BLK, d), lambda i, *_: (i, 0)),
                  pl.BlockSpec((MOE_BLK, TOP_K), lambda i, *_: (i + blk0, 0)),
                  pl.BlockSpec((MOE_BLK, TOP_K), lambda i, *_: (i + blk0, 0)),
                  pl.BlockSpec(memory_space=pl.ANY)],
        out_specs=pl.BlockSpec((MOE_BLK, d), lambda i, *_: (i, 0)),
        scratch_shapes=[pltpu.VMEM((2, BLK_ROWS, d), BF16),
                        pltpu.SemaphoreType.DMA((2,))],
    )
    return pl.pallas_call(
        functools.partial(_combine_kernel, blk0=blk0),
        grid_spec=grid_spec,
        out_shape=jax.ShapeDtypeStruct((t, d), F32),
        compiler_params=_cparams("arbitrary"),
        name="combine",
    )(piece_cnt, piece_loc, piece_glob, h, pos, gates, yb)


def _moe(hn_all, logits_all, h_parts, w1, b1, w2, b2):
    t_all = hn_all.shape[0]
    nb = t_all // MOE_BLK
    pos, post, gates, counts_f = _route(logits_all)
    cnt = counts_f.reshape(nb, LANES)[:, :N_EXPERTS].astype(I32)
    seg_len = (cnt + RUN_ALIGN - 1) // RUN_ALIGN * RUN_ALIGN
    before_e = jnp.arange(N_EXPERTS)[:, None] < jnp.arange(N_EXPERTS)[None, :]
    before_b = jnp.arange(nb)[None, :] < jnp.arange(nb)[:, None]
    seg_off = jnp.sum(jnp.where(before_e[None], seg_len[:, :, None], 0), axis=1)
    e_len = jnp.sum(seg_len, axis=0)
    e_tiles = (e_len + MOE_TILE - 1) // MOE_TILE
    tile_start = jnp.sum(jnp.where(before_e, e_tiles[:, None], 0), axis=0)
    tile_end = tile_start + e_tiles
    e_start = tile_start * MOE_TILE
    seg_dst = e_start[None, :] + jnp.sum(jnp.where(before_b[:, :, None], seg_len[None], 0), axis=1)
    max_rows = t_all * TOP_K + nb * N_EXPERTS * (RUN_ALIGN - 1) + N_EXPERTS * (MOE_TILE - RUN_ALIGN)
    n_tiles = -(-max_rows // MOE_TILE)
    n_rows = n_tiles * MOE_TILE
    n_used = tile_end[-1:].astype(I32)
    tile_expert = jnp.minimum(jnp.sum(tile_end[None, :] <= jnp.arange(n_tiles, dtype=I32)[:, None], axis=1),
                              N_EXPERTS - 1).astype(I32)
    sizes = jnp.array(RUN_SIZES, I32)[None, :, None]
    n_run = seg_len[:, None, :]
    has = (n_run & sizes) != 0
    piece_at = n_run & ~(2 * sizes - 1)
    rank = jnp.sum(jnp.where(before_e[None, None], has[:, :, :, None], False), axis=2)
    slot = jnp.arange(N_EXPERTS)
    put = has[..., None] & (rank[..., None] == slot)
    listed = lambda v: jnp.sum(jnp.where(put, v[..., None], 0), axis=2)
    piece_loc = listed(seg_off[:, None, :] + piece_at)
    piece_glob = listed(seg_dst[:, None, :] + piece_at)
    piece_cnt = jnp.sum(has, axis=2)
    flat = lambda a: a.reshape(-1).astype(I32)
    tables = (flat(piece_cnt), flat(piece_loc), flat(piece_glob))
    xs = _dispatch(hn_all, post, *tables, flat(e_start), flat(e_len), n_used, n_rows)
    yb = _experts(xs, tile_expert, n_used, w1, b1, w2, b2)
    outs = []
    row = 0
    for h in h_parts:
        outs.append(_combine(h, pos, gates, *tables, yb, row // MOE_BLK))
        row += h.shape[0]
    return outs


GDN_ROWS = 4 * GDN_CHUNK
CONV_HALO = SUBLANES
NEUMANN_SPLIT = 2


def _softplus(x):
    return jnp.maximum(x, 0.0) + jnp.log1p(jnp.exp(-jnp.abs(x)))


def _gdn_prompt_kernel(u_ref, ab_ref, gate_ref, cw_ref, alog_ref, dtb_ref, ng_ref, o_ref, s_ref, ubuf_ref):
    step = pl.program_id(0)
    NB = u_ref.shape[0]
    R = GDN_ROWS
    C = GDN_CHUNK
    NC = R // C

    @pl.when(step == 0)
    def _():
        ubuf_ref[:, 0:CONV_HALO, :] = jnp.zeros((NB, CONV_HALO, ubuf_ref.shape[2]), F32)
        s_ref[...] = jnp.zeros_like(s_ref)

    ri = lax.broadcasted_iota(I32, (R, R), 0)
    ci = lax.broadcasted_iota(I32, (R, R), 1)
    shift = C.bit_length() - 1
    same = lax.shift_right_logical(ri, shift) == lax.shift_right_logical(ci, shift)
    incl = same & (ci <= ri)
    strict = same & (ci < ri)
    tri = jnp.where(incl, 1.0, 0.0).astype(BF16)
    blk = jnp.where(same, 1.0, 0.0).astype(BF16)
    cw = cw_ref[...]
    ng = ng_ref[...]

    chains = []
    for b in range(NB):
        u = u_ref[b]
        ubuf_ref[b, CONV_HALO:CONV_HALO + R, :] = u
        y = u * cw[GDN_CONV - 1:GDN_CONV, :]
        for j in range(1, GDN_CONV):
            y = y + ubuf_ref[b, CONV_HALO - j:CONV_HALO - j + R, :] * cw[GDN_CONV - 1 - j:GDN_CONV - j, :]
        ubuf_ref[b, 0:CONV_HALO, :] = u[R - CONV_HALO:, :]
        qkv = y * jax.nn.sigmoid(y)
        ab = ab_ref[b]
        g_t = -jnp.exp(alog_ref[...]) * _softplus(ab + dtb_ref[...])
        beta_t = jax.nn.sigmoid(ab)
        parts = _split3(g_t)
        gcum = sum(jnp.dot(tri, p, preferred_element_type=F32) for p in parts)
        gtot = sum(jnp.dot(blk, p, preferred_element_type=F32) for p in parts)
        gcum_t = gcum.T
        qk_n = []
        for t in range(2 * GDN_QK_COLS // LANES):
            x = qkv[:, t * LANES:(t + 1) * LANES]
            x = x * lax.rsqrt(_pair_sumsq(x) + EPS)
            qk_n.append(x * (GDN_DK ** -0.5) if t < GDN_QK_COLS // LANES else x)
        qk_n = jnp.concatenate(qk_n, axis=-1)
        for h in range(N_GDN_HEADS):
            q = qk_n[:, h * GDN_DK:(h + 1) * GDN_DK]
            k = qk_n[:, GDN_QK_COLS + h * GDN_DK:GDN_QK_COLS + (h + 1) * GDN_DK]
            v = qkv[:, 2 * GDN_QK_COLS + h * GDN_DV:2 * GDN_QK_COLS + (h + 1) * GDN_DV]
            gc = gcum[:, h:h + 1]
            gt = gtot[:, h:h + 1]
            beta = beta_t[:, N_GDN_HEADS + h:N_GDN_HEADS + h + 1]
            decay = jnp.exp(jnp.where(incl, gc - gcum_t[h:h + 1, :], -jnp.inf))
            chains.append(dict(
                b=b, h=h,
                a=jnp.where(strict, beta * _bdot_nt(k, k) * decay, 0.0),
                qk=_bdot_nt(q, k) * decay,
                r=jnp.concatenate([v * beta, k * (beta * jnp.exp(gc))], axis=-1),
                q_dec=q * jnp.exp(gc), k_dec=k * jnp.exp(gt - gc), g_last=jnp.exp(gt)))

    dot = lambda x, y: jnp.dot(x, y, preferred_element_type=F32)
    level, j = 1, 0
    while level < C:
        last = 2 * level >= C
        for ch in chains:
            if j < NEUMANN_SPLIT:
                a_hi, a_lo = _split2(ch['a'])
                r_hi, r_lo = _split2(ch['r'])
                upd = dot(a_hi, r_hi) + dot(a_hi, r_lo) + dot(a_lo, r_hi)
            else:
                a_hi = ch['a'].astype(BF16)
                upd = dot(a_hi, ch['r'].astype(BF16))
            ch['r'] = ch['r'] - upd if level == 1 else ch['r'] + upd
            if not last:
                sq = dot(a_hi, a_hi)
                if j + 1 < NEUMANN_SPLIT:
                    sq = sq + dot(a_hi, a_lo) + dot(a_lo, a_hi)
                ch['a'] = sq
        level *= 2
        j += 1

    for ch in chains:
        ch['u'], ch['w'] = ch['r'][:, :GDN_DV], ch['r'][:, GDN_DV:]
        ch['S'] = s_ref[ch['b'], ch['h']]
        ch['k_dec_t'] = ch['k_dec'].T
        ch['outs'] = []
    for c in range(NC):
        sl = slice(c * C, (c + 1) * C)
        for ch in chains:
            S = ch['S']
            v_new = ch['u'][sl] - _bdot(ch['w'][sl], S)
            ch['outs'].append(_bdot(ch['q_dec'][sl], S) + _bdot(ch['qk'][sl, sl], v_new))
            ch['S'] = S * ch['g_last'][c * C:c * C + 1, :] + _bdot(ch['k_dec_t'][:, sl], v_new)
    for ch in chains:
        b, h = ch['b'], ch['h']
        s_ref[b, h] = ch['S']
        o = jnp.concatenate(ch['outs'], axis=0)
        o = o * lax.rsqrt(jnp.mean(o * o, axis=-1, keepdims=True) + EPS) * ng
        gh = gate_ref[b, :, h * GDN_DV:(h + 1) * GDN_DV]
        o_ref[b, :, h * GDN_DV:(h + 1) * GDN_DV] = o * (gh * jax.nn.sigmoid(gh))


def _gdn_prompt(z3, conv_w, a_log, dt_bias, norm_g):
    B, S, _ = z3.shape
    R = GDN_ROWS
    lanes4 = lambda a: jnp.pad(a, (0, LANES - a.shape[0])).reshape(1, LANES)
    return pl.pallas_call(
        _gdn_prompt_kernel,
        grid=(S // R,),
        in_specs=[pl.BlockSpec((B, R, GDN_CONV_CH), lambda s: (0, s, Z_GDN // GDN_CONV_CH)),
                  pl.BlockSpec((B, R, LANES), lambda s: (0, s, Z_AB // LANES)),
                  pl.BlockSpec((B, R, GDN_V_COLS), lambda s: (0, s, Z_GATE // GDN_V_COLS)),
                  pl.BlockSpec((GDN_CONV, GDN_CONV_CH), lambda s: (0, 0)),
                  pl.BlockSpec((1, LANES), lambda s: (0, 0)),
                  pl.BlockSpec((1, LANES), lambda s: (0, 0)),
                  pl.BlockSpec((1, GDN_DV), lambda s: (0, 0))],
        out_specs=[pl.BlockSpec((B, R, GDN_V_COLS), lambda s: (0, s, 0)),
                   pl.BlockSpec((B, N_GDN_HEADS, GDN_DK, GDN_DV), lambda s: (0, 0, 0, 0))],
        out_shape=[jax.ShapeDtypeStruct((B, S, GDN_V_COLS), F32),
                   jax.ShapeDtypeStruct((B, N_GDN_HEADS, GDN_DK, GDN_DV), F32)],
        scratch_shapes=[pltpu.VMEM((B, CONV_HALO + R, GDN_CONV_CH), F32)],
        compiler_params=_cparams("arbitrary"),
        name="gdn_prompt",
    )(z3, z3, z3, conv_w, lanes4(a_log), lanes4(dt_bias), norm_g.reshape(1, GDN_DV))


def _pair_sumsq(x):
    li = lax.broadcasted_iota(I32, (LANES, LANES), 0) // HEAD_DIM
    lj = lax.broadcasted_iota(I32, (LANES, LANES), 1) // HEAD_DIM
    same = jnp.where(li == lj, 1.0, 0.0).astype(BF16)
    hi, lo = _split2(x * x)
    return jnp.dot(hi, same, preferred_element_type=F32) + jnp.dot(lo, same, preferred_element_type=F32)


def _pair_rms(x, g):
    return x * lax.rsqrt(_pair_sumsq(x) * (1.0 / HEAD_DIM) + EPS) * g


def _first_half(shape):
    return lax.broadcasted_iota(I32, shape, 1) < HEAD_DIM


LOG2E = 1.4426950408889634


SWA_BLOCKS = 2


def _swa_prompt_kernel(sink_ref, q_ref, kc_ref, kp_ref, vc_ref, vp_ref, qg_ref, kg_ref, o_ref, kn_ref,
                       bias_ref):
    first = (pl.program_id(0) == 0) & (pl.program_id(1) == 0)
    n = pl.program_id(1)
    W = WINDOW
    NQ = SWA_BLOCKS

    @pl.when(first)
    def _():
        qi = lax.broadcasted_iota(I32, (W, 2 * W), 0)
        kj = lax.broadcasted_iota(I32, (W, 2 * W), 1)
        dist = qi + W - kj
        band = (dist >= 0) & (dist < W)
        distf = dist.astype(F32)
        for has_prev in range(2):
            mask = jnp.where(band & ((has_prev == 1) | (kj >= W)), 0.0, -jnp.inf)
            for head in range(N_SWA_HEADS):
                slope = 2.0 ** (-(8.0 / N_SWA_HEADS) * (head + 1))
                bias_ref[has_prev, head] = mask - (slope * LOG2E) * distf

    kg = kg_ref[...]
    qg = qg_ref[...]
    kc = _pair_rms(kc_ref[0], kg)
    kn_ref[0] = kc[(NQ - 1) * W:]
    k3 = jnp.concatenate([_pair_rms(kp_ref[0], kg), kc], axis=0)
    v3 = jnp.concatenate([vp_ref[0], vc_ref[0]], axis=0)
    fh = _first_half(k3.shape)
    k3r = pltpu.roll(k3, HEAD_DIM, 1)
    v3r = pltpu.roll(v3, HEAD_DIM, 1)
    kdup = (jnp.where(fh, k3, k3r).astype(BF16), jnp.where(fh, k3r, k3).astype(BF16))
    vdup = (jnp.where(fh, v3, v3r).astype(BF16), jnp.where(fh, v3r, v3).astype(BF16))
    fq = _first_half((W, LANES))
    kv_of = lambda head: head // SWA_GROUP
    probs = [(j, head) for j in range(NQ) for head in range(N_SWA_HEADS)]
    keys = lambda j: slice(j * W, (j + 2) * W)
    qts = [[_pair_rms(q_ref[0, j * W:(j + 1) * W, t * LANES:(t + 1) * LANES], qg) * (ATTN_SCALE * LOG2E)
            for t in range(SWA_Q_COLS // LANES)] for j in range(NQ)]
    qms = [jnp.where(fq == (head % 2 == 0), qts[j][head // 2], 0.0).astype(BF16) for j, head in probs]
    table = [jnp.where(n > 0, 1, 0)] + [1] * (NQ - 1)
    ss = [_bdot_nt(qms[i], kdup[kv_of(head)][keys(j)]) + bias_ref[table[j], head]
          for i, (j, head) in enumerate(probs)]
    sinks = [sink_ref[head] * LOG2E for head in range(N_SWA_HEADS)]
    ms = [jnp.maximum(jnp.max(ss[i], axis=-1, keepdims=True), sinks[head]) for i, (j, head) in enumerate(probs)]
    ps = [jnp.exp2(ss[i] - ms[i]) for i in range(len(probs))]
    dens = [jnp.sum(ps[i], axis=-1, keepdims=True) + jnp.exp2(sinks[head] - ms[i])
            for i, (j, head) in enumerate(probs)]
    outs = [_bdot(ps[i], vdup[kv_of(head)][keys(j)]) / dens[i] for i, (j, head) in enumerate(probs)]
    for j in range(NQ):
        for t in range(SWA_Q_COLS // LANES):
            o_ref[0, j * W:(j + 1) * W, t * LANES:(t + 1) * LANES] = jnp.where(
                fq, outs[j * N_SWA_HEADS + 2 * t], outs[j * N_SWA_HEADS + 2 * t + 1])


def _swa_prompt(z3, q_g, k_g, sinks):
    B, S, _ = z3.shape
    W = WINDOW
    NQ = SWA_BLOCKS
    twice = lambda g: jnp.concatenate([g, g]).reshape(1, LANES)
    kcol, vcol = Z_K // LANES, Z_V // LANES
    prev = lambda col: pl.BlockSpec((1, W, LANES), lambda b, n: (b, jnp.maximum(NQ * n - 1, 0), col))
    grid_spec = pltpu.PrefetchScalarGridSpec(
        num_scalar_prefetch=0,
        grid=(B, S // (NQ * W)),
        in_specs=[pl.BlockSpec(memory_space=pltpu.SMEM),
                  pl.BlockSpec((1, NQ * W, SWA_Q_COLS), lambda b, n: (b, n, 0)),
                  pl.BlockSpec((1, NQ * W, LANES), lambda b, n: (b, n, kcol)),
                  prev(kcol),
                  pl.BlockSpec((1, NQ * W, LANES), lambda b, n: (b, n, vcol)),
                  prev(vcol),
                  pl.BlockSpec((1, LANES), lambda b, n: (0, 0)),
                  pl.BlockSpec((1, LANES), lambda b, n: (0, 0))],
        out_specs=[pl.BlockSpec((1, NQ * W, SWA_Q_COLS), lambda b, n: (b, n, 0)),
                   pl.BlockSpec((1, W, LANES), lambda b, n: (b, 0, 0))],
        scratch_shapes=[pltpu.VMEM((2, N_SWA_HEADS, W, 2 * W), F32)],
    )
    return pl.pallas_call(
        _swa_prompt_kernel,
        grid_spec=grid_spec,
        out_shape=[jax.ShapeDtypeStruct((B, S, SWA_Q_COLS), F32),
                   jax.ShapeDtypeStruct((B, W, LANES), F32)],
        compiler_params=_cparams("arbitrary", "arbitrary"),
        name="swa_prompt",
    )(sinks, z3, z3, z3, z3, z3, twice(q_g), twice(k_g))


def _mem_kv_kernel(m_ref, g_ref, w_ref, kg_ref, k_ref, v_ref):
    n = _rms_rows(m_ref[...], g_ref[...])
    kv = jnp.dot(n.astype(BF16), w_ref[...], preferred_element_type=F32)
    kg = kg_ref[...]
    for t in range(MEM_Q_COLS // LANES):
        k_ref[:, t * LANES:(t + 1) * LANES] = _pair_rms(kv[:, t * LANES:(t + 1) * LANES], kg)
    v_ref[...] = kv[:, MEM_Q_COLS:]


def _mem_kv(mem2d, ln_g, w_kv, k_g):
    r, d = mem2d.shape
    twice = jnp.concatenate([k_g, k_g]).reshape(1, LANES)
    full = lambda shape: pl.BlockSpec(shape, lambda i: (0,) * len(shape))
    return pl.pallas_call(
        _mem_kv_kernel,
        grid=(1,),
        in_specs=[full((r, d)), full((1, d)), full((d, 2 * MEM_Q_COLS)), full((1, LANES))],
        out_specs=[full((r, MEM_Q_COLS)), full((r, MEM_Q_COLS))],
        out_shape=[jax.ShapeDtypeStruct((r, MEM_Q_COLS), F32), jax.ShapeDtypeStruct((r, MEM_Q_COLS), F32)],
        compiler_params=_cparams("arbitrary"),
        name="mem_kv",
    )(mem2d, ln_g.reshape(1, d), w_kv.astype(BF16), twice)


def _mem_attn_kernel(q_ref, k_ref, v_ref, qg_ref, o_ref):
    qg = qg_ref[...]
    rows = q_ref.shape[1]
    fq = _first_half((rows, LANES))
    heads = range(N_MEM_HEADS)
    tile = lambda t: slice(t * LANES, (t + 1) * LANES)
    qts = [_pair_rms(q_ref[0, :, tile(t)], qg) * (ATTN_SCALE * LOG2E) for t in range(MEM_Q_COLS // LANES)]
    kts = [k_ref[0, :, tile(t)].astype(BF16) for t in range(MEM_Q_COLS // LANES)]
    vts = [v_ref[0, :, tile(t)].astype(BF16) for t in range(MEM_Q_COLS // LANES)]
    ss = [_bdot_nt(jnp.where(fq == (h % 2 == 0), qts[h // 2], 0.0), kts[h // 2]) for h in heads]
    ps = [jnp.exp2(s - jnp.max(s, axis=-1, keepdims=True)) for s in ss]
    outs = [_bdot(ps[h], vts[h // 2]) / jnp.sum(ps[h], axis=-1, keepdims=True) for h in heads]
    for t in range(MEM_Q_COLS // LANES):
        o_ref[0, :, tile(t)] = jnp.where(fq, outs[2 * t], outs[2 * t + 1])


MEM_Q_TILE = 512


def _mem_attn_prompt(z3, mem_k, mem_v, q_g):
    B, S, _ = z3.shape
    M = mem_k.shape[1]
    tq = MEM_Q_TILE
    twice = jnp.concatenate([q_g, q_g]).reshape(1, LANES)
    return pl.pallas_call(
        _mem_attn_kernel,
        grid=(B, S // tq),
        in_specs=[pl.BlockSpec((1, tq, MEM_Q_COLS), lambda b, i: (b, i, Z_QM // MEM_Q_COLS)),
                  pl.BlockSpec((1, M, MEM_Q_COLS), lambda b, i: (b, 0, 0)),
                  pl.BlockSpec((1, M, MEM_Q_COLS), lambda b, i: (b, 0, 0)),
                  pl.BlockSpec((1, LANES), lambda b, i: (0, 0))],
        out_specs=pl.BlockSpec((1, tq, MEM_Q_COLS), lambda b, i: (b, i, 0)),
        out_shape=jax.ShapeDtypeStruct((B, S, MEM_Q_COLS), F32),
        compiler_params=_cparams("parallel", "parallel"),
        name="mem_attn_prompt",
    )(z3, mem_k, mem_v, twice)


PAIR = 2


def _swa_sample_kernel(sink_ref, q_ref, k_ref, v_ref, ck_ref, cv_ref, qg_ref, kg_ref, o_ref, kn_ref, *, L):
    n_seq = ck_ref.shape[0]
    Wb = ck_ref.shape[1]
    rows8 = SUBLANES
    nh = N_SWA_HEADS
    kn = _pair_rms(k_ref[...], kg_ref[...])
    kn_ref[...] = kn
    qg = qg_ref[...]
    R = nh * rows8
    row = lax.broadcasted_iota(I32, (R, 1), 0)
    head = row // rows8
    seq_in_pair = (row % rows8) // L
    step = (row % L).astype(F32)
    slope = jnp.exp2(-(8.0 / N_SWA_HEADS) * (head.astype(F32) + 1.0))
    sink = jnp.zeros((R, 1), F32)
    for h in range(nh):
        sink = jnp.where(head == h, sink_ref[h], sink)
    key = lax.broadcasted_iota(I32, (R, Wb), 1).astype(F32)
    dist_c = float(Wb) + step - key
    bias_c = jnp.where(dist_c < float(WINDOW), 0.0, -jnp.inf)
    col = lax.broadcasted_iota(I32, (R, rows8), 1)
    dist_n = step - (col % L).astype(F32)
    bias_n = jnp.where((dist_n >= 0.0) & ((col // L) == seq_in_pair), 0.0, -jnp.inf)
    fh8 = _first_half((rows8, LANES))
    fhR = _first_half((R, LANES))
    kv_first = head < SWA_GROUP
    pairs = range(n_seq // PAIR)
    bias_c = bias_c - slope * dist_c
    bias_n = bias_n - slope * dist_n

    def stacked_queries(pr):
        r0 = pr * rows8
        pieces = []
        for t in range(SWA_Q_COLS // LANES):
            qt = _pair_rms(q_ref[r0:r0 + rows8, t * LANES:(t + 1) * LANES], qg) * ATTN_SCALE
            qr = pltpu.roll(qt, HEAD_DIM, 1)
            kv = t // (SWA_GROUP // 2)
            for half in range(2):
                src = qt if half == kv else qr
                pieces.append(jnp.where(fh8 == (kv == 0), src, 0.0))
        return jnp.concatenate(pieces, axis=0).astype(BF16)

    qs = [stacked_queries(pr) for pr in pairs]
    s_c = [jnp.where(seq_in_pair == 0, _bdot_nt(qs[pr], ck_ref[pr * PAIR]), _bdot_nt(qs[pr], ck_ref[pr * PAIR + 1]))
           + bias_c for pr in pairs]
    s_n = [_bdot_nt(qs[pr], kn[pr * rows8:(pr + 1) * rows8]) + bias_n for pr in pairs]
    m = [jnp.maximum(jnp.maximum(jnp.max(s_c[pr], axis=-1, keepdims=True),
                                 jnp.max(s_n[pr], axis=-1, keepdims=True)), sink) for pr in pairs]
    p_c = [jnp.exp(s_c[pr] - m[pr]) for pr in pairs]
    p_n = [jnp.exp(s_n[pr] - m[pr]) for pr in pairs]
    den = [jnp.sum(p_c[pr], axis=-1, keepdims=True) + jnp.sum(p_n[pr], axis=-1, keepdims=True)
           + jnp.exp(sink - m[pr]) for pr in pairs]
    outs = [(_bdot(p_n[pr], v_ref[pr * rows8:(pr + 1) * rows8, :])
             + _bdot(jnp.where(seq_in_pair == 0, p_c[pr], 0.0), cv_ref[pr * PAIR])
             + _bdot(jnp.where(seq_in_pair == 1, p_c[pr], 0.0), cv_ref[pr * PAIR + 1])) / den[pr] for pr in pairs]
    for pr in pairs:
        r0 = pr * rows8
        o = jnp.where(fhR == kv_first, outs[pr], 0.0)
        o_r = pltpu.roll(o, HEAD_DIM, 1)
        for t in range(SWA_Q_COLS // LANES):
            kv = t // (SWA_GROUP // 2)
            halves = []
            for half in range(2):
                h = 2 * t + half
                src = o if half == kv else o_r
                halves.append(src[h * rows8:(h + 1) * rows8])
            o_ref[r0:r0 + rows8, t * LANES:(t + 1) * LANES] = jnp.where(fh8, halves[0], halves[1])


SAMPLE_SEQS = 8


def _swa_sample(z_s, cache_k, cache_v, q_g, k_g, sinks, L):
    t = z_s.shape[0]
    DB, Wb, _ = cache_k.shape
    ns = SAMPLE_SEQS
    rows = ns * L
    twice = lambda g: jnp.concatenate([g, g]).reshape(1, LANES)
    return pl.pallas_call(
        functools.partial(_swa_sample_kernel, L=L),
        grid=(DB // ns,),
        in_specs=[pl.BlockSpec(memory_space=pltpu.SMEM),
                  pl.BlockSpec((rows, SWA_Q_COLS), lambda i: (i, 0)),
                  pl.BlockSpec((rows, LANES), lambda i: (i, Z_K // LANES)),
                  pl.BlockSpec((rows, LANES), lambda i: (i, Z_V // LANES)),
                  pl.BlockSpec((ns, Wb, LANES), lambda i: (i, 0, 0)),
                  pl.BlockSpec((ns, Wb, LANES), lambda i: (i, 0, 0)),
                  pl.BlockSpec((1, LANES), lambda i: (0, 0)),
                  pl.BlockSpec((1, LANES), lambda i: (0, 0))],
        out_specs=[pl.BlockSpec((rows, SWA_Q_COLS), lambda i: (i, 0)),
                   pl.BlockSpec((rows, LANES), lambda i: (i, 0))],
        out_shape=[jax.ShapeDtypeStruct((t, SWA_Q_COLS), F32),
                   jax.ShapeDtypeStruct((t, LANES), F32)],
        compiler_params=_cparams("parallel"),
        name="swa_sample",
    )(sinks, z_s, z_s, z_s, cache_k, cache_v, twice(q_g), twice(k_g))


def _mem_sample_kernel(q_ref, mk_ref, mv_ref, qg_ref, o_ref, *, L):
    n_seq = mk_ref.shape[0]
    rows8 = SUBLANES
    qg = qg_ref[...]
    R = 2 * rows8
    row = lax.broadcasted_iota(I32, (R, 1), 0)
    seq_in_pair = (row % rows8) // L
    fh8 = _first_half((rows8, LANES))
    probs = [(pr, t) for pr in range(n_seq // PAIR) for t in range(MEM_Q_COLS // LANES)]
    cols = lambda t: slice(t * LANES, (t + 1) * LANES)

    def stacked_queries(pr, t):
        qt = _pair_rms(q_ref[pr * rows8:(pr + 1) * rows8, cols(t)], qg) * ATTN_SCALE
        return jnp.concatenate([jnp.where(fh8, qt, 0.0), jnp.where(fh8, 0.0, qt)], axis=0).astype(BF16)

    qs = [stacked_queries(pr, t) for pr, t in probs]
    ss = [jnp.where(seq_in_pair == 0, _bdot_nt(q, mk_ref[pr * PAIR, :, cols(t)]),
                    _bdot_nt(q, mk_ref[pr * PAIR + 1, :, cols(t)])) for q, (pr, t) in zip(qs, probs)]
    ps = [jnp.exp(s - jnp.max(s, axis=-1, keepdims=True)) for s in ss]
    outs = [(_bdot(jnp.where(seq_in_pair == 0, p, 0.0), mv_ref[pr * PAIR, :, cols(t)])
             + _bdot(jnp.where(seq_in_pair == 1, p, 0.0), mv_ref[pr * PAIR + 1, :, cols(t)]))
            / jnp.sum(p, axis=-1, keepdims=True) for p, (pr, t) in zip(ps, probs)]
    for o, (pr, t) in zip(outs, probs):
        o_ref[pr * rows8:(pr + 1) * rows8, cols(t)] = jnp.where(fh8, o[:rows8], o[rows8:])


def _mem_attn_sample(z_s, mem_k, mem_v, q_g, L):
    t = z_s.shape[0]
    DB, M, _ = mem_k.shape
    ns = SAMPLE_SEQS
    rows = ns * L
    twice = jnp.concatenate([q_g, q_g]).reshape(1, LANES)
    return pl.pallas_call(
        functools.partial(_mem_sample_kernel, L=L),
        grid=(DB // ns,),
        in_specs=[pl.BlockSpec((rows, MEM_Q_COLS), lambda i: (i, Z_QM // MEM_Q_COLS)),
                  pl.BlockSpec((ns, M, MEM_Q_COLS), lambda i: (i, 0, 0)),
                  pl.BlockSpec((ns, M, MEM_Q_COLS), lambda i: (i, 0, 0)),
                  pl.BlockSpec((1, LANES), lambda i: (0, 0))],
        out_specs=pl.BlockSpec((rows, MEM_Q_COLS), lambda i: (i, 0)),
        out_shape=jax.ShapeDtypeStruct((t, MEM_Q_COLS), F32),
        compiler_params=_cparams("parallel"),
        name="mem_attn_sample",
    )(z_s, mem_k, mem_v, twice)


def _gdn_sample_kernel(uq_ref, uk_ref, uv_ref, bq_ref, bk_ref, bv_ref, wq_ref, wk_ref, wv_ref,
                       ab_ref, gate_ref, alog_ref, dtb_ref, ng_ref, s_in_ref, o_ref, s_out_ref, s_ref, kq_ref):
    h = pl.program_id(0)
    L = uq_ref.shape[0]
    nbuf = bq_ref.shape[0]
    DK = GDN_DK
    DB = s_in_ref.shape[0]

    def conv(u_ref, b_ref, w_ref, t):
        up = [b_ref[i] for i in range(nbuf)] + [u_ref[i] for i in range(L)]
        y = up[t] * w_ref[0]
        for i in range(1, GDN_CONV):
            y = y + up[t + i] * w_ref[i]
        return y * jax.nn.sigmoid(y)

    s_ref[...] = s_in_ref[...].T.reshape(DK, GDN_DV, DB)
    ng = ng_ref[...]
    hsel = lax.broadcasted_iota(I32, (SUBLANES, 1), 0)
    pick = lambda m, r: jnp.sum(jnp.where(hsel == r, m, 0.0), axis=0, keepdims=True)
    alog = pick(alog_ref[...], h)
    dtb = pick(dtb_ref[...], h)
    for t in range(L):
        q = conv(uq_ref, bq_ref, wq_ref, t)
        k = conv(uk_ref, bk_ref, wk_ref, t)
        v = conv(uv_ref, bv_ref, wv_ref, t)
        q = q * lax.rsqrt(jnp.sum(q * q, axis=0, keepdims=True) + EPS) * (GDN_DK ** -0.5)
        k = k * lax.rsqrt(jnp.sum(k * k, axis=0, keepdims=True) + EPS)
        ab = ab_ref[t]
        a = pick(ab, h)
        bb = pick(ab, h + N_GDN_HEADS)
        decay = jnp.exp(-jnp.exp(alog) * _softplus(a + dtb))
        beta = jax.nn.sigmoid(bb)
        kq_ref[0] = k
        kq_ref[1] = q

        def decay_and_project(dk, acc):
            s = s_ref[dk] * decay
            s_ref[dk] = s
            return acc + s * kq_ref[0, pl.ds(dk, 1), :]

        sk = lax.fori_loop(0, DK, decay_and_project, jnp.zeros_like(v), unroll=8)
        u = beta * (v - sk)

        def update_and_read(dk, acc):
            s = s_ref[dk] + kq_ref[0, pl.ds(dk, 1), :] * u
            s_ref[dk] = s
            return acc + s * kq_ref[1, pl.ds(dk, 1), :]

        o = lax.fori_loop(0, DK, update_and_read, jnp.zeros_like(v), unroll=8)
        o = o * lax.rsqrt(jnp.mean(o * o, axis=0, keepdims=True) + EPS) * ng
        g = gate_ref[t]
        o_ref[t] = o * (g * jax.nn.sigmoid(g))
    s_out_ref[...] = s_ref[...].reshape(DK * GDN_DV, DB).T


def _gdn_sample(z_s, conv_buf, state, conv_w, a_log, dt_bias, norm_g, DB, L):
    H = N_GDN_HEADS
    z3 = z_s.reshape(DB, L, Z_COLS)
    u_t = jnp.transpose(z3[:, :, Z_GDN:Z_GATE], (1, 2, 0))
    gate_t = jnp.transpose(z3[:, :, Z_GATE:Z_QM], (1, 2, 0))
    ab_t = jnp.transpose(z3[:, :, Z_AB:Z_AB + SUBLANES], (1, 2, 0))
    buf_t = jnp.transpose(conv_buf, (1, 2, 0))
    s_2d = state.reshape(DB, H * GDN_DK * GDN_DV)
    w_col = conv_w.reshape(GDN_CONV, GDN_CONV_CH, 1)
    col8 = lambda a: jnp.pad(a, (0, SUBLANES - a.shape[0])).reshape(SUBLANES, 1)
    nbuf = conv_buf.shape[1]
    part = lambda n, j: pl.BlockSpec((n, GDN_DK, DB), lambda h: (0, j * H + h, 0))
    wpart = lambda j: pl.BlockSpec((GDN_CONV, GDN_DK, 1), lambda h: (0, j * H + h, 0))
    whole = lambda shape: pl.BlockSpec(shape, lambda h: (0,) * len(shape))
    o_t, s_new = pl.pallas_call(
        _gdn_sample_kernel,
        grid=(H,),
        in_specs=[part(L, 0), part(L, 1), part(L, 2), part(nbuf, 0), part(nbuf, 1), part(nbuf, 2),
                  wpart(0), wpart(1), wpart(2),
                  whole((L, SUBLANES, DB)),
                  pl.BlockSpec((L, GDN_DV, DB), lambda h: (0, h, 0)),
                  whole((SUBLANES, 1)), whole((SUBLANES, 1)), whole((GDN_DV, 1)),
                  pl.BlockSpec((DB, GDN_DK * GDN_DV), lambda h: (0, h))],
        out_specs=[pl.BlockSpec((L, GDN_DV, DB), lambda h: (0, h, 0)),
                   pl.BlockSpec((DB, GDN_DK * GDN_DV), lambda h: (0, h))],
        out_shape=[jax.ShapeDtypeStruct((L, H * GDN_DV, DB), F32),
                   jax.ShapeDtypeStruct((DB, H * GDN_DK * GDN_DV), F32)],
        scratch_shapes=[pltpu.VMEM((GDN_DK, GDN_DV, DB), F32), pltpu.VMEM((2, GDN_DK, DB), F32)],
        compiler_params=_cparams("parallel"),
        name="gdn_sample",
    )(u_t, u_t, u_t, buf_t, buf_t, buf_t, w_col, w_col, w_col, ab_t, gate_t,
      col8(a_log), col8(dt_bias), norm_g.reshape(GDN_DV, 1), s_2d)
    o = jnp.transpose(o_t, (2, 0, 1)).reshape(DB * L, H * GDN_DV)
    return o, s_new.reshape(DB, H, GDN_DK, GDN_DV)


def kernel(x_prompt, x_sample, cache_swa_k, cache_swa_v, state_gdn, state_gdn_conv, cache_mem_k, cache_mem_v,
           mem_prompt, ln1_g, w_in, swa_q_norm, swa_k_norm, swa_sinks, gdn_conv_w, gdn_a_log, gdn_dt_bias,
           gdn_norm_g, mem_ln_g, w_mem_kv, mem_q_norm, mem_k_norm, w_o, ln2_g, router_w, router_b,
           moe_w1, moe_b1, moe_w2, moe_b2):
    B, S, D = x_prompt.shape
    DB, DL, _ = x_sample.shape
    depth = ln1_g.shape[0]
    assert depth == 1
    l = 0
    tp, ts = B * S, DB * DL
    t_all = tp + ts
    n_ab = 2 * N_GDN_HEADS
    c_ab = SWA_Q_COLS + 2 * SWA_KV_COLS + GDN_CONV_CH
    w = w_in[l]
    w_z = jnp.concatenate([w[:, :c_ab], w[:, c_ab + n_ab:], w[:, c_ab:c_ab + n_ab],
                           jnp.zeros((D, LANES - n_ab), F32)], axis=1).astype(BF16)
    rw = jnp.pad(router_w[l], ((0, 0), (0, LANES - N_EXPERTS)))
    rw_hi = rw.astype(BF16)
    rw_lo = (rw - rw_hi.astype(F32)).astype(BF16)
    rb = jnp.pad(router_b[l], (0, LANES - N_EXPERTS)).reshape(1, LANES)
    wo = w_o[l].astype(BF16)
    w1 = moe_w1[l]
    w2 = moe_w2[l]
    b1 = moe_b1[l].reshape(N_EXPERTS, 1, -1)
    b2 = moe_b2[l].reshape(N_EXPERTS, 1, -1)
    p = {'q_norm': swa_q_norm[l], 'k_norm': swa_k_norm[l], 'sinks': swa_sinks[l], 'conv_w': gdn_conv_w[l],
         'a_log': gdn_a_log[l], 'dt_bias': gdn_dt_bias[l], 'gdn_norm': gdn_norm_g[l], 'mem_q_norm': mem_q_norm[l]}

    xp = x_prompt.reshape(tp, D)
    xs = x_sample.reshape(ts, D)
    z_p = _inproj(xp, ln1_g[l], w_z)
    z_s = _inproj(xs, ln1_g[l], w_z)

    M = mem_prompt.shape[1]
    z_p3 = z_p.reshape(B, S, Z_COLS)
    mk2, mv2 = _mem_kv(mem_prompt.reshape(B * M, D), mem_ln_g[l], w_mem_kv[l], mem_k_norm[l])
    mk = mk2.reshape(B, M, N_MEM_HEADS, HEAD_DIM)
    mv = mv2.reshape(B, M, N_MEM_HEADS, HEAD_DIM)
    os_p, pk = _swa_prompt(z_p3, p['q_norm'], p['k_norm'], p['sinks'])
    od_p, ps = _gdn_prompt(z_p3, p['conv_w'], p['a_log'], p['dt_bias'], p['gdn_norm'])
    om_p = _mem_attn_prompt(z_p3, mk2.reshape(B, M, MEM_Q_COLS), mv2.reshape(B, M, MEM_Q_COLS), p['mem_q_norm'])
    os_p, od_p, om_p = os_p.reshape(tp, -1), od_p.reshape(tp, -1), om_p.reshape(tp, -1)
    pk = pk.reshape(B, WINDOW, N_SWA_KV, HEAD_DIM)
    pv = z_p3[:, S - WINDOW:, Z_V:Z_GDN].reshape(B, WINDOW, N_SWA_KV, HEAD_DIM)
    pc = z_p3[:, S - (GDN_CONV - 1):, Z_GDN:Z_GATE]
    Wb = cache_swa_k.shape[2]
    ck = cache_swa_k[l].reshape(DB, Wb, SWA_KV_COLS)
    cv = cache_swa_v[l].reshape(DB, Wb, SWA_KV_COLS)
    os_s, k_new = _swa_sample(z_s, ck, cv, p['q_norm'], p['k_norm'], p['sinks'], DL)
    od_s, ss = _gdn_sample(z_s, state_gdn_conv[l], state_gdn[l], p['conv_w'], p['a_log'], p['dt_bias'],
                           p['gdn_norm'], DB, DL)
    om_s = _mem_attn_sample(z_s, cache_mem_k[l].reshape(DB, -1, MEM_Q_COLS),
                            cache_mem_v[l].reshape(DB, -1, MEM_Q_COLS), p['mem_q_norm'], DL)
    z_s3 = z_s.reshape(DB, DL, Z_COLS)
    sk = jnp.concatenate([ck, k_new.reshape(DB, DL, SWA_KV_COLS)], axis=1)[:, DL:]
    sv = jnp.concatenate([cv, z_s3[:, :, Z_V:Z_GDN]], axis=1)[:, DL:]
    sk = sk.reshape(DB, Wb, N_SWA_KV, HEAD_DIM)
    sv = sv.reshape(DB, Wb, N_SWA_KV, HEAD_DIM)
    sc = jnp.concatenate([state_gdn_conv[l], z_s3[:, :, Z_GDN:Z_GATE]], axis=1)[:, DL:]

    h_p, hn_all, lg_all = _outproj(xp, os_p, od_p, om_p, wo, ln2_g[l], rw_hi, rw_lo, rb, t_all, 0)
    h_s, hn_all, lg_all = _outproj(xs, os_s, od_s, om_s, wo, ln2_g[l], rw_hi, rw_lo, rb, t_all, tp,
                                   prev=(hn_all, lg_all))
    y_p, y_s = _moe(hn_all, lg_all, [h_p, h_s], w1, b1, w2, b2)
    return (y_p.reshape(B, S, D), y_s.reshape(DB, DL, D), pk[None], pv[None], ps[None], pc[None], mk[None],
            mv[None], sk[None], sv[None], ss[None], sc[None])
```

```python
import functools

import jax
import jax.numpy as jnp
from jax import lax
from jax.experimental import pallas as pl
from jax.experimental.pallas import tpu as pltpu

F32 = jnp.float32
BF16 = jnp.bfloat16
I32 = jnp.int32

HEAD_DIM = 64
N_SWA_HEADS = 8
N_SWA_KV = 2
SWA_GROUP = N_SWA_HEADS // N_SWA_KV
WINDOW = 128
N_GDN_HEADS = 4
GDN_DK = 64
GDN_DV = 64
GDN_CONV = 4
GDN_CHUNK = 64
N_MEM_HEADS = 4
N_EXPERTS = 32
TOP_K = 4
SWIGLU_ALPHA = 1.702
SWIGLU_LIMIT = 7.0
EPS = 1e-6
ATTN_SCALE = HEAD_DIM ** -0.5

SWA_Q_COLS = N_SWA_HEADS * HEAD_DIM
SWA_KV_COLS = N_SWA_KV * HEAD_DIM
GDN_QK_COLS = N_GDN_HEADS * GDN_DK
GDN_V_COLS = N_GDN_HEADS * GDN_DV
GDN_CONV_CH = 2 * GDN_QK_COLS + GDN_V_COLS
MEM_Q_COLS = N_MEM_HEADS * HEAD_DIM

LANES = 128
SUBLANES = 8
VMEM_LIMIT = 56 * 1024 * 1024

Z_Q = 0
Z_K = Z_Q + SWA_Q_COLS
Z_V = Z_K + SWA_KV_COLS
Z_GDN = Z_V + SWA_KV_COLS
Z_GATE = Z_GDN + GDN_CONV_CH
Z_QM = Z_GATE + GDN_V_COLS
Z_AB = Z_QM + MEM_Q_COLS
Z_COLS = Z_AB + LANES

ROW_TILE = 512
MOE_TILE = 512
MOE_BLK = 512
PERM_CHUNK = 256
RUN_ALIGN = 16
BLK_ROWS = -(-(MOE_BLK * TOP_K + N_EXPERTS * (RUN_ALIGN - 1)) // PERM_CHUNK) * PERM_CHUNK


def _cparams(*sem):
    return pltpu.CompilerParams(dimension_semantics=sem, vmem_limit_bytes=VMEM_LIMIT)


def _bdot(a, b):
    return jnp.dot(a.astype(BF16), b.astype(BF16), preferred_element_type=F32)


def _bdot_nt(a, b):
    return lax.dot_general(a.astype(BF16), b.astype(BF16), (((1,), (1,)), ((), ())),
                           preferred_element_type=F32)


def _bdot_tn(a, b):
    return lax.dot_general(a.astype(BF16), b.astype(BF16), (((0,), (0,)), ((), ())),
                           preferred_element_type=F32)


def _split2(x):
    hi = x.astype(BF16)
    lo = (x - hi.astype(F32)).astype(BF16)
    return hi, lo


def _split3(x):
    hi = x.astype(BF16)
    r = x - hi.astype(F32)
    mid = r.astype(BF16)
    lo = (r - mid.astype(F32)).astype(BF16)
    return hi, mid, lo


def _rms_rows(x, g):
    ms = jnp.mean(x * x, axis=-1, keepdims=True)
    return x * lax.rsqrt(ms + EPS) * g


def _inproj_kernel(x_ref, g_ref, w_ref, z_ref):
    n = _rms_rows(x_ref[...], g_ref[...])
    z_ref[...] = jnp.dot(n.astype(BF16), w_ref[...], preferred_element_type=F32)


def _inproj(x2d, ln_g, w_z):
    t, d = x2d.shape
    tm = min(ROW_TILE, t)
    return pl.pallas_call(
        _inproj_kernel,
        grid=(t // tm,),
        in_specs=[pl.BlockSpec((tm, d), lambda i: (i, 0)),
                  pl.BlockSpec((1, d), lambda i: (0, 0)),
                  pl.BlockSpec((d, Z_COLS), lambda i: (0, 0))],
        out_specs=pl.BlockSpec((tm, Z_COLS), lambda i: (i, 0)),
        out_shape=jax.ShapeDtypeStruct((t, Z_COLS), F32),
        compiler_params=_cparams("parallel"),
        name="inproj",
    )(x2d, ln_g.reshape(1, d), w_z)


def _outproj_kernel(x_ref, os_ref, od_ref, om_ref, wo_ref, g_ref, rwh_ref, rwl_ref, rb_ref,
                    *refs, n_own):
    h_ref, hn_ref, lg_ref = refs[-3:]
    i = pl.program_id(0)

    @pl.when(i < n_own)
    def _():
        n_s = os_ref.shape[1]
        n_d = od_ref.shape[1]
        h = x_ref[...]
        h = h + jnp.dot(os_ref[...].astype(BF16), wo_ref[0:n_s, :], preferred_element_type=F32)
        h = h + jnp.dot(od_ref[...].astype(BF16), wo_ref[n_s:n_s + n_d, :], preferred_element_type=F32)
        h = h + jnp.dot(om_ref[...].astype(BF16), wo_ref[n_s + n_d:, :], preferred_element_type=F32)
        h_ref[...] = h
        hn = _rms_rows(h, g_ref[...])
        hn_ref[...] = hn.astype(BF16)
        hi, lo = _split2(hn)
        lg = (jnp.dot(hi, rwh_ref[...], preferred_element_type=F32)
              + jnp.dot(lo, rwh_ref[...], preferred_element_type=F32)
              + jnp.dot(hi, rwl_ref[...], preferred_element_type=F32))
        lg_ref[...] = lg + rb_ref[...]

    @pl.when(i >= n_own)
    def _():
        hn_ref[...] = jnp.zeros_like(hn_ref)
        lg_ref[...] = jnp.zeros_like(lg_ref)


def _outproj(x2d, o_s, o_d, o_m, w_o, ln_g, rw_hi, rw_lo, rb, t_all, row0, prev=None):
    t, d = x2d.shape
    tm = min(ROW_TILE, t)
    blk0 = row0 // tm
    n_own = t // tm
    n_steps = n_own if prev is not None else t_all // tm
    row = lambda i: (jnp.minimum(i, n_own - 1), 0)
    row_off = lambda i: (i + blk0, 0)
    const = lambda i: (0, 0)
    in_specs = [pl.BlockSpec((tm, d), row),
                pl.BlockSpec((tm, o_s.shape[1]), row),
                pl.BlockSpec((tm, o_d.shape[1]), row),
                pl.BlockSpec((tm, o_m.shape[1]), row),
                pl.BlockSpec((d, d), const),
                pl.BlockSpec((1, d), const),
                pl.BlockSpec((d, LANES), const),
                pl.BlockSpec((d, LANES), const),
                pl.BlockSpec((1, LANES), const)]
    args = [x2d, o_s, o_d, o_m, w_o, ln_g.reshape(1, d), rw_hi, rw_lo, rb]
    aliases = {}
    if prev is not None:
        in_specs += [pl.BlockSpec(memory_space=pl.ANY), pl.BlockSpec(memory_space=pl.ANY)]
        aliases = {len(args): 1, len(args) + 1: 2}
        args += list(prev)
    return pl.pallas_call(
        functools.partial(_outproj_kernel, n_own=n_own),
        grid=(n_steps,),
        in_specs=in_specs,
        out_specs=[pl.BlockSpec((tm, d), row),
                   pl.BlockSpec((tm, d), row_off),
                   pl.BlockSpec((tm, LANES), row_off)],
        out_shape=[jax.ShapeDtypeStruct((t, d), F32),
                   jax.ShapeDtypeStruct((t_all, d), BF16),
                   jax.ShapeDtypeStruct((t_all, LANES), F32)],
        input_output_aliases=aliases,
        compiler_params=_cparams("arbitrary"),
        name="outproj_router",
    )(*args)


def _route_kernel(lg_ref, pos_ref, post_ref, g_ref, cnt_ref):
    tm = lg_ref.shape[0]
    lane = lax.broadcasted_iota(I32, (tm, LANES), 1).astype(F32)
    l = jnp.where(lane < N_EXPERTS, lg_ref[...], -jnp.inf)
    vals, idxs = [], []
    for _k in range(TOP_K):
        m = jnp.max(l, axis=-1, keepdims=True)
        idx = jnp.min(jnp.where(l == m, lane, float(LANES)), axis=-1, keepdims=True)
        l = jnp.where(lane == idx, -jnp.inf, l)
        vals.append(m)
        idxs.append(idx)
    ex = [jnp.exp(v - vals[0]) for v in vals]
    den = ex[0] + ex[1] + ex[2] + ex[3]
    member = jnp.zeros((tm, LANES), F32)
    for idx in idxs:
        member = member + jnp.where(lane == idx, 1.0, 0.0)
    ri = lax.broadcasted_iota(I32, (tm, tm), 0)
    ci = lax.broadcasted_iota(I32, (tm, tm), 1)
    strict = jnp.where(ci < ri, 1.0, 0.0).astype(BF16)
    prefix = jnp.dot(strict, member.astype(BF16), preferred_element_type=F32)
    cnt = jnp.sum(member, axis=0, keepdims=True)
    cpad = jnp.ceil(cnt * (1.0 / RUN_ALIGN)) * float(RUN_ALIGN)
    c_hi = jnp.floor(cpad * (1.0 / 256.0))
    c_lo = cpad - 256.0 * c_hi
    ej = lax.broadcasted_iota(I32, (LANES, LANES), 0)
    ee = lax.broadcasted_iota(I32, (LANES, LANES), 1)
    before = jnp.where(ej < ee, 1.0, 0.0).astype(BF16)
    bcast = lambda v: jnp.broadcast_to(v, (SUBLANES, LANES)).astype(BF16)
    off = (256.0 * jnp.dot(bcast(c_hi), before, preferred_element_type=F32)
           + jnp.dot(bcast(c_lo), before, preferred_element_type=F32))[0:1]
    where_in_run = prefix + off
    p_out = jnp.zeros((tm, LANES), F32)
    g_out = jnp.zeros((tm, LANES), F32)
    for k in range(TOP_K):
        pos = jnp.sum(jnp.where(lane == idxs[k], where_in_run, 0.0), axis=-1, keepdims=True)
        p_out = jnp.where(lane == float(k), pos, p_out)
        g_out = jnp.where(lane == float(k), ex[k] / den, g_out)
    pos_ref[...] = p_out[:, :TOP_K]
    post_ref[...] = p_out.T[:SUBLANES, :]
    g_ref[...] = g_out[:, :TOP_K]
    cnt_ref[0] = cnt


def _route(logits):
    t = logits.shape[0]
    tm = MOE_BLK
    nb = t // tm
    return pl.pallas_call(
        _route_kernel,
        grid=(nb,),
        in_specs=[pl.BlockSpec((tm, LANES), lambda i: (i, 0))],
        out_specs=[pl.BlockSpec((tm, TOP_K), lambda i: (i, 0)),
                   pl.BlockSpec((SUBLANES, tm), lambda i: (0, i)),
                   pl.BlockSpec((tm, TOP_K), lambda i: (i, 0)),
                   pl.BlockSpec((1, 1, LANES), lambda i: (i, 0, 0))],
        out_shape=[jax.ShapeDtypeStruct((t, TOP_K), F32),
                   jax.ShapeDtypeStruct((SUBLANES, t), F32),
                   jax.ShapeDtypeStruct((t, TOP_K), F32),
                   jax.ShapeDtypeStruct((nb, 1, LANES), F32)],
        compiler_params=_cparams("parallel"),
        name="route",
    )(logits)


def _run_copies(n, max_rows, src_ref, src0, dst_ref, dst0, sem, wait):
    pos = 0
    bit = max_rows
    while bit >= RUN_ALIGN:
        take = (n & bit) != 0

        def go(pos=pos, bit=bit):
            cp = pltpu.make_async_copy(src_ref.at[pl.ds(pl.multiple_of(src0 + pos, RUN_ALIGN), bit)],
                                       dst_ref.at[pl.ds(pl.multiple_of(dst0 + pos, RUN_ALIGN), bit)], sem)
            cp.wait() if wait else cp.start()

        pl.when(take)(go)
        pos = pos + jnp.where(take, bit, 0)
        bit //= 2


RUN_SIZES = tuple(MOE_BLK >> i for i in range((MOE_BLK // RUN_ALIGN).bit_length()))


def _piece_copies(b, cnt_ref, loc_ref, glob_ref, local_ref, global_hbm, sem, to_global, wait):
    for c, rows in enumerate(RUN_SIZES):
        base = b * len(RUN_SIZES) + c

        def body(s, carry, rows=rows, base=base):
            j = base * N_EXPERTS + s
            loc = local_ref.at[pl.ds(pl.multiple_of(loc_ref[j], RUN_ALIGN), rows)]
            glob = global_hbm.at[pl.ds(pl.multiple_of(glob_ref[j], RUN_ALIGN), rows)]
            cp = pltpu.make_async_copy(loc, glob, sem) if to_global else pltpu.make_async_copy(glob, loc, sem)
            cp.wait() if wait else cp.start()
            return carry

        lax.fori_loop(0, cnt_ref[base], body, 0)


def _dispatch_kernel(cnt_ref, loc_ref, glob_ref, estart_ref, elen_ref, nused_ref,
                     hn_ref, post_ref, xs_hbm, buf_ref, zero_ref, sem, zsem):
    b = pl.program_id(0)
    nb = pl.num_programs(0)
    slot = b % 2
    tm = hn_ref.shape[0]
    x = hn_ref[...].astype(BF16)
    post = post_ref[...]
    P = PERM_CHUNK
    for c in range(BLK_ROWS // P):
        r = (lax.broadcasted_iota(I32, (P, tm), 0) + c * P).astype(F32)
        sel = jnp.zeros((P, tm), F32)
        for k in range(TOP_K):
            sel = jnp.where(r == post[k:k + 1, :], 1.0, sel)
        buf_ref[slot, c * P:(c + 1) * P, :] = jnp.dot(sel.astype(BF16), x,
                                                      preferred_element_type=F32).astype(BF16)

    def runs(blk, s, wait):
        _piece_copies(blk, cnt_ref, loc_ref, glob_ref, buf_ref.at[s], xs_hbm, sem.at[s], True, wait)

    runs(b, slot, False)

    @pl.when(b == 0)
    def _():
        zero_ref[...] = jnp.zeros_like(zero_ref)

        def tail(wait):
            def body(e, c):
                n = (MOE_TILE - elen_ref[e] % MOE_TILE) % MOE_TILE
                _run_copies(n, MOE_TILE // 2, zero_ref, 0, xs_hbm, estart_ref[e] + elen_ref[e], zsem, wait)
                return c
            lax.fori_loop(0, N_EXPERTS, body, 0)

            def free_tile(ti, c):
                for half in range(2):
                    _run_copies(MOE_TILE // 2, MOE_TILE // 2, zero_ref, 0, xs_hbm,
                                ti * MOE_TILE + half * (MOE_TILE // 2), zsem, wait)
                return c
            lax.fori_loop(nused_ref[0], xs_hbm.shape[0] // MOE_TILE, free_tile, 0)

        tail(False)
        tail(True)

    pl.when(b > 0)(lambda: runs(b - 1, 1 - slot, True))
    pl.when(b == nb - 1)(lambda: runs(b, slot, True))


def _dispatch(hn, post, piece_cnt, piece_loc, piece_glob, e_start, e_len, n_used, n_rows):
    t, d = hn.shape
    grid_spec = pltpu.PrefetchScalarGridSpec(
        num_scalar_prefetch=6,
        grid=(t // MOE_BLK,),
        in_specs=[pl.BlockSpec((MOE_BLK, d), lambda i, *_: (i, 0)),
                  pl.BlockSpec((SUBLANES, MOE_BLK), lambda i, *_: (0, i))],
        out_specs=pl.BlockSpec(memory_space=pl.ANY),
        scratch_shapes=[pltpu.VMEM((2, BLK_ROWS, d), BF16),
                        pltpu.VMEM((MOE_TILE, d), BF16),
                        pltpu.SemaphoreType.DMA((2,)),
                        pltpu.SemaphoreType.DMA(())],
    )
    return pl.pallas_call(
        _dispatch_kernel,
        grid_spec=grid_spec,
        out_shape=jax.ShapeDtypeStruct((n_rows, d), BF16),
        compiler_params=_cparams("arbitrary"),
        name="dispatch",
    )(piece_cnt, piece_loc, piece_glob, e_start, e_len, n_used, hn, post)


def _expert_kernel(te_ref, nu_ref, x_ref, w1_ref, b1_ref, w2_ref, b2_ref, y_ref, w1b_ref, w2b_ref):
    i = pl.program_id(0)
    live = i < nu_ref[0]

    @pl.when(live & ((i == 0) | (te_ref[i] != te_ref[jnp.maximum(i - 1, 0)])))
    def _():
        w1b_ref[...] = w1_ref[0].astype(BF16)
        w2b_ref[...] = w2_ref[0].astype(BF16)

    @pl.when(live)
    def _():
        f = w2_ref.shape[1]
        h = jnp.dot(x_ref[...], w1b_ref[...], preferred_element_type=F32) + b1_ref[0]
        glu = jnp.minimum(h[:, :f], SWIGLU_LIMIT)
        lin = jnp.clip(h[:, f:], -SWIGLU_LIMIT, SWIGLU_LIMIT)
        act = glu * jax.nn.sigmoid(SWIGLU_ALPHA * glu) * (lin + 1.0)
        y = jnp.dot(act.astype(BF16), w2b_ref[...], preferred_element_type=F32) + b2_ref[0]
        y_ref[...] = y.astype(BF16)

    @pl.when(i >= nu_ref[0])
    def _():
        y_ref[...] = jnp.zeros_like(y_ref)


def _experts(xs, tile_expert, n_used, w1, b1, w2, b2):
    n_rows, d = xs.shape
    f2 = w1.shape[2]
    f = w2.shape[1]
    n_tiles = n_rows // MOE_TILE
    live = lambda i, te, nu: (jnp.minimum(i, nu[0] - 1), 0)
    every = lambda i, te, nu: (i, 0)
    wsel = lambda i, te, nu: (te[i], 0, 0)
    grid_spec = pltpu.PrefetchScalarGridSpec(
        num_scalar_prefetch=2,
        grid=(n_tiles,),
        in_specs=[pl.BlockSpec((MOE_TILE, d), live),
                  pl.BlockSpec((1, d, f2), wsel),
                  pl.BlockSpec((1, 1, f2), wsel),
                  pl.BlockSpec((1, f, d), wsel),
                  pl.BlockSpec((1, 1, d), wsel)],
        out_specs=pl.BlockSpec((MOE_TILE, d), every),
        scratch_shapes=[pltpu.VMEM((d, f2), BF16), pltpu.VMEM((f, d), BF16)],
    )
    return pl.pallas_call(
        _expert_kernel,
        grid_spec=grid_spec,
        out_shape=jax.ShapeDtypeStruct((n_rows, d), BF16),
        compiler_params=_cparams("arbitrary"),
        name="experts",
    )(tile_expert, n_used, xs, w1, b1, w2, b2)


def _combine_kernel(cnt_ref, loc_ref, glob_ref, h_ref, pos_ref, g_ref, yb_hbm, y_ref, buf_ref, sem, *, blk0):
    i = pl.program_id(0)
    n_steps = pl.num_programs(0)
    b = i + blk0
    slot = i % 2
    tm, d = h_ref.shape

    def runs(blk, s, wait):
        _piece_copies(blk, cnt_ref, loc_ref, glob_ref, buf_ref.at[s], yb_hbm, sem.at[s], False, wait)

    @pl.when(i == 0)
    def _():
        buf_ref[...] = jnp.zeros_like(buf_ref)
        runs(b, slot, False)

    pl.when(i + 1 < n_steps)(lambda: runs(b + 1, 1 - slot, False))
    runs(b, slot, True)
    pos = pos_ref[...]
    g = g_ref[...]
    y = h_ref[...]
    P = PERM_CHUNK
    for c in range(BLK_ROWS // P):
        col = (lax.broadcasted_iota(I32, (tm, P), 1) + c * P).astype(F32)
        wgt = jnp.zeros((tm, P), F32)
        for k in range(TOP_K):
            wgt = jnp.where(col == pos[:, k:k + 1], g[:, k:k + 1], wgt)
        y = y + jnp.dot(wgt.astype(BF16), buf_ref[slot, c * P:(c + 1) * P, :], preferred_element_type=F32)
    y_ref[...] = y


def _combine(h, pos, gates, piece_cnt, piece_loc, piece_glob, yb, blk0):
    t, d = h.shape
    grid_spec = pltpu.PrefetchScalarGridSpec(
        num_scalar_prefetch=3,
        grid=(t // MOE_BLK,),
        in_specs=[pl.BlockSpec((MOE_BLK, d), lambda i, *_: (i, 0)),
                  pl.BlockSpec((MOE_BLK, TOP_K), lambda i, *_: (i + blk0, 0)),
                  pl.BlockSpec((MOE_BLK, TOP_K), lambda i, *_: (i + blk0, 0)),
                  pl.BlockSpec(memory_space=pl.ANY)],
        out_specs=pl.BlockSpec((MOE_BLK, d), lambda i, *_: (i, 0)),
        scratch_shapes=[pltpu.VMEM((2, BLK_ROWS, d), BF16),
                        pltpu.SemaphoreType.DMA((2,))],
    )
    return pl.pallas_call(
        functools.partial(_combine_kernel, blk0=blk0),
        grid_spec=grid_spec,
        out_shape=jax.ShapeDtypeStruct((t, d), F32),
        compiler_params=_cparams("arbitrary"),
        name="combine",
    )(piece_cnt, piece_loc, piece_glob, h, pos, gates, yb)


def _moe(hn_all, logits_all, h_parts, w1, b1, w2, b2):
    t_all = hn_all.shape[0]
    nb = t_all // MOE_BLK
    pos, post, gates, counts_f = _route(logits_all)
    cnt = counts_f.reshape(nb, LANES)[:, :N_EXPERTS].astype(I32)
    seg_len = (cnt + RUN_ALIGN - 1) // RUN_ALIGN * RUN_ALIGN
    before_e = jnp.arange(N_EXPERTS)[:, None] < jnp.arange(N_EXPERTS)[None, :]
    before_b = jnp.arange(nb)[None, :] < jnp.arange(nb)[:, None]
    seg_off = jnp.sum(jnp.where(before_e[None], seg_len[:, :, None], 0), axis=1)
    e_len = jnp.sum(seg_len, axis=0)
    e_tiles = (e_len + MOE_TILE - 1) // MOE_TILE
    tile_start = jnp.sum(jnp.where(before_e, e_tiles[:, None], 0), axis=0)
    tile_end = tile_start + e_tiles
    e_start = tile_start * MOE_TILE
    seg_dst = e_start[None, :] + jnp.sum(jnp.where(before_b[:, :, None], seg_len[None], 0), axis=1)
    max_rows = t_all * TOP_K + nb * N_EXPERTS * (RUN_ALIGN - 1) + N_EXPERTS * (MOE_TILE - RUN_ALIGN)
    n_tiles = -(-max_rows // MOE_TILE)
    n_rows = n_tiles * MOE_TILE
    n_used = tile_end[-1:].astype(I32)
    tile_expert = jnp.minimum(jnp.sum(tile_end[None, :] <= jnp.arange(n_tiles, dtype=I32)[:, None], axis=1),
                              N_EXPERTS - 1).astype(I32)
    sizes = jnp.array(RUN_SIZES, I32)[None, :, None]
    n_run = seg_len[:, None, :]
    has = (n_run & sizes) != 0
    piece_at = n_run & ~(2 * sizes - 1)
    rank = jnp.sum(jnp.where(before_e[None, None], has[:, :, :, None], False), axis=2)
    slot = jnp.arange(N_EXPERTS)
    put = has[..., None] & (rank[..., None] == slot)
    listed = lambda v: jnp.sum(jnp.where(put, v[..., None], 0), axis=2)
    piece_loc = listed(seg_off[:, None, :] + piece_at)
    piece_glob = listed(seg_dst[:, None, :] + piece_at)
    piece_cnt = jnp.sum(has, axis=2)
    flat = lambda a: a.reshape(-1).astype(I32)
    tables = (flat(piece_cnt), flat(piece_loc), flat(piece_glob))
    xs = _dispatch(hn_all, post, *tables, flat(e_start), flat(e_len), n_used, n_rows)
    yb = _experts(xs, tile_expert, n_used, w1, b1, w2, b2)
    outs = []
    row = 0
    for h in h_parts:
        outs.append(_combine(h, pos, gates, *tables, yb, row // MOE_BLK))
        row += h.shape[0]
    return outs


GDN_ROWS = 4 * GDN_CHUNK
CONV_HALO = SUBLANES
NEUMANN_SPLIT = 2


def _softplus(x):
    return jnp.maximum(x, 0.0) + jnp.log1p(jnp.exp(-jnp.abs(x)))


def _gdn_prompt_kernel(u_ref, ab_ref, gate_ref, cw_ref, alog_ref, dtb_ref, ng_ref, o_ref, s_ref, ubuf_ref):
    step = pl.program_id(0)
    NB = u_ref.shape[0]
    R = GDN_ROWS
    C = GDN_CHUNK
    NC = R // C

    @pl.when(step == 0)
    def _():
        ubuf_ref[:, 0:CONV_HALO, :] = jnp.zeros((NB, CONV_HALO, ubuf_ref.shape[2]), F32)
        s_ref[...] = jnp.zeros_like(s_ref)

    ri = lax.broadcasted_iota(I32, (R, R), 0)
    ci = lax.broadcasted_iota(I32, (R, R), 1)
    shift = C.bit_length() - 1
    same = lax.shift_right_logical(ri, shift) == lax.shift_right_logical(ci, shift)
    incl = same & (ci <= ri)
    strict = same & (ci < ri)
    tri = jnp.where(incl, 1.0, 0.0).astype(BF16)
    blk = jnp.where(same, 1.0, 0.0).astype(BF16)
    cw = cw_ref[...]
    ng = ng_ref[...]

    chains = []
    for b in range(NB):
        u = u_ref[b]
        ubuf_ref[b, CONV_HALO:CONV_HALO + R, :] = u
        y = u * cw[GDN_CONV - 1:GDN_CONV, :]
        for j in range(1, GDN_CONV):
            y = y + ubuf_ref[b, CONV_HALO - j:CONV_HALO - j + R, :] * cw[GDN_CONV - 1 - j:GDN_CONV - j, :]
        ubuf_ref[b, 0:CONV_HALO, :] = u[R - CONV_HALO:, :]
        qkv = y * jax.nn.sigmoid(y)
        ab = ab_ref[b]
        g_t = -jnp.exp(alog_ref[...]) * _softplus(ab + dtb_ref[...])
        beta_t = jax.nn.sigmoid(ab)
        parts = _split3(g_t)
        gcum = sum(jnp.dot(tri, p, preferred_element_type=F32) for p in parts)
        gtot = sum(jnp.dot(blk, p, preferred_element_type=F32) for p in parts)
        gcum_t = gcum.T
        qk_n = []
        for t in range(2 * GDN_QK_COLS // LANES):
            x = qkv[:, t * LANES:(t + 1) * LANES]
            x = x * lax.rsqrt(_pair_sumsq(x) + EPS)
            qk_n.append(x * (GDN_DK ** -0.5) if t < GDN_QK_COLS // LANES else x)
        qk_n = jnp.concatenate(qk_n, axis=-1)
        for h in range(N_GDN_HEADS):
            q = qk_n[:, h * GDN_DK:(h + 1) * GDN_DK]
            k = qk_n[:, GDN_QK_COLS + h * GDN_DK:GDN_QK_COLS + (h + 1) * GDN_DK]
            v = qkv[:, 2 * GDN_QK_COLS + h * GDN_DV:2 * GDN_QK_COLS + (h + 1) * GDN_DV]
            gc = gcum[:, h:h + 1]
            gt = gtot[:, h:h + 1]
            beta = beta_t[:, N_GDN_HEADS + h:N_GDN_HEADS + h + 1]
            decay = jnp.exp(jnp.where(incl, gc - gcum_t[h:h + 1, :], -jnp.inf))
            chains.append(dict(
                b=b, h=h,
                a=jnp.where(strict, beta * _bdot_nt(k, k) * decay, 0.0),
                qk=_bdot_nt(q, k) * decay,
                r=jnp.concatenate([v * beta, k * (beta * jnp.exp(gc))], axis=-1),
                q_dec=q * jnp.exp(gc), k_dec=k * jnp.exp(gt - gc), g_last=jnp.exp(gt)))

    dot = lambda x, y: jnp.dot(x, y, preferred_element_type=F32)
    level, j = 1, 0
    while level < C:
        last = 2 * level >= C
        for ch in chains:
            if j < NEUMANN_SPLIT:
                a_hi, a_lo = _split2(ch['a'])
                r_hi, r_lo = _split2(ch['r'])
                upd = dot(a_hi, r_hi) + dot(a_hi, r_lo) + dot(a_lo, r_hi)
            else:
                a_hi = ch['a'].astype(BF16)
                upd = dot(a_hi, ch['r'].astype(BF16))
            ch['r'] = ch['r'] - upd if level == 1 else ch['r'] + upd
            if not last:
                sq = dot(a_hi, a_hi)
                if j + 1 < NEUMANN_SPLIT:
                    sq = sq + dot(a_hi, a_lo) + dot(a_lo, a_hi)
                ch['a'] = sq
        level *= 2
        j += 1

    for ch in chains:
        ch['u'], ch['w'] = ch['r'][:, :GDN_DV], ch['r'][:, GDN_DV:]
        ch['S'] = s_ref[ch['b'], ch['h']]
        ch['k_dec_t'] = ch['k_dec'].T
        ch['outs'] = []
    for c in range(NC):
        sl = slice(c * C, (c + 1) * C)
        for ch in chains:
            S = ch['S']
            v_new = ch['u'][sl] - _bdot(ch['w'][sl], S)
            ch['outs'].append(_bdot(ch['q_dec'][sl], S) + _bdot(ch['qk'][sl, sl], v_new))
            ch['S'] = S * ch['g_last'][c * C:c * C + 1, :] + _bdot(ch['k_dec_t'][:, sl], v_new)
    for ch in chains:
        b, h = ch['b'], ch['h']
        s_ref[b, h] = ch['S']
        o = jnp.concatenate(ch['outs'], axis=0)
        o = o * lax.rsqrt(jnp.mean(o * o, axis=-1, keepdims=True) + EPS) * ng
        gh = gate_ref[b, :, h * GDN_DV:(h + 1) * GDN_DV]
        o_ref[b, :, h * GDN_DV:(h + 1) * GDN_DV] = o * (gh * jax.nn.sigmoid(gh))


def _gdn_prompt(z3, conv_w, a_log, dt_bias, norm_g):
    B, S, _ = z3.shape
    R = GDN_ROWS
    lanes4 = lambda a: jnp.pad(a, (0, LANES - a.shape[0])).reshape(1, LANES)
    return pl.pallas_call(
        _gdn_prompt_kernel,
        grid=(S // R,),
        in_specs=[pl.BlockSpec((B, R, GDN_CONV_CH), lambda s: (0, s, Z_GDN // GDN_CONV_CH)),
                  pl.BlockSpec((B, R, LANES), lambda s: (0, s, Z_AB // LANES)),
                  pl.BlockSpec((B, R, GDN_V_COLS), lambda s: (0, s, Z_GATE // GDN_V_COLS)),
                  pl.BlockSpec((GDN_CONV, GDN_CONV_CH), lambda s: (0, 0)),
                  pl.BlockSpec((1, LANES), lambda s: (0, 0)),
                  pl.BlockSpec((1, LANES), lambda s: (0, 0)),
                  pl.BlockSpec((1, GDN_DV), lambda s: (0, 0))],
        out_specs=[pl.BlockSpec((B, R, GDN_V_COLS), lambda s: (0, s, 0)),
                   pl.BlockSpec((B, N_GDN_HEADS, GDN_DK, GDN_DV), lambda s: (0, 0, 0, 0))],
        out_shape=[jax.ShapeDtypeStruct((B, S, GDN_V_COLS), F32),
                   jax.ShapeDtypeStruct((B, N_GDN_HEADS, GDN_DK, GDN_DV), F32)],
        scratch_shapes=[pltpu.VMEM((B, CONV_HALO + R, GDN_CONV_CH), F32)],
        compiler_params=_cparams("arbitrary"),
        name="gdn_prompt",
    )(z3, z3, z3, conv_w, lanes4(a_log), lanes4(dt_bias), norm_g.reshape(1, GDN_DV))


def _pair_sumsq(x):
    li = lax.broadcasted_iota(I32, (LANES, LANES), 0) // HEAD_DIM
    lj = lax.broadcasted_iota(I32, (LANES, LANES), 1) // HEAD_DIM
    same = jnp.where(li == lj, 1.0, 0.0).astype(BF16)
    hi, lo = _split2(x * x)
    return jnp.dot(hi, same, preferred_element_type=F32) + jnp.dot(lo, same, preferred_element_type=F32)


def _pair_rms(x, g):
    return x * lax.rsqrt(_pair_sumsq(x) * (1.0 / HEAD_DIM) + EPS) * g


def _first_half(shape):
    return lax.broadcasted_iota(I32, shape, 1) < HEAD_DIM


LOG2E = 1.4426950408889634


SWA_BLOCKS = 2


def _swa_prompt_kernel(sink_ref, q_ref, kc_ref, kp_ref, vc_ref, vp_ref, qg_ref, kg_ref, o_ref, kn_ref,
                       bias_ref):
    first = (pl.program_id(0) == 0) & (pl.program_id(1) == 0)
    n = pl.program_id(1)
    W = WINDOW
    NQ = SWA_BLOCKS

    @pl.when(first)
    def _():
        qi = lax.broadcasted_iota(I32, (W, 2 * W), 0)
        kj = lax.broadcasted_iota(I32, (W, 2 * W), 1)
        dist = qi + W - kj
        band = (dist >= 0) & (dist < W)
        distf = dist.astype(F32)
        for has_prev in range(2):
            mask = jnp.where(band & ((has_prev == 1) | (kj >= W)), 0.0, -jnp.inf)
            for head in range(N_SWA_HEADS):
                slope = 2.0 ** (-(8.0 / N_SWA_HEADS) * (head + 1))
                bias_ref[has_prev, head] = mask - (slope * LOG2E) * distf

    kg = kg_ref[...]
    qg = qg_ref[...]
    kc = _pair_rms(kc_ref[0], kg)
    kn_ref[0] = kc[(NQ - 1) * W:]
    k3 = jnp.concatenate([_pair_rms(kp_ref[0], kg), kc], axis=0)
    v3 = jnp.concatenate([vp_ref[0], vc_ref[0]], axis=0)
    fh = _first_half(k3.shape)
    k3r = pltpu.roll(k3, HEAD_DIM, 1)
    v3r = pltpu.roll(v3, HEAD_DIM, 1)
    kdup = (jnp.where(fh, k3, k3r).astype(BF16), jnp.where(fh, k3r, k3).astype(BF16))
    vdup = (jnp.where(fh, v3, v3r).astype(BF16), jnp.where(fh, v3r, v3).astype(BF16))
    fq = _first_half((W, LANES))
    kv_of = lambda head: head // SWA_GROUP
    probs = [(j, head) for j in range(NQ) for head in range(N_SWA_HEADS)]
    keys = lambda j: slice(j * W, (j + 2) * W)
    qts = [[_pair_rms(q_ref[0, j * W:(j + 1) * W, t * LANES:(t + 1) * LANES], qg) * (ATTN_SCALE * LOG2E)
            for t in range(SWA_Q_COLS // LANES)] for j in range(NQ)]
    qms = [jnp.where(fq == (head % 2 == 0), qts[j][head // 2], 0.0).astype(BF16) for j, head in probs]
    table = [jnp.where(n > 0, 1, 0)] + [1] * (NQ - 1)
    ss = [_bdot_nt(qms[i], kdup[kv_of(head)][keys(j)]) + bias_ref[table[j], head]
          for i, (j, head) in enumerate(probs)]
    sinks = [sink_ref[head] * LOG2E for head in range(N_SWA_HEADS)]
    ms = [jnp.maximum(jnp.max(ss[i], axis=-1, keepdims=True), sinks[head]) for i, (j, head) in enumerate(probs)]
    ps = [jnp.exp2(ss[i] - ms[i]) for i in range(len(probs))]
    dens = [jnp.sum(ps[i], axis=-1, keepdims=True) + jnp.exp2(sinks[head] - ms[i])
            for i, (j, head) in enumerate(probs)]
    outs = [_bdot(ps[i], vdup[kv_of(head)][keys(j)]) / dens[i] for i, (j, head) in enumerate(probs)]
    for j in range(NQ):
        for t in range(SWA_Q_COLS // LANES):
            o_ref[0, j * W:(j + 1) * W, t * LANES:(t + 1) * LANES] = jnp.where(
                fq, outs[j * N_SWA_HEADS + 2 * t], outs[j * N_SWA_HEADS + 2 * t + 1])


def _swa_prompt(z3, q_g, k_g, sinks):
    B, S, _ = z3.shape
    W = WINDOW
    NQ = SWA_BLOCKS
    twice = lambda g: jnp.concatenate([g, g]).reshape(1, LANES)
    kcol, vcol = Z_K // LANES, Z_V // LANES
    prev = lambda col: pl.BlockSpec((1, W, LANES), lambda b, n: (b, jnp.maximum(NQ * n - 1, 0), col))
    grid_spec = pltpu.PrefetchScalarGridSpec(
        num_scalar_prefetch=0,
        grid=(B, S // (NQ * W)),
        in_specs=[pl.BlockSpec(memory_space=pltpu.SMEM),
                  pl.BlockSpec((1, NQ * W, SWA_Q_COLS), lambda b, n: (b, n, 0)),
                  pl.BlockSpec((1, NQ * W, LANES), lambda b, n: (b, n, kcol)),
                  prev(kcol),
                  pl.BlockSpec((1, NQ * W, LANES), lambda b, n: (b, n, vcol)),
                  prev(vcol),
                  pl.BlockSpec((1, LANES), lambda b, n: (0, 0)),
                  pl.BlockSpec((1, LANES), lambda b, n: (0, 0))],
        out_specs=[pl.BlockSpec((1, NQ * W, SWA_Q_COLS), lambda b, n: (b, n, 0)),
                   pl.BlockSpec((1, W, LANES), lambda b, n: (b, 0, 0))],
        scratch_shapes=[pltpu.VMEM((2, N_SWA_HEADS, W, 2 * W), F32)],
    )
    return pl.pallas_call(
        _swa_prompt_kernel,
        grid_spec=grid_spec,
        out_shape=[jax.ShapeDtypeStruct((B, S, SWA_Q_COLS), F32),
                   jax.ShapeDtypeStruct((B, W, LANES), F32)],
        compiler_params=_cparams("arbitrary", "arbitrary"),
        name="swa_prompt",
    )(sinks, z3, z3, z3, z3, z3, twice(q_g), twice(k_g))


def _mem_kv_kernel(m_ref, g_ref, w_ref, kg_ref, k_ref, v_ref):
    n = _rms_rows(m_ref[...], g_ref[...])
    kv = jnp.dot(n.astype(BF16), w_ref[...], preferred_element_type=F32)
    kg = kg_ref[...]
    for t in range(MEM_Q_COLS // LANES):
        k_ref[:, t * LANES:(t + 1) * LANES] = _pair_rms(kv[:, t * LANES:(t + 1) * LANES], kg)
    v_ref[...] = kv[:, MEM_Q_COLS:]


def _mem_kv(mem2d, ln_g, w_kv, k_g):
    r, d = mem2d.shape
    twice = jnp.concatenate([k_g, k_g]).reshape(1, LANES)
    full = lambda shape: pl.BlockSpec(shape, lambda i: (0,) * len(shape))
    return pl.pallas_call(
        _mem_kv_kernel,
        grid=(1,),
        in_specs=[full((r, d)), full((1, d)), full((d, 2 * MEM_Q_COLS)), full((1, LANES))],
        out_specs=[full((r, MEM_Q_COLS)), full((r, MEM_Q_COLS))],
        out_shape=[jax.ShapeDtypeStruct((r, MEM_Q_COLS), F32), jax.ShapeDtypeStruct((r, MEM_Q_COLS), F32)],
        compiler_params=_cparams("arbitrary"),
        name="mem_kv",
    )(mem2d, ln_g.reshape(1, d), w_kv.astype(BF16), twice)


def _mem_attn_kernel(q_ref, k_ref, v_ref, qg_ref, o_ref):
    qg = qg_ref[...]
    rows = q_ref.shape[1]
    fq = _first_half((rows, LANES))
    heads = range(N_MEM_HEADS)
    tile = lambda t: slice(t * LANES, (t + 1) * LANES)
    qts = [_pair_rms(q_ref[0, :, tile(t)], qg) * (ATTN_SCALE * LOG2E) for t in range(MEM_Q_COLS // LANES)]
    kts = [k_ref[0, :, tile(t)].astype(BF16) for t in range(MEM_Q_COLS // LANES)]
    vts = [v_ref[0, :, tile(t)].astype(BF16) for t in range(MEM_Q_COLS // LANES)]
    ss = [_bdot_nt(jnp.where(fq == (h % 2 == 0), qts[h // 2], 0.0), kts[h // 2]) for h in heads]
    ps = [jnp.exp2(s - jnp.max(s, axis=-1, keepdims=True)) for s in ss]
    outs = [_bdot(ps[h], vts[h // 2]) / jnp.sum(ps[h], axis=-1, keepdims=True) for h in heads]
    for t in range(MEM_Q_COLS // LANES):
        o_ref[0, :, tile(t)] = jnp.where(fq, outs[2 * t], outs[2 * t + 1])


MEM_Q_TILE = 512


def _mem_attn_prompt(z3, mem_k, mem_v, q_g):
    B, S, _ = z3.shape
    M = mem_k.shape[1]
    tq = MEM_Q_TILE
    twice = jnp.concatenate([q_g, q_g]).reshape(1, LANES)
    return pl.pallas_call(
        _mem_attn_kernel,
        grid=(B, S // tq),
        in_specs=[pl.BlockSpec((1, tq, MEM_Q_COLS), lambda b, i: (b, i, Z_QM // MEM_Q_COLS)),
                  pl.BlockSpec((1, M, MEM_Q_COLS), lambda b, i: (b, 0, 0)),
                  pl.BlockSpec((1, M, MEM_Q_COLS), lambda b, i: (b, 0, 0)),
                  pl.BlockSpec((1, LANES), lambda b, i: (0, 0))],
        out_specs=pl.BlockSpec((1, tq, MEM_Q_COLS), lambda b, i: (b, i, 0)),
        out_shape=jax.ShapeDtypeStruct((B, S, MEM_Q_COLS), F32),
        compiler_params=_cparams("parallel", "parallel"),
        name="mem_attn_prompt",
    )(z3, mem_k, mem_v, twice)


PAIR = 2


def _swa_sample_kernel(sink_ref, q_ref, k_ref, v_ref, ck_ref, cv_ref, qg_ref, kg_ref, o_ref, nk_ref, nv_ref, *, L):
    n_seq = ck_ref.shape[0]
    Wb = ck_ref.shape[2]
    rows8 = SUBLANES
    nh = N_SWA_HEADS
    kn = _pair_rms(k_ref[...], kg_ref[...])
    qg = qg_ref[...]
    R = nh * rows8
    row = lax.broadcasted_iota(I32, (R, 1), 0)
    head = row // rows8
    seq_in_pair = (row % rows8) // L
    step = (row % L).astype(F32)
    slope = jnp.exp2(-(8.0 / N_SWA_HEADS) * (head.astype(F32) + 1.0))
    sink = jnp.zeros((R, 1), F32)
    for h in range(nh):
        sink = jnp.where(head == h, sink_ref[h], sink)
    key = lax.broadcasted_iota(I32, (R, Wb), 1).astype(F32)
    dist_c = float(Wb) + step - key
    bias_c = jnp.where(dist_c < float(WINDOW), 0.0, -jnp.inf)
    col = lax.broadcasted_iota(I32, (R, rows8), 1)
    dist_n = step - (col % L).astype(F32)
    bias_n = jnp.where((dist_n >= 0.0) & ((col // L) == seq_in_pair), 0.0, -jnp.inf)
    fh8 = _first_half((rows8, LANES))
    fhR = _first_half((R, LANES))
    kv_first = head < SWA_GROUP
    pairs = range(n_seq // PAIR)
    bias_c = bias_c - slope * dist_c
    bias_n = bias_n - slope * dist_n

    def stacked_queries(pr):
        r0 = pr * rows8
        pieces = []
        for t in range(SWA_Q_COLS // LANES):
            qt = _pair_rms(q_ref[r0:r0 + rows8, t * LANES:(t + 1) * LANES], qg) * ATTN_SCALE
            qr = pltpu.roll(qt, HEAD_DIM, 1)
            kv = t // (SWA_GROUP // 2)
            for half in range(2):
                src = qt if half == kv else qr
                pieces.append(jnp.where(fh8 == (kv == 0), src, 0.0))
        return jnp.concatenate(pieces, axis=0).astype(BF16)

    qs = [stacked_queries(pr) for pr in pairs]
    s_c = [jnp.where(seq_in_pair == 0, _bdot(qs[pr], ck_ref[pr * PAIR]), _bdot(qs[pr], ck_ref[pr * PAIR + 1]))
           + bias_c for pr in pairs]
    s_n = [_bdot_nt(qs[pr], kn[pr * rows8:(pr + 1) * rows8]) + bias_n for pr in pairs]
    m = [jnp.maximum(jnp.maximum(jnp.max(s_c[pr], axis=-1, keepdims=True),
                                 jnp.max(s_n[pr], axis=-1, keepdims=True)), sink) for pr in pairs]
    p_c = [jnp.exp(s_c[pr] - m[pr]) for pr in pairs]
    p_n = [jnp.exp(s_n[pr] - m[pr]) for pr in pairs]
    den = [jnp.sum(p_c[pr], axis=-1, keepdims=True) + jnp.sum(p_n[pr], axis=-1, keepdims=True)
           + jnp.exp(sink - m[pr]) for pr in pairs]
    outs = [(_bdot(p_n[pr], v_ref[pr * rows8:(pr + 1) * rows8, :])
             + _bdot_nt(jnp.where(seq_in_pair == 0, p_c[pr], 0.0), cv_ref[pr * PAIR])
             + _bdot_nt(jnp.where(seq_in_pair == 1, p_c[pr], 0.0), cv_ref[pr * PAIR + 1])) / den[pr] for pr in pairs]

    pos_r = lax.broadcasted_iota(I32, (Wb, rows8), 0)
    new_c = lax.broadcasted_iota(I32, (Wb, rows8), 1)
    tail = lax.broadcasted_iota(I32, (SWA_KV_COLS, Wb), 1) >= Wb - L

    def shifted(old, new8, j):
        place = jnp.where((pos_r == Wb - L + new_c % L) & (new_c // L == j), 1.0, 0.0).astype(BF16)
        rows_at_tail = sum(jnp.dot(place, part, preferred_element_type=F32) for part in _split3(new8))
        return jnp.where(tail, rows_at_tail.T, pltpu.roll(old, Wb - L, 1))

    for pr in pairs:
        for j in range(PAIR):
            s = pr * PAIR + j
            nk_ref[s] = shifted(ck_ref[s], kn[pr * rows8:(pr + 1) * rows8], j)
            nv_ref[s] = shifted(cv_ref[s], v_ref[pr * rows8:(pr + 1) * rows8, :], j)
    for pr in pairs:
        r0 = pr * rows8
        o = jnp.where(fhR == kv_first, outs[pr], 0.0)
        o_r = pltpu.roll(o, HEAD_DIM, 1)
        for t in range(SWA_Q_COLS // LANES):
            kv = t // (SWA_GROUP // 2)
            halves = []
            for half in range(2):
                h = 2 * t + half
                src = o if half == kv else o_r
                halves.append(src[h * rows8:(h + 1) * rows8])
            o_ref[r0:r0 + rows8, t * LANES:(t + 1) * LANES] = jnp.where(fh8, halves[0], halves[1])


SAMPLE_SEQS = 8


def _swa_sample(z_s, cache_k, cache_v, q_g, k_g, sinks, L):
    t = z_s.shape[0]
    DB, Wb, KV, HD = cache_k.shape
    ns = SAMPLE_SEQS
    rows = ns * L
    twice = lambda g: jnp.concatenate([g, g]).reshape(1, LANES)
    fm = lambda c: jnp.transpose(c, (0, 2, 3, 1)).reshape(DB, KV * HD, Wb)
    back = lambda c: jnp.transpose(c.reshape(DB, KV, HD, Wb), (0, 3, 1, 2))
    cache = pl.BlockSpec((ns, KV * HD, Wb), lambda i: (i, 0, 0))
    o, nk, nv = pl.pallas_call(
        functools.partial(_swa_sample_kernel, L=L),
        grid=(DB // ns,),
        in_specs=[pl.BlockSpec(memory_space=pltpu.SMEM),
                  pl.BlockSpec((rows, SWA_Q_COLS), lambda i: (i, 0)),
                  pl.BlockSpec((rows, LANES), lambda i: (i, Z_K // LANES)),
                  pl.BlockSpec((rows, LANES), lambda i: (i, Z_V // LANES)),
                  cache, cache,
                  pl.BlockSpec((1, LANES), lambda i: (0, 0)),
                  pl.BlockSpec((1, LANES), lambda i: (0, 0))],
        out_specs=[pl.BlockSpec((rows, SWA_Q_COLS), lambda i: (i, 0)), cache, cache],
        out_shape=[jax.ShapeDtypeStruct((t, SWA_Q_COLS), F32),
                   jax.ShapeDtypeStruct((DB, KV * HD, Wb), F32),
                   jax.ShapeDtypeStruct((DB, KV * HD, Wb), F32)],
        compiler_params=_cparams("parallel"),
        name="swa_sample",
    )(sinks, z_s, z_s, z_s, fm(cache_k), fm(cache_v), twice(q_g), twice(k_g))
    return o, back(nk), back(nv)


def _mem_sample_kernel(q_ref, mk_ref, mv_ref, qg_ref, o_ref, *, L):
    n_seq = mk_ref.shape[0]
    rows8 = SUBLANES
    nh = N_MEM_HEADS
    qg = qg_ref[...]
    R = nh * rows8
    row = lax.broadcasted_iota(I32, (R, 1), 0)
    seq_in_pair = (row % rows8) // L
    lane_head8 = lax.broadcasted_iota(I32, (rows8, MEM_Q_COLS), 1) // HEAD_DIM
    pairs = range(n_seq // PAIR)

    def stacked_queries(pr):
        qn = jnp.concatenate([_pair_rms(q_ref[pr * rows8:(pr + 1) * rows8, t * LANES:(t + 1) * LANES], qg)
                              for t in range(MEM_Q_COLS // LANES)], axis=-1) * ATTN_SCALE
        return jnp.concatenate([jnp.where(lane_head8 == h, qn, 0.0) for h in range(nh)], axis=0).astype(BF16)

    qs = [stacked_queries(pr) for pr in pairs]
    ss = [jnp.where(seq_in_pair == 0, _bdot(qs[pr], mk_ref[pr * PAIR]), _bdot(qs[pr], mk_ref[pr * PAIR + 1]))
          for pr in pairs]
    ps = [jnp.exp(s - jnp.max(s, axis=-1, keepdims=True)) for s in ss]
    outs = [(_bdot_nt(jnp.where(seq_in_pair == 0, ps[pr], 0.0), mv_ref[pr * PAIR])
             + _bdot_nt(jnp.where(seq_in_pair == 1, ps[pr], 0.0), mv_ref[pr * PAIR + 1]))
            / jnp.sum(ps[pr], axis=-1, keepdims=True) for pr in pairs]
    for pr in pairs:
        o = jnp.zeros((rows8, MEM_Q_COLS), F32)
        for h in range(nh):
            o = jnp.where(lane_head8 == h, outs[pr][h * rows8:(h + 1) * rows8], o)
        o_ref[pr * rows8:(pr + 1) * rows8, :] = o


def _mem_attn_sample(z_s, mem_k, mem_v, q_g, L):
    t = z_s.shape[0]
    DB, M, H, HD = mem_k.shape
    ns = SAMPLE_SEQS
    rows = ns * L
    twice = jnp.concatenate([q_g, q_g]).reshape(1, LANES)
    fm = lambda c: jnp.transpose(c, (0, 2, 3, 1)).reshape(DB, H * HD, M)
    cache = pl.BlockSpec((ns, H * HD, M), lambda i: (i, 0, 0))
    return pl.pallas_call(
        functools.partial(_mem_sample_kernel, L=L),
        grid=(DB // ns,),
        in_specs=[pl.BlockSpec((rows, MEM_Q_COLS), lambda i: (i, Z_QM // MEM_Q_COLS)),
                  cache, cache,
                  pl.BlockSpec((1, LANES), lambda i: (0, 0))],
        out_specs=pl.BlockSpec((rows, MEM_Q_COLS), lambda i: (i, 0)),
        out_shape=jax.ShapeDtypeStruct((t, MEM_Q_COLS), F32),
        compiler_params=_cparams("parallel"),
        name="mem_attn_sample",
    )(z_s, fm(mem_k), fm(mem_v), twice)


def _gdn_sample_kernel(uq_ref, uk_ref, uv_ref, bq_ref, bk_ref, bv_ref, wq_ref, wk_ref, wv_ref,
                       ab_ref, gate_ref, alog_ref, dtb_ref, ng_ref, s_in_ref, o_ref, s_ref, kq_ref):
    h = pl.program_id(0)
    L = uq_ref.shape[0]
    nbuf = bq_ref.shape[0]
    DK = GDN_DK

    def conv(u_ref, b_ref, w_ref, t):
        up = [b_ref[i] for i in range(nbuf)] + [u_ref[i] for i in range(L)]
        y = up[t] * w_ref[0]
        for i in range(1, GDN_CONV):
            y = y + up[t + i] * w_ref[i]
        return y * jax.nn.sigmoid(y)

    s_ref[...] = s_in_ref[...]
    ng = ng_ref[...]
    hsel = lax.broadcasted_iota(I32, (SUBLANES, 1), 0)
    pick = lambda m, r: jnp.sum(jnp.where(hsel == r, m, 0.0), axis=0, keepdims=True)
    alog = pick(alog_ref[...], h)
    dtb = pick(dtb_ref[...], h)
    for t in range(L):
        q = conv(uq_ref, bq_ref, wq_ref, t)
        k = conv(uk_ref, bk_ref, wk_ref, t)
        v = conv(uv_ref, bv_ref, wv_ref, t)
        q = q * lax.rsqrt(jnp.sum(q * q, axis=0, keepdims=True) + EPS) * (GDN_DK ** -0.5)
        k = k * lax.rsqrt(jnp.sum(k * k, axis=0, keepdims=True) + EPS)
        ab = ab_ref[t]
        a = pick(ab, h)
        bb = pick(ab, h + N_GDN_HEADS)
        decay = jnp.exp(-jnp.exp(alog) * _softplus(a + dtb))
        beta = jax.nn.sigmoid(bb)
        kq_ref[0] = k
        kq_ref[1] = q

        def decay_and_project(dk, acc):
            s = s_ref[0, dk] * decay
            s_ref[0, dk] = s
            return acc + s * kq_ref[0, pl.ds(dk, 1), :]

        sk = lax.fori_loop(0, DK, decay_and_project, jnp.zeros_like(v), unroll=8)
        u = beta * (v - sk)

        def update_and_read(dk, acc):
            s = s_ref[0, dk] + kq_ref[0, pl.ds(dk, 1), :] * u
            s_ref[0, dk] = s
            return acc + s * kq_ref[1, pl.ds(dk, 1), :]

        o = lax.fori_loop(0, DK, update_and_read, jnp.zeros_like(v), unroll=8)
        o = o * lax.rsqrt(jnp.mean(o * o, axis=0, keepdims=True) + EPS) * ng
        g = gate_ref[t]
        o_ref[t] = o * (g * jax.nn.sigmoid(g))


def _gdn_sample(z_s, conv_buf, state, conv_w, a_log, dt_bias, norm_g, DB, L):
    H = N_GDN_HEADS
    z3 = z_s.reshape(DB, L, Z_COLS)
    u_t = jnp.transpose(z3[:, :, Z_GDN:Z_GATE], (1, 2, 0))
    gate_t = jnp.transpose(z3[:, :, Z_GATE:Z_QM], (1, 2, 0))
    ab_t = jnp.transpose(z3[:, :, Z_AB:Z_AB + SUBLANES], (1, 2, 0))
    buf_t = jnp.transpose(conv_buf, (1, 2, 0))
    s_t = jnp.transpose(state, (1, 2, 3, 0))
    w_col = conv_w.reshape(GDN_CONV, GDN_CONV_CH, 1)
    col8 = lambda a: jnp.pad(a, (0, SUBLANES - a.shape[0])).reshape(SUBLANES, 1)
    nbuf = conv_buf.shape[1]
    part = lambda n, j: pl.BlockSpec((n, GDN_DK, DB), lambda h: (0, j * H + h, 0))
    wpart = lambda j: pl.BlockSpec((GDN_CONV, GDN_DK, 1), lambda h: (0, j * H + h, 0))
    whole = lambda shape: pl.BlockSpec(shape, lambda h: (0,) * len(shape))
    o_t, s_new = pl.pallas_call(
        _gdn_sample_kernel,
        grid=(H,),
        in_specs=[part(L, 0), part(L, 1), part(L, 2), part(nbuf, 0), part(nbuf, 1), part(nbuf, 2),
                  wpart(0), wpart(1), wpart(2),
                  whole((L, SUBLANES, DB)),
                  pl.BlockSpec((L, GDN_DV, DB), lambda h: (0, h, 0)),
                  whole((SUBLANES, 1)), whole((SUBLANES, 1)), whole((GDN_DV, 1)),
                  pl.BlockSpec((1, GDN_DK, GDN_DV, DB), lambda h: (h, 0, 0, 0))],
        out_specs=[pl.BlockSpec((L, GDN_DV, DB), lambda h: (0, h, 0)),
                   pl.BlockSpec((1, GDN_DK, GDN_DV, DB), lambda h: (h, 0, 0, 0))],
        out_shape=[jax.ShapeDtypeStruct((L, H * GDN_DV, DB), F32),
                   jax.ShapeDtypeStruct((H, GDN_DK, GDN_DV, DB), F32)],
        scratch_shapes=[pltpu.VMEM((2, GDN_DK, DB), F32)],
        compiler_params=_cparams("parallel"),
        name="gdn_sample",
    )(u_t, u_t, u_t, buf_t, buf_t, buf_t, w_col, w_col, w_col, ab_t, gate_t,
      col8(a_log), col8(dt_bias), norm_g.reshape(GDN_DV, 1), s_t)
    o = jnp.transpose(o_t, (2, 0, 1)).reshape(DB * L, H * GDN_DV)
    return o, jnp.transpose(s_new, (3, 0, 1, 2))


def kernel(x_prompt, x_sample, cache_swa_k, cache_swa_v, state_gdn, state_gdn_conv, cache_mem_k, cache_mem_v,
           mem_prompt, ln1_g, w_in, swa_q_norm, swa_k_norm, swa_sinks, gdn_conv_w, gdn_a_log, gdn_dt_bias,
           gdn_norm_g, mem_ln_g, w_mem_kv, mem_q_norm, mem_k_norm, w_o, ln2_g, router_w, router_b,
           moe_w1, moe_b1, moe_w2, moe_b2):
    B, S, D = x_prompt.shape
    DB, DL, _ = x_sample.shape
    depth = ln1_g.shape[0]
    assert depth == 1
    l = 0
    tp, ts = B * S, DB * DL
    t_all = tp + ts
    n_ab = 2 * N_GDN_HEADS
    c_ab = SWA_Q_COLS + 2 * SWA_KV_COLS + GDN_CONV_CH
    w = w_in[l]
    w_z = jnp.concatenate([w[:, :c_ab], w[:, c_ab + n_ab:], w[:, c_ab:c_ab + n_ab],
                           jnp.zeros((D, LANES - n_ab), F32)], axis=1).astype(BF16)
    rw = jnp.pad(router_w[l], ((0, 0), (0, LANES - N_EXPERTS)))
    rw_hi = rw.astype(BF16)
    rw_lo = (rw - rw_hi.astype(F32)).astype(BF16)
    rb = jnp.pad(router_b[l], (0, LANES - N_EXPERTS)).reshape(1, LANES)
    wo = w_o[l].astype(BF16)
    w1 = moe_w1[l]
    w2 = moe_w2[l]
    b1 = moe_b1[l].reshape(N_EXPERTS, 1, -1)
    b2 = moe_b2[l].reshape(N_EXPERTS, 1, -1)
    p = {'q_norm': swa_q_norm[l], 'k_norm': swa_k_norm[l], 'sinks': swa_sinks[l], 'conv_w': gdn_conv_w[l],
         'a_log': gdn_a_log[l], 'dt_bias': gdn_dt_bias[l], 'gdn_norm': gdn_norm_g[l], 'mem_q_norm': mem_q_norm[l]}

    xp = x_prompt.reshape(tp, D)
    xs = x_sample.reshape(ts, D)
    z_p = _inproj(xp, ln1_g[l], w_z)
    z_s = _inproj(xs, ln1_g[l], w_z)

    M = mem_prompt.shape[1]
    z_p3 = z_p.reshape(B, S, Z_COLS)
    mk2, mv2 = _mem_kv(mem_prompt.reshape(B * M, D), mem_ln_g[l], w_mem_kv[l], mem_k_norm[l])
    mk = mk2.reshape(B, M, N_MEM_HEADS, HEAD_DIM)
    mv = mv2.reshape(B, M, N_MEM_HEADS, HEAD_DIM)
    os_p, pk = _swa_prompt(z_p3, p['q_norm'], p['k_norm'], p['sinks'])
    od_p, ps = _gdn_prompt(z_p3, p['conv_w'], p['a_log'], p['dt_bias'], p['gdn_norm'])
    om_p = _mem_attn_prompt(z_p3, mk2.reshape(B, M, MEM_Q_COLS), mv2.reshape(B, M, MEM_Q_COLS), p['mem_q_norm'])
    os_p, od_p, om_p = os_p.reshape(tp, -1), od_p.reshape(tp, -1), om_p.reshape(tp, -1)
    pk = pk.reshape(B, WINDOW, N_SWA_KV, HEAD_DIM)
    pv = z_p3[:, S - WINDOW:, Z_V:Z_GDN].reshape(B, WINDOW, N_SWA_KV, HEAD_DIM)
    pc = z_p3[:, S - (GDN_CONV - 1):, Z_GDN:Z_GATE]
    os_s, sk, sv = _swa_sample(z_s, cache_swa_k[l], cache_swa_v[l], p['q_norm'], p['k_norm'], p['sinks'], DL)
    od_s, ss = _gdn_sample(z_s, state_gdn_conv[l], state_gdn[l], p['conv_w'], p['a_log'], p['dt_bias'],
                           p['gdn_norm'], DB, DL)
    om_s = _mem_attn_sample(z_s, cache_mem_k[l], cache_mem_v[l], p['mem_q_norm'], DL)
    z_s3 = z_s.reshape(DB, DL, Z_COLS)
    sc = jnp.concatenate([state_gdn_conv[l], z_s3[:, :, Z_GDN:Z_GATE]], axis=1)[:, DL:]

    h_p, hn_all, lg_all = _outproj(xp, os_p, od_p, om_p, wo, ln2_g[l], rw_hi, rw_lo, rb, t_all, 0)
    h_s, hn_all, lg_all = _outproj(xs, os_s, od_s, om_s, wo, ln2_g[l], rw_hi, rw_lo, rb, t_all, tp,
                                   prev=(hn_all, lg_all))
    y_p, y_s = _moe(hn_all, lg_all, [h_p, h_s], w1, b1, w2, b2)
    return (y_p.reshape(B, S, D), y_s.reshape(DB, DL, D), pk[None], pv[None], ps[None], pc[None], mk[None],
            mv[None], sk[None], sv[None], ss[None], sc[None])
```

```python
import functools

import jax
import jax.numpy as jnp
from jax import lax
from jax.experimental import pallas as pl
from jax.experimental.pallas import tpu as pltpu

F32 = jnp.float32
BF16 = jnp.bfloat16
I32 = jnp.int32

HEAD_DIM = 64
N_SWA_HEADS = 8
N_SWA_KV = 2
SWA_GROUP = N_SWA_HEADS // N_SWA_KV
WINDOW = 128
N_GDN_HEADS = 4
GDN_DK = 64
GDN_DV = 64
GDN_CONV = 4
GDN_CHUNK = 64
N_MEM_HEADS = 4
N_EXPERTS = 32
TOP_K = 4
SWIGLU_ALPHA = 1.702
SWIGLU_LIMIT = 7.0
EPS = 1e-6
ATTN_SCALE = HEAD_DIM ** -0.5

SWA_Q_COLS = N_SWA_HEADS * HEAD_DIM
SWA_KV_COLS = N_SWA_KV * HEAD_DIM
GDN_QK_COLS = N_GDN_HEADS * GDN_DK
GDN_V_COLS = N_GDN_HEADS * GDN_DV
GDN_CONV_CH = 2 * GDN_QK_COLS + GDN_V_COLS
MEM_Q_COLS = N_MEM_HEADS * HEAD_DIM

LANES = 128
SUBLANES = 8
VMEM_LIMIT = 56 * 1024 * 1024

Z_Q = 0
Z_K = Z_Q + SWA_Q_COLS
Z_V = Z_K + SWA_KV_COLS
Z_GDN = Z_V + SWA_KV_COLS
Z_GATE = Z_GDN + GDN_CONV_CH
Z_QM = Z_GATE + GDN_V_COLS
Z_AB = Z_QM + MEM_Q_COLS
Z_COLS = Z_AB + LANES

ROW_TILE = 512
MOE_TILE = 512
MOE_BLK = 512
PERM_CHUNK = 256
RUN_ALIGN = 16
BLK_ROWS = -(-(MOE_BLK * TOP_K + N_EXPERTS * (RUN_ALIGN - 1)) // PERM_CHUNK) * PERM_CHUNK


def _cparams(*sem):
    return pltpu.CompilerParams(dimension_semantics=sem, vmem_limit_bytes=VMEM_LIMIT)


def _bdot(a, b):
    return jnp.dot(a.astype(BF16), b.astype(BF16), preferred_element_type=F32)


def _bdot_nt(a, b):
    return lax.dot_general(a.astype(BF16), b.astype(BF16), (((1,), (1,)), ((), ())),
                           preferred_element_type=F32)


def _bdot_tn(a, b):
    return lax.dot_general(a.astype(BF16), b.astype(BF16), (((0,), (0,)), ((), ())),
                           preferred_element_type=F32)


def _split2(x):
    hi = x.astype(BF16)
    lo = (x - hi.astype(F32)).astype(BF16)
    return hi, lo


def _split3(x):
    hi = x.astype(BF16)
    r = x - hi.astype(F32)
    mid = r.astype(BF16)
    lo = (r - mid.astype(F32)).astype(BF16)
    return hi, mid, lo


def _rms_rows(x, g):
    ms = jnp.mean(x * x, axis=-1, keepdims=True)
    return x * lax.rsqrt(ms + EPS) * g


def _inproj_kernel(x_ref, g_ref, w_ref, z_ref):
    n = _rms_rows(x_ref[...], g_ref[...])
    z_ref[...] = jnp.dot(n.astype(BF16), w_ref[...], preferred_element_type=F32)


def _inproj(x2d, ln_g, w_z):
    t, d = x2d.shape
    tm = min(ROW_TILE, t)
    return pl.pallas_call(
        _inproj_kernel,
        grid=(t // tm,),
        in_specs=[pl.BlockSpec((tm, d), lambda i: (i, 0)),
                  pl.BlockSpec((1, d), lambda i: (0, 0)),
                  pl.BlockSpec((d, Z_COLS), lambda i: (0, 0))],
        out_specs=pl.BlockSpec((tm, Z_COLS), lambda i: (i, 0)),
        out_shape=jax.ShapeDtypeStruct((t, Z_COLS), F32),
        compiler_params=_cparams("parallel"),
        name="inproj",
    )(x2d, ln_g.reshape(1, d), w_z)


def _outproj_kernel(x_ref, os_ref, od_ref, om_ref, wo_ref, g_ref, rwh_ref, rwl_ref, rb_ref,
                    *refs, n_own):
    h_ref, hn_ref, lg_ref = refs[-3:]
    i = pl.program_id(0)

    @pl.when(i < n_own)
    def _():
        n_s = os_ref.shape[1]
        n_d = od_ref.shape[1]
        h = x_ref[...]
        h = h + jnp.dot(os_ref[...].astype(BF16), wo_ref[0:n_s, :], preferred_element_type=F32)
        h = h + jnp.dot(od_ref[...].astype(BF16), wo_ref[n_s:n_s + n_d, :], preferred_element_type=F32)
        h = h + jnp.dot(om_ref[...].astype(BF16), wo_ref[n_s + n_d:, :], preferred_element_type=F32)
        h_ref[...] = h
        hn = _rms_rows(h, g_ref[...])
        hn_ref[...] = hn.astype(BF16)
        hi, lo = _split2(hn)
        lg = (jnp.dot(hi, rwh_ref[...], preferred_element_type=F32)
              + jnp.dot(lo, rwh_ref[...], preferred_element_type=F32)
              + jnp.dot(hi, rwl_ref[...], preferred_element_type=F32))
        lg_ref[...] = lg + rb_ref[...]

    @pl.when(i >= n_own)
    def _():
        hn_ref[...] = jnp.zeros_like(hn_ref)
        lg_ref[...] = jnp.zeros_like(lg_ref)


def _outproj(x2d, o_s, o_d, o_m, w_o, ln_g, rw_hi, rw_lo, rb, t_all, row0, prev=None):
    t, d = x2d.shape
    tm = min(ROW_TILE, t)
    blk0 = row0 // tm
    n_own = t // tm
    n_steps = n_own if prev is not None else t_all // tm
    row = lambda i: (jnp.minimum(i, n_own - 1), 0)
    row_off = lambda i: (i + blk0, 0)
    const = lambda i: (0, 0)
    in_specs = [pl.BlockSpec((tm, d), row),
                pl.BlockSpec((tm, o_s.shape[1]), row),
                pl.BlockSpec((tm, o_d.shape[1]), row),
                pl.BlockSpec((tm, o_m.shape[1]), row),
                pl.BlockSpec((d, d), const),
                pl.BlockSpec((1, d), const),
                pl.BlockSpec((d, LANES), const),
                pl.BlockSpec((d, LANES), const),
                pl.BlockSpec((1, LANES), const)]
    args = [x2d, o_s, o_d, o_m, w_o, ln_g.reshape(1, d), rw_hi, rw_lo, rb]
    aliases = {}
    if prev is not None:
        in_specs += [pl.BlockSpec(memory_space=pl.ANY), pl.BlockSpec(memory_space=pl.ANY)]
        aliases = {len(args): 1, len(args) + 1: 2}
        args += list(prev)
    return pl.pallas_call(
        functools.partial(_outproj_kernel, n_own=n_own),
        grid=(n_steps,),
        in_specs=in_specs,
        out_specs=[pl.BlockSpec((tm, d), row),
                   pl.BlockSpec((tm, d), row_off),
                   pl.BlockSpec((tm, LANES), row_off)],
        out_shape=[jax.ShapeDtypeStruct((t, d), F32),
                   jax.ShapeDtypeStruct((t_all, d), BF16),
                   jax.ShapeDtypeStruct((t_all, LANES), F32)],
        input_output_aliases=aliases,
        compiler_params=_cparams("arbitrary"),
        name="outproj_router",
    )(*args)


def _route_kernel(lg_ref, pos_ref, post_ref, g_ref, cnt_ref):
    tm = lg_ref.shape[0]
    lane = lax.broadcasted_iota(I32, (tm, LANES), 1).astype(F32)
    l = jnp.where(lane < N_EXPERTS, lg_ref[...], -jnp.inf)
    vals, idxs = [], []
    for _k in range(TOP_K):
        m = jnp.max(l, axis=-1, keepdims=True)
        idx = jnp.min(jnp.where(l == m, lane, float(LANES)), axis=-1, keepdims=True)
        l = jnp.where(lane == idx, -jnp.inf, l)
        vals.append(m)
        idxs.append(idx)
    ex = [jnp.exp(v - vals[0]) for v in vals]
    den = ex[0] + ex[1] + ex[2] + ex[3]
    member = jnp.zeros((tm, LANES), F32)
    for idx in idxs:
        member = member + jnp.where(lane == idx, 1.0, 0.0)
    ri = lax.broadcasted_iota(I32, (tm, tm), 0)
    ci = lax.broadcasted_iota(I32, (tm, tm), 1)
    strict = jnp.where(ci < ri, 1.0, 0.0).astype(BF16)
    prefix = jnp.dot(strict, member.astype(BF16), preferred_element_type=F32)
    cnt = jnp.sum(member, axis=0, keepdims=True)
    cpad = jnp.ceil(cnt * (1.0 / RUN_ALIGN)) * float(RUN_ALIGN)
    c_hi = jnp.floor(cpad * (1.0 / 256.0))
    c_lo = cpad - 256.0 * c_hi
    ej = lax.broadcasted_iota(I32, (LANES, LANES), 0)
    ee = lax.broadcasted_iota(I32, (LANES, LANES), 1)
    before = jnp.where(ej < ee, 1.0, 0.0).astype(BF16)
    bcast = lambda v: jnp.broadcast_to(v, (SUBLANES, LANES)).astype(BF16)
    off = (256.0 * jnp.dot(bcast(c_hi), before, preferred_element_type=F32)
           + jnp.dot(bcast(c_lo), before, preferred_element_type=F32))[0:1]
    where_in_run = prefix + off
    p_out = jnp.zeros((tm, LANES), F32)
    g_out = jnp.zeros((tm, LANES), F32)
    for k in range(TOP_K):
        pos = jnp.sum(jnp.where(lane == idxs[k], where_in_run, 0.0), axis=-1, keepdims=True)
        p_out = jnp.where(lane == float(k), pos, p_out)
        g_out = jnp.where(lane == float(k), ex[k] / den, g_out)
    pos_ref[...] = p_out[:, :TOP_K]
    post_ref[...] = p_out.T[:SUBLANES, :]
    g_ref[...] = g_out[:, :TOP_K]
    cnt_ref[0] = cnt


def _route(logits):
    t = logits.shape[0]
    tm = MOE_BLK
    nb = t // tm
    return pl.pallas_call(
        _route_kernel,
        grid=(nb,),
        in_specs=[pl.BlockSpec((tm, LANES), lambda i: (i, 0))],
        out_specs=[pl.BlockSpec((tm, TOP_K), lambda i: (i, 0)),
                   pl.BlockSpec((SUBLANES, tm), lambda i: (0, i)),
                   pl.BlockSpec((tm, TOP_K), lambda i: (i, 0)),
                   pl.BlockSpec((1, 1, LANES), lambda i: (i, 0, 0))],
        out_shape=[jax.ShapeDtypeStruct((t, TOP_K), F32),
                   jax.ShapeDtypeStruct((SUBLANES, t), F32),
                   jax.ShapeDtypeStruct((t, TOP_K), F32),
                   jax.ShapeDtypeStruct((nb, 1, LANES), F32)],
        compiler_params=_cparams("parallel"),
        name="route",
    )(logits)


def _run_copies(n, max_rows, src_ref, src0, dst_ref, dst0, sem, wait):
    pos = 0
    bit = max_rows
    while bit >= RUN_ALIGN:
        take = (n & bit) != 0

        def go(pos=pos, bit=bit):
            cp = pltpu.make_async_copy(src_ref.at[pl.ds(pl.multiple_of(src0 + pos, RUN_ALIGN), bit)],
                                       dst_ref.at[pl.ds(pl.multiple_of(dst0 + pos, RUN_ALIGN), bit)], sem)
            cp.wait() if wait else cp.start()

        pl.when(take)(go)
        pos = pos + jnp.where(take, bit, 0)
        bit //= 2


RUN_SIZES = tuple(MOE_BLK >> i for i in range((MOE_BLK // RUN_ALIGN).bit_length()))


def _piece_copies(b, cnt_ref, loc_ref, glob_ref, local_ref, global_hbm, sem, to_global, wait):
    for c, rows in enumerate(RUN_SIZES):
        base = b * len(RUN_SIZES) + c

        def body(s, carry, rows=rows, base=base):
            j = base * N_EXPERTS + s
            loc = local_ref.at[pl.ds(pl.multiple_of(loc_ref[j], RUN_ALIGN), rows)]
            glob = global_hbm.at[pl.ds(pl.multiple_of(glob_ref[j], RUN_ALIGN), rows)]
            cp = pltpu.make_async_copy(loc, glob, sem) if to_global else pltpu.make_async_copy(glob, loc, sem)
            cp.wait() if wait else cp.start()
            return carry

        lax.fori_loop(0, cnt_ref[base], body, 0)


def _dispatch_kernel(cnt_ref, loc_ref, glob_ref, estart_ref, elen_ref, nused_ref,
                     hn_ref, post_ref, xs_hbm, buf_ref, zero_ref, sem, zsem):
    b = pl.program_id(0)
    nb = pl.num_programs(0)
    slot = b % 2
    tm = hn_ref.shape[0]
    x = hn_ref[...].astype(BF16)
    post = post_ref[...]
    P = PERM_CHUNK
    for c in range(BLK_ROWS // P):
        r = (lax.broadcasted_iota(I32, (P, tm), 0) + c * P).astype(F32)
        sel = jnp.zeros((P, tm), F32)
        for k in range(TOP_K):
            sel = jnp.where(r == post[k:k + 1, :], 1.0, sel)
        buf_ref[slot, c * P:(c + 1) * P, :] = jnp.dot(sel.astype(BF16), x,
                                                      preferred_element_type=F32).astype(BF16)

    def runs(blk, s, wait):
        _piece_copies(blk, cnt_ref, loc_ref, glob_ref, buf_ref.at[s], xs_hbm, sem.at[s], True, wait)

    runs(b, slot, False)

    @pl.when(b == 0)
    def _():
        zero_ref[...] = jnp.zeros_like(zero_ref)

        def tail(wait):
            def body(e, c):
                n = (MOE_TILE - elen_ref[e] % MOE_TILE) % MOE_TILE
                _run_copies(n, MOE_TILE // 2, zero_ref, 0, xs_hbm, estart_ref[e] + elen_ref[e], zsem, wait)
                return c
            lax.fori_loop(0, N_EXPERTS, body, 0)

            def free_tile(ti, c):
                for half in range(2):
                    _run_copies(MOE_TILE // 2, MOE_TILE // 2, zero_ref, 0, xs_hbm,
                                ti * MOE_TILE + half * (MOE_TILE // 2), zsem, wait)
                return c
            lax.fori_loop(nused_ref[0], xs_hbm.shape[0] // MOE_TILE, free_tile, 0)

        tail(False)
        tail(True)

    pl.when(b > 0)(lambda: runs(b - 1, 1 - slot, True))
    pl.when(b == nb - 1)(lambda: runs(b, slot, True))


def _dispatch(hn, post, piece_cnt, piece_loc, piece_glob, e_start, e_len, n_used, n_rows):
    t, d = hn.shape
    grid_spec = pltpu.PrefetchScalarGridSpec(
        num_scalar_prefetch=6,
        grid=(t // MOE_BLK,),
        in_specs=[pl.BlockSpec((MOE_BLK, d), lambda i, *_: (i, 0)),
                  pl.BlockSpec((SUBLANES, MOE_BLK), lambda i, *_: (0, i))],
        out_specs=pl.BlockSpec(memory_space=pl.ANY),
        scratch_shapes=[pltpu.VMEM((2, BLK_ROWS, d), BF16),
                        pltpu.VMEM((MOE_TILE, d), BF16),
                        pltpu.SemaphoreType.DMA((2,)),
                        pltpu.SemaphoreType.DMA(())],
    )
    return pl.pallas_call(
        _dispatch_kernel,
        grid_spec=grid_spec,
        out_shape=jax.ShapeDtypeStruct((n_rows, d), BF16),
        compiler_params=_cparams("arbitrary"),
        name="dispatch",
    )(piece_cnt, piece_loc, piece_glob, e_start, e_len, n_used, hn, post)


def _expert_kernel(te_ref, nu_ref, nxt_ref, slot_ref, x_ref, w1_hbm, b1_ref, w2_hbm, b2_ref, y_ref,
                   w1f_ref, w2f_ref, w1b_ref, w2b_ref, sem):
    i = pl.program_id(0)
    live = i < nu_ref[0]
    e = te_ref[i]
    s = slot_ref[e]

    def weight_copies(expert, slot):
        return (pltpu.make_async_copy(w1_hbm.at[expert], w1f_ref.at[slot], sem.at[0, slot]),
                pltpu.make_async_copy(w2_hbm.at[expert], w2f_ref.at[slot], sem.at[1, slot]))

    @pl.when(live & (i == 0))
    def _():
        for cp in weight_copies(e, s):
            cp.start()

    @pl.when(live & ((i == 0) | (e != te_ref[jnp.maximum(i - 1, 0)])))
    def _():
        for cp in weight_copies(e, s):
            cp.wait()
        w1b_ref[...] = w1f_ref[s].astype(BF16)
        w2b_ref[...] = w2f_ref[s].astype(BF16)

        @pl.when(nxt_ref[e] >= 0)
        def _():
            for cp in weight_copies(nxt_ref[e], 1 - s):
                cp.start()

    @pl.when(live)
    def _():
        f = w2b_ref.shape[0]
        h = jnp.dot(x_ref[...], w1b_ref[...], preferred_element_type=F32) + b1_ref[0]
        glu = jnp.minimum(h[:, :f], SWIGLU_LIMIT)
        lin = jnp.clip(h[:, f:], -SWIGLU_LIMIT, SWIGLU_LIMIT)
        act = glu * jax.nn.sigmoid(SWIGLU_ALPHA * glu) * (lin + 1.0)
        y = jnp.dot(act.astype(BF16), w2b_ref[...], preferred_element_type=F32) + b2_ref[0]
        y_ref[...] = y.astype(BF16)

    @pl.when(i >= nu_ref[0])
    def _():
        y_ref[...] = jnp.zeros_like(y_ref)


def _experts(xs, tile_expert, n_used, next_expert, expert_slot, w1, b1, w2, b2):
    n_rows, d = xs.shape
    f2 = w1.shape[2]
    f = w2.shape[1]
    n_tiles = n_rows // MOE_TILE
    live = lambda i, te, nu, *_: (jnp.minimum(i, nu[0] - 1), 0)
    every = lambda i, *_: (i, 0)
    wsel = lambda i, te, *_: (te[i], 0, 0)
    grid_spec = pltpu.PrefetchScalarGridSpec(
        num_scalar_prefetch=4,
        grid=(n_tiles,),
        in_specs=[pl.BlockSpec((MOE_TILE, d), live),
                  pl.BlockSpec(memory_space=pl.ANY),
                  pl.BlockSpec((1, 1, f2), wsel),
                  pl.BlockSpec(memory_space=pl.ANY),
                  pl.BlockSpec((1, 1, d), wsel)],
        out_specs=pl.BlockSpec((MOE_TILE, d), every),
        scratch_shapes=[pltpu.VMEM((2, d, f2), F32), pltpu.VMEM((2, f, d), F32),
                        pltpu.VMEM((d, f2), BF16), pltpu.VMEM((f, d), BF16),
                        pltpu.SemaphoreType.DMA((2, 2))],
    )
    return pl.pallas_call(
        _expert_kernel,
        grid_spec=grid_spec,
        out_shape=jax.ShapeDtypeStruct((n_rows, d), BF16),
        compiler_params=_cparams("arbitrary"),
        name="experts",
    )(tile_expert, n_used, next_expert, expert_slot, xs, w1, b1, w2, b2)


def _combine_kernel(cnt_ref, loc_ref, glob_ref, h_ref, pos_ref, g_ref, yb_hbm, y_ref, buf_ref, sem, *, blk0):
    i = pl.program_id(0)
    n_steps = pl.num_programs(0)
    b = i + blk0
    slot = i % 2
    tm, d = h_ref.shape

    def runs(blk, s, wait):
        _piece_copies(blk, cnt_ref, loc_ref, glob_ref, buf_ref.at[s], yb_hbm, sem.at[s], False, wait)

    @pl.when(i == 0)
    def _():
        buf_ref[...] = jnp.zeros_like(buf_ref)
        runs(b, slot, False)

    pl.when(i + 1 < n_steps)(lambda: runs(b + 1, 1 - slot, False))
    runs(b, slot, True)
    pos = pos_ref[...]
    g = g_ref[...]
    y = h_ref[...]
    P = PERM_CHUNK
    for c in range(BLK_ROWS // P):
        col = (lax.broadcasted_iota(I32, (tm, P), 1) + c * P).astype(F32)
        wgt = jnp.zeros((tm, P), F32)
        for k in range(TOP_K):
            wgt = jnp.where(col == pos[:, k:k + 1], g[:, k:k + 1], wgt)
        y = y + jnp.dot(wgt.astype(BF16), buf_ref[slot, c * P:(c + 1) * P, :], preferred_element_type=F32)
    y_ref[...] = y


def _combine(h, pos, gates, piece_cnt, piece_loc, piece_glob, yb, blk0):
    t, d = h.shape
    grid_spec = pltpu.PrefetchScalarGridSpec(
        num_scalar_prefetch=3,
        grid=(t // MOE_BLK,),
        in_specs=[pl.BlockSpec((MOE_BLK, d), lambda i, *_: (i, 0)),
                  pl.BlockSpec((MOE_BLK, TOP_K), lambda i, *_: (i + blk0, 0)),
                  pl.BlockSpec((MOE_BLK, TOP_K), lambda i, *_: (i + blk0, 0)),
                  pl.BlockSpec(memory_space=pl.ANY)],
        out_specs=pl.BlockSpec((MOE_BLK, d), lambda i, *_: (i, 0)),
        scratch_shapes=[pltpu.VMEM((2, BLK_ROWS, d), BF16),
                        pltpu.SemaphoreType.DMA((2,))],
    )
    return pl.pallas_call(
        functools.partial(_combine_kernel, blk0=blk0),
        grid_spec=grid_spec,
        out_shape=jax.ShapeDtypeStruct((t, d), F32),
        compiler_params=_cparams("arbitrary"),
        name="combine",
    )(piece_cnt, piece_loc, piece_glob, h, pos, gates, yb)


def _moe(hn_all, logits_all, h_parts, w1, b1, w2, b2):
    t_all = hn_all.shape[0]
    nb = t_all // MOE_BLK
    pos, post, gates, counts_f = _route(logits_all)
    cnt = counts_f.reshape(nb, LANES)[:, :N_EXPERTS].astype(I32)
    seg_len = (cnt + RUN_ALIGN - 1) // RUN_ALIGN * RUN_ALIGN
    before_e = jnp.arange(N_EXPERTS)[:, None] < jnp.arange(N_EXPERTS)[None, :]
    before_b = jnp.arange(nb)[None, :] < jnp.arange(nb)[:, None]
    seg_off = jnp.sum(jnp.where(before_e[None], seg_len[:, :, None], 0), axis=1)
    e_len = jnp.sum(seg_len, axis=0)
    e_tiles = (e_len + MOE_TILE - 1) // MOE_TILE
    tile_start = jnp.sum(jnp.where(before_e, e_tiles[:, None], 0), axis=0)
    tile_end = tile_start + e_tiles
    e_start = tile_start * MOE_TILE
    seg_dst = e_start[None, :] + jnp.sum(jnp.where(before_b[:, :, None], seg_len[None], 0), axis=1)
    max_rows = t_all * TOP_K + nb * N_EXPERTS * (RUN_ALIGN - 1) + N_EXPERTS * (MOE_TILE - RUN_ALIGN)
    n_tiles = -(-max_rows // MOE_TILE)
    n_rows = n_tiles * MOE_TILE
    n_used = tile_end[-1:].astype(I32)
    tile_expert = jnp.minimum(jnp.sum(tile_end[None, :] <= jnp.arange(n_tiles, dtype=I32)[:, None], axis=1),
                              N_EXPERTS - 1).astype(I32)
    sizes = jnp.array(RUN_SIZES, I32)[None, :, None]
    n_run = seg_len[:, None, :]
    has = (n_run & sizes) != 0
    piece_at = n_run & ~(2 * sizes - 1)
    rank = jnp.sum(jnp.where(before_e[None, None], has[:, :, :, None], False), axis=2)
    slot = jnp.arange(N_EXPERTS)
    put = has[..., None] & (rank[..., None] == slot)
    listed = lambda v: jnp.sum(jnp.where(put, v[..., None], 0), axis=2)
    piece_loc = listed(seg_off[:, None, :] + piece_at)
    piece_glob = listed(seg_dst[:, None, :] + piece_at)
    piece_cnt = jnp.sum(has, axis=2)
    flat = lambda a: a.reshape(-1).astype(I32)
    tables = (flat(piece_cnt), flat(piece_loc), flat(piece_glob))
    xs = _dispatch(hn_all, post, *tables, flat(e_start), flat(e_len), n_used, n_rows)
    e_ids = jnp.arange(N_EXPERTS)
    used = e_tiles > 0
    next_expert = jnp.min(jnp.where(before_e & used[None, :], e_ids[None, :], N_EXPERTS), axis=1)
    next_expert = jnp.where(next_expert == N_EXPERTS, -1, next_expert)
    expert_slot = jnp.sum(jnp.where(before_e & used[:, None], 1, 0), axis=0) % 2
    yb = _experts(xs, tile_expert, n_used, flat(next_expert), flat(expert_slot), w1, b1, w2, b2)
    outs = []
    row = 0
    for h in h_parts:
        outs.append(_combine(h, pos, gates, *tables, yb, row // MOE_BLK))
        row += h.shape[0]
    return outs


GDN_ROWS = 4 * GDN_CHUNK
CONV_HALO = SUBLANES
NEUMANN_SPLIT = 2
AB_LANES = 2 * N_GDN_HEADS


def _softplus(x):
    return jnp.maximum(x, 0.0) + jnp.log1p(jnp.exp(-jnp.abs(x)))


def _gdn_prompt_kernel(u_ref, ab_ref, gate_ref, cw_ref, alog_ref, dtb_ref, ng_ref, o_ref, s_ref, ubuf_ref):
    step = pl.program_id(0)
    NB = u_ref.shape[0]
    R = GDN_ROWS
    C = GDN_CHUNK
    NC = R // C

    @pl.when(step == 0)
    def _():
        ubuf_ref[:, 0:CONV_HALO, :] = jnp.zeros((NB, CONV_HALO, ubuf_ref.shape[2]), F32)
        s_ref[...] = jnp.zeros_like(s_ref)

    ri = lax.broadcasted_iota(I32, (R, R), 0)
    ci = lax.broadcasted_iota(I32, (R, R), 1)
    shift = C.bit_length() - 1
    same = lax.shift_right_logical(ri, shift) == lax.shift_right_logical(ci, shift)
    incl = same & (ci <= ri)
    strict = same & (ci < ri)
    tri = jnp.where(incl, 1.0, 0.0).astype(BF16)
    blk = jnp.where(same, 1.0, 0.0).astype(BF16)
    cw = cw_ref[...]
    ng = ng_ref[...]

    lane = lax.broadcasted_iota(I32, (R, LANES), 1)
    lane1 = lax.broadcasted_iota(I32, (1, LANES), 1)
    ab = ab_ref[0]
    alog = alog_ref[...]
    dtb = dtb_ref[...]
    for b in range(1, NB):
        own = (lane >= b * AB_LANES) & (lane < (b + 1) * AB_LANES)
        own1 = (lane1 >= b * AB_LANES) & (lane1 < (b + 1) * AB_LANES)
        ab = jnp.where(own, pltpu.roll(ab_ref[b], b * AB_LANES, 1), ab)
        alog = jnp.where(own1, pltpu.roll(alog_ref[...], b * AB_LANES, 1), alog)
        dtb = jnp.where(own1, pltpu.roll(dtb_ref[...], b * AB_LANES, 1), dtb)
    g_t = -jnp.exp(alog) * _softplus(ab + dtb)
    beta_t = jax.nn.sigmoid(ab)
    both = jnp.concatenate([tri, blk], axis=0)
    sums = sum(jnp.dot(both, p, preferred_element_type=F32) for p in _split3(g_t))
    gcum, gtot = sums[:R], sums[R:]
    gcum_t = gcum.T

    chains = []
    for b in range(NB):
        u = u_ref[b]
        ubuf_ref[b, CONV_HALO:CONV_HALO + R, :] = u
        y = u * cw[GDN_CONV - 1:GDN_CONV, :]
        for j in range(1, GDN_CONV):
            y = y + ubuf_ref[b, CONV_HALO - j:CONV_HALO - j + R, :] * cw[GDN_CONV - 1 - j:GDN_CONV - j, :]
        ubuf_ref[b, 0:CONV_HALO, :] = u[R - CONV_HALO:, :]
        qkv = y * jax.nn.sigmoid(y)
        qk_n = []
        for t in range(2 * GDN_QK_COLS // LANES):
            x = qkv[:, t * LANES:(t + 1) * LANES]
            x = x * lax.rsqrt(_pair_sumsq(x) + EPS)
            qk_n.append(x * (GDN_DK ** -0.5) if t < GDN_QK_COLS // LANES else x)
        qk_n = jnp.concatenate(qk_n, axis=-1)
        for h in range(N_GDN_HEADS):
            q = qk_n[:, h * GDN_DK:(h + 1) * GDN_DK]
            k = qk_n[:, GDN_QK_COLS + h * GDN_DK:GDN_QK_COLS + (h + 1) * GDN_DK]
            v = qkv[:, 2 * GDN_QK_COLS + h * GDN_DV:2 * GDN_QK_COLS + (h + 1) * GDN_DV]
            col = b * AB_LANES + h
            gc = gcum[:, col:col + 1]
            gt = gtot[:, col:col + 1]
            beta = beta_t[:, col + N_GDN_HEADS:col + N_GDN_HEADS + 1]
            decay = jnp.exp(jnp.where(incl, gc - gcum_t[col:col + 1, :], -jnp.inf))
            chains.append(dict(
                b=b, h=h,
                a=jnp.where(strict, beta * _bdot_nt(k, k) * decay, 0.0),
                qk=_bdot_nt(q, k) * decay,
                r=jnp.concatenate([v * beta, k * (beta * jnp.exp(gc))], axis=-1),
                q_dec=q * jnp.exp(gc), k_dec=k * jnp.exp(gt - gc), g_last=jnp.exp(gt)))

    dot = lambda x, y: jnp.dot(x, y, preferred_element_type=F32)
    level, j = 1, 0
    while level < C:
        last = 2 * level >= C
        for ch in chains:
            if j < NEUMANN_SPLIT:
                a_hi, a_lo = _split2(ch['a'])
                r_hi, r_lo = _split2(ch['r'])
                upd = dot(a_hi, r_hi) + dot(a_hi, r_lo) + dot(a_lo, r_hi)
            else:
                a_hi = ch['a'].astype(BF16)
                upd = dot(a_hi, ch['r'].astype(BF16))
            ch['r'] = ch['r'] - upd if level == 1 else ch['r'] + upd
            if not last:
                sq = dot(a_hi, a_hi)
                if j + 1 < NEUMANN_SPLIT:
                    sq = sq + dot(a_hi, a_lo) + dot(a_lo, a_hi)
                ch['a'] = sq
        level *= 2
        j += 1

    for ch in chains:
        ch['u'], ch['w'] = ch['r'][:, :GDN_DV], ch['r'][:, GDN_DV:]
        ch['S'] = s_ref[ch['b'], ch['h']]
        ch['k_dec_t'] = ch['k_dec'].T
        ch['outs'] = []
    for c in range(NC):
        sl = slice(c * C, (c + 1) * C)
        for ch in chains:
            S = ch['S']
            v_new = ch['u'][sl] - _bdot(ch['w'][sl], S)
            ch['outs'].append(_bdot(ch['q_dec'][sl], S) + _bdot(ch['qk'][sl, sl], v_new))
            ch['S'] = S * ch['g_last'][c * C:c * C + 1, :] + _bdot(ch['k_dec_t'][:, sl], v_new)
    for ch in chains:
        b, h = ch['b'], ch['h']
        s_ref[b, h] = ch['S']
        o = jnp.concatenate(ch['outs'], axis=0)
        o = o * lax.rsqrt(jnp.mean(o * o, axis=-1, keepdims=True) + EPS) * ng
        gh = gate_ref[b, :, h * GDN_DV:(h + 1) * GDN_DV]
        o_ref[b, :, h * GDN_DV:(h + 1) * GDN_DV] = (o * (gh * jax.nn.sigmoid(gh))).astype(o_ref.dtype)


def _gdn_prompt(z3, conv_w, a_log, dt_bias, norm_g):
    B, S, _ = z3.shape
    R = GDN_ROWS
    lanes4 = lambda a: jnp.pad(a, (0, LANES - a.shape[0])).reshape(1, LANES)
    return pl.pallas_call(
        _gdn_prompt_kernel,
        grid=(S // R,),
        in_specs=[pl.BlockSpec((B, R, GDN_CONV_CH), lambda s: (0, s, Z_GDN // GDN_CONV_CH)),
                  pl.BlockSpec((B, R, LANES), lambda s: (0, s, Z_AB // LANES)),
                  pl.BlockSpec((B, R, GDN_V_COLS), lambda s: (0, s, Z_GATE // GDN_V_COLS)),
                  pl.BlockSpec((GDN_CONV, GDN_CONV_CH), lambda s: (0, 0)),
                  pl.BlockSpec((1, LANES), lambda s: (0, 0)),
                  pl.BlockSpec((1, LANES), lambda s: (0, 0)),
                  pl.BlockSpec((1, GDN_DV), lambda s: (0, 0))],
        out_specs=[pl.BlockSpec((B, R, GDN_V_COLS), lambda s: (0, s, 0)),
                   pl.BlockSpec((B, N_GDN_HEADS, GDN_DK, GDN_DV), lambda s: (0, 0, 0, 0))],
        out_shape=[jax.ShapeDtypeStruct((B, S, GDN_V_COLS), BF16),
                   jax.ShapeDtypeStruct((B, N_GDN_HEADS, GDN_DK, GDN_DV), F32)],
        scratch_shapes=[pltpu.VMEM((B, CONV_HALO + R, GDN_CONV_CH), F32)],
        compiler_params=_cparams("arbitrary"),
        name="gdn_prompt",
    )(z3, z3, z3, conv_w, lanes4(a_log), lanes4(dt_bias), norm_g.reshape(1, GDN_DV))


def _pair_sumsq(x):
    li = lax.broadcasted_iota(I32, (LANES, LANES), 0) // HEAD_DIM
    lj = lax.broadcasted_iota(I32, (LANES, LANES), 1) // HEAD_DIM
    same = jnp.where(li == lj, 1.0, 0.0).astype(BF16)
    hi, lo = _split2(x * x)
    return jnp.dot(hi, same, preferred_element_type=F32) + jnp.dot(lo, same, preferred_element_type=F32)


def _pair_rms(x, g):
    return x * lax.rsqrt(_pair_sumsq(x) * (1.0 / HEAD_DIM) + EPS) * g


def _first_half(shape):
    return lax.broadcasted_iota(I32, shape, 1) < HEAD_DIM


LOG2E = 1.4426950408889634


SWA_BLOCKS = 2


def _swa_prompt_kernel(sink_ref, q_ref, kc_ref, kp_ref, vc_ref, vp_ref, qg_ref, kg_ref, o_ref, kn_ref,
                       bias_ref):
    first = (pl.program_id(0) == 0) & (pl.program_id(1) == 0)
    n = pl.program_id(1)
    W = WINDOW
    NQ = SWA_BLOCKS

    @pl.when(first)
    def _():
        qi = lax.broadcasted_iota(I32, (W, 2 * W), 0)
        kj = lax.broadcasted_iota(I32, (W, 2 * W), 1)
        dist = qi + W - kj
        band = (dist >= 0) & (dist < W)
        distf = dist.astype(F32)
        for has_prev in range(2):
            mask = jnp.where(band & ((has_prev == 1) | (kj >= W)), 0.0, -jnp.inf)
            for head in range(N_SWA_HEADS):
                slope = 2.0 ** (-(8.0 / N_SWA_HEADS) * (head + 1))
                bias_ref[has_prev, head] = mask - (slope * LOG2E) * distf

    kg = kg_ref[...]
    qg = qg_ref[...]
    kc = _pair_rms(kc_ref[0], kg)
    kn_ref[0] = kc[(NQ - 1) * W:]
    k3 = jnp.concatenate([_pair_rms(kp_ref[0], kg), kc], axis=0)
    v3 = jnp.concatenate([vp_ref[0], vc_ref[0]], axis=0)
    fh = _first_half(k3.shape)
    k3r = pltpu.roll(k3, HEAD_DIM, 1)
    v3r = pltpu.roll(v3, HEAD_DIM, 1)
    kdup = (jnp.where(fh, k3, k3r).astype(BF16), jnp.where(fh, k3r, k3).astype(BF16))
    vdup = (jnp.where(fh, v3, v3r).astype(BF16), jnp.where(fh, v3r, v3).astype(BF16))
    fq = _first_half((W, LANES))
    kv_of = lambda head: head // SWA_GROUP
    probs = [(j, head) for j in range(NQ) for head in range(N_SWA_HEADS)]
    keys = lambda j: slice(j * W, (j + 2) * W)
    qts = [[_pair_rms(q_ref[0, j * W:(j + 1) * W, t * LANES:(t + 1) * LANES], qg) * (ATTN_SCALE * LOG2E)
            for t in range(SWA_Q_COLS // LANES)] for j in range(NQ)]
    qms = [jnp.where(fq == (head % 2 == 0), qts[j][head // 2], 0.0).astype(BF16) for j, head in probs]
    table = [jnp.where(n > 0, 1, 0)] + [1] * (NQ - 1)
    ss = [_bdot_nt(qms[i], kdup[kv_of(head)][keys(j)]) + bias_ref[table[j], head]
          for i, (j, head) in enumerate(probs)]
    sinks = [sink_ref[head] * LOG2E for head in range(N_SWA_HEADS)]
    ms = [jnp.maximum(jnp.max(ss[i], axis=-1, keepdims=True), sinks[head]) for i, (j, head) in enumerate(probs)]
    ps = [jnp.exp2(ss[i] - ms[i]) for i in range(len(probs))]
    dens = [jnp.sum(ps[i], axis=-1, keepdims=True) + jnp.exp2(sinks[head] - ms[i])
            for i, (j, head) in enumerate(probs)]
    outs = [_bdot(ps[i], vdup[kv_of(head)][keys(j)]) / dens[i] for i, (j, head) in enumerate(probs)]
    for j in range(NQ):
        for t in range(SWA_Q_COLS // LANES):
            o_ref[0, j * W:(j + 1) * W, t * LANES:(t + 1) * LANES] = jnp.where(
                fq, outs[j * N_SWA_HEADS + 2 * t], outs[j * N_SWA_HEADS + 2 * t + 1]).astype(o_ref.dtype)


def _swa_prompt(z3, q_g, k_g, sinks):
    B, S, _ = z3.shape
    W = WINDOW
    NQ = SWA_BLOCKS
    twice = lambda g: jnp.concatenate([g, g]).reshape(1, LANES)
    kcol, vcol = Z_K // LANES, Z_V // LANES
    prev = lambda col: pl.BlockSpec((1, W, LANES), lambda b, n: (b, jnp.maximum(NQ * n - 1, 0), col))
    grid_spec = pltpu.PrefetchScalarGridSpec(
        num_scalar_prefetch=0,
        grid=(B, S // (NQ * W)),
        in_specs=[pl.BlockSpec(memory_space=pltpu.SMEM),
                  pl.BlockSpec((1, NQ * W, SWA_Q_COLS), lambda b, n: (b, n, 0)),
                  pl.BlockSpec((1, NQ * W, LANES), lambda b, n: (b, n, kcol)),
                  prev(kcol),
                  pl.BlockSpec((1, NQ * W, LANES), lambda b, n: (b, n, vcol)),
                  prev(vcol),
                  pl.BlockSpec((1, LANES), lambda b, n: (0, 0)),
                  pl.BlockSpec((1, LANES), lambda b, n: (0, 0))],
        out_specs=[pl.BlockSpec((1, NQ * W, SWA_Q_COLS), lambda b, n: (b, n, 0)),
                   pl.BlockSpec((1, W, LANES), lambda b, n: (b, 0, 0))],
        scratch_shapes=[pltpu.VMEM((2, N_SWA_HEADS, W, 2 * W), F32)],
    )
    return pl.pallas_call(
        _swa_prompt_kernel,
        grid_spec=grid_spec,
        out_shape=[jax.ShapeDtypeStruct((B, S, SWA_Q_COLS), BF16),
                   jax.ShapeDtypeStruct((B, W, LANES), F32)],
        compiler_params=_cparams("arbitrary", "arbitrary"),
        name="swa_prompt",
    )(sinks, z3, z3, z3, z3, z3, twice(q_g), twice(k_g))


def _mem_kv_kernel(m_ref, g_ref, w_ref, kg_ref, k_ref, v_ref):
    n = _rms_rows(m_ref[...], g_ref[...])
    kv = jnp.dot(n.astype(BF16), w_ref[...], preferred_element_type=F32)
    kg = kg_ref[...]
    for t in range(MEM_Q_COLS // LANES):
        k_ref[:, t * LANES:(t + 1) * LANES] = _pair_rms(kv[:, t * LANES:(t + 1) * LANES], kg)
    v_ref[...] = kv[:, MEM_Q_COLS:]


def _mem_kv(mem2d, ln_g, w_kv, k_g):
    r, d = mem2d.shape
    twice = jnp.concatenate([k_g, k_g]).reshape(1, LANES)
    full = lambda shape: pl.BlockSpec(shape, lambda i: (0,) * len(shape))
    return pl.pallas_call(
        _mem_kv_kernel,
        grid=(1,),
        in_specs=[full((r, d)), full((1, d)), full((d, 2 * MEM_Q_COLS)), full((1, LANES))],
        out_specs=[full((r, MEM_Q_COLS)), full((r, MEM_Q_COLS))],
        out_shape=[jax.ShapeDtypeStruct((r, MEM_Q_COLS), F32), jax.ShapeDtypeStruct((r, MEM_Q_COLS), F32)],
        compiler_params=_cparams("arbitrary"),
        name="mem_kv",
    )(mem2d, ln_g.reshape(1, d), w_kv.astype(BF16), twice)


def _mem_attn_kernel(q_ref, k_ref, v_ref, qg_ref, o_ref):
    qg = qg_ref[...]
    rows = q_ref.shape[1]
    fq = _first_half((rows, LANES))
    heads = range(N_MEM_HEADS)
    tile = lambda t: slice(t * LANES, (t + 1) * LANES)
    qts = [_pair_rms(q_ref[0, :, tile(t)], qg) * (ATTN_SCALE * LOG2E) for t in range(MEM_Q_COLS // LANES)]
    kts = [k_ref[0, :, tile(t)].astype(BF16) for t in range(MEM_Q_COLS // LANES)]
    vts = [v_ref[0, :, tile(t)].astype(BF16) for t in range(MEM_Q_COLS // LANES)]
    ss = [_bdot_nt(jnp.where(fq == (h % 2 == 0), qts[h // 2], 0.0), kts[h // 2]) for h in heads]
    ps = [jnp.exp2(s - jnp.max(s, axis=-1, keepdims=True)) for s in ss]
    outs = [_bdot(ps[h], vts[h // 2]) / jnp.sum(ps[h], axis=-1, keepdims=True) for h in heads]
    for t in range(MEM_Q_COLS // LANES):
        o_ref[0, :, tile(t)] = jnp.where(fq, outs[2 * t], outs[2 * t + 1]).astype(o_ref.dtype)


MEM_Q_TILE = 512


def _mem_attn_prompt(z3, mem_k, mem_v, q_g):
    B, S, _ = z3.shape
    M = mem_k.shape[1]
    tq = MEM_Q_TILE
    twice = jnp.concatenate([q_g, q_g]).reshape(1, LANES)
    return pl.pallas_call(
        _mem_attn_kernel,
        grid=(B, S // tq),
        in_specs=[pl.BlockSpec((1, tq, MEM_Q_COLS), lambda b, i: (b, i, Z_QM // MEM_Q_COLS)),
                  pl.BlockSpec((1, M, MEM_Q_COLS), lambda b, i: (b, 0, 0)),
                  pl.BlockSpec((1, M, MEM_Q_COLS), lambda b, i: (b, 0, 0)),
                  pl.BlockSpec((1, LANES), lambda b, i: (0, 0))],
        out_specs=pl.BlockSpec((1, tq, MEM_Q_COLS), lambda b, i: (b, i, 0)),
        out_shape=jax.ShapeDtypeStruct((B, S, MEM_Q_COLS), BF16),
        compiler_params=_cparams("parallel", "parallel"),
        name="mem_attn_prompt",
    )(z3, mem_k, mem_v, twice)


PAIR = 2


def _swa_sample_kernel(sink_ref, q_ref, k_ref, v_ref, ck_ref, cv_ref, qg_ref, kg_ref, o_ref, nk_ref, nv_ref, *, L):
    n_seq = ck_ref.shape[0]
    Wb = ck_ref.shape[2]
    rows8 = SUBLANES
    nh = N_SWA_HEADS
    kn = _pair_rms(k_ref[...], kg_ref[...])
    qg = qg_ref[...]
    R = nh * rows8
    row = lax.broadcasted_iota(I32, (R, 1), 0)
    head = row // rows8
    seq_in_pair = (row % rows8) // L
    step = (row % L).astype(F32)
    slope = jnp.exp2(-(8.0 / N_SWA_HEADS) * (head.astype(F32) + 1.0))
    sink = jnp.zeros((R, 1), F32)
    for h in range(nh):
        sink = jnp.where(head == h, sink_ref[h], sink)
    key = lax.broadcasted_iota(I32, (R, Wb), 1).astype(F32)
    dist_c = float(Wb) + step - key
    bias_c = jnp.where(dist_c < float(WINDOW), 0.0, -jnp.inf)
    col = lax.broadcasted_iota(I32, (R, rows8), 1)
    dist_n = step - (col % L).astype(F32)
    bias_n = jnp.where((dist_n >= 0.0) & ((col // L) == seq_in_pair), 0.0, -jnp.inf)
    fh8 = _first_half((rows8, LANES))
    fhR = _first_half((R, LANES))
    kv_first = head < SWA_GROUP
    pairs = range(n_seq // PAIR)
    bias_c = bias_c - slope * dist_c
    bias_n = bias_n - slope * dist_n

    def stacked_queries(pr):
        r0 = pr * rows8
        pieces = []
        for t in range(SWA_Q_COLS // LANES):
            qt = _pair_rms(q_ref[r0:r0 + rows8, t * LANES:(t + 1) * LANES], qg) * ATTN_SCALE
            qr = pltpu.roll(qt, HEAD_DIM, 1)
            kv = t // (SWA_GROUP // 2)
            for half in range(2):
                src = qt if half == kv else qr
                pieces.append(jnp.where(fh8 == (kv == 0), src, 0.0))
        return jnp.concatenate(pieces, axis=0).astype(BF16)

    qs = [stacked_queries(pr) for pr in pairs]
    s_c = [jnp.where(seq_in_pair == 0, _bdot(qs[pr], ck_ref[pr * PAIR]), _bdot(qs[pr], ck_ref[pr * PAIR + 1]))
           + bias_c for pr in pairs]
    s_n = [_bdot_nt(qs[pr], kn[pr * rows8:(pr + 1) * rows8]) + bias_n for pr in pairs]
    m = [jnp.maximum(jnp.maximum(jnp.max(s_c[pr], axis=-1, keepdims=True),
                                 jnp.max(s_n[pr], axis=-1, keepdims=True)), sink) for pr in pairs]
    p_c = [jnp.exp(s_c[pr] - m[pr]) for pr in pairs]
    p_n = [jnp.exp(s_n[pr] - m[pr]) for pr in pairs]
    den = [jnp.sum(p_c[pr], axis=-1, keepdims=True) + jnp.sum(p_n[pr], axis=-1, keepdims=True)
           + jnp.exp(sink - m[pr]) for pr in pairs]
    outs = [(_bdot(p_n[pr], v_ref[pr * rows8:(pr + 1) * rows8, :])
             + _bdot_nt(jnp.where(seq_in_pair == 0, p_c[pr], 0.0), cv_ref[pr * PAIR])
             + _bdot_nt(jnp.where(seq_in_pair == 1, p_c[pr], 0.0), cv_ref[pr * PAIR + 1])) / den[pr] for pr in pairs]

    pos_r = lax.broadcasted_iota(I32, (Wb, rows8), 0)
    new_c = lax.broadcasted_iota(I32, (Wb, rows8), 1)
    tail = lax.broadcasted_iota(I32, (SWA_KV_COLS, Wb), 1) >= Wb - L

    def shifted(old, new8, j):
        place = jnp.where((pos_r == Wb - L + new_c % L) & (new_c // L == j), 1.0, 0.0).astype(BF16)
        rows_at_tail = sum(jnp.dot(place, part, preferred_element_type=F32) for part in _split3(new8))
        return jnp.where(tail, rows_at_tail.T, pltpu.roll(old, Wb - L, 1))

    for pr in pairs:
        for j in range(PAIR):
            s = pr * PAIR + j
            nk_ref[s] = shifted(ck_ref[s], kn[pr * rows8:(pr + 1) * rows8], j)
            nv_ref[s] = shifted(cv_ref[s], v_ref[pr * rows8:(pr + 1) * rows8, :], j)
    for pr in pairs:
        r0 = pr * rows8
        o = jnp.where(fhR == kv_first, outs[pr], 0.0)
        o_r = pltpu.roll(o, HEAD_DIM, 1)
        for t in range(SWA_Q_COLS // LANES):
            kv = t // (SWA_GROUP // 2)
            halves = []
            for half in range(2):
                h = 2 * t + half
                src = o if half == kv else o_r
                halves.append(src[h * rows8:(h + 1) * rows8])
            o_ref[r0:r0 + rows8, t * LANES:(t + 1) * LANES] = jnp.where(fh8, halves[0], halves[1])


SAMPLE_SEQS = 8


def _swa_sample(z_s, cache_k, cache_v, q_g, k_g, sinks, L):
    t = z_s.shape[0]
    DB, Wb, KV, HD = cache_k.shape
    ns = SAMPLE_SEQS
    rows = ns * L
    twice = lambda g: jnp.concatenate([g, g]).reshape(1, LANES)
    fm = lambda c: jnp.transpose(c, (0, 2, 3, 1)).reshape(DB, KV * HD, Wb)
    back = lambda c: jnp.transpose(c.reshape(DB, KV, HD, Wb), (0, 3, 1, 2))
    cache = pl.BlockSpec((ns, KV * HD, Wb), lambda i: (i, 0, 0))
    o, nk, nv = pl.pallas_call(
        functools.partial(_swa_sample_kernel, L=L),
        grid=(DB // ns,),
        in_specs=[pl.BlockSpec(memory_space=pltpu.SMEM),
                  pl.BlockSpec((rows, SWA_Q_COLS), lambda i: (i, 0)),
                  pl.BlockSpec((rows, LANES), lambda i: (i, Z_K // LANES)),
                  pl.BlockSpec((rows, LANES), lambda i: (i, Z_V // LANES)),
                  cache, cache,
                  pl.BlockSpec((1, LANES), lambda i: (0, 0)),
                  pl.BlockSpec((1, LANES), lambda i: (0, 0))],
        out_specs=[pl.BlockSpec((rows, SWA_Q_COLS), lambda i: (i, 0)), cache, cache],
        out_shape=[jax.ShapeDtypeStruct((t, SWA_Q_COLS), F32),
                   jax.ShapeDtypeStruct((DB, KV * HD, Wb), F32),
                   jax.ShapeDtypeStruct((DB, KV * HD, Wb), F32)],
        compiler_params=_cparams("parallel"),
        name="swa_sample",
    )(sinks, z_s, z_s, z_s, fm(cache_k), fm(cache_v), twice(q_g), twice(k_g))
    return o, back(nk), back(nv)


def _mem_sample_kernel(q_ref, mk_ref, mv_ref, qg_ref, o_ref, *, L):
    n_seq = mk_ref.shape[0]
    rows8 = SUBLANES
    nh = N_MEM_HEADS
    qg = qg_ref[...]
    R = nh * rows8
    row = lax.broadcasted_iota(I32, (R, 1), 0)
    seq_in_pair = (row % rows8) // L
    lane_head8 = lax.broadcasted_iota(I32, (rows8, MEM_Q_COLS), 1) // HEAD_DIM
    pairs = range(n_seq // PAIR)

    def stacked_queries(pr):
        qn = jnp.concatenate([_pair_rms(q_ref[pr * rows8:(pr + 1) * rows8, t * LANES:(t + 1) * LANES], qg)
                              for t in range(MEM_Q_COLS // LANES)], axis=-1) * ATTN_SCALE
        return jnp.concatenate([jnp.where(lane_head8 == h, qn, 0.0) for h in range(nh)], axis=0).astype(BF16)

    qs = [stacked_queries(pr) for pr in pairs]
    ss = [jnp.where(seq_in_pair == 0, _bdot(qs[pr], mk_ref[pr * PAIR]), _bdot(qs[pr], mk_ref[pr * PAIR + 1]))
          for pr in pairs]
    ps = [jnp.exp(s - jnp.max(s, axis=-1, keepdims=True)) for s in ss]
    outs = [(_bdot_nt(jnp.where(seq_in_pair == 0, ps[pr], 0.0), mv_ref[pr * PAIR])
             + _bdot_nt(jnp.where(seq_in_pair == 1, ps[pr], 0.0), mv_ref[pr * PAIR + 1]))
            / jnp.sum(ps[pr], axis=-1, keepdims=True) for pr in pairs]
    for pr in pairs:
        o = jnp.zeros((rows8, MEM_Q_COLS), F32)
        for h in range(nh):
            o = jnp.where(lane_head8 == h, outs[pr][h * rows8:(h + 1) * rows8], o)
        o_ref[pr * rows8:(pr + 1) * rows8, :] = o


def _mem_attn_sample(z_s, mem_k, mem_v, q_g, L):
    t = z_s.shape[0]
    DB, M, H, HD = mem_k.shape
    ns = SAMPLE_SEQS
    rows = ns * L
    twice = jnp.concatenate([q_g, q_g]).reshape(1, LANES)
    fm = lambda c: jnp.transpose(c, (0, 2, 3, 1)).reshape(DB, H * HD, M)
    cache = pl.BlockSpec((ns, H * HD, M), lambda i: (i, 0, 0))
    return pl.pallas_call(
        functools.partial(_mem_sample_kernel, L=L),
        grid=(DB // ns,),
        in_specs=[pl.BlockSpec((rows, MEM_Q_COLS), lambda i: (i, Z_QM // MEM_Q_COLS)),
                  cache, cache,
                  pl.BlockSpec((1, LANES), lambda i: (0, 0))],
        out_specs=pl.BlockSpec((rows, MEM_Q_COLS), lambda i: (i, 0)),
        out_shape=jax.ShapeDtypeStruct((t, MEM_Q_COLS), F32),
        compiler_params=_cparams("parallel"),
        name="mem_attn_sample",
    )(z_s, fm(mem_k), fm(mem_v), twice)


def _gdn_sample_kernel(uq_ref, uk_ref, uv_ref, bq_ref, bk_ref, bv_ref, wq_ref, wk_ref, wv_ref,
                       ab_ref, gate_ref, alog_ref, dtb_ref, ng_ref, s_in_ref, o_ref, s_ref, kq_ref):
    h = pl.program_id(0)
    L = uq_ref.shape[0]
    nbuf = bq_ref.shape[0]
    DK = GDN_DK

    def conv(u_ref, b_ref, w_ref, t):
        up = [b_ref[i] for i in range(nbuf)] + [u_ref[i] for i in range(L)]
        y = up[t] * w_ref[0]
        for i in range(1, GDN_CONV):
            y = y + up[t + i] * w_ref[i]
        return y * jax.nn.sigmoid(y)

    s_ref[...] = s_in_ref[...]
    ng = ng_ref[...]
    hsel = lax.broadcasted_iota(I32, (SUBLANES, 1), 0)
    pick = lambda m, r: jnp.sum(jnp.where(hsel == r, m, 0.0), axis=0, keepdims=True)
    alog = pick(alog_ref[...], h)
    dtb = pick(dtb_ref[...], h)
    for t in range(L):
        q = conv(uq_ref, bq_ref, wq_ref, t)
        k = conv(uk_ref, bk_ref, wk_ref, t)
        v = conv(uv_ref, bv_ref, wv_ref, t)
        q = q * lax.rsqrt(jnp.sum(q * q, axis=0, keepdims=True) + EPS) * (GDN_DK ** -0.5)
        k = k * lax.rsqrt(jnp.sum(k * k, axis=0, keepdims=True) + EPS)
        ab = ab_ref[t]
        a = pick(ab, h)
        bb = pick(ab, h + N_GDN_HEADS)
        decay = jnp.exp(-jnp.exp(alog) * _softplus(a + dtb))
        beta = jax.nn.sigmoid(bb)
        kq_ref[0] = k
        kq_ref[1] = q

        def decay_and_project(dk, acc):
            s = s_ref[0, dk] * decay
            s_ref[0, dk] = s
            return acc + s * kq_ref[0, pl.ds(dk, 1), :]

        sk = lax.fori_loop(0, DK, decay_and_project, jnp.zeros_like(v), unroll=8)
        u = beta * (v - sk)

        def update_and_read(dk, acc):
            s = s_ref[0, dk] + kq_ref[0, pl.ds(dk, 1), :] * u
            s_ref[0, dk] = s
            return acc + s * kq_ref[1, pl.ds(dk, 1), :]

        o = lax.fori_loop(0, DK, update_and_read, jnp.zeros_like(v), unroll=8)
        o = o * lax.rsqrt(jnp.mean(o * o, axis=0, keepdims=True) + EPS) * ng
        g = gate_ref[t]
        o_ref[t] = o * (g * jax.nn.sigmoid(g))


def _gdn_sample(z_s, conv_buf, state, conv_w, a_log, dt_bias, norm_g, DB, L):
    H = N_GDN_HEADS
    z3 = z_s.reshape(DB, L, Z_COLS)
    u_t = jnp.transpose(z3[:, :, Z_GDN:Z_GATE], (1, 2, 0))
    gate_t = jnp.transpose(z3[:, :, Z_GATE:Z_QM], (1, 2, 0))
    ab_t = jnp.transpose(z3[:, :, Z_AB:Z_AB + SUBLANES], (1, 2, 0))
    buf_t = jnp.transpose(conv_buf, (1, 2, 0))
    s_t = jnp.transpose(state, (1, 2, 3, 0))
    w_col = conv_w.reshape(GDN_CONV, GDN_CONV_CH, 1)
    col8 = lambda a: jnp.pad(a, (0, SUBLANES - a.shape[0])).reshape(SUBLANES, 1)
    nbuf = conv_buf.shape[1]
    part = lambda n, j: pl.BlockSpec((n, GDN_DK, DB), lambda h: (0, j * H + h, 0))
    wpart = lambda j: pl.BlockSpec((GDN_CONV, GDN_DK, 1), lambda h: (0, j * H + h, 0))
    whole = lambda shape: pl.BlockSpec(shape, lambda h: (0,) * len(shape))
    o_t, s_new = pl.pallas_call(
        _gdn_sample_kernel,
        grid=(H,),
        in_specs=[part(L, 0), part(L, 1), part(L, 2), part(nbuf, 0), part(nbuf, 1), part(nbuf, 2),
                  wpart(0), wpart(1), wpart(2),
                  whole((L, SUBLANES, DB)),
                  pl.BlockSpec((L, GDN_DV, DB), lambda h: (0, h, 0)),
                  whole((SUBLANES, 1)), whole((SUBLANES, 1)), whole((GDN_DV, 1)),
                  pl.BlockSpec((1, GDN_DK, GDN_DV, DB), lambda h: (h, 0, 0, 0))],
        out_specs=[pl.BlockSpec((L, GDN_DV, DB), lambda h: (0, h, 0)),
                   pl.BlockSpec((1, GDN_DK, GDN_DV, DB), lambda h: (h, 0, 0, 0))],
        out_shape=[jax.ShapeDtypeStruct((L, H * GDN_DV, DB), F32),
                   jax.ShapeDtypeStruct((H, GDN_DK, GDN_DV, DB), F32)],
        scratch_shapes=[pltpu.VMEM((2, GDN_DK, DB), F32)],
        compiler_params=_cparams("parallel"),
        name="gdn_sample",
    )(u_t, u_t, u_t, buf_t, buf_t, buf_t, w_col, w_col, w_col, ab_t, gate_t,
      col8(a_log), col8(dt_bias), norm_g.reshape(GDN_DV, 1), s_t)
    o = jnp.transpose(o_t, (2, 0, 1)).reshape(DB * L, H * GDN_DV)
    return o, jnp.transpose(s_new, (3, 0, 1, 2))


def kernel(x_prompt, x_sample, cache_swa_k, cache_swa_v, state_gdn, state_gdn_conv, cache_mem_k, cache_mem_v,
           mem_prompt, ln1_g, w_in, swa_q_norm, swa_k_norm, swa_sinks, gdn_conv_w, gdn_a_log, gdn_dt_bias,
           gdn_norm_g, mem_ln_g, w_mem_kv, mem_q_norm, mem_k_norm, w_o, ln2_g, router_w, router_b,
           moe_w1, moe_b1, moe_w2, moe_b2):
    B, S, D = x_prompt.shape
    DB, DL, _ = x_sample.shape
    depth = ln1_g.shape[0]
    assert depth == 1
    l = 0
    tp, ts = B * S, DB * DL
    t_all = tp + ts
    n_ab = 2 * N_GDN_HEADS
    c_ab = SWA_Q_COLS + 2 * SWA_KV_COLS + GDN_CONV_CH
    w = w_in[l]
    w_z = jnp.concatenate([w[:, :c_ab], w[:, c_ab + n_ab:], w[:, c_ab:c_ab + n_ab],
                           jnp.zeros((D, LANES - n_ab), F32)], axis=1).astype(BF16)
    rw = jnp.pad(router_w[l], ((0, 0), (0, LANES - N_EXPERTS)))
    rw_hi = rw.astype(BF16)
    rw_lo = (rw - rw_hi.astype(F32)).astype(BF16)
    rb = jnp.pad(router_b[l], (0, LANES - N_EXPERTS)).reshape(1, LANES)
    wo = w_o[l].astype(BF16)
    w1 = moe_w1[l]
    w2 = moe_w2[l]
    b1 = moe_b1[l].reshape(N_EXPERTS, 1, -1)
    b2 = moe_b2[l].reshape(N_EXPERTS, 1, -1)
    p = {'q_norm': swa_q_norm[l], 'k_norm': swa_k_norm[l], 'sinks': swa_sinks[l], 'conv_w': gdn_conv_w[l],
         'a_log': gdn_a_log[l], 'dt_bias': gdn_dt_bias[l], 'gdn_norm': gdn_norm_g[l], 'mem_q_norm': mem_q_norm[l]}

    xp = x_prompt.reshape(tp, D)
    xs = x_sample.reshape(ts, D)
    z_p = _inproj(xp, ln1_g[l], w_z)
    z_s = _inproj(xs, ln1_g[l], w_z)

    M = mem_prompt.shape[1]
    z_p3 = z_p.reshape(B, S, Z_COLS)
    mk2, mv2 = _mem_kv(mem_prompt.reshape(B * M, D), mem_ln_g[l], w_mem_kv[l], mem_k_norm[l])
    mk = mk2.reshape(B, M, N_MEM_HEADS, HEAD_DIM)
    mv = mv2.reshape(B, M, N_MEM_HEADS, HEAD_DIM)
    os_p, pk = _swa_prompt(z_p3, p['q_norm'], p['k_norm'], p['sinks'])
    od_p, ps = _gdn_prompt(z_p3, p['conv_w'], p['a_log'], p['dt_bias'], p['gdn_norm'])
    om_p = _mem_attn_prompt(z_p3, mk2.reshape(B, M, MEM_Q_COLS), mv2.reshape(B, M, MEM_Q_COLS), p['mem_q_norm'])
    os_p, od_p, om_p = os_p.reshape(tp, -1), od_p.reshape(tp, -1), om_p.reshape(tp, -1)
    pk = pk.reshape(B, WINDOW, N_SWA_KV, HEAD_DIM)
    pv = z_p3[:, S - WINDOW:, Z_V:Z_GDN].reshape(B, WINDOW, N_SWA_KV, HEAD_DIM)
    pc = z_p3[:, S - (GDN_CONV - 1):, Z_GDN:Z_GATE]
    os_s, sk, sv = _swa_sample(z_s, cache_swa_k[l], cache_swa_v[l], p['q_norm'], p['k_norm'], p['sinks'], DL)
    od_s, ss = _gdn_sample(z_s, state_gdn_conv[l], state_gdn[l], p['conv_w'], p['a_log'], p['dt_bias'],
                           p['gdn_norm'], DB, DL)
    om_s = _mem_attn_sample(z_s, cache_mem_k[l], cache_mem_v[l], p['mem_q_norm'], DL)
    z_s3 = z_s.reshape(DB, DL, Z_COLS)
    sc = jnp.concatenate([state_gdn_conv[l], z_s3[:, :, Z_GDN:Z_GATE]], axis=1)[:, DL:]

    h_p, hn_all, lg_all = _outproj(xp, os_p, od_p, om_p, wo, ln2_g[l], rw_hi, rw_lo, rb, t_all, 0)
    h_s, hn_all, lg_all = _outproj(xs, os_s, od_s, om_s, wo, ln2_g[l], rw_hi, rw_lo, rb, t_all, tp,
                                   prev=(hn_all, lg_all))
    y_p, y_s = _moe(hn_all, lg_all, [h_p, h_s], w1, b1, w2, b2)
    return (y_p.reshape(B, S, D), y_s.reshape(DB, DL, D), pk[None], pv[None], ps[None], pc[None], mk[None],
            mv[None], sk[None], sv[None], ss[None], sc[None])
```

```python
import functools

import jax
import jax.numpy as jnp
from jax import lax
from jax.experimental import pallas as pl
from jax.experimental.pallas import tpu as pltpu

F32 = jnp.float32
BF16 = jnp.bfloat16
I32 = jnp.int32

HEAD_DIM = 64
N_SWA_HEADS = 8
N_SWA_KV = 2
SWA_GROUP = N_SWA_HEADS // N_SWA_KV
WINDOW = 128
N_GDN_HEADS = 4
GDN_DK = 64
GDN_DV = 64
GDN_CONV = 4
GDN_CHUNK = 64
N_MEM_HEADS = 4
N_EXPERTS = 32
TOP_K = 4
SWIGLU_ALPHA = 1.702
SWIGLU_LIMIT = 7.0
EPS = 1e-6
ATTN_SCALE = HEAD_DIM ** -0.5

SWA_Q_COLS = N_SWA_HEADS * HEAD_DIM
SWA_KV_COLS = N_SWA_KV * HEAD_DIM
GDN_QK_COLS = N_GDN_HEADS * GDN_DK
GDN_V_COLS = N_GDN_HEADS * GDN_DV
GDN_CONV_CH = 2 * GDN_QK_COLS + GDN_V_COLS
MEM_Q_COLS = N_MEM_HEADS * HEAD_DIM

LANES = 128
SUBLANES = 8
VMEM_LIMIT = 56 * 1024 * 1024

Z_Q = 0
Z_K = Z_Q + SWA_Q_COLS
Z_V = Z_K + SWA_KV_COLS
Z_GDN = Z_V + SWA_KV_COLS
Z_GATE = Z_GDN + GDN_CONV_CH
Z_QM = Z_GATE + GDN_V_COLS
Z_AB = Z_QM + MEM_Q_COLS
Z_COLS = Z_AB + LANES

ROW_TILE = 512
MOE_TILE = 512
MOE_BLK = 512
PERM_CHUNK = 256
RUN_ALIGN = 16
BLK_ROWS = -(-(MOE_BLK * TOP_K + N_EXPERTS * (RUN_ALIGN - 1)) // PERM_CHUNK) * PERM_CHUNK


def _cparams(*sem):
    return pltpu.CompilerParams(dimension_semantics=sem, vmem_limit_bytes=VMEM_LIMIT)


def _bdot(a, b):
    return jnp.dot(a.astype(BF16), b.astype(BF16), preferred_element_type=F32)


def _bdot_nt(a, b):
    return lax.dot_general(a.astype(BF16), b.astype(BF16), (((1,), (1,)), ((), ())),
                           preferred_element_type=F32)


def _bdot_tn(a, b):
    return lax.dot_general(a.astype(BF16), b.astype(BF16), (((0,), (0,)), ((), ())),
                           preferred_element_type=F32)


def _split2(x):
    hi = x.astype(BF16)
    lo = (x - hi.astype(F32)).astype(BF16)
    return hi, lo


def _split3(x):
    hi = x.astype(BF16)
    r = x - hi.astype(F32)
    mid = r.astype(BF16)
    lo = (r - mid.astype(F32)).astype(BF16)
    return hi, mid, lo


def _rms_rows(x, g):
    ms = jnp.mean(x * x, axis=-1, keepdims=True)
    return x * lax.rsqrt(ms + EPS) * g


def _inproj_kernel(x_ref, g_ref, w_ref, z_ref):
    n = _rms_rows(x_ref[...], g_ref[...])
    z_ref[...] = jnp.dot(n.astype(BF16), w_ref[...], preferred_element_type=F32)


def _inproj(x2d, ln_g, w_z):
    t, d = x2d.shape
    tm = min(ROW_TILE, t)
    return pl.pallas_call(
        _inproj_kernel,
        grid=(t // tm,),
        in_specs=[pl.BlockSpec((tm, d), lambda i: (i, 0)),
                  pl.BlockSpec((1, d), lambda i: (0, 0)),
                  pl.BlockSpec((d, Z_COLS), lambda i: (0, 0))],
        out_specs=pl.BlockSpec((tm, Z_COLS), lambda i: (i, 0)),
        out_shape=jax.ShapeDtypeStruct((t, Z_COLS), F32),
        compiler_params=_cparams("parallel"),
        name="inproj",
    )(x2d, ln_g.reshape(1, d), w_z)


def _outproj_kernel(x_ref, os_ref, od_ref, om_ref, wo_ref, g_ref, rwh_ref, rwl_ref, rb_ref,
                    *refs, n_own):
    h_ref, hn_ref, lg_ref = refs[-3:]
    i = pl.program_id(0)

    @pl.when(i < n_own)
    def _():
        n_s = os_ref.shape[1]
        n_d = od_ref.shape[1]
        h = x_ref[...]
        h = h + jnp.dot(os_ref[...].astype(BF16), wo_ref[0:n_s, :], preferred_element_type=F32)
        h = h + jnp.dot(od_ref[...].astype(BF16), wo_ref[n_s:n_s + n_d, :], preferred_element_type=F32)
        h = h + jnp.dot(om_ref[...].astype(BF16), wo_ref[n_s + n_d:, :], preferred_element_type=F32)
        h_ref[...] = h
        hn = _rms_rows(h, g_ref[...])
        hn_ref[...] = hn.astype(BF16)
        hi, lo = _split2(hn)
        lg = (jnp.dot(hi, rwh_ref[...], preferred_element_type=F32)
              + jnp.dot(lo, rwh_ref[...], preferred_element_type=F32)
              + jnp.dot(hi, rwl_ref[...], preferred_element_type=F32))
        lg_ref[...] = lg + rb_ref[...]

    @pl.when(i >= n_own)
    def _():
        hn_ref[...] = jnp.zeros_like(hn_ref)
        lg_ref[...] = jnp.zeros_like(lg_ref)


def _outproj(x2d, o_s, o_d, o_m, w_o, ln_g, rw_hi, rw_lo, rb, t_all, row0, prev=None):
    t, d = x2d.shape
    tm = min(ROW_TILE, t)
    blk0 = row0 // tm
    n_own = t // tm
    n_steps = n_own if prev is not None else t_all // tm
    row = lambda i: (jnp.minimum(i, n_own - 1), 0)
    row_off = lambda i: (i + blk0, 0)
    const = lambda i: (0, 0)
    in_specs = [pl.BlockSpec((tm, d), row),
                pl.BlockSpec((tm, o_s.shape[1]), row),
                pl.BlockSpec((tm, o_d.shape[1]), row),
                pl.BlockSpec((tm, o_m.shape[1]), row),
                pl.BlockSpec((d, d), const),
                pl.BlockSpec((1, d), const),
                pl.BlockSpec((d, LANES), const),
                pl.BlockSpec((d, LANES), const),
                pl.BlockSpec((1, LANES), const)]
    args = [x2d, o_s, o_d, o_m, w_o, ln_g.reshape(1, d), rw_hi, rw_lo, rb]
    aliases = {}
    if prev is not None:
        in_specs += [pl.BlockSpec(memory_space=pl.ANY), pl.BlockSpec(memory_space=pl.ANY)]
        aliases = {len(args): 1, len(args) + 1: 2}
        args += list(prev)
    return pl.pallas_call(
        functools.partial(_outproj_kernel, n_own=n_own),
        grid=(n_steps,),
        in_specs=in_specs,
        out_specs=[pl.BlockSpec((tm, d), row),
                   pl.BlockSpec((tm, d), row_off),
                   pl.BlockSpec((tm, LANES), row_off)],
        out_shape=[jax.ShapeDtypeStruct((t, d), F32),
                   jax.ShapeDtypeStruct((t_all, d), BF16),
                   jax.ShapeDtypeStruct((t_all, LANES), F32)],
        input_output_aliases=aliases,
        compiler_params=_cparams("arbitrary"),
        name="outproj_router",
    )(*args)


def _route_kernel(lg_ref, pos_ref, post_ref, g_ref, cnt_ref):
    tm = lg_ref.shape[0]
    lane = lax.broadcasted_iota(I32, (tm, LANES), 1).astype(F32)
    l = jnp.where(lane < N_EXPERTS, lg_ref[...], -jnp.inf)
    vals, idxs = [], []
    for _k in range(TOP_K):
        m = jnp.max(l, axis=-1, keepdims=True)
        idx = jnp.min(jnp.where(l == m, lane, float(LANES)), axis=-1, keepdims=True)
        l = jnp.where(lane == idx, -jnp.inf, l)
        vals.append(m)
        idxs.append(idx)
    ex = [jnp.exp(v - vals[0]) for v in vals]
    den = ex[0] + ex[1] + ex[2] + ex[3]
    member = jnp.zeros((tm, LANES), F32)
    for idx in idxs:
        member = member + jnp.where(lane == idx, 1.0, 0.0)
    ri = lax.broadcasted_iota(I32, (tm, tm), 0)
    ci = lax.broadcasted_iota(I32, (tm, tm), 1)
    strict = jnp.where(ci < ri, 1.0, 0.0).astype(BF16)
    prefix = jnp.dot(strict, member.astype(BF16), preferred_element_type=F32)
    cnt = jnp.sum(member, axis=0, keepdims=True)
    cpad = jnp.ceil(cnt * (1.0 / RUN_ALIGN)) * float(RUN_ALIGN)
    c_hi = jnp.floor(cpad * (1.0 / 256.0))
    c_lo = cpad - 256.0 * c_hi
    ej = lax.broadcasted_iota(I32, (LANES, LANES), 0)
    ee = lax.broadcasted_iota(I32, (LANES, LANES), 1)
    before = jnp.where(ej < ee, 1.0, 0.0).astype(BF16)
    bcast = lambda v: jnp.broadcast_to(v, (SUBLANES, LANES)).astype(BF16)
    off = (256.0 * jnp.dot(bcast(c_hi), before, preferred_element_type=F32)
           + jnp.dot(bcast(c_lo), before, preferred_element_type=F32))[0:1]
    where_in_run = prefix + off
    p_out = jnp.zeros((tm, LANES), F32)
    g_out = jnp.zeros((tm, LANES), F32)
    for k in range(TOP_K):
        pos = jnp.sum(jnp.where(lane == idxs[k], where_in_run, 0.0), axis=-1, keepdims=True)
        p_out = jnp.where(lane == float(k), pos, p_out)
        g_out = jnp.where(lane == float(k), ex[k] / den, g_out)
    pos_ref[...] = p_out[:, :TOP_K]
    post_ref[...] = p_out.T[:SUBLANES, :]
    g_ref[...] = g_out[:, :TOP_K]
    cnt_ref[0] = cnt


def _route(logits):
    t = logits.shape[0]
    tm = MOE_BLK
    nb = t // tm
    return pl.pallas_call(
        _route_kernel,
        grid=(nb,),
        in_specs=[pl.BlockSpec((tm, LANES), lambda i: (i, 0))],
        out_specs=[pl.BlockSpec((tm, TOP_K), lambda i: (i, 0)),
                   pl.BlockSpec((SUBLANES, tm), lambda i: (0, i)),
                   pl.BlockSpec((tm, TOP_K), lambda i: (i, 0)),
                   pl.BlockSpec((1, 1, LANES), lambda i: (i, 0, 0))],
        out_shape=[jax.ShapeDtypeStruct((t, TOP_K), F32),
                   jax.ShapeDtypeStruct((SUBLANES, t), F32),
                   jax.ShapeDtypeStruct((t, TOP_K), F32),
                   jax.ShapeDtypeStruct((nb, 1, LANES), F32)],
        compiler_params=_cparams("parallel"),
        name="route",
    )(logits)


def _run_copies(n, max_rows, src_ref, src0, dst_ref, dst0, sem, wait):
    pos = 0
    bit = max_rows
    while bit >= RUN_ALIGN:
        take = (n & bit) != 0

        def go(pos=pos, bit=bit):
            cp = pltpu.make_async_copy(src_ref.at[pl.ds(pl.multiple_of(src0 + pos, RUN_ALIGN), bit)],
                                       dst_ref.at[pl.ds(pl.multiple_of(dst0 + pos, RUN_ALIGN), bit)], sem)
            cp.wait() if wait else cp.start()

        pl.when(take)(go)
        pos = pos + jnp.where(take, bit, 0)
        bit //= 2


RUN_SIZES = tuple(MOE_BLK >> i for i in range((MOE_BLK // RUN_ALIGN).bit_length()))


def _piece_copies(b, cnt_ref, loc_ref, glob_ref, local_ref, global_hbm, sem, to_global, wait):
    for c, rows in enumerate(RUN_SIZES):
        base = b * len(RUN_SIZES) + c

        def body(s, carry, rows=rows, base=base):
            j = base * N_EXPERTS + s
            loc = local_ref.at[pl.ds(pl.multiple_of(loc_ref[j], RUN_ALIGN), rows)]
            glob = global_hbm.at[pl.ds(pl.multiple_of(glob_ref[j], RUN_ALIGN), rows)]
            cp = pltpu.make_async_copy(loc, glob, sem) if to_global else pltpu.make_async_copy(glob, loc, sem)
            cp.wait() if wait else cp.start()
            return carry

        lax.fori_loop(0, cnt_ref[base], body, 0)


def _dispatch_kernel(cnt_ref, loc_ref, glob_ref, rows_ref, estart_ref, elen_ref, nused_ref,
                     hn_ref, post_ref, xs_hbm, buf_ref, zero_ref, sem, zsem):
    b = pl.program_id(0)
    nb = pl.num_programs(0)
    slot = b % 2
    tm = hn_ref.shape[0]
    x = hn_ref[...].astype(BF16)
    post = post_ref[...]
    P = PERM_CHUNK

    def sort_rows(c):
        r = (lax.broadcasted_iota(I32, (P, tm), 0) + c * P).astype(F32)
        sel = jnp.zeros((P, tm), F32)
        for k in range(TOP_K):
            sel = jnp.where(r == post[k:k + 1, :], 1.0, sel)
        buf_ref[slot, c * P:(c + 1) * P, :] = jnp.dot(sel.astype(BF16), x,
                                                      preferred_element_type=F32).astype(BF16)

    for c in range(BLK_ROWS // P):
        if c * P < tm * TOP_K:
            sort_rows(c)
        else:
            pl.when(rows_ref[b] > c * P)(functools.partial(sort_rows, c))

    def runs(blk, s, wait):
        _piece_copies(blk, cnt_ref, loc_ref, glob_ref, buf_ref.at[s], xs_hbm, sem.at[s], True, wait)

    runs(b, slot, False)

    @pl.when(b == 0)
    def _():
        zero_ref[...] = jnp.zeros_like(zero_ref)

        def tail(wait):
            def body(e, c):
                n = (MOE_TILE - elen_ref[e] % MOE_TILE) % MOE_TILE
                _run_copies(n, MOE_TILE // 2, zero_ref, 0, xs_hbm, estart_ref[e] + elen_ref[e], zsem, wait)
                return c
            lax.fori_loop(0, N_EXPERTS, body, 0)

            def free_tile(ti, c):
                for half in range(2):
                    _run_copies(MOE_TILE // 2, MOE_TILE // 2, zero_ref, 0, xs_hbm,
                                ti * MOE_TILE + half * (MOE_TILE // 2), zsem, wait)
                return c
            lax.fori_loop(nused_ref[0], xs_hbm.shape[0] // MOE_TILE, free_tile, 0)

        tail(False)
        tail(True)

    pl.when(b > 0)(lambda: runs(b - 1, 1 - slot, True))
    pl.when(b == nb - 1)(lambda: runs(b, slot, True))


def _dispatch(hn, post, piece_cnt, piece_loc, piece_glob, blk_rows, e_start, e_len, n_used, n_rows):
    t, d = hn.shape
    grid_spec = pltpu.PrefetchScalarGridSpec(
        num_scalar_prefetch=7,
        grid=(t // MOE_BLK,),
        in_specs=[pl.BlockSpec((MOE_BLK, d), lambda i, *_: (i, 0)),
                  pl.BlockSpec((SUBLANES, MOE_BLK), lambda i, *_: (0, i))],
        out_specs=pl.BlockSpec(memory_space=pl.ANY),
        scratch_shapes=[pltpu.VMEM((2, BLK_ROWS, d), BF16),
                        pltpu.VMEM((MOE_TILE, d), BF16),
                        pltpu.SemaphoreType.DMA((2,)),
                        pltpu.SemaphoreType.DMA(())],
    )
    return pl.pallas_call(
        _dispatch_kernel,
        grid_spec=grid_spec,
        out_shape=jax.ShapeDtypeStruct((n_rows, d), BF16),
        compiler_params=_cparams("arbitrary"),
        name="dispatch",
    )(piece_cnt, piece_loc, piece_glob, blk_rows, e_start, e_len, n_used, hn, post)


def _expert_kernel(te_ref, nu_ref, nxt_ref, slot_ref, x_ref, w1_hbm, b1_ref, w2_hbm, b2_ref, y_ref,
                   w1f_ref, w2f_ref, w1b_ref, w2b_ref, sem):
    i = pl.program_id(0)
    live = i < nu_ref[0]
    e = te_ref[i]
    s = slot_ref[e]

    def weight_copies(expert, slot):
        return (pltpu.make_async_copy(w1_hbm.at[expert], w1f_ref.at[slot], sem.at[0, slot]),
                pltpu.make_async_copy(w2_hbm.at[expert], w2f_ref.at[slot], sem.at[1, slot]))

    @pl.when(live & (i == 0))
    def _():
        for cp in weight_copies(e, s):
            cp.start()

    @pl.when(live & ((i == 0) | (e != te_ref[jnp.maximum(i - 1, 0)])))
    def _():
        for cp in weight_copies(e, s):
            cp.wait()
        w1b_ref[...] = w1f_ref[s].astype(BF16)
        w2b_ref[...] = w2f_ref[s].astype(BF16)

        @pl.when(nxt_ref[e] >= 0)
        def _():
            for cp in weight_copies(nxt_ref[e], 1 - s):
                cp.start()

    @pl.when(live)
    def _():
        f = w2b_ref.shape[0]
        h = jnp.dot(x_ref[...], w1b_ref[...], preferred_element_type=F32) + b1_ref[0]
        glu = jnp.minimum(h[:, :f], SWIGLU_LIMIT)
        lin = jnp.clip(h[:, f:], -SWIGLU_LIMIT, SWIGLU_LIMIT)
        act = glu * jax.nn.sigmoid(SWIGLU_ALPHA * glu) * (lin + 1.0)
        y = jnp.dot(act.astype(BF16), w2b_ref[...], preferred_element_type=F32) + b2_ref[0]
        y_ref[...] = y.astype(BF16)

    @pl.when(i >= nu_ref[0])
    def _():
        y_ref[...] = jnp.zeros_like(y_ref)


def _experts(xs, tile_expert, n_used, next_expert, expert_slot, w1, b1, w2, b2):
    n_rows, d = xs.shape
    f2 = w1.shape[2]
    f = w2.shape[1]
    n_tiles = n_rows // MOE_TILE
    live = lambda i, te, nu, *_: (jnp.minimum(i, nu[0] - 1), 0)
    every = lambda i, *_: (i, 0)
    wsel = lambda i, te, *_: (te[i], 0, 0)
    grid_spec = pltpu.PrefetchScalarGridSpec(
        num_scalar_prefetch=4,
        grid=(n_tiles,),
        in_specs=[pl.BlockSpec((MOE_TILE, d), live),
                  pl.BlockSpec(memory_space=pl.ANY),
                  pl.BlockSpec((1, 1, f2), wsel),
                  pl.BlockSpec(memory_space=pl.ANY),
                  pl.BlockSpec((1, 1, d), wsel)],
        out_specs=pl.BlockSpec((MOE_TILE, d), every),
        scratch_shapes=[pltpu.VMEM((2, d, f2), F32), pltpu.VMEM((2, f, d), F32),
                        pltpu.VMEM((d, f2), BF16), pltpu.VMEM((f, d), BF16),
                        pltpu.SemaphoreType.DMA((2, 2))],
    )
    return pl.pallas_call(
        _expert_kernel,
        grid_spec=grid_spec,
        out_shape=jax.ShapeDtypeStruct((n_rows, d), BF16),
        compiler_params=_cparams("arbitrary"),
        name="experts",
    )(tile_expert, n_used, next_expert, expert_slot, xs, w1, b1, w2, b2)


def _combine_kernel(cnt_ref, loc_ref, glob_ref, rows_ref, h_ref, pos_ref, g_ref, yb_hbm, y_ref, buf_ref, sem,
                    *, blk0):
    i = pl.program_id(0)
    n_steps = pl.num_programs(0)
    b = i + blk0
    slot = i % 2
    tm, d = h_ref.shape

    def runs(blk, s, wait):
        _piece_copies(blk, cnt_ref, loc_ref, glob_ref, buf_ref.at[s], yb_hbm, sem.at[s], False, wait)

    @pl.when(i == 0)
    def _():
        buf_ref[...] = jnp.zeros_like(buf_ref)
        runs(b, slot, False)

    pl.when(i + 1 < n_steps)(lambda: runs(b + 1, 1 - slot, False))
    runs(b, slot, True)
    pos = pos_ref[...]
    g = g_ref[...]
    P = PERM_CHUNK

    def weighted_rows(c):
        col = (lax.broadcasted_iota(I32, (tm, P), 1) + c * P).astype(F32)
        wgt = jnp.zeros((tm, P), F32)
        for k in range(TOP_K):
            wgt = jnp.where(col == pos[:, k:k + 1], g[:, k:k + 1], wgt)
        return jnp.dot(wgt.astype(BF16), buf_ref[slot, c * P:(c + 1) * P, :], preferred_element_type=F32)

    n_sure = tm * TOP_K // P
    y = h_ref[...]
    for c in range(n_sure):
        y = y + weighted_rows(c)
    y_ref[...] = y
    for c in range(n_sure, BLK_ROWS // P):
        @pl.when(rows_ref[b] > c * P)
        def _(c=c):
            y_ref[...] += weighted_rows(c)


def _combine(h, pos, gates, piece_cnt, piece_loc, piece_glob, blk_rows, yb, blk0):
    t, d = h.shape
    grid_spec = pltpu.PrefetchScalarGridSpec(
        num_scalar_prefetch=4,
        grid=(t // MOE_BLK,),
        in_specs=[pl.BlockSpec((MOE_BLK, d), lambda i, *_: (i, 0)),
                  pl.BlockSpec((MOE_BLK, TOP_K), lambda i, *_: (i + blk0, 0)),
                  pl.BlockSpec((MOE_BLK, TOP_K), lambda i, *_: (i + blk0, 0)),
                  pl.BlockSpec(memory_space=pl.ANY)],
        out_specs=pl.BlockSpec((MOE_BLK, d), lambda i, *_: (i, 0)),
        scratch_shapes=[pltpu.VMEM((2, BLK_ROWS, d), BF16),
                        pltpu.SemaphoreType.DMA((2,))],
    )
    return pl.pallas_call(
        functools.partial(_combine_kernel, blk0=blk0),
        grid_spec=grid_spec,
        out_shape=jax.ShapeDtypeStruct((t, d), F32),
        compiler_params=_cparams("arbitrary"),
        name="combine",
    )(piece_cnt, piece_loc, piece_glob, blk_rows, h, pos, gates, yb)


def _moe(hn_all, logits_all, h_parts, w1, b1, w2, b2):
    t_all = hn_all.shape[0]
    nb = t_all // MOE_BLK
    pos, post, gates, counts_f = _route(logits_all)
    cnt = counts_f.reshape(nb, LANES)[:, :N_EXPERTS].astype(I32)
    seg_len = (cnt + RUN_ALIGN - 1) // RUN_ALIGN * RUN_ALIGN
    before_e = jnp.arange(N_EXPERTS)[:, None] < jnp.arange(N_EXPERTS)[None, :]
    before_b = jnp.arange(nb)[None, :] < jnp.arange(nb)[:, None]
    seg_off = jnp.sum(jnp.where(before_e[None], seg_len[:, :, None], 0), axis=1)
    e_len = jnp.sum(seg_len, axis=0)
    e_tiles = (e_len + MOE_TILE - 1) // MOE_TILE
    tile_start = jnp.sum(jnp.where(before_e, e_tiles[:, None], 0), axis=0)
    tile_end = tile_start + e_tiles
    e_start = tile_start * MOE_TILE
    seg_dst = e_start[None, :] + jnp.sum(jnp.where(before_b[:, :, None], seg_len[None], 0), axis=1)
    max_rows = t_all * TOP_K + nb * N_EXPERTS * (RUN_ALIGN - 1) + N_EXPERTS * (MOE_TILE - RUN_ALIGN)
    n_tiles = -(-max_rows // MOE_TILE)
    n_rows = n_tiles * MOE_TILE
    n_used = tile_end[-1:].astype(I32)
    tile_expert = jnp.minimum(jnp.sum(tile_end[None, :] <= jnp.arange(n_tiles, dtype=I32)[:, None], axis=1),
                              N_EXPERTS - 1).astype(I32)
    sizes = jnp.array(RUN_SIZES, I32)[None, :, None]
    n_run = seg_len[:, None, :]
    has = (n_run & sizes) != 0
    piece_at = n_run & ~(2 * sizes - 1)
    rank = jnp.sum(jnp.where(before_e[None, None], has[:, :, :, None], False), axis=2)
    slot = jnp.arange(N_EXPERTS)
    put = has[..., None] & (rank[..., None] == slot)
    listed = lambda v: jnp.sum(jnp.where(put, v[..., None], 0), axis=2)
    piece_loc = listed(seg_off[:, None, :] + piece_at)
    piece_glob = listed(seg_dst[:, None, :] + piece_at)
    piece_cnt = jnp.sum(has, axis=2)
    flat = lambda a: a.reshape(-1).astype(I32)
    tables = (flat(piece_cnt), flat(piece_loc), flat(piece_glob), flat(jnp.sum(seg_len, axis=1)))
    xs = _dispatch(hn_all, post, *tables, flat(e_start), flat(e_len), n_used, n_rows)
    e_ids = jnp.arange(N_EXPERTS)
    used = e_tiles > 0
    next_expert = jnp.min(jnp.where(before_e & used[None, :], e_ids[None, :], N_EXPERTS), axis=1)
    next_expert = jnp.where(next_expert == N_EXPERTS, -1, next_expert)
    expert_slot = jnp.sum(jnp.where(before_e & used[:, None], 1, 0), axis=0) % 2
    yb = _experts(xs, tile_expert, n_used, flat(next_expert), flat(expert_slot), w1, b1, w2, b2)
    outs = []
    row = 0
    for h in h_parts:
        outs.append(_combine(h, pos, gates, *tables, yb, row // MOE_BLK))
        row += h.shape[0]
    return outs


GDN_ROWS = 4 * GDN_CHUNK
CONV_HALO = SUBLANES
NEUMANN_SPLIT = 2
AB_LANES = 2 * N_GDN_HEADS
SOLVE_ROWS = 2 * GDN_CHUNK


def _softplus(x):
    return jnp.maximum(x, 0.0) + jnp.log1p(jnp.exp(-jnp.abs(x)))


def _gdn_prompt_kernel(u_ref, ab_ref, gate_ref, cw_ref, alog_ref, dtb_ref, ng_ref, o_ref, s_ref, ubuf_ref):
    step = pl.program_id(0)
    NB = u_ref.shape[0]
    R = GDN_ROWS
    C = GDN_CHUNK
    NC = R // C

    @pl.when(step == 0)
    def _():
        ubuf_ref[:, 0:CONV_HALO, :] = jnp.zeros((NB, CONV_HALO, ubuf_ref.shape[2]), F32)
        s_ref[...] = jnp.zeros_like(s_ref)

    ri = lax.broadcasted_iota(I32, (R, R), 0)
    ci = lax.broadcasted_iota(I32, (R, R), 1)
    shift = C.bit_length() - 1
    same = lax.shift_right_logical(ri, shift) == lax.shift_right_logical(ci, shift)
    incl = same & (ci <= ri)
    strict = same & (ci < ri)
    tri = jnp.where(incl, 1.0, 0.0).astype(BF16)
    blk = jnp.where(same, 1.0, 0.0).astype(BF16)
    cw = cw_ref[...]
    ng = ng_ref[...]

    lane = lax.broadcasted_iota(I32, (R, LANES), 1)
    lane1 = lax.broadcasted_iota(I32, (1, LANES), 1)
    ab = ab_ref[0]
    alog = alog_ref[...]
    dtb = dtb_ref[...]
    for b in range(1, NB):
        own = (lane >= b * AB_LANES) & (lane < (b + 1) * AB_LANES)
        own1 = (lane1 >= b * AB_LANES) & (lane1 < (b + 1) * AB_LANES)
        ab = jnp.where(own, pltpu.roll(ab_ref[b], b * AB_LANES, 1), ab)
        alog = jnp.where(own1, pltpu.roll(alog_ref[...], b * AB_LANES, 1), alog)
        dtb = jnp.where(own1, pltpu.roll(dtb_ref[...], b * AB_LANES, 1), dtb)
    g_t = -jnp.exp(alog) * _softplus(ab + dtb)
    beta_t = jax.nn.sigmoid(ab)
    both = jnp.concatenate([tri, blk], axis=0)
    sums = sum(jnp.dot(both, p, preferred_element_type=F32) for p in _split3(g_t))
    gcum, gtot = sums[:R], sums[R:]
    gcum_t = gcum.T

    def conv_and_norms(b):
        u = u_ref[b]
        ubuf_ref[b, CONV_HALO:CONV_HALO + R, :] = u
        y = u * cw[GDN_CONV - 1:GDN_CONV, :]
        for j in range(1, GDN_CONV):
            y = y + ubuf_ref[b, CONV_HALO - j:CONV_HALO - j + R, :] * cw[GDN_CONV - 1 - j:GDN_CONV - j, :]
        ubuf_ref[b, 0:CONV_HALO, :] = u[R - CONV_HALO:, :]
        qkv = y * jax.nn.sigmoid(y)
        qk_n = []
        for t in range(2 * GDN_QK_COLS // LANES):
            x = qkv[:, t * LANES:(t + 1) * LANES]
            x = x * lax.rsqrt(_pair_sumsq(x) + EPS)
            qk_n.append(x * (GDN_DK ** -0.5) if t < GDN_QK_COLS // LANES else x)
        return qkv, jnp.concatenate(qk_n, axis=-1)

    def make_chain(b, h, qkv, qk_n):
        q = qk_n[:, h * GDN_DK:(h + 1) * GDN_DK]
        k = qk_n[:, GDN_QK_COLS + h * GDN_DK:GDN_QK_COLS + (h + 1) * GDN_DK]
        v = qkv[:, 2 * GDN_QK_COLS + h * GDN_DV:2 * GDN_QK_COLS + (h + 1) * GDN_DV]
        col = b * AB_LANES + h
        gc = gcum[:, col:col + 1]
        gt = gtot[:, col:col + 1]
        beta = beta_t[:, col + N_GDN_HEADS:col + N_GDN_HEADS + 1]
        r = jnp.concatenate([v * beta, k * (beta * jnp.exp(gc))], axis=-1)
        a_blocks, qk_blocks, r_blocks = [], [], []
        for sb in range(R // SOLVE_ROWS):
            rows = slice(sb * SOLVE_ROWS, (sb + 1) * SOLVE_ROWS)
            decay = jnp.exp(jnp.where(incl[:SOLVE_ROWS, :SOLVE_ROWS], gc[rows] - gcum_t[col:col + 1, rows], -jnp.inf))
            a_blocks.append(jnp.where(strict[:SOLVE_ROWS, :SOLVE_ROWS],
                                      beta[rows] * _bdot_nt(k[rows], k[rows]) * decay, 0.0))
            qk_blocks.append(_bdot_nt(q[rows], k[rows]) * decay)
            r_blocks.append(r[rows])
        return dict(b=b, h=h, a=a_blocks, qk=qk_blocks, r=r_blocks,
                    q_dec=q * jnp.exp(gc), k_dec=k * jnp.exp(gt - gc), g_last=jnp.exp(gt))

    dot = lambda x, y: jnp.dot(x, y, preferred_element_type=F32)
    n_levels = C.bit_length() - 1

    def neumann_level(chains, j):
        for ch in chains:
            for sb in range(len(ch['a'])):
                a, r = ch['a'][sb], ch['r'][sb]
                if j < NEUMANN_SPLIT:
                    a_hi, a_lo = _split2(a)
                    r_hi, r_lo = _split2(r)
                    upd = dot(a_hi, r_hi) + dot(a_hi, r_lo) + dot(a_lo, r_hi)
                else:
                    a_hi = a.astype(BF16)
                    upd = dot(a_hi, r.astype(BF16))
                ch['r'][sb] = r - upd if j == 0 else r + upd
                if j + 1 < n_levels:
                    sq = dot(a_hi, a_hi)
                    if j + 1 < NEUMANN_SPLIT:
                        sq = sq + dot(a_hi, a_lo) + dot(a_lo, a_hi)
                    ch['a'][sb] = sq

    def chunk_begin(chains):
        for ch in chains:
            r = jnp.concatenate(ch['r'], axis=0)
            ch['u'], ch['w'] = r[:, :GDN_DV], r[:, GDN_DV:]
            ch['S'] = s_ref[ch['b'], ch['h']]
            ch['k_dec_t'] = ch['k_dec'].T
            ch['outs'] = []

    def chunk_step(chains, c):
        sl = slice(c * C, (c + 1) * C)
        per = SOLVE_ROWS // C
        loc = slice((c % per) * C, (c % per + 1) * C)
        for ch in chains:
            S = ch['S']
            v_new = ch['u'][sl] - _bdot(ch['w'][sl], S)
            ch['outs'].append(_bdot(ch['q_dec'][sl], S) + _bdot(ch['qk'][c // per][loc, loc], v_new))
            ch['S'] = S * ch['g_last'][c * C:c * C + 1, :] + _bdot(ch['k_dec_t'][:, sl], v_new)

    def chunk_end(chains):
        for ch in chains:
            b, h = ch['b'], ch['h']
            s_ref[b, h] = ch['S']
            o = jnp.concatenate(ch['outs'], axis=0)
            o = o * lax.rsqrt(jnp.mean(o * o, axis=-1, keepdims=True) + EPS) * ng
            gh = gate_ref[b, :, h * GDN_DV:(h + 1) * GDN_DV]
            o_ref[b, :, h * GDN_DV:(h + 1) * GDN_DV] = (o * (gh * jax.nn.sigmoid(gh))).astype(o_ref.dtype)

    setup = [None] * NB
    chains = [[] for _ in range(NB)]
    slots = max(n_levels, N_GDN_HEADS, NC)
    for stage in range(NB + 2):
        b_set, b_sol, b_rec = stage, stage - 1, stage - 2
        if 0 <= b_set < NB:
            setup[b_set] = conv_and_norms(b_set)
        if 0 <= b_rec < NB:
            chunk_begin(chains[b_rec])
        for j in range(slots):
            if 0 <= b_sol < NB and j < n_levels:
                neumann_level(chains[b_sol], j)
            if 0 <= b_set < NB and j < N_GDN_HEADS:
                chains[b_set].append(make_chain(b_set, j, *setup[b_set]))
            if 0 <= b_rec < NB and j < NC:
                chunk_step(chains[b_rec], j)
        if 0 <= b_rec < NB:
            chunk_end(chains[b_rec])


def _gdn_prompt(z3, conv_w, a_log, dt_bias, norm_g):
    B, S, _ = z3.shape
    R = GDN_ROWS
    lanes4 = lambda a: jnp.pad(a, (0, LANES - a.shape[0])).reshape(1, LANES)
    return pl.pallas_call(
        _gdn_prompt_kernel,
        grid=(S // R,),
        in_specs=[pl.BlockSpec((B, R, GDN_CONV_CH), lambda s: (0, s, Z_GDN // GDN_CONV_CH)),
                  pl.BlockSpec((B, R, LANES), lambda s: (0, s, Z_AB // LANES)),
                  pl.BlockSpec((B, R, GDN_V_COLS), lambda s: (0, s, Z_GATE // GDN_V_COLS)),
                  pl.BlockSpec((GDN_CONV, GDN_CONV_CH), lambda s: (0, 0)),
                  pl.BlockSpec((1, LANES), lambda s: (0, 0)),
                  pl.BlockSpec((1, LANES), lambda s: (0, 0)),
                  pl.BlockSpec((1, GDN_DV), lambda s: (0, 0))],
        out_specs=[pl.BlockSpec((B, R, GDN_V_COLS), lambda s: (0, s, 0)),
                   pl.BlockSpec((B, N_GDN_HEADS, GDN_DK, GDN_DV), lambda s: (0, 0, 0, 0))],
        out_shape=[jax.ShapeDtypeStruct((B, S, GDN_V_COLS), BF16),
                   jax.ShapeDtypeStruct((B, N_GDN_HEADS, GDN_DK, GDN_DV), F32)],
        scratch_shapes=[pltpu.VMEM((B, CONV_HALO + R, GDN_CONV_CH), F32)],
        compiler_params=_cparams("arbitrary"),
        name="gdn_prompt",
    )(z3, z3, z3, conv_w, lanes4(a_log), lanes4(dt_bias), norm_g.reshape(1, GDN_DV))


def _pair_sumsq(x):
    li = lax.broadcasted_iota(I32, (LANES, LANES), 0) // HEAD_DIM
    lj = lax.broadcasted_iota(I32, (LANES, LANES), 1) // HEAD_DIM
    same = jnp.where(li == lj, 1.0, 0.0).astype(BF16)
    hi, lo = _split2(x * x)
    return jnp.dot(hi, same, preferred_element_type=F32) + jnp.dot(lo, same, preferred_element_type=F32)


def _pair_rms(x, g):
    return x * lax.rsqrt(_pair_sumsq(x) * (1.0 / HEAD_DIM) + EPS) * g


def _first_half(shape):
    return lax.broadcasted_iota(I32, shape, 1) < HEAD_DIM


LOG2E = 1.4426950408889634


SWA_BLOCKS = 2


def _swa_prompt_kernel(sink_ref, q_ref, kc_ref, kp_ref, vc_ref, vp_ref, qg_ref, kg_ref, o_ref, kn_ref,
                       bias_ref):
    first = (pl.program_id(0) == 0) & (pl.program_id(1) == 0)
    n = pl.program_id(1)
    W = WINDOW
    NQ = SWA_BLOCKS

    @pl.when(first)
    def _():
        qi = lax.broadcasted_iota(I32, (W, 2 * W), 0)
        kj = lax.broadcasted_iota(I32, (W, 2 * W), 1)
        dist = qi + W - kj
        band = (dist >= 0) & (dist < W)
        distf = dist.astype(F32)
        for has_prev in range(2):
            mask = jnp.where(band & ((has_prev == 1) | (kj >= W)), 0.0, -jnp.inf)
            for head in range(N_SWA_HEADS):
                slope = 2.0 ** (-(8.0 / N_SWA_HEADS) * (head + 1))
                bias_ref[has_prev, head] = mask - (slope * LOG2E) * distf

    kg = kg_ref[...]
    qg = qg_ref[...]
    kc = _pair_rms(kc_ref[0], kg)
    kn_ref[0] = kc[(NQ - 1) * W:]
    k3 = jnp.concatenate([_pair_rms(kp_ref[0], kg), kc], axis=0)
    v3 = jnp.concatenate([vp_ref[0], vc_ref[0]], axis=0)
    fh = _first_half(k3.shape)
    k3r = pltpu.roll(k3, HEAD_DIM, 1)
    v3r = pltpu.roll(v3, HEAD_DIM, 1)
    kdup = (jnp.where(fh, k3, k3r).astype(BF16), jnp.where(fh, k3r, k3).astype(BF16))
    vdup = (jnp.where(fh, v3, v3r).astype(BF16), jnp.where(fh, v3r, v3).astype(BF16))
    fq = _first_half((W, LANES))
    kv_of = lambda head: head // SWA_GROUP
    probs = [(j, head) for j in range(NQ) for head in range(N_SWA_HEADS)]
    keys = lambda j: slice(j * W, (j + 2) * W)
    qts = [[_pair_rms(q_ref[0, j * W:(j + 1) * W, t * LANES:(t + 1) * LANES], qg) * (ATTN_SCALE * LOG2E)
            for t in range(SWA_Q_COLS // LANES)] for j in range(NQ)]
    qms = [jnp.where(fq == (head % 2 == 0), qts[j][head // 2], 0.0).astype(BF16) for j, head in probs]
    table = [jnp.where(n > 0, 1, 0)] + [1] * (NQ - 1)
    ss = [_bdot_nt(qms[i], kdup[kv_of(head)][keys(j)]) + bias_ref[table[j], head]
          for i, (j, head) in enumerate(probs)]
    sinks = [sink_ref[head] * LOG2E for head in range(N_SWA_HEADS)]
    ms = [jnp.maximum(jnp.max(ss[i], axis=-1, keepdims=True), sinks[head]) for i, (j, head) in enumerate(probs)]
    ps = [jnp.exp2(ss[i] - ms[i]) for i in range(len(probs))]
    dens = [jnp.sum(ps[i], axis=-1, keepdims=True) + jnp.exp2(sinks[head] - ms[i])
            for i, (j, head) in enumerate(probs)]
    outs = [_bdot(ps[i], vdup[kv_of(head)][keys(j)]) / dens[i] for i, (j, head) in enumerate(probs)]
    for j in range(NQ):
        for t in range(SWA_Q_COLS // LANES):
            o_ref[0, j * W:(j + 1) * W, t * LANES:(t + 1) * LANES] = jnp.where(
                fq, outs[j * N_SWA_HEADS + 2 * t], outs[j * N_SWA_HEADS + 2 * t + 1]).astype(o_ref.dtype)


def _swa_prompt(z3, q_g, k_g, sinks):
    B, S, _ = z3.shape
    W = WINDOW
    NQ = SWA_BLOCKS
    twice = lambda g: jnp.concatenate([g, g]).reshape(1, LANES)
    kcol, vcol = Z_K // LANES, Z_V // LANES
    prev = lambda col: pl.BlockSpec((1, W, LANES), lambda b, n: (b, jnp.maximum(NQ * n - 1, 0), col))
    grid_spec = pltpu.PrefetchScalarGridSpec(
        num_scalar_prefetch=0,
        grid=(B, S // (NQ * W)),
        in_specs=[pl.BlockSpec(memory_space=pltpu.SMEM),
                  pl.BlockSpec((1, NQ * W, SWA_Q_COLS), lambda b, n: (b, n, 0)),
                  pl.BlockSpec((1, NQ * W, LANES), lambda b, n: (b, n, kcol)),
                  prev(kcol),
                  pl.BlockSpec((1, NQ * W, LANES), lambda b, n: (b, n, vcol)),
                  prev(vcol),
                  pl.BlockSpec((1, LANES), lambda b, n: (0, 0)),
                  pl.BlockSpec((1, LANES), lambda b, n: (0, 0))],
        out_specs=[pl.BlockSpec((1, NQ * W, SWA_Q_COLS), lambda b, n: (b, n, 0)),
                   pl.BlockSpec((1, W, LANES), lambda b, n: (b, 0, 0))],
        scratch_shapes=[pltpu.VMEM((2, N_SWA_HEADS, W, 2 * W), F32)],
    )
    return pl.pallas_call(
        _swa_prompt_kernel,
        grid_spec=grid_spec,
        out_shape=[jax.ShapeDtypeStruct((B, S, SWA_Q_COLS), BF16),
                   jax.ShapeDtypeStruct((B, W, LANES), F32)],
        compiler_params=_cparams("arbitrary", "arbitrary"),
        name="swa_prompt",
    )(sinks, z3, z3, z3, z3, z3, twice(q_g), twice(k_g))


def _mem_kv_kernel(m_ref, g_ref, w_ref, kg_ref, k_ref, v_ref):
    n = _rms_rows(m_ref[...], g_ref[...])
    kv = jnp.dot(n.astype(BF16), w_ref[...], preferred_element_type=F32)
    kg = kg_ref[...]
    for t in range(MEM_Q_COLS // LANES):
        k_ref[:, t * LANES:(t + 1) * LANES] = _pair_rms(kv[:, t * LANES:(t + 1) * LANES], kg)
    v_ref[...] = kv[:, MEM_Q_COLS:]


def _mem_kv(mem2d, ln_g, w_kv, k_g):
    r, d = mem2d.shape
    twice = jnp.concatenate([k_g, k_g]).reshape(1, LANES)
    full = lambda shape: pl.BlockSpec(shape, lambda i: (0,) * len(shape))
    return pl.pallas_call(
        _mem_kv_kernel,
        grid=(1,),
        in_specs=[full((r, d)), full((1, d)), full((d, 2 * MEM_Q_COLS)), full((1, LANES))],
        out_specs=[full((r, MEM_Q_COLS)), full((r, MEM_Q_COLS))],
        out_shape=[jax.ShapeDtypeStruct((r, MEM_Q_COLS), F32), jax.ShapeDtypeStruct((r, MEM_Q_COLS), F32)],
        compiler_params=_cparams("arbitrary"),
        name="mem_kv",
    )(mem2d, ln_g.reshape(1, d), w_kv.astype(BF16), twice)


def _mem_attn_kernel(q_ref, k_ref, v_ref, qg_ref, o_ref):
    qg = qg_ref[...]
    rows = q_ref.shape[1]
    fq = _first_half((rows, LANES))
    heads = range(N_MEM_HEADS)
    tile = lambda t: slice(t * LANES, (t + 1) * LANES)
    qts = [_pair_rms(q_ref[0, :, tile(t)], qg) * (ATTN_SCALE * LOG2E) for t in range(MEM_Q_COLS // LANES)]
    kts = [k_ref[0, :, tile(t)].astype(BF16) for t in range(MEM_Q_COLS // LANES)]
    vts = [v_ref[0, :, tile(t)].astype(BF16) for t in range(MEM_Q_COLS // LANES)]
    ss = [_bdot_nt(jnp.where(fq == (h % 2 == 0), qts[h // 2], 0.0), kts[h // 2]) for h in heads]
    ps = [jnp.exp2(s - jnp.max(s, axis=-1, keepdims=True)) for s in ss]
    outs = [_bdot(ps[h], vts[h // 2]) / jnp.sum(ps[h], axis=-1, keepdims=True) for h in heads]
    for t in range(MEM_Q_COLS // LANES):
        o_ref[0, :, tile(t)] = jnp.where(fq, outs[2 * t], outs[2 * t + 1]).astype(o_ref.dtype)


MEM_Q_TILE = 512


def _mem_attn_prompt(z3, mem_k, mem_v, q_g):
    B, S, _ = z3.shape
    M = mem_k.shape[1]
    tq = MEM_Q_TILE
    twice = jnp.concatenate([q_g, q_g]).reshape(1, LANES)
    return pl.pallas_call(
        _mem_attn_kernel,
        grid=(B, S // tq),
        in_specs=[pl.BlockSpec((1, tq, MEM_Q_COLS), lambda b, i: (b, i, Z_QM // MEM_Q_COLS)),
                  pl.BlockSpec((1, M, MEM_Q_COLS), lambda b, i: (b, 0, 0)),
                  pl.BlockSpec((1, M, MEM_Q_COLS), lambda b, i: (b, 0, 0)),
                  pl.BlockSpec((1, LANES), lambda b, i: (0, 0))],
        out_specs=pl.BlockSpec((1, tq, MEM_Q_COLS), lambda b, i: (b, i, 0)),
        out_shape=jax.ShapeDtypeStruct((B, S, MEM_Q_COLS), BF16),
        compiler_params=_cparams("parallel", "parallel"),
        name="mem_attn_prompt",
    )(z3, mem_k, mem_v, twice)


PAIR = 2


def _swa_sample_kernel(sink_ref, q_ref, k_ref, v_ref, ck_ref, cv_ref, qg_ref, kg_ref, o_ref, nk_ref, nv_ref, *, L):
    n_seq = ck_ref.shape[0]
    Wb = ck_ref.shape[2]
    rows8 = SUBLANES
    nh = N_SWA_HEADS
    kn = _pair_rms(k_ref[...], kg_ref[...])
    qg = qg_ref[...]
    R = nh * rows8
    row = lax.broadcasted_iota(I32, (R, 1), 0)
    head = row // rows8
    seq_in_pair = (row % rows8) // L
    step = (row % L).astype(F32)
    slope = jnp.exp2(-(8.0 / N_SWA_HEADS) * (head.astype(F32) + 1.0))
    sink = jnp.zeros((R, 1), F32)
    for h in range(nh):
        sink = jnp.where(head == h, sink_ref[h], sink)
    key = lax.broadcasted_iota(I32, (R, Wb), 1).astype(F32)
    dist_c = float(Wb) + step - key
    bias_c = jnp.where(dist_c < float(WINDOW), 0.0, -jnp.inf)
    col = lax.broadcasted_iota(I32, (R, rows8), 1)
    dist_n = step - (col % L).astype(F32)
    bias_n = jnp.where((dist_n >= 0.0) & ((col // L) == seq_in_pair), 0.0, -jnp.inf)
    fh8 = _first_half((rows8, LANES))
    fhR = _first_half((R, LANES))
    kv_first = head < SWA_GROUP
    pairs = range(n_seq // PAIR)
    bias_c = bias_c - slope * dist_c
    bias_n = bias_n - slope * dist_n

    def stacked_queries(pr):
        r0 = pr * rows8
        pieces = []
        for t in range(SWA_Q_COLS // LANES):
            qt = _pair_rms(q_ref[r0:r0 + rows8, t * LANES:(t + 1) * LANES], qg) * ATTN_SCALE
            qr = pltpu.roll(qt, HEAD_DIM, 1)
            kv = t // (SWA_GROUP // 2)
            for half in range(2):
                src = qt if half == kv else qr
                pieces.append(jnp.where(fh8 == (kv == 0), src, 0.0))
        return jnp.concatenate(pieces, axis=0).astype(BF16)

    qs = [stacked_queries(pr) for pr in pairs]
    s_c = [jnp.where(seq_in_pair == 0, _bdot(qs[pr], ck_ref[pr * PAIR]), _bdot(qs[pr], ck_ref[pr * PAIR + 1]))
           + bias_c for pr in pairs]
    s_n = [_bdot_nt(qs[pr], kn[pr * rows8:(pr + 1) * rows8]) + bias_n for pr in pairs]
    m = [jnp.maximum(jnp.maximum(jnp.max(s_c[pr], axis=-1, keepdims=True),
                                 jnp.max(s_n[pr], axis=-1, keepdims=True)), sink) for pr in pairs]
    p_c = [jnp.exp(s_c[pr] - m[pr]) for pr in pairs]
    p_n = [jnp.exp(s_n[pr] - m[pr]) for pr in pairs]
    den = [jnp.sum(p_c[pr], axis=-1, keepdims=True) + jnp.sum(p_n[pr], axis=-1, keepdims=True)
           + jnp.exp(sink - m[pr]) for pr in pairs]
    outs = [(_bdot(p_n[pr], v_ref[pr * rows8:(pr + 1) * rows8, :])
             + _bdot_nt(jnp.where(seq_in_pair == 0, p_c[pr], 0.0), cv_ref[pr * PAIR])
             + _bdot_nt(jnp.where(seq_in_pair == 1, p_c[pr], 0.0), cv_ref[pr * PAIR + 1])) / den[pr] for pr in pairs]

    pos_r = lax.broadcasted_iota(I32, (Wb, rows8), 0)
    new_c = lax.broadcasted_iota(I32, (Wb, rows8), 1)
    tail = lax.broadcasted_iota(I32, (SWA_KV_COLS, Wb), 1) >= Wb - L

    def shifted(old, new8, j):
        place = jnp.where((pos_r == Wb - L + new_c % L) & (new_c // L == j), 1.0, 0.0).astype(BF16)
        rows_at_tail = sum(jnp.dot(place, part, preferred_element_type=F32) for part in _split3(new8))
        return jnp.where(tail, rows_at_tail.T, pltpu.roll(old, Wb - L, 1))

    for pr in pairs:
        for j in range(PAIR):
            s = pr * PAIR + j
            nk_ref[s] = shifted(ck_ref[s], kn[pr * rows8:(pr + 1) * rows8], j)
            nv_ref[s] = shifted(cv_ref[s], v_ref[pr * rows8:(pr + 1) * rows8, :], j)
    for pr in pairs:
        r0 = pr * rows8
        o = jnp.where(fhR == kv_first, outs[pr], 0.0)
        o_r = pltpu.roll(o, HEAD_DIM, 1)
        for t in range(SWA_Q_COLS // LANES):
            kv = t // (SWA_GROUP // 2)
            halves = []
            for half in range(2):
                h = 2 * t + half
                src = o if half == kv else o_r
                halves.append(src[h * rows8:(h + 1) * rows8])
            o_ref[r0:r0 + rows8, t * LANES:(t + 1) * LANES] = jnp.where(fh8, halves[0], halves[1])


SAMPLE_SEQS = 8


def _swa_sample(z_s, cache_k, cache_v, q_g, k_g, sinks, L):
    t = z_s.shape[0]
    DB, Wb, KV, HD = cache_k.shape
    ns = SAMPLE_SEQS
    rows = ns * L
    twice = lambda g: jnp.concatenate([g, g]).reshape(1, LANES)
    fm = lambda c: jnp.transpose(c, (0, 2, 3, 1)).reshape(DB, KV * HD, Wb)
    back = lambda c: jnp.transpose(c.reshape(DB, KV, HD, Wb), (0, 3, 1, 2))
    cache = pl.BlockSpec((ns, KV * HD, Wb), lambda i: (i, 0, 0))
    o, nk, nv = pl.pallas_call(
        functools.partial(_swa_sample_kernel, L=L),
        grid=(DB // ns,),
        in_specs=[pl.BlockSpec(memory_space=pltpu.SMEM),
                  pl.BlockSpec((rows, SWA_Q_COLS), lambda i: (i, 0)),
                  pl.BlockSpec((rows, LANES), lambda i: (i, Z_K // LANES)),
                  pl.BlockSpec((rows, LANES), lambda i: (i, Z_V // LANES)),
                  cache, cache,
                  pl.BlockSpec((1, LANES), lambda i: (0, 0)),
                  pl.BlockSpec((1, LANES), lambda i: (0, 0))],
        out_specs=[pl.BlockSpec((rows, SWA_Q_COLS), lambda i: (i, 0)), cache, cache],
        out_shape=[jax.ShapeDtypeStruct((t, SWA_Q_COLS), F32),
                   jax.ShapeDtypeStruct((DB, KV * HD, Wb), F32),
                   jax.ShapeDtypeStruct((DB, KV * HD, Wb), F32)],
        compiler_params=_cparams("parallel"),
        name="swa_sample",
    )(sinks, z_s, z_s, z_s, fm(cache_k), fm(cache_v), twice(q_g), twice(k_g))
    return o, back(nk), back(nv)


def _mem_sample_kernel(q_ref, mk_ref, mv_ref, qg_ref, o_ref, *, L):
    n_seq = mk_ref.shape[0]
    rows8 = SUBLANES
    nh = N_MEM_HEADS
    qg = qg_ref[...]
    R = nh * rows8
    row = lax.broadcasted_iota(I32, (R, 1), 0)
    seq_in_pair = (row % rows8) // L
    lane_head8 = lax.broadcasted_iota(I32, (rows8, MEM_Q_COLS), 1) // HEAD_DIM
    pairs = range(n_seq // PAIR)

    def stacked_queries(pr):
        qn = jnp.concatenate([_pair_rms(q_ref[pr * rows8:(pr + 1) * rows8, t * LANES:(t + 1) * LANES], qg)
                              for t in range(MEM_Q_COLS // LANES)], axis=-1) * ATTN_SCALE
        return jnp.concatenate([jnp.where(lane_head8 == h, qn, 0.0) for h in range(nh)], axis=0).astype(BF16)

    qs = [stacked_queries(pr) for pr in pairs]
    ss = [jnp.where(seq_in_pair == 0, _bdot(qs[pr], mk_ref[pr * PAIR]), _bdot(qs[pr], mk_ref[pr * PAIR + 1]))
          for pr in pairs]
    ps = [jnp.exp(s - jnp.max(s, axis=-1, keepdims=True)) for s in ss]
    outs = [(_bdot_nt(jnp.where(seq_in_pair == 0, ps[pr], 0.0), mv_ref[pr * PAIR])
             + _bdot_nt(jnp.where(seq_in_pair == 1, ps[pr], 0.0), mv_ref[pr * PAIR + 1]))
            / jnp.sum(ps[pr], axis=-1, keepdims=True) for pr in pairs]
    for pr in pairs:
        o = jnp.zeros((rows8, MEM_Q_COLS), F32)
        for h in range(nh):
            o = jnp.where(lane_head8 == h, outs[pr][h * rows8:(h + 1) * rows8], o)
        o_ref[pr * rows8:(pr + 1) * rows8, :] = o


def _mem_attn_sample(z_s, mem_k, mem_v, q_g, L):
    t = z_s.shape[0]
    DB, M, H, HD = mem_k.shape
    ns = SAMPLE_SEQS
    rows = ns * L
    twice = jnp.concatenate([q_g, q_g]).reshape(1, LANES)
    fm = lambda c: jnp.transpose(c, (0, 2, 3, 1)).reshape(DB, H * HD, M)
    cache = pl.BlockSpec((ns, H * HD, M), lambda i: (i, 0, 0))
    return pl.pallas_call(
        functools.partial(_mem_sample_kernel, L=L),
        grid=(DB // ns,),
        in_specs=[pl.BlockSpec((rows, MEM_Q_COLS), lambda i: (i, Z_QM // MEM_Q_COLS)),
                  cache, cache,
                  pl.BlockSpec((1, LANES), lambda i: (0, 0))],
        out_specs=pl.BlockSpec((rows, MEM_Q_COLS), lambda i: (i, 0)),
        out_shape=jax.ShapeDtypeStruct((t, MEM_Q_COLS), F32),
        compiler_params=_cparams("parallel"),
        name="mem_attn_sample",
    )(z_s, fm(mem_k), fm(mem_v), twice)


def _gdn_sample_kernel(uq_ref, uk_ref, uv_ref, bq_ref, bk_ref, bv_ref, wq_ref, wk_ref, wv_ref,
                       ab_ref, gate_ref, alog_ref, dtb_ref, ng_ref, s_in_ref, o_ref, s_ref, kq_ref):
    h = pl.program_id(0)
    L = uq_ref.shape[0]
    nbuf = bq_ref.shape[0]
    DK = GDN_DK

    def conv(u_ref, b_ref, w_ref, t):
        up = [b_ref[i] for i in range(nbuf)] + [u_ref[i] for i in range(L)]
        y = up[t] * w_ref[0]
        for i in range(1, GDN_CONV):
            y = y + up[t + i] * w_ref[i]
        return y * jax.nn.sigmoid(y)

    s_ref[...] = s_in_ref[...]
    ng = ng_ref[...]
    hsel = lax.broadcasted_iota(I32, (SUBLANES, 1), 0)
    pick = lambda m, r: jnp.sum(jnp.where(hsel == r, m, 0.0), axis=0, keepdims=True)
    alog = pick(alog_ref[...], h)
    dtb = pick(dtb_ref[...], h)
    for t in range(L):
        q = conv(uq_ref, bq_ref, wq_ref, t)
        k = conv(uk_ref, bk_ref, wk_ref, t)
        v = conv(uv_ref, bv_ref, wv_ref, t)
        q = q * lax.rsqrt(jnp.sum(q * q, axis=0, keepdims=True) + EPS) * (GDN_DK ** -0.5)
        k = k * lax.rsqrt(jnp.sum(k * k, axis=0, keepdims=True) + EPS)
        ab = ab_ref[t]
        a = pick(ab, h)
        bb = pick(ab, h + N_GDN_HEADS)
        decay = jnp.exp(-jnp.exp(alog) * _softplus(a + dtb))
        beta = jax.nn.sigmoid(bb)
        kq_ref[0] = k
        kq_ref[1] = q

        def decay_and_project(dk, acc):
            s = s_ref[0, dk] * decay
            s_ref[0, dk] = s
            return acc + s * kq_ref[0, pl.ds(dk, 1), :]

        sk = lax.fori_loop(0, DK, decay_and_project, jnp.zeros_like(v), unroll=8)
        u = beta * (v - sk)

        def update_and_read(dk, acc):
            s = s_ref[0, dk] + kq_ref[0, pl.ds(dk, 1), :] * u
            s_ref[0, dk] = s
            return acc + s * kq_ref[1, pl.ds(dk, 1), :]

        o = lax.fori_loop(0, DK, update_and_read, jnp.zeros_like(v), unroll=8)
        o = o * lax.rsqrt(jnp.mean(o * o, axis=0, keepdims=True) + EPS) * ng
        g = gate_ref[t]
        o_ref[t] = o * (g * jax.nn.sigmoid(g))


def _gdn_sample(z_s, conv_buf, state, conv_w, a_log, dt_bias, norm_g, DB, L):
    H = N_GDN_HEADS
    z3 = z_s.reshape(DB, L, Z_COLS)
    u_t = jnp.transpose(z3[:, :, Z_GDN:Z_GATE], (1, 2, 0))
    gate_t = jnp.transpose(z3[:, :, Z_GATE:Z_QM], (1, 2, 0))
    ab_t = jnp.transpose(z3[:, :, Z_AB:Z_AB + SUBLANES], (1, 2, 0))
    buf_t = jnp.transpose(conv_buf, (1, 2, 0))
    s_t = jnp.transpose(state, (1, 2, 3, 0))
    w_col = conv_w.reshape(GDN_CONV, GDN_CONV_CH, 1)
    col8 = lambda a: jnp.pad(a, (0, SUBLANES - a.shape[0])).reshape(SUBLANES, 1)
    nbuf = conv_buf.shape[1]
    part = lambda n, j: pl.BlockSpec((n, GDN_DK, DB), lambda h: (0, j * H + h, 0))
    wpart = lambda j: pl.BlockSpec((GDN_CONV, GDN_DK, 1), lambda h: (0, j * H + h, 0))
    whole = lambda shape: pl.BlockSpec(shape, lambda h: (0,) * len(shape))
    o_t, s_new = pl.pallas_call(
        _gdn_sample_kernel,
        grid=(H,),
        in_specs=[part(L, 0), part(L, 1), part(L, 2), part(nbuf, 0), part(nbuf, 1), part(nbuf, 2),
                  wpart(0), wpart(1), wpart(2),
                  whole((L, SUBLANES, DB)),
                  pl.BlockSpec((L, GDN_DV, DB), lambda h: (0, h, 0)),
                  whole((SUBLANES, 1)), whole((SUBLANES, 1)), whole((GDN_DV, 1)),
                  pl.BlockSpec((1, GDN_DK, GDN_DV, DB), lambda h: (h, 0, 0, 0))],
        out_specs=[pl.BlockSpec((L, GDN_DV, DB), lambda h: (0, h, 0)),
                   pl.BlockSpec((1, GDN_DK, GDN_DV, DB), lambda h: (h, 0, 0, 0))],
        out_shape=[jax.ShapeDtypeStruct((L, H * GDN_DV, DB), F32),
                   jax.ShapeDtypeStruct((H, GDN_DK, GDN_DV, DB), F32)],
        scratch_shapes=[pltpu.VMEM((2, GDN_DK, DB), F32)],
        compiler_params=_cparams("parallel"),
        name="gdn_sample",
    )(u_t, u_t, u_t, buf_t, buf_t, buf_t, w_col, w_col, w_col, ab_t, gate_t,
      col8(a_log), col8(dt_bias), norm_g.reshape(GDN_DV, 1), s_t)
    o = jnp.transpose(o_t, (2, 0, 1)).reshape(DB * L, H * GDN_DV)
    return o, jnp.transpose(s_new, (3, 0, 1, 2))


def kernel(x_prompt, x_sample, cache_swa_k, cache_swa_v, state_gdn, state_gdn_conv, cache_mem_k, cache_mem_v,
           mem_prompt, ln1_g, w_in, swa_q_norm, swa_k_norm, swa_sinks, gdn_conv_w, gdn_a_log, gdn_dt_bias,
           gdn_norm_g, mem_ln_g, w_mem_kv, mem_q_norm, mem_k_norm, w_o, ln2_g, router_w, router_b,
           moe_w1, moe_b1, moe_w2, moe_b2):
    B, S, D = x_prompt.shape
    DB, DL, _ = x_sample.shape
    depth = ln1_g.shape[0]
    assert depth == 1
    l = 0
    tp, ts = B * S, DB * DL
    t_all = tp + ts
    n_ab = 2 * N_GDN_HEADS
    c_ab = SWA_Q_COLS + 2 * SWA_KV_COLS + GDN_CONV_CH
    w = w_in[l]
    w_z = jnp.concatenate([w[:, :c_ab], w[:, c_ab + n_ab:], w[:, c_ab:c_ab + n_ab],
                           jnp.zeros((D, LANES - n_ab), F32)], axis=1).astype(BF16)
    rw = jnp.pad(router_w[l], ((0, 0), (0, LANES - N_EXPERTS)))
    rw_hi = rw.astype(BF16)
    rw_lo = (rw - rw_hi.astype(F32)).astype(BF16)
    rb = jnp.pad(router_b[l], (0, LANES - N_EXPERTS)).reshape(1, LANES)
    wo = w_o[l].astype(BF16)
    w1 = moe_w1[l]
    w2 = moe_w2[l]
    b1 = moe_b1[l].reshape(N_EXPERTS, 1, -1)
    b2 = moe_b2[l].reshape(N_EXPERTS, 1, -1)
    p = {'q_norm': swa_q_norm[l], 'k_norm': swa_k_norm[l], 'sinks': swa_sinks[l], 'conv_w': gdn_conv_w[l],
         'a_log': gdn_a_log[l], 'dt_bias': gdn_dt_bias[l], 'gdn_norm': gdn_norm_g[l], 'mem_q_norm': mem_q_norm[l]}

    xp = x_prompt.reshape(tp, D)
    xs = x_sample.reshape(ts, D)
    z_p = _inproj(xp, ln1_g[l], w_z)
    z_s = _inproj(xs, ln1_g[l], w_z)

    M = mem_prompt.shape[1]
    z_p3 = z_p.reshape(B, S, Z_COLS)
    mk2, mv2 = _mem_kv(mem_prompt.reshape(B * M, D), mem_ln_g[l], w_mem_kv[l], mem_k_norm[l])
    mk = mk2.reshape(B, M, N_MEM_HEADS, HEAD_DIM)
    mv = mv2.reshape(B, M, N_MEM_HEADS, HEAD_DIM)
    os_p, pk = _swa_prompt(z_p3, p['q_norm'], p['k_norm'], p['sinks'])
    od_p, ps = _gdn_prompt(z_p3, p['conv_w'], p['a_log'], p['dt_bias'], p['gdn_norm'])
    om_p = _mem_attn_prompt(z_p3, mk2.reshape(B, M, MEM_Q_COLS), mv2.reshape(B, M, MEM_Q_COLS), p['mem_q_norm'])
    os_p, od_p, om_p = os_p.reshape(tp, -1), od_p.reshape(tp, -1), om_p.reshape(tp, -1)
    pk = pk.reshape(B, WINDOW, N_SWA_KV, HEAD_DIM)
    pv = z_p3[:, S - WINDOW:, Z_V:Z_GDN].reshape(B, WINDOW, N_SWA_KV, HEAD_DIM)
    pc = z_p3[:, S - (GDN_CONV - 1):, Z_GDN:Z_GATE]
    os_s, sk, sv = _swa_sample(z_s, cache_swa_k[l], cache_swa_v[l], p['q_norm'], p['k_norm'], p['sinks'], DL)
    od_s, ss = _gdn_sample(z_s, state_gdn_conv[l], state_gdn[l], p['conv_w'], p['a_log'], p['dt_bias'],
                           p['gdn_norm'], DB, DL)
    om_s = _mem_attn_sample(z_s, cache_mem_k[l], cache_mem_v[l], p['mem_q_norm'], DL)
    z_s3 = z_s.reshape(DB, DL, Z_COLS)
    sc = jnp.concatenate([state_gdn_conv[l], z_s3[:, :, Z_GDN:Z_GATE]], axis=1)[:, DL:]

    h_p, hn_all, lg_all = _outproj(xp, os_p, od_p, om_p, wo, ln2_g[l], rw_hi, rw_lo, rb, t_all, 0)
    h_s, hn_all, lg_all = _outproj(xs, os_s, od_s, om_s, wo, ln2_g[l], rw_hi, rw_lo, rb, t_all, tp,
                                   prev=(hn_all, lg_all))
    y_p, y_s = _moe(hn_all, lg_all, [h_p, h_s], w1, b1, w2, b2)
    return (y_p.reshape(B, S, D), y_s.reshape(DB, DL, D), pk[None], pv[None], ps[None], pc[None], mk[None],
            mv[None], sk[None], sv[None], ss[None], sc[None])
```

```python
import functools

import jax
import jax.numpy as jnp
from jax import lax
from jax.experimental import pallas as pl
from jax.experimental.pallas import tpu as pltpu

F32 = jnp.float32
BF16 = jnp.bfloat16
I32 = jnp.int32

HEAD_DIM = 64
N_SWA_HEADS = 8
N_SWA_KV = 2
SWA_GROUP = N_SWA_HEADS // N_SWA_KV
WINDOW = 128
N_GDN_HEADS = 4
GDN_DK = 64
GDN_DV = 64
GDN_CONV = 4
GDN_CHUNK = 64
N_MEM_HEADS = 4
N_EXPERTS = 32
TOP_K = 4
SWIGLU_ALPHA = 1.702
SWIGLU_LIMIT = 7.0
EPS = 1e-6
ATTN_SCALE = HEAD_DIM ** -0.5

SWA_Q_COLS = N_SWA_HEADS * HEAD_DIM
SWA_KV_COLS = N_SWA_KV * HEAD_DIM
GDN_QK_COLS = N_GDN_HEADS * GDN_DK
GDN_V_COLS = N_GDN_HEADS * GDN_DV
GDN_CONV_CH = 2 * GDN_QK_COLS + GDN_V_COLS
MEM_Q_COLS = N_MEM_HEADS * HEAD_DIM

LANES = 128
SUBLANES = 8
BF16_EXACT_INT = 256.0
VMEM_LIMIT = 56 * 1024 * 1024

Z_Q = 0
Z_K = Z_Q + SWA_Q_COLS
Z_V = Z_K + SWA_KV_COLS
Z_GDN = Z_V + SWA_KV_COLS
Z_GATE = Z_GDN + GDN_CONV_CH
Z_QM = Z_GATE + GDN_V_COLS
Z_AB = Z_QM + MEM_Q_COLS
Z_COLS = Z_AB + LANES

INPROJ_TILE = 1024
ROW_TILE = 512
MOE_TILE = 512
MOE_BLK = 512
PERM_CHUNK = 256
RUN_ALIGN = 16
BLK_ROWS = -(-(MOE_BLK * TOP_K + N_EXPERTS * (RUN_ALIGN - 1)) // PERM_CHUNK) * PERM_CHUNK


def _cparams(*sem):
    return pltpu.CompilerParams(dimension_semantics=sem, vmem_limit_bytes=VMEM_LIMIT)


def _bdot(a, b):
    return jnp.dot(a.astype(BF16), b.astype(BF16), preferred_element_type=F32)


def _bdot_nt(a, b):
    return lax.dot_general(a.astype(BF16), b.astype(BF16), (((1,), (1,)), ((), ())),
                           preferred_element_type=F32)


def _split2(x):
    hi = x.astype(BF16)
    lo = (x - hi.astype(F32)).astype(BF16)
    return hi, lo


def _split3(x):
    hi = x.astype(BF16)
    r = x - hi.astype(F32)
    mid = r.astype(BF16)
    lo = (r - mid.astype(F32)).astype(BF16)
    return hi, mid, lo


def _rms_rows(x, g):
    ms = jnp.mean(x * x, axis=-1, keepdims=True)
    return x * lax.rsqrt(ms + EPS) * g


def _inproj_kernel(x_ref, g_ref, w_ref, z_ref):
    n = _rms_rows(x_ref[...], g_ref[...])
    z_ref[...] = jnp.dot(n.astype(BF16), w_ref[...], preferred_element_type=F32)


def _inproj(x2d, ln_g, w_z):
    t, d = x2d.shape
    tm = min(INPROJ_TILE, t)
    return pl.pallas_call(
        _inproj_kernel,
        grid=(t // tm,),
        in_specs=[pl.BlockSpec((tm, d), lambda i: (i, 0)),
                  pl.BlockSpec((1, d), lambda i: (0, 0)),
                  pl.BlockSpec((d, Z_COLS), lambda i: (0, 0))],
        out_specs=pl.BlockSpec((tm, Z_COLS), lambda i: (i, 0)),
        out_shape=jax.ShapeDtypeStruct((t, Z_COLS), F32),
        compiler_params=_cparams("parallel"),
        name="inproj",
    )(x2d, ln_g.reshape(1, d), w_z)


def _outproj_kernel(x_ref, os_ref, od_ref, om_ref, wo_ref, g_ref, rwh_ref, rwl_ref, rb_ref,
                    *refs, n_own):
    h_ref, hn_ref, lg_ref = refs[-3:]
    i = pl.program_id(0)

    @pl.when(i < n_own)
    def _():
        n_s = os_ref.shape[1]
        n_d = od_ref.shape[1]
        h = x_ref[...]
        h = h + jnp.dot(os_ref[...].astype(BF16), wo_ref[0:n_s, :], preferred_element_type=F32)
        h = h + jnp.dot(od_ref[...].astype(BF16), wo_ref[n_s:n_s + n_d, :], preferred_element_type=F32)
        h = h + jnp.dot(om_ref[...].astype(BF16), wo_ref[n_s + n_d:, :], preferred_element_type=F32)
        h_ref[...] = h
        hn = _rms_rows(h, g_ref[...])
        hn_ref[...] = hn.astype(BF16)
        hi, lo = _split2(hn)
        lg = (jnp.dot(hi, rwh_ref[...], preferred_element_type=F32)
              + jnp.dot(lo, rwh_ref[...], preferred_element_type=F32)
              + jnp.dot(hi, rwl_ref[...], preferred_element_type=F32))
        lg_ref[...] = lg + rb_ref[...]

    @pl.when(i >= n_own)
    def _():
        hn_ref[...] = jnp.zeros_like(hn_ref)
        lg_ref[...] = jnp.zeros_like(lg_ref)


def _outproj(x2d, o_s, o_d, o_m, w_o, ln_g, rw_hi, rw_lo, rb, t_all, row0, prev=None):
    t, d = x2d.shape
    tm = min(ROW_TILE, t)
    blk0 = row0 // tm
    n_own = t // tm
    n_steps = n_own if prev is not None else t_all // tm
    row = lambda i: (jnp.minimum(i, n_own - 1), 0)
    row_off = lambda i: (i + blk0, 0)
    const = lambda i: (0, 0)
    in_specs = [pl.BlockSpec((tm, d), row),
                pl.BlockSpec((tm, o_s.shape[1]), row),
                pl.BlockSpec((tm, o_d.shape[1]), row),
                pl.BlockSpec((tm, o_m.shape[1]), row),
                pl.BlockSpec((d, d), const),
                pl.BlockSpec((1, d), const),
                pl.BlockSpec((d, LANES), const),
                pl.BlockSpec((d, LANES), const),
                pl.BlockSpec((1, LANES), const)]
    args = [x2d, o_s, o_d, o_m, w_o, ln_g.reshape(1, d), rw_hi, rw_lo, rb]
    aliases = {}
    if prev is not None:
        in_specs += [pl.BlockSpec(memory_space=pl.ANY), pl.BlockSpec(memory_space=pl.ANY)]
        aliases = {len(args): 1, len(args) + 1: 2}
        args += list(prev)
    return pl.pallas_call(
        functools.partial(_outproj_kernel, n_own=n_own),
        grid=(n_steps,),
        in_specs=in_specs,
        out_specs=[pl.BlockSpec((tm, d), row),
                   pl.BlockSpec((tm, d), row_off),
                   pl.BlockSpec((tm, LANES), row_off)],
        out_shape=[jax.ShapeDtypeStruct((t, d), F32),
                   jax.ShapeDtypeStruct((t_all, d), BF16),
                   jax.ShapeDtypeStruct((t_all, LANES), F32)],
        input_output_aliases=aliases,
        compiler_params=_cparams("arbitrary"),
        name="outproj_router",
    )(*args)


def _route_kernel(lg_ref, pos_ref, post_ref, g_ref, cnt_ref):
    tm = lg_ref.shape[0]
    lane = lax.broadcasted_iota(I32, (tm, LANES), 1).astype(F32)
    l = jnp.where(lane < N_EXPERTS, lg_ref[...], -jnp.inf)
    vals, idxs = [], []
    for _k in range(TOP_K):
        m = jnp.max(l, axis=-1, keepdims=True)
        idx = jnp.min(jnp.where(l == m, lane, float(LANES)), axis=-1, keepdims=True)
        l = jnp.where(lane == idx, -jnp.inf, l)
        vals.append(m)
        idxs.append(idx)
    ex = [jnp.exp(v - vals[0]) for v in vals]
    den = ex[0] + ex[1] + ex[2] + ex[3]
    member = jnp.zeros((tm, LANES), F32)
    for idx in idxs:
        member = member + jnp.where(lane == idx, 1.0, 0.0)
    ri = lax.broadcasted_iota(I32, (tm, tm), 0)
    ci = lax.broadcasted_iota(I32, (tm, tm), 1)
    strict = jnp.where(ci < ri, 1.0, 0.0).astype(BF16)
    prefix = jnp.dot(strict, member.astype(BF16), preferred_element_type=F32)
    cnt = jnp.sum(member, axis=0, keepdims=True)
    cpad = jnp.ceil(cnt * (1.0 / RUN_ALIGN)) * float(RUN_ALIGN)
    c_hi = jnp.floor(cpad * (1.0 / BF16_EXACT_INT))
    c_lo = cpad - BF16_EXACT_INT * c_hi
    ej = lax.broadcasted_iota(I32, (LANES, LANES), 0)
    ee = lax.broadcasted_iota(I32, (LANES, LANES), 1)
    before = jnp.where(ej < ee, 1.0, 0.0).astype(BF16)
    bcast = lambda v: jnp.broadcast_to(v, (SUBLANES, LANES)).astype(BF16)
    off = (BF16_EXACT_INT * jnp.dot(bcast(c_hi), before, preferred_element_type=F32)
           + jnp.dot(bcast(c_lo), before, preferred_element_type=F32))[0:1]
    where_in_run = prefix + off
    p_out = jnp.zeros((tm, LANES), F32)
    g_out = jnp.zeros((tm, LANES), F32)
    for k in range(TOP_K):
        pos = jnp.sum(jnp.where(lane == idxs[k], where_in_run, 0.0), axis=-1, keepdims=True)
        p_out = jnp.where(lane == float(k), pos, p_out)
        g_out = jnp.where(lane == float(k), ex[k] / den, g_out)
    pos_ref[...] = p_out[:, :TOP_K]
    post_ref[...] = p_out.T[:SUBLANES, :]
    g_ref[...] = g_out[:, :TOP_K]
    cnt_ref[0] = cnt


def _route(logits):
    t = logits.shape[0]
    tm = MOE_BLK
    nb = t // tm
    return pl.pallas_call(
        _route_kernel,
        grid=(nb,),
        in_specs=[pl.BlockSpec((tm, LANES), lambda i: (i, 0))],
        out_specs=[pl.BlockSpec((tm, TOP_K), lambda i: (i, 0)),
                   pl.BlockSpec((SUBLANES, tm), lambda i: (0, i)),
                   pl.BlockSpec((tm, TOP_K), lambda i: (i, 0)),
                   pl.BlockSpec((1, 1, LANES), lambda i: (i, 0, 0))],
        out_shape=[jax.ShapeDtypeStruct((t, TOP_K), F32),
                   jax.ShapeDtypeStruct((SUBLANES, t), F32),
                   jax.ShapeDtypeStruct((t, TOP_K), F32),
                   jax.ShapeDtypeStruct((nb, 1, LANES), F32)],
        compiler_params=_cparams("parallel"),
        name="route",
    )(logits)


def _run_copies(n, max_rows, src_ref, src0, dst_ref, dst0, sem, wait):
    pos = 0
    bit = max_rows
    while bit >= RUN_ALIGN:
        take = (n & bit) != 0

        def go(pos=pos, bit=bit):
            cp = pltpu.make_async_copy(src_ref.at[pl.ds(pl.multiple_of(src0 + pos, RUN_ALIGN), bit)],
                                       dst_ref.at[pl.ds(pl.multiple_of(dst0 + pos, RUN_ALIGN), bit)], sem)
            cp.wait() if wait else cp.start()

        pl.when(take)(go)
        pos = pos + jnp.where(take, bit, 0)
        bit //= 2


RUN_SIZES = tuple(MOE_BLK >> i for i in range((MOE_BLK // RUN_ALIGN).bit_length()))


def _piece_copies(b, cnt_ref, loc_ref, glob_ref, local_ref, global_hbm, sem, to_global, wait):
    for c, rows in enumerate(RUN_SIZES):
        base = b * len(RUN_SIZES) + c

        def body(s, carry, rows=rows, base=base):
            j = base * N_EXPERTS + s
            loc = local_ref.at[pl.ds(pl.multiple_of(loc_ref[j], RUN_ALIGN), rows)]
            glob = global_hbm.at[pl.ds(pl.multiple_of(glob_ref[j], RUN_ALIGN), rows)]
            cp = pltpu.make_async_copy(loc, glob, sem) if to_global else pltpu.make_async_copy(glob, loc, sem)
            cp.wait() if wait else cp.start()
            return carry

        lax.fori_loop(0, cnt_ref[base], body, 0)


def _dispatch_kernel(cnt_ref, loc_ref, glob_ref, rows_ref, estart_ref, elen_ref, nused_ref,
                     hn_ref, post_ref, xs_hbm, buf_ref, zero_ref, sem, zsem):
    b = pl.program_id(0)
    nb = pl.num_programs(0)
    slot = b % 2
    tm = hn_ref.shape[0]
    x = hn_ref[...].astype(BF16)
    post = post_ref[...]
    P = PERM_CHUNK

    def sort_rows(c):
        r = (lax.broadcasted_iota(I32, (P, tm), 0) + c * P).astype(F32)
        sel = jnp.zeros((P, tm), F32)
        for k in range(TOP_K):
            sel = jnp.where(r == post[k:k + 1, :], 1.0, sel)
        buf_ref[slot, c * P:(c + 1) * P, :] = jnp.dot(sel.astype(BF16), x,
                                                      preferred_element_type=F32).astype(BF16)

    for c in range(BLK_ROWS // P):
        if c * P < tm * TOP_K:
            sort_rows(c)
        else:
            pl.when(rows_ref[b] > c * P)(functools.partial(sort_rows, c))

    def runs(blk, s, wait):
        _piece_copies(blk, cnt_ref, loc_ref, glob_ref, buf_ref.at[s], xs_hbm, sem.at[s], True, wait)

    runs(b, slot, False)

    @pl.when(b == 0)
    def _():
        zero_ref[...] = jnp.zeros_like(zero_ref)

        def tail(wait):
            def body(e, c):
                n = (MOE_TILE - elen_ref[e] % MOE_TILE) % MOE_TILE
                _run_copies(n, MOE_TILE // 2, zero_ref, 0, xs_hbm, estart_ref[e] + elen_ref[e], zsem, wait)
                return c
            lax.fori_loop(0, N_EXPERTS, body, 0)

            def free_tile(ti, c):
                for half in range(2):
                    _run_copies(MOE_TILE // 2, MOE_TILE // 2, zero_ref, 0, xs_hbm,
                                ti * MOE_TILE + half * (MOE_TILE // 2), zsem, wait)
                return c
            lax.fori_loop(nused_ref[0], xs_hbm.shape[0] // MOE_TILE, free_tile, 0)

        tail(False)
        tail(True)

    pl.when(b > 0)(lambda: runs(b - 1, 1 - slot, True))
    pl.when(b == nb - 1)(lambda: runs(b, slot, True))


def _dispatch(hn, post, piece_cnt, piece_loc, piece_glob, blk_rows, e_start, e_len, n_used, n_rows):
    t, d = hn.shape
    grid_spec = pltpu.PrefetchScalarGridSpec(
        num_scalar_prefetch=7,
        grid=(t // MOE_BLK,),
        in_specs=[pl.BlockSpec((MOE_BLK, d), lambda i, *_: (i, 0)),
                  pl.BlockSpec((SUBLANES, MOE_BLK), lambda i, *_: (0, i))],
        out_specs=pl.BlockSpec(memory_space=pl.ANY),
        scratch_shapes=[pltpu.VMEM((2, BLK_ROWS, d), BF16),
                        pltpu.VMEM((MOE_TILE, d), BF16),
                        pltpu.SemaphoreType.DMA((2,)),
                        pltpu.SemaphoreType.DMA(())],
    )
    return pl.pallas_call(
        _dispatch_kernel,
        grid_spec=grid_spec,
        out_shape=jax.ShapeDtypeStruct((n_rows, d), BF16),
        compiler_params=_cparams("arbitrary"),
        name="dispatch",
    )(piece_cnt, piece_loc, piece_glob, blk_rows, e_start, e_len, n_used, hn, post)


def _expert_kernel(te_ref, nu_ref, nxt_ref, slot_ref, x_ref, w1_hbm, b1_ref, w2_hbm, b2_ref, y_ref,
                   w1f_ref, w2f_ref, w1b_ref, w2b_ref, sem):
    i = pl.program_id(0)
    live = i < nu_ref[0]
    e = te_ref[i]
    s = slot_ref[e]

    def weight_copies(expert, slot):
        return (pltpu.make_async_copy(w1_hbm.at[expert], w1f_ref.at[slot], sem.at[0, slot]),
                pltpu.make_async_copy(w2_hbm.at[expert], w2f_ref.at[slot], sem.at[1, slot]))

    @pl.when(live & (i == 0))
    def _():
        for cp in weight_copies(e, s):
            cp.start()

    @pl.when(live & ((i == 0) | (e != te_ref[jnp.maximum(i - 1, 0)])))
    def _():
        for cp in weight_copies(e, s):
            cp.wait()
        w1b_ref[...] = w1f_ref[s].astype(BF16)
        w2b_ref[...] = w2f_ref[s].astype(BF16)

        @pl.when(nxt_ref[e] >= 0)
        def _():
            for cp in weight_copies(nxt_ref[e], 1 - s):
                cp.start()

    @pl.when(live)
    def _():
        f = w2b_ref.shape[0]
        h = jnp.dot(x_ref[...], w1b_ref[...], preferred_element_type=F32) + b1_ref[0]
        glu = jnp.minimum(h[:, :f], SWIGLU_LIMIT)
        lin = jnp.clip(h[:, f:], -SWIGLU_LIMIT, SWIGLU_LIMIT)
        act = glu * jax.nn.sigmoid(SWIGLU_ALPHA * glu) * (lin + 1.0)
        y = jnp.dot(act.astype(BF16), w2b_ref[...], preferred_element_type=F32) + b2_ref[0]
        y_ref[...] = y.astype(BF16)

    @pl.when(i >= nu_ref[0])
    def _():
        y_ref[...] = jnp.zeros_like(y_ref)


def _experts(xs, tile_expert, n_used, next_expert, expert_slot, w1, b1, w2, b2):
    n_rows, d = xs.shape
    f2 = w1.shape[2]
    f = w2.shape[1]
    n_tiles = n_rows // MOE_TILE
    live = lambda i, te, nu, *_: (jnp.minimum(i, nu[0] - 1), 0)
    every = lambda i, *_: (i, 0)
    wsel = lambda i, te, *_: (te[i], 0, 0)
    grid_spec = pltpu.PrefetchScalarGridSpec(
        num_scalar_prefetch=4,
        grid=(n_tiles,),
        in_specs=[pl.BlockSpec((MOE_TILE, d), live),
                  pl.BlockSpec(memory_space=pl.ANY),
                  pl.BlockSpec((1, 1, f2), wsel),
                  pl.BlockSpec(memory_space=pl.ANY),
                  pl.BlockSpec((1, 1, d), wsel)],
        out_specs=pl.BlockSpec((MOE_TILE, d), every),
        scratch_shapes=[pltpu.VMEM((2, d, f2), F32), pltpu.VMEM((2, f, d), F32),
                        pltpu.VMEM((d, f2), BF16), pltpu.VMEM((f, d), BF16),
                        pltpu.SemaphoreType.DMA((2, 2))],
    )
    return pl.pallas_call(
        _expert_kernel,
        grid_spec=grid_spec,
        out_shape=jax.ShapeDtypeStruct((n_rows, d), BF16),
        compiler_params=_cparams("arbitrary"),
        name="experts",
    )(tile_expert, n_used, next_expert, expert_slot, xs, w1, b1, w2, b2)


def _combine_kernel(cnt_ref, loc_ref, glob_ref, rows_ref, h_ref, pos_ref, g_ref, yb_hbm, y_ref, buf_ref, sem,
                    *, blk0):
    i = pl.program_id(0)
    n_steps = pl.num_programs(0)
    b = i + blk0
    slot = i % 2
    tm, d = h_ref.shape

    def runs(blk, s, wait):
        _piece_copies(blk, cnt_ref, loc_ref, glob_ref, buf_ref.at[s], yb_hbm, sem.at[s], False, wait)

    @pl.when(i == 0)
    def _():
        buf_ref[...] = jnp.zeros_like(buf_ref)
        runs(b, slot, False)

    pl.when(i + 1 < n_steps)(lambda: runs(b + 1, 1 - slot, False))
    runs(b, slot, True)
    pos = pos_ref[...]
    g = g_ref[...]
    P = PERM_CHUNK

    def weighted_rows(c):
        col = (lax.broadcasted_iota(I32, (tm, P), 1) + c * P).astype(F32)
        wgt = jnp.zeros((tm, P), F32)
        for k in range(TOP_K):
            wgt = jnp.where(col == pos[:, k:k + 1], g[:, k:k + 1], wgt)
        return jnp.dot(wgt.astype(BF16), buf_ref[slot, c * P:(c + 1) * P, :], preferred_element_type=F32)

    n_sure = tm * TOP_K // P
    y = h_ref[...]
    for c in range(n_sure):
        y = y + weighted_rows(c)
    y_ref[...] = y
    for c in range(n_sure, BLK_ROWS // P):
        @pl.when(rows_ref[b] > c * P)
        def _(c=c):
            y_ref[...] += weighted_rows(c)


def _combine(h, pos, gates, piece_cnt, piece_loc, piece_glob, blk_rows, yb, blk0):
    t, d = h.shape
    grid_spec = pltpu.PrefetchScalarGridSpec(
        num_scalar_prefetch=4,
        grid=(t // MOE_BLK,),
        in_specs=[pl.BlockSpec((MOE_BLK, d), lambda i, *_: (i, 0)),
                  pl.BlockSpec((MOE_BLK, TOP_K), lambda i, *_: (i + blk0, 0)),
                  pl.BlockSpec((MOE_BLK, TOP_K), lambda i, *_: (i + blk0, 0)),
                  pl.BlockSpec(memory_space=pl.ANY)],
        out_specs=pl.BlockSpec((MOE_BLK, d), lambda i, *_: (i, 0)),
        scratch_shapes=[pltpu.VMEM((2, BLK_ROWS, d), BF16),
                        pltpu.SemaphoreType.DMA((2,))],
    )
    return pl.pallas_call(
        functools.partial(_combine_kernel, blk0=blk0),
        grid_spec=grid_spec,
        out_shape=jax.ShapeDtypeStruct((t, d), F32),
        compiler_params=_cparams("arbitrary"),
        name="combine",
    )(piece_cnt, piece_loc, piece_glob, blk_rows, h, pos, gates, yb)


def _moe(hn_all, logits_all, h_parts, w1, b1, w2, b2):
    t_all = hn_all.shape[0]
    nb = t_all // MOE_BLK
    pos, post, gates, counts_f = _route(logits_all)
    cnt = counts_f.reshape(nb, LANES)[:, :N_EXPERTS].astype(I32)
    seg_len = (cnt + RUN_ALIGN - 1) // RUN_ALIGN * RUN_ALIGN
    before_e = jnp.arange(N_EXPERTS)[:, None] < jnp.arange(N_EXPERTS)[None, :]
    before_b = jnp.arange(nb)[None, :] < jnp.arange(nb)[:, None]
    seg_off = jnp.sum(jnp.where(before_e[None], seg_len[:, :, None], 0), axis=1)
    e_len = jnp.sum(seg_len, axis=0)
    e_tiles = (e_len + MOE_TILE - 1) // MOE_TILE
    tile_start = jnp.sum(jnp.where(before_e, e_tiles[:, None], 0), axis=0)
    tile_end = tile_start + e_tiles
    e_start = tile_start * MOE_TILE
    seg_dst = e_start[None, :] + jnp.sum(jnp.where(before_b[:, :, None], seg_len[None], 0), axis=1)
    max_rows = t_all * TOP_K + nb * N_EXPERTS * (RUN_ALIGN - 1) + N_EXPERTS * (MOE_TILE - RUN_ALIGN)
    n_tiles = -(-max_rows // MOE_TILE)
    n_rows = n_tiles * MOE_TILE
    n_used = tile_end[-1:].astype(I32)
    tile_expert = jnp.minimum(jnp.sum(tile_end[None, :] <= jnp.arange(n_tiles, dtype=I32)[:, None], axis=1),
                              N_EXPERTS - 1).astype(I32)
    sizes = jnp.array(RUN_SIZES, I32)[None, :, None]
    n_run = seg_len[:, None, :]
    has = (n_run & sizes) != 0
    piece_at = n_run & ~(2 * sizes - 1)
    rank = jnp.sum(jnp.where(before_e[None, None], has[:, :, :, None], False), axis=2)
    slot = jnp.arange(N_EXPERTS)
    put = has[..., None] & (rank[..., None] == slot)
    listed = lambda v: jnp.sum(jnp.where(put, v[..., None], 0), axis=2)
    piece_loc = listed(seg_off[:, None, :] + piece_at)
    piece_glob = listed(seg_dst[:, None, :] + piece_at)
    piece_cnt = jnp.sum(has, axis=2)
    flat = lambda a: a.reshape(-1).astype(I32)
    tables = (flat(piece_cnt), flat(piece_loc), flat(piece_glob), flat(jnp.sum(seg_len, axis=1)))
    xs = _dispatch(hn_all, post, *tables, flat(e_start), flat(e_len), n_used, n_rows)
    e_ids = jnp.arange(N_EXPERTS)
    used = e_tiles > 0
    next_expert = jnp.min(jnp.where(before_e & used[None, :], e_ids[None, :], N_EXPERTS), axis=1)
    next_expert = jnp.where(next_expert == N_EXPERTS, -1, next_expert)
    expert_slot = jnp.sum(jnp.where(before_e & used[:, None], 1, 0), axis=0) % 2
    yb = _experts(xs, tile_expert, n_used, flat(next_expert), flat(expert_slot), w1, b1, w2, b2)
    outs = []
    row = 0
    for h in h_parts:
        outs.append(_combine(h, pos, gates, *tables, yb, row // MOE_BLK))
        row += h.shape[0]
    return outs


GDN_ROWS = 4 * GDN_CHUNK
CONV_HALO = SUBLANES
NEUMANN_SPLIT = 2
AB_LANES = 2 * N_GDN_HEADS
SOLVE_ROWS = 2 * GDN_CHUNK


def _softplus(x):
    return jnp.maximum(x, 0.0) + jnp.log1p(jnp.exp(-jnp.abs(x)))


def _gdn_prompt_kernel(u_ref, ab_ref, gate_ref, cw_ref, alog_ref, dtb_ref, ng_ref, o_ref, s_ref, ubuf_ref):
    step = pl.program_id(0)
    NB = u_ref.shape[0]
    R = GDN_ROWS
    C = GDN_CHUNK
    NC = R // C

    @pl.when(step == 0)
    def _():
        ubuf_ref[:, 0:CONV_HALO, :] = jnp.zeros((NB, CONV_HALO, ubuf_ref.shape[2]), F32)
        s_ref[...] = jnp.zeros_like(s_ref)

    ri = lax.broadcasted_iota(I32, (R, R), 0)
    ci = lax.broadcasted_iota(I32, (R, R), 1)
    shift = C.bit_length() - 1
    same = lax.shift_right_logical(ri, shift) == lax.shift_right_logical(ci, shift)
    incl = same & (ci <= ri)
    strict = same & (ci < ri)
    tri = jnp.where(incl, 1.0, 0.0).astype(BF16)
    blk = jnp.where(same, 1.0, 0.0).astype(BF16)
    cw = cw_ref[...]
    ng = ng_ref[...]

    lane = lax.broadcasted_iota(I32, (R, LANES), 1)
    lane1 = lax.broadcasted_iota(I32, (1, LANES), 1)
    ab = ab_ref[0]
    alog = alog_ref[...]
    dtb = dtb_ref[...]
    for b in range(1, NB):
        own = (lane >= b * AB_LANES) & (lane < (b + 1) * AB_LANES)
        own1 = (lane1 >= b * AB_LANES) & (lane1 < (b + 1) * AB_LANES)
        ab = jnp.where(own, pltpu.roll(ab_ref[b], b * AB_LANES, 1), ab)
        alog = jnp.where(own1, pltpu.roll(alog_ref[...], b * AB_LANES, 1), alog)
        dtb = jnp.where(own1, pltpu.roll(dtb_ref[...], b * AB_LANES, 1), dtb)
    g_t = -jnp.exp(alog) * _softplus(ab + dtb)
    beta_t = jax.nn.sigmoid(ab)
    both = jnp.concatenate([tri, blk], axis=0)
    sums = sum(jnp.dot(both, p, preferred_element_type=F32) for p in _split3(g_t))
    gcum, gtot = sums[:R], sums[R:]
    gcum_t = gcum.T

    def conv_and_norms(b):
        u = u_ref[b]
        ubuf_ref[b, CONV_HALO:CONV_HALO + R, :] = u
        y = u * cw[GDN_CONV - 1:GDN_CONV, :]
        for j in range(1, GDN_CONV):
            y = y + ubuf_ref[b, CONV_HALO - j:CONV_HALO - j + R, :] * cw[GDN_CONV - 1 - j:GDN_CONV - j, :]
        ubuf_ref[b, 0:CONV_HALO, :] = u[R - CONV_HALO:, :]
        qkv = y * jax.nn.sigmoid(y)
        qk_n = []
        for t in range(2 * GDN_QK_COLS // LANES):
            x = qkv[:, t * LANES:(t + 1) * LANES]
            x = x * lax.rsqrt(_pair_sumsq(x) + EPS)
            qk_n.append(x * (GDN_DK ** -0.5) if t < GDN_QK_COLS // LANES else x)
        return qkv, jnp.concatenate(qk_n, axis=-1)

    rhs_tiles = {}

    def make_chain(b, h, qkv, qk_n):
        q = qk_n[:, h * GDN_DK:(h + 1) * GDN_DK]
        k = qk_n[:, GDN_QK_COLS + h * GDN_DK:GDN_QK_COLS + (h + 1) * GDN_DK]
        col = b * AB_LANES + h
        gc = gcum[:, col:col + 1]
        gt = gtot[:, col:col + 1]
        beta = beta_t[:, col + N_GDN_HEADS:col + N_GDN_HEADS + 1]
        pair = (b, h // 2)
        if pair not in rhs_tiles:
            c0 = b * AB_LANES + 2 * (h // 2)
            first = _first_half((R, LANES))
            per_lane = lambda m, off: jnp.where(first, m[:, c0 + off:c0 + off + 1], m[:, c0 + off + 1:c0 + off + 2])
            beta_l = per_lane(beta_t, N_GDN_HEADS)
            t0 = (h // 2) * LANES
            gc_l, gt_l = per_lane(gcum, 0), per_lane(gtot, 0)
            e_gc = jnp.exp(gc_l)
            k_tile = qk_n[:, GDN_QK_COLS + t0:GDN_QK_COLS + t0 + LANES]
            vb = qkv[:, 2 * GDN_QK_COLS + t0:2 * GDN_QK_COLS + t0 + LANES] * beta_l
            kb = k_tile * (beta_l * e_gc)
            rhs_tiles[pair] = (jnp.where(first, vb, pltpu.roll(kb, GDN_DK, 1)),
                               jnp.where(first, pltpu.roll(vb, GDN_DV, 1), kb),
                               qk_n[:, t0:t0 + LANES] * e_gc,
                               k_tile * jnp.exp(gt_l - gc_l))
        r = rhs_tiles[pair][h % 2]
        half = slice((h % 2) * GDN_DK, (h % 2 + 1) * GDN_DK)
        q_dec = rhs_tiles[pair][2][:, half]
        k_dec = rhs_tiles[pair][3][:, half]
        a_blocks, qk_blocks, r_blocks = [], [], []
        for sb in range(R // SOLVE_ROWS):
            rows = slice(sb * SOLVE_ROWS, (sb + 1) * SOLVE_ROWS)
            decay = jnp.exp(jnp.where(incl[:SOLVE_ROWS, :SOLVE_ROWS], gc[rows] - gcum_t[col:col + 1, rows], -jnp.inf))
            a_blocks.append(jnp.where(strict[:SOLVE_ROWS, :SOLVE_ROWS],
                                      beta[rows] * _bdot_nt(k[rows], k[rows]) * decay, 0.0))
            qk_blocks.append(_bdot_nt(q[rows], k[rows]) * decay)
            r_blocks.append(r[rows])
        return dict(b=b, h=h, a=a_blocks, qk=qk_blocks, r=r_blocks,
                    q_dec=q_dec, k_dec=k_dec, g_last=jnp.exp(gt))

    dot = lambda x, y: jnp.dot(x, y, preferred_element_type=F32)
    n_levels = C.bit_length() - 1

    def neumann_level(chains, j):
        for ch in chains:
            for sb in range(len(ch['a'])):
                a, r = ch['a'][sb], ch['r'][sb]
                if j < NEUMANN_SPLIT:
                    a_hi, a_lo = _split2(a)
                    r_hi, r_lo = _split2(r)
                    upd = dot(a_hi, r_hi) + dot(a_hi, r_lo) + dot(a_lo, r_hi)
                else:
                    a_hi = a.astype(BF16)
                    upd = dot(a_hi, r.astype(BF16))
                ch['r'][sb] = r - upd if j == 0 else r + upd
                if j + 1 < n_levels:
                    sq = dot(a_hi, a_hi)
                    if j + 1 < NEUMANN_SPLIT:
                        sq = sq + dot(a_hi, a_lo) + dot(a_lo, a_hi)
                    ch['a'][sb] = sq

    def chunk_begin(chains):
        for ch in chains:
            r = jnp.concatenate(ch['r'], axis=0)
            ch['u'], ch['w'] = r[:, :GDN_DV], r[:, GDN_DV:]
            ch['S'] = s_ref[ch['b'], ch['h']]
            ch['k_dec_t'] = ch['k_dec'].T
            ch['outs'] = []

    def chunk_step(chains, c):
        sl = slice(c * C, (c + 1) * C)
        per = SOLVE_ROWS // C
        loc = slice((c % per) * C, (c % per + 1) * C)
        for ch in chains:
            S = ch['S']
            v_new = ch['u'][sl] - _bdot(ch['w'][sl], S)
            ch['outs'].append(_bdot(ch['q_dec'][sl], S) + _bdot(ch['qk'][c // per][loc, loc], v_new))
            ch['S'] = S * ch['g_last'][c * C:c * C + 1, :] + _bdot(ch['k_dec_t'][:, sl], v_new)

    def chunk_end(chains):
        for ch in chains:
            b, h = ch['b'], ch['h']
            s_ref[b, h] = ch['S']
            o = jnp.concatenate(ch['outs'], axis=0)
            o = o * lax.rsqrt(jnp.mean(o * o, axis=-1, keepdims=True) + EPS) * ng
            gh = gate_ref[b, :, h * GDN_DV:(h + 1) * GDN_DV]
            o_ref[b, :, h * GDN_DV:(h + 1) * GDN_DV] = (o * (gh * jax.nn.sigmoid(gh))).astype(o_ref.dtype)

    chains = []
    for b in range(NB):
        qkv, qk_n = conv_and_norms(b)
        chains += [make_chain(b, h, qkv, qk_n) for h in range(N_GDN_HEADS)]
    for j in range(n_levels):
        neumann_level(chains, j)
    chunk_begin(chains)
    for c in range(NC):
        chunk_step(chains, c)
    chunk_end(chains)


def _gdn_prompt(z3, conv_w, a_log, dt_bias, norm_g):
    B, S, _ = z3.shape
    R = GDN_ROWS
    lanes4 = lambda a: jnp.pad(a, (0, LANES - a.shape[0])).reshape(1, LANES)
    return pl.pallas_call(
        _gdn_prompt_kernel,
        grid=(S // R,),
        in_specs=[pl.BlockSpec((B, R, GDN_CONV_CH), lambda s: (0, s, Z_GDN // GDN_CONV_CH)),
                  pl.BlockSpec((B, R, LANES), lambda s: (0, s, Z_AB // LANES)),
                  pl.BlockSpec((B, R, GDN_V_COLS), lambda s: (0, s, Z_GATE // GDN_V_COLS)),
                  pl.BlockSpec((GDN_CONV, GDN_CONV_CH), lambda s: (0, 0)),
                  pl.BlockSpec((1, LANES), lambda s: (0, 0)),
                  pl.BlockSpec((1, LANES), lambda s: (0, 0)),
                  pl.BlockSpec((1, GDN_DV), lambda s: (0, 0))],
        out_specs=[pl.BlockSpec((B, R, GDN_V_COLS), lambda s: (0, s, 0)),
                   pl.BlockSpec((B, N_GDN_HEADS, GDN_DK, GDN_DV), lambda s: (0, 0, 0, 0))],
        out_shape=[jax.ShapeDtypeStruct((B, S, GDN_V_COLS), BF16),
                   jax.ShapeDtypeStruct((B, N_GDN_HEADS, GDN_DK, GDN_DV), F32)],
        scratch_shapes=[pltpu.VMEM((B, CONV_HALO + R, GDN_CONV_CH), F32)],
        compiler_params=_cparams("arbitrary"),
        name="gdn_prompt",
    )(z3, z3, z3, conv_w, lanes4(a_log), lanes4(dt_bias), norm_g.reshape(1, GDN_DV))


def _pair_sumsq(x):
    li = lax.broadcasted_iota(I32, (LANES, LANES), 0) // HEAD_DIM
    lj = lax.broadcasted_iota(I32, (LANES, LANES), 1) // HEAD_DIM
    same = jnp.where(li == lj, 1.0, 0.0).astype(BF16)
    hi, lo = _split2(x * x)
    return jnp.dot(hi, same, preferred_element_type=F32) + jnp.dot(lo, same, preferred_element_type=F32)


def _pair_rms(x, g):
    return x * lax.rsqrt(_pair_sumsq(x) * (1.0 / HEAD_DIM) + EPS) * g


def _first_half(shape):
    return lax.broadcasted_iota(I32, shape, 1) < HEAD_DIM


LOG2E = 1.4426950408889634


SWA_BLOCKS = 4


def _swa_prompt_kernel(sink_ref, q_ref, kc_ref, kp_ref, vc_ref, vp_ref, qg_ref, kg_ref, o_ref, kn_ref,
                       bias_ref):
    first = (pl.program_id(0) == 0) & (pl.program_id(1) == 0)
    n = pl.program_id(1)
    W = WINDOW
    NQ = SWA_BLOCKS

    @pl.when(first)
    def _():
        qi = lax.broadcasted_iota(I32, (W, 2 * W), 0)
        kj = lax.broadcasted_iota(I32, (W, 2 * W), 1)
        dist = qi + W - kj
        band = (dist >= 0) & (dist < W)
        distf = dist.astype(F32)
        for has_prev in range(2):
            mask = jnp.where(band & ((has_prev == 1) | (kj >= W)), 0.0, -jnp.inf)
            for head in range(N_SWA_HEADS):
                slope = 2.0 ** (-(8.0 / N_SWA_HEADS) * (head + 1))
                bias_ref[has_prev, head] = mask - (slope * LOG2E) * distf

    kg = kg_ref[...]
    qg = qg_ref[...]
    kc = _pair_rms(kc_ref[0], kg)
    kn_ref[0] = kc[(NQ - 1) * W:]
    k3 = jnp.concatenate([_pair_rms(kp_ref[0], kg), kc], axis=0)
    v3 = jnp.concatenate([vp_ref[0], vc_ref[0]], axis=0)
    fh = _first_half(k3.shape)
    k3r = pltpu.roll(k3, HEAD_DIM, 1)
    v3r = pltpu.roll(v3, HEAD_DIM, 1)
    kdup = (jnp.where(fh, k3, k3r).astype(BF16), jnp.where(fh, k3r, k3).astype(BF16))
    vdup = (jnp.where(fh, v3, v3r).astype(BF16), jnp.where(fh, v3r, v3).astype(BF16))
    fq = _first_half((W, LANES))
    kv_of = lambda head: head // SWA_GROUP
    probs = [(j, head) for j in range(NQ) for head in range(N_SWA_HEADS)]
    keys = lambda j: slice(j * W, (j + 2) * W)
    qts = [[_pair_rms(q_ref[0, j * W:(j + 1) * W, t * LANES:(t + 1) * LANES], qg) * (ATTN_SCALE * LOG2E)
            for t in range(SWA_Q_COLS // LANES)] for j in range(NQ)]
    qms = [jnp.where(fq == (head % 2 == 0), qts[j][head // 2], 0.0).astype(BF16) for j, head in probs]
    table = [jnp.where(n > 0, 1, 0)] + [1] * (NQ - 1)
    ss = [_bdot_nt(qms[i], kdup[kv_of(head)][keys(j)]) + bias_ref[table[j], head]
          for i, (j, head) in enumerate(probs)]
    sinks = [sink_ref[head] * LOG2E for head in range(N_SWA_HEADS)]
    ms = [jnp.maximum(jnp.max(ss[i], axis=-1, keepdims=True), sinks[head]) for i, (j, head) in enumerate(probs)]
    ps = [jnp.exp2(ss[i] - ms[i]) for i in range(len(probs))]
    dens = [jnp.sum(ps[i], axis=-1, keepdims=True) + jnp.exp2(sinks[head] - ms[i])
            for i, (j, head) in enumerate(probs)]
    outs = [_bdot(ps[i], vdup[kv_of(head)][keys(j)]) / dens[i] for i, (j, head) in enumerate(probs)]
    for j in range(NQ):
        for t in range(SWA_Q_COLS // LANES):
            o_ref[0, j * W:(j + 1) * W, t * LANES:(t + 1) * LANES] = jnp.where(
                fq, outs[j * N_SWA_HEADS + 2 * t], outs[j * N_SWA_HEADS + 2 * t + 1]).astype(o_ref.dtype)


def _swa_prompt(z3, q_g, k_g, sinks):
    B, S, _ = z3.shape
    W = WINDOW
    NQ = SWA_BLOCKS
    twice = lambda g: jnp.concatenate([g, g]).reshape(1, LANES)
    kcol, vcol = Z_K // LANES, Z_V // LANES
    prev = lambda col: pl.BlockSpec((1, W, LANES), lambda b, n: (b, jnp.maximum(NQ * n - 1, 0), col))
    grid_spec = pltpu.PrefetchScalarGridSpec(
        num_scalar_prefetch=0,
        grid=(B, S // (NQ * W)),
        in_specs=[pl.BlockSpec(memory_space=pltpu.SMEM),
                  pl.BlockSpec((1, NQ * W, SWA_Q_COLS), lambda b, n: (b, n, 0)),
                  pl.BlockSpec((1, NQ * W, LANES), lambda b, n: (b, n, kcol)),
                  prev(kcol),
                  pl.BlockSpec((1, NQ * W, LANES), lambda b, n: (b, n, vcol)),
                  prev(vcol),
                  pl.BlockSpec((1, LANES), lambda b, n: (0, 0)),
                  pl.BlockSpec((1, LANES), lambda b, n: (0, 0))],
        out_specs=[pl.BlockSpec((1, NQ * W, SWA_Q_COLS), lambda b, n: (b, n, 0)),
                   pl.BlockSpec((1, W, LANES), lambda b, n: (b, 0, 0))],
        scratch_shapes=[pltpu.VMEM((2, N_SWA_HEADS, W, 2 * W), F32)],
    )
    return pl.pallas_call(
        _swa_prompt_kernel,
        grid_spec=grid_spec,
        out_shape=[jax.ShapeDtypeStruct((B, S, SWA_Q_COLS), BF16),
                   jax.ShapeDtypeStruct((B, W, LANES), F32)],
        compiler_params=_cparams("arbitrary", "arbitrary"),
        name="swa_prompt",
    )(sinks, z3, z3, z3, z3, z3, twice(q_g), twice(k_g))


def _mem_kv_kernel(m_ref, g_ref, w_ref, kg_ref, k_ref, v_ref):
    n = _rms_rows(m_ref[...], g_ref[...])
    kv = jnp.dot(n.astype(BF16), w_ref[...], preferred_element_type=F32)
    kg = kg_ref[...]
    for t in range(MEM_Q_COLS // LANES):
        k_ref[:, t * LANES:(t + 1) * LANES] = _pair_rms(kv[:, t * LANES:(t + 1) * LANES], kg)
    v_ref[...] = kv[:, MEM_Q_COLS:]


def _mem_kv(mem2d, ln_g, w_kv, k_g):
    r, d = mem2d.shape
    twice = jnp.concatenate([k_g, k_g]).reshape(1, LANES)
    full = lambda shape: pl.BlockSpec(shape, lambda i: (0,) * len(shape))
    return pl.pallas_call(
        _mem_kv_kernel,
        grid=(1,),
        in_specs=[full((r, d)), full((1, d)), full((d, 2 * MEM_Q_COLS)), full((1, LANES))],
        out_specs=[full((r, MEM_Q_COLS)), full((r, MEM_Q_COLS))],
        out_shape=[jax.ShapeDtypeStruct((r, MEM_Q_COLS), F32), jax.ShapeDtypeStruct((r, MEM_Q_COLS), F32)],
        compiler_params=_cparams("arbitrary"),
        name="mem_kv",
    )(mem2d, ln_g.reshape(1, d), w_kv.astype(BF16), twice)


def _mem_attn_kernel(q_ref, k_ref, v_ref, qg_ref, o_ref):
    qg = qg_ref[...]
    rows = q_ref.shape[1]
    fq = _first_half((rows, LANES))
    heads = range(N_MEM_HEADS)
    tile = lambda t: slice(t * LANES, (t + 1) * LANES)
    qts = [_pair_rms(q_ref[0, :, tile(t)], qg) * (ATTN_SCALE * LOG2E) for t in range(MEM_Q_COLS // LANES)]
    kts = [k_ref[0, :, tile(t)].astype(BF16) for t in range(MEM_Q_COLS // LANES)]
    vts = [v_ref[0, :, tile(t)].astype(BF16) for t in range(MEM_Q_COLS // LANES)]
    ss = [_bdot_nt(jnp.where(fq == (h % 2 == 0), qts[h // 2], 0.0), kts[h // 2]) for h in heads]
    ps = [jnp.exp2(s - jnp.max(s, axis=-1, keepdims=True)) for s in ss]
    outs = [_bdot(ps[h], vts[h // 2]) / jnp.sum(ps[h], axis=-1, keepdims=True) for h in heads]
    for t in range(MEM_Q_COLS // LANES):
        o_ref[0, :, tile(t)] = jnp.where(fq, outs[2 * t], outs[2 * t + 1]).astype(o_ref.dtype)


MEM_Q_TILE = 512


def _mem_attn_prompt(z3, mem_k, mem_v, q_g):
    B, S, _ = z3.shape
    M = mem_k.shape[1]
    tq = MEM_Q_TILE
    twice = jnp.concatenate([q_g, q_g]).reshape(1, LANES)
    return pl.pallas_call(
        _mem_attn_kernel,
        grid=(B, S // tq),
        in_specs=[pl.BlockSpec((1, tq, MEM_Q_COLS), lambda b, i: (b, i, Z_QM // MEM_Q_COLS)),
                  pl.BlockSpec((1, M, MEM_Q_COLS), lambda b, i: (b, 0, 0)),
                  pl.BlockSpec((1, M, MEM_Q_COLS), lambda b, i: (b, 0, 0)),
                  pl.BlockSpec((1, LANES), lambda b, i: (0, 0))],
        out_specs=pl.BlockSpec((1, tq, MEM_Q_COLS), lambda b, i: (b, i, 0)),
        out_shape=jax.ShapeDtypeStruct((B, S, MEM_Q_COLS), BF16),
        compiler_params=_cparams("parallel", "parallel"),
        name="mem_attn_prompt",
    )(z3, mem_k, mem_v, twice)


PAIR = 2


def _swa_sample_kernel(sink_ref, q_ref, k_ref, v_ref, ck_ref, cv_ref, qg_ref, kg_ref, o_ref, nk_ref, nv_ref, *, L):
    n_seq = ck_ref.shape[0]
    Wb = ck_ref.shape[2]
    rows8 = SUBLANES
    nh = N_SWA_HEADS
    kn = _pair_rms(k_ref[...], kg_ref[...])
    qg = qg_ref[...]
    R = nh * rows8
    row = lax.broadcasted_iota(I32, (R, 1), 0)
    head = row // rows8
    seq_in_pair = (row % rows8) // L
    step = (row % L).astype(F32)
    slope = jnp.exp2(-(8.0 / N_SWA_HEADS) * (head.astype(F32) + 1.0))
    sink = jnp.zeros((R, 1), F32)
    for h in range(nh):
        sink = jnp.where(head == h, sink_ref[h], sink)
    key = lax.broadcasted_iota(I32, (R, Wb), 1).astype(F32)
    dist_c = float(Wb) + step - key
    bias_c = jnp.where(dist_c < float(WINDOW), 0.0, -jnp.inf)
    col = lax.broadcasted_iota(I32, (R, rows8), 1)
    dist_n = step - (col % L).astype(F32)
    bias_n = jnp.where((dist_n >= 0.0) & ((col // L) == seq_in_pair), 0.0, -jnp.inf)
    fh8 = _first_half((rows8, LANES))
    fhR = _first_half((R, LANES))
    kv_first = head < SWA_GROUP
    pairs = range(n_seq // PAIR)
    bias_c = bias_c - slope * dist_c
    bias_n = bias_n - slope * dist_n

    def stacked_queries(pr):
        r0 = pr * rows8
        pieces = []
        for t in range(SWA_Q_COLS // LANES):
            qt = _pair_rms(q_ref[r0:r0 + rows8, t * LANES:(t + 1) * LANES], qg) * ATTN_SCALE
            qr = pltpu.roll(qt, HEAD_DIM, 1)
            kv = t // (SWA_GROUP // 2)
            for half in range(2):
                src = qt if half == kv else qr
                pieces.append(jnp.where(fh8 == (kv == 0), src, 0.0))
        return jnp.concatenate(pieces, axis=0).astype(BF16)

    qs = [stacked_queries(pr) for pr in pairs]
    s_c = [jnp.where(seq_in_pair == 0, _bdot(qs[pr], ck_ref[pr * PAIR]), _bdot(qs[pr], ck_ref[pr * PAIR + 1]))
           + bias_c for pr in pairs]
    s_n = [_bdot_nt(qs[pr], kn[pr * rows8:(pr + 1) * rows8]) + bias_n for pr in pairs]
    m = [jnp.maximum(jnp.maximum(jnp.max(s_c[pr], axis=-1, keepdims=True),
                                 jnp.max(s_n[pr], axis=-1, keepdims=True)), sink) for pr in pairs]
    p_c = [jnp.exp(s_c[pr] - m[pr]) for pr in pairs]
    p_n = [jnp.exp(s_n[pr] - m[pr]) for pr in pairs]
    den = [jnp.sum(p_c[pr], axis=-1, keepdims=True) + jnp.sum(p_n[pr], axis=-1, keepdims=True)
           + jnp.exp(sink - m[pr]) for pr in pairs]
    outs = [(_bdot(p_n[pr], v_ref[pr * rows8:(pr + 1) * rows8, :])
             + _bdot_nt(jnp.where(seq_in_pair == 0, p_c[pr], 0.0), cv_ref[pr * PAIR])
             + _bdot_nt(jnp.where(seq_in_pair == 1, p_c[pr], 0.0), cv_ref[pr * PAIR + 1])) / den[pr] for pr in pairs]

    pos_r = lax.broadcasted_iota(I32, (Wb, rows8), 0)
    new_c = lax.broadcasted_iota(I32, (Wb, rows8), 1)
    tail = lax.broadcasted_iota(I32, (SWA_KV_COLS, Wb), 1) >= Wb - L

    def shifted(old, new8, j):
        place = jnp.where((pos_r == Wb - L + new_c % L) & (new_c // L == j), 1.0, 0.0).astype(BF16)
        rows_at_tail = sum(jnp.dot(place, part, preferred_element_type=F32) for part in _split3(new8))
        return jnp.where(tail, rows_at_tail.T, pltpu.roll(old, Wb - L, 1))

    for pr in pairs:
        for j in range(PAIR):
            s = pr * PAIR + j
            nk_ref[s] = shifted(ck_ref[s], kn[pr * rows8:(pr + 1) * rows8], j)
            nv_ref[s] = shifted(cv_ref[s], v_ref[pr * rows8:(pr + 1) * rows8, :], j)
    for pr in pairs:
        r0 = pr * rows8
        o = jnp.where(fhR == kv_first, outs[pr], 0.0)
        o_r = pltpu.roll(o, HEAD_DIM, 1)
        for t in range(SWA_Q_COLS // LANES):
            kv = t // (SWA_GROUP // 2)
            halves = []
            for half in range(2):
                h = 2 * t + half
                src = o if half == kv else o_r
                halves.append(src[h * rows8:(h + 1) * rows8])
            o_ref[r0:r0 + rows8, t * LANES:(t + 1) * LANES] = jnp.where(fh8, halves[0], halves[1])


SAMPLE_SEQS = 8


def _swa_sample(z_s, cache_k, cache_v, q_g, k_g, sinks, L):
    t = z_s.shape[0]
    DB, Wb, KV, HD = cache_k.shape
    ns = SAMPLE_SEQS
    rows = ns * L
    twice = lambda g: jnp.concatenate([g, g]).reshape(1, LANES)
    fm = lambda c: jnp.transpose(c, (0, 2, 3, 1)).reshape(DB, KV * HD, Wb)
    back = lambda c: jnp.transpose(c.reshape(DB, KV, HD, Wb), (0, 3, 1, 2))
    cache = pl.BlockSpec((ns, KV * HD, Wb), lambda i: (i, 0, 0))
    o, nk, nv = pl.pallas_call(
        functools.partial(_swa_sample_kernel, L=L),
        grid=(DB // ns,),
        in_specs=[pl.BlockSpec(memory_space=pltpu.SMEM),
                  pl.BlockSpec((rows, SWA_Q_COLS), lambda i: (i, 0)),
                  pl.BlockSpec((rows, LANES), lambda i: (i, Z_K // LANES)),
                  pl.BlockSpec((rows, LANES), lambda i: (i, Z_V // LANES)),
                  cache, cache,
                  pl.BlockSpec((1, LANES), lambda i: (0, 0)),
                  pl.BlockSpec((1, LANES), lambda i: (0, 0))],
        out_specs=[pl.BlockSpec((rows, SWA_Q_COLS), lambda i: (i, 0)), cache, cache],
        out_shape=[jax.ShapeDtypeStruct((t, SWA_Q_COLS), F32),
                   jax.ShapeDtypeStruct((DB, KV * HD, Wb), F32),
                   jax.ShapeDtypeStruct((DB, KV * HD, Wb), F32)],
        compiler_params=_cparams("parallel"),
        name="swa_sample",
    )(sinks, z_s, z_s, z_s, fm(cache_k), fm(cache_v), twice(q_g), twice(k_g))
    return o, back(nk), back(nv)


def _mem_sample_kernel(q_ref, mk_ref, mv_ref, qg_ref, o_ref, *, L):
    n_seq = mk_ref.shape[0]
    rows8 = SUBLANES
    nh = N_MEM_HEADS
    qg = qg_ref[...]
    R = nh * rows8
    row = lax.broadcasted_iota(I32, (R, 1), 0)
    seq_in_pair = (row % rows8) // L
    lane_head8 = lax.broadcasted_iota(I32, (rows8, MEM_Q_COLS), 1) // HEAD_DIM
    pairs = range(n_seq // PAIR)

    def stacked_queries(pr):
        qn = jnp.concatenate([_pair_rms(q_ref[pr * rows8:(pr + 1) * rows8, t * LANES:(t + 1) * LANES], qg)
                              for t in range(MEM_Q_COLS // LANES)], axis=-1) * ATTN_SCALE
        return jnp.concatenate([jnp.where(lane_head8 == h, qn, 0.0) for h in range(nh)], axis=0).astype(BF16)

    qs = [stacked_queries(pr) for pr in pairs]
    ss = [jnp.where(seq_in_pair == 0, _bdot(qs[pr], mk_ref[pr * PAIR]), _bdot(qs[pr], mk_ref[pr * PAIR + 1]))
          for pr in pairs]
    ps = [jnp.exp(s - jnp.max(s, axis=-1, keepdims=True)) for s in ss]
    outs = [(_bdot_nt(jnp.where(seq_in_pair == 0, ps[pr], 0.0), mv_ref[pr * PAIR])
             + _bdot_nt(jnp.where(seq_in_pair == 1, ps[pr], 0.0), mv_ref[pr * PAIR + 1]))
            / jnp.sum(ps[pr], axis=-1, keepdims=True) for pr in pairs]
    for pr in pairs:
        o = jnp.zeros((rows8, MEM_Q_COLS), F32)
        for h in range(nh):
            o = jnp.where(lane_head8 == h, outs[pr][h * rows8:(h + 1) * rows8], o)
        o_ref[pr * rows8:(pr + 1) * rows8, :] = o


def _mem_attn_sample(z_s, mem_k, mem_v, q_g, L):
    t = z_s.shape[0]
    DB, M, H, HD = mem_k.shape
    ns = SAMPLE_SEQS
    rows = ns * L
    twice = jnp.concatenate([q_g, q_g]).reshape(1, LANES)
    fm = lambda c: jnp.transpose(c, (0, 2, 3, 1)).reshape(DB, H * HD, M)
    cache = pl.BlockSpec((ns, H * HD, M), lambda i: (i, 0, 0))
    return pl.pallas_call(
        functools.partial(_mem_sample_kernel, L=L),
        grid=(DB // ns,),
        in_specs=[pl.BlockSpec((rows, MEM_Q_COLS), lambda i: (i, Z_QM // MEM_Q_COLS)),
                  cache, cache,
                  pl.BlockSpec((1, LANES), lambda i: (0, 0))],
        out_specs=pl.BlockSpec((rows, MEM_Q_COLS), lambda i: (i, 0)),
        out_shape=jax.ShapeDtypeStruct((t, MEM_Q_COLS), F32),
        compiler_params=_cparams("parallel"),
        name="mem_attn_sample",
    )(z_s, fm(mem_k), fm(mem_v), twice)


def _gdn_sample_kernel(uq_ref, uk_ref, uv_ref, bq_ref, bk_ref, bv_ref, wq_ref, wk_ref, wv_ref,
                       ab_ref, gate_ref, alog_ref, dtb_ref, ng_ref, s_in_ref, o_ref, s_ref, kq_ref):
    h = pl.program_id(0)
    L = uq_ref.shape[0]
    nbuf = bq_ref.shape[0]
    DK = GDN_DK

    def conv(u_ref, b_ref, w_ref, t):
        up = [b_ref[i] for i in range(nbuf)] + [u_ref[i] for i in range(L)]
        y = up[t] * w_ref[0]
        for i in range(1, GDN_CONV):
            y = y + up[t + i] * w_ref[i]
        return y * jax.nn.sigmoid(y)

    s_ref[...] = s_in_ref[...]
    ng = ng_ref[...]
    hsel = lax.broadcasted_iota(I32, (SUBLANES, 1), 0)
    pick = lambda m, r: jnp.sum(jnp.where(hsel == r, m, 0.0), axis=0, keepdims=True)
    alog = pick(alog_ref[...], h)
    dtb = pick(dtb_ref[...], h)
    for t in range(L):
        q = conv(uq_ref, bq_ref, wq_ref, t)
        k = conv(uk_ref, bk_ref, wk_ref, t)
        v = conv(uv_ref, bv_ref, wv_ref, t)
        q = q * lax.rsqrt(jnp.sum(q * q, axis=0, keepdims=True) + EPS) * (GDN_DK ** -0.5)
        k = k * lax.rsqrt(jnp.sum(k * k, axis=0, keepdims=True) + EPS)
        ab = ab_ref[t]
        a = pick(ab, h)
        bb = pick(ab, h + N_GDN_HEADS)
        decay = jnp.exp(-jnp.exp(alog) * _softplus(a + dtb))
        beta = jax.nn.sigmoid(bb)
        kq_ref[0] = k
        kq_ref[1] = q

        def decay_and_project(dk, acc):
            s = s_ref[0, dk] * decay
            s_ref[0, dk] = s
            return acc + s * kq_ref[0, pl.ds(dk, 1), :]

        sk = lax.fori_loop(0, DK, decay_and_project, jnp.zeros_like(v), unroll=8)
        u = beta * (v - sk)

        def update_and_read(dk, acc):
            s = s_ref[0, dk] + kq_ref[0, pl.ds(dk, 1), :] * u
            s_ref[0, dk] = s
            return acc + s * kq_ref[1, pl.ds(dk, 1), :]

        o = lax.fori_loop(0, DK, update_and_read, jnp.zeros_like(v), unroll=8)
        o = o * lax.rsqrt(jnp.mean(o * o, axis=0, keepdims=True) + EPS) * ng
        g = gate_ref[t]
        o_ref[t] = o * (g * jax.nn.sigmoid(g))


def _gdn_sample(z_s, conv_buf, state, conv_w, a_log, dt_bias, norm_g, DB, L):
    H = N_GDN_HEADS
    z3 = z_s.reshape(DB, L, Z_COLS)
    u_t = jnp.transpose(z3[:, :, Z_GDN:Z_GATE], (1, 2, 0))
    gate_t = jnp.transpose(z3[:, :, Z_GATE:Z_QM], (1, 2, 0))
    ab_t = jnp.transpose(z3[:, :, Z_AB:Z_AB + SUBLANES], (1, 2, 0))
    buf_t = jnp.transpose(conv_buf, (1, 2, 0))
    s_t = jnp.transpose(state, (1, 2, 3, 0))
    w_col = conv_w.reshape(GDN_CONV, GDN_CONV_CH, 1)
    col8 = lambda a: jnp.pad(a, (0, SUBLANES - a.shape[0])).reshape(SUBLANES, 1)
    nbuf = conv_buf.shape[1]
    part = lambda n, j: pl.BlockSpec((n, GDN_DK, DB), lambda h: (0, j * H + h, 0))
    wpart = lambda j: pl.BlockSpec((GDN_CONV, GDN_DK, 1), lambda h: (0, j * H + h, 0))
    whole = lambda shape: pl.BlockSpec(shape, lambda h: (0,) * len(shape))
    o_t, s_new = pl.pallas_call(
        _gdn_sample_kernel,
        grid=(H,),
        in_specs=[part(L, 0), part(L, 1), part(L, 2), part(nbuf, 0), part(nbuf, 1), part(nbuf, 2),
                  wpart(0), wpart(1), wpart(2),
                  whole((L, SUBLANES, DB)),
                  pl.BlockSpec((L, GDN_DV, DB), lambda h: (0, h, 0)),
                  whole((SUBLANES, 1)), whole((SUBLANES, 1)), whole((GDN_DV, 1)),
                  pl.BlockSpec((1, GDN_DK, GDN_DV, DB), lambda h: (h, 0, 0, 0))],
        out_specs=[pl.BlockSpec((L, GDN_DV, DB), lambda h: (0, h, 0)),
                   pl.BlockSpec((1, GDN_DK, GDN_DV, DB), lambda h: (h, 0, 0, 0))],
        out_shape=[jax.ShapeDtypeStruct((L, H * GDN_DV, DB), F32),
                   jax.ShapeDtypeStruct((H, GDN_DK, GDN_DV, DB), F32)],
        scratch_shapes=[pltpu.VMEM((2, GDN_DK, DB), F32)],
        compiler_params=_cparams("parallel"),
        name="gdn_sample",
    )(u_t, u_t, u_t, buf_t, buf_t, buf_t, w_col, w_col, w_col, ab_t, gate_t,
      col8(a_log), col8(dt_bias), norm_g.reshape(GDN_DV, 1), s_t)
    o = jnp.transpose(o_t, (2, 0, 1)).reshape(DB * L, H * GDN_DV)
    return o, jnp.transpose(s_new, (3, 0, 1, 2))


def kernel(x_prompt, x_sample, cache_swa_k, cache_swa_v, state_gdn, state_gdn_conv, cache_mem_k, cache_mem_v,
           mem_prompt, ln1_g, w_in, swa_q_norm, swa_k_norm, swa_sinks, gdn_conv_w, gdn_a_log, gdn_dt_bias,
           gdn_norm_g, mem_ln_g, w_mem_kv, mem_q_norm, mem_k_norm, w_o, ln2_g, router_w, router_b,
           moe_w1, moe_b1, moe_w2, moe_b2):
    B, S, D = x_prompt.shape
    DB, DL, _ = x_sample.shape
    depth = ln1_g.shape[0]
    assert depth == 1
    l = 0
    tp, ts = B * S, DB * DL
    t_all = tp + ts
    n_ab = 2 * N_GDN_HEADS
    c_ab = SWA_Q_COLS + 2 * SWA_KV_COLS + GDN_CONV_CH
    w = w_in[l]
    w_z = jnp.concatenate([w[:, :c_ab], w[:, c_ab + n_ab:], w[:, c_ab:c_ab + n_ab],
                           jnp.zeros((D, LANES - n_ab), F32)], axis=1).astype(BF16)
    rw = jnp.pad(router_w[l], ((0, 0), (0, LANES - N_EXPERTS)))
    rw_hi = rw.astype(BF16)
    rw_lo = (rw - rw_hi.astype(F32)).astype(BF16)
    rb = jnp.pad(router_b[l], (0, LANES - N_EXPERTS)).reshape(1, LANES)
    wo = w_o[l].astype(BF16)
    w1 = moe_w1[l]
    w2 = moe_w2[l]
    b1 = moe_b1[l].reshape(N_EXPERTS, 1, -1)
    b2 = moe_b2[l].reshape(N_EXPERTS, 1, -1)
    p = {'q_norm': swa_q_norm[l], 'k_norm': swa_k_norm[l], 'sinks': swa_sinks[l], 'conv_w': gdn_conv_w[l],
         'a_log': gdn_a_log[l], 'dt_bias': gdn_dt_bias[l], 'gdn_norm': gdn_norm_g[l], 'mem_q_norm': mem_q_norm[l]}

    xp = x_prompt.reshape(tp, D)
    xs = x_sample.reshape(ts, D)
    z_p = _inproj(xp, ln1_g[l], w_z)
    z_s = _inproj(xs, ln1_g[l], w_z)

    M = mem_prompt.shape[1]
    z_p3 = z_p.reshape(B, S, Z_COLS)
    mk2, mv2 = _mem_kv(mem_prompt.reshape(B * M, D), mem_ln_g[l], w_mem_kv[l], mem_k_norm[l])
    mk = mk2.reshape(B, M, N_MEM_HEADS, HEAD_DIM)
    mv = mv2.reshape(B, M, N_MEM_HEADS, HEAD_DIM)
    os_p, pk = _swa_prompt(z_p3, p['q_norm'], p['k_norm'], p['sinks'])
    od_p, ps = _gdn_prompt(z_p3, p['conv_w'], p['a_log'], p['dt_bias'], p['gdn_norm'])
    om_p = _mem_attn_prompt(z_p3, mk2.reshape(B, M, MEM_Q_COLS), mv2.reshape(B, M, MEM_Q_COLS), p['mem_q_norm'])
    os_p, od_p, om_p = os_p.reshape(tp, -1), od_p.reshape(tp, -1), om_p.reshape(tp, -1)
    pk = pk.reshape(B, WINDOW, N_SWA_KV, HEAD_DIM)
    pv = z_p3[:, S - WINDOW:, Z_V:Z_GDN].reshape(B, WINDOW, N_SWA_KV, HEAD_DIM)
    pc = z_p3[:, S - (GDN_CONV - 1):, Z_GDN:Z_GATE]
    os_s, sk, sv = _swa_sample(z_s, cache_swa_k[l], cache_swa_v[l], p['q_norm'], p['k_norm'], p['sinks'], DL)
    od_s, ss = _gdn_sample(z_s, state_gdn_conv[l], state_gdn[l], p['conv_w'], p['a_log'], p['dt_bias'],
                           p['gdn_norm'], DB, DL)
    om_s = _mem_attn_sample(z_s, cache_mem_k[l], cache_mem_v[l], p['mem_q_norm'], DL)
    z_s3 = z_s.reshape(DB, DL, Z_COLS)
    sc = jnp.concatenate([state_gdn_conv[l], z_s3[:, :, Z_GDN:Z_GATE]], axis=1)[:, DL:]

    h_p, hn_all, lg_all = _outproj(xp, os_p, od_p, om_p, wo, ln2_g[l], rw_hi, rw_lo, rb, t_all, 0)
    h_s, hn_all, lg_all = _outproj(xs, os_s, od_s, om_s, wo, ln2_g[l], rw_hi, rw_lo, rb, t_all, tp,
                                   prev=(hn_all, lg_all))
    y_p, y_s = _moe(hn_all, lg_all, [h_p, h_s], w1, b1, w2, b2)
    return (y_p.reshape(B, S, D), y_s.reshape(DB, DL, D), pk[None], pv[None], ps[None], pc[None], mk[None],
            mv[None], sk[None], sv[None], ss[None], sc[None])
```

```python
import functools

import jax
import jax.numpy as jnp
from jax import lax
from jax.experimental import pallas as pl
from jax.experimental.pallas import tpu as pltpu

F32 = jnp.float32
BF16 = jnp.bfloat16
I32 = jnp.int32

HEAD_DIM = 64
N_SWA_HEADS = 8
N_SWA_KV = 2
SWA_GROUP = N_SWA_HEADS // N_SWA_KV
WINDOW = 128
N_GDN_HEADS = 4
GDN_DK = 64
GDN_DV = 64
GDN_CONV = 4
GDN_CHUNK = 64
N_MEM_HEADS = 4
N_EXPERTS = 32
TOP_K = 4
SWIGLU_ALPHA = 1.702
SWIGLU_LIMIT = 7.0
EPS = 1e-6
ATTN_SCALE = HEAD_DIM ** -0.5

SWA_Q_COLS = N_SWA_HEADS * HEAD_DIM
SWA_KV_COLS = N_SWA_KV * HEAD_DIM
GDN_QK_COLS = N_GDN_HEADS * GDN_DK
GDN_V_COLS = N_GDN_HEADS * GDN_DV
GDN_CONV_CH = 2 * GDN_QK_COLS + GDN_V_COLS
MEM_Q_COLS = N_MEM_HEADS * HEAD_DIM

LANES = 128
SUBLANES = 8
BF16_EXACT_INT = 256.0
VMEM_LIMIT = 56 * 1024 * 1024

Z_Q = 0
Z_K = Z_Q + SWA_Q_COLS
Z_V = Z_K + SWA_KV_COLS
Z_GDN = Z_V + SWA_KV_COLS
Z_GATE = Z_GDN + GDN_CONV_CH
Z_QM = Z_GATE + GDN_V_COLS
Z_AB = Z_QM + MEM_Q_COLS
Z_COLS = Z_AB + LANES

INPROJ_TILE = 1024
ROW_TILE = 512
MOE_TILE = 512
MOE_BLK = 512
PERM_CHUNK = 256
RUN_ALIGN = 16
BLK_ROWS = -(-(MOE_BLK * TOP_K + N_EXPERTS * (RUN_ALIGN - 1)) // PERM_CHUNK) * PERM_CHUNK


def _cparams(*sem):
    return pltpu.CompilerParams(dimension_semantics=sem, vmem_limit_bytes=VMEM_LIMIT)


def _bdot(a, b):
    return jnp.dot(a.astype(BF16), b.astype(BF16), preferred_element_type=F32)


def _bdot_nt(a, b):
    return lax.dot_general(a.astype(BF16), b.astype(BF16), (((1,), (1,)), ((), ())),
                           preferred_element_type=F32)


def _split2(x):
    hi = x.astype(BF16)
    lo = (x - hi.astype(F32)).astype(BF16)
    return hi, lo


def _split3(x):
    hi = x.astype(BF16)
    r = x - hi.astype(F32)
    mid = r.astype(BF16)
    lo = (r - mid.astype(F32)).astype(BF16)
    return hi, mid, lo


def _rms_rows(x, g):
    ms = jnp.mean(x * x, axis=-1, keepdims=True)
    return x * lax.rsqrt(ms + EPS) * g


def _inproj_kernel(x_ref, g_ref, w_ref, z_ref):
    n = _rms_rows(x_ref[...], g_ref[...])
    z_ref[...] = jnp.dot(n.astype(BF16), w_ref[...], preferred_element_type=F32)


def _inproj(x2d, ln_g, w_z):
    t, d = x2d.shape
    tm = min(INPROJ_TILE, t)
    return pl.pallas_call(
        _inproj_kernel,
        grid=(t // tm,),
        in_specs=[pl.BlockSpec((tm, d), lambda i: (i, 0)),
                  pl.BlockSpec((1, d), lambda i: (0, 0)),
                  pl.BlockSpec((d, Z_COLS), lambda i: (0, 0))],
        out_specs=pl.BlockSpec((tm, Z_COLS), lambda i: (i, 0)),
        out_shape=jax.ShapeDtypeStruct((t, Z_COLS), F32),
        compiler_params=_cparams("parallel"),
        name="inproj",
    )(x2d, ln_g.reshape(1, d), w_z)


def _outproj_kernel(x_ref, os_ref, od_ref, om_ref, wo_ref, g_ref, rwh_ref, rwl_ref, rb_ref,
                    *refs, n_own):
    h_ref, hn_ref, lg_ref = refs[-3:]
    i = pl.program_id(0)

    @pl.when(i < n_own)
    def _():
        n_s = os_ref.shape[1]
        n_d = od_ref.shape[1]
        h = x_ref[...]
        h = h + jnp.dot(os_ref[...].astype(BF16), wo_ref[0:n_s, :], preferred_element_type=F32)
        h = h + jnp.dot(od_ref[...].astype(BF16), wo_ref[n_s:n_s + n_d, :], preferred_element_type=F32)
        h = h + jnp.dot(om_ref[...].astype(BF16), wo_ref[n_s + n_d:, :], preferred_element_type=F32)
        h_ref[...] = h
        hn = _rms_rows(h, g_ref[...])
        hn_ref[...] = hn.astype(BF16)
        hi, lo = _split2(hn)
        lg = (jnp.dot(hi, rwh_ref[...], preferred_element_type=F32)
              + jnp.dot(lo, rwh_ref[...], preferred_element_type=F32)
              + jnp.dot(hi, rwl_ref[...], preferred_element_type=F32))
        lg_ref[...] = lg + rb_ref[...]

    @pl.when(i >= n_own)
    def _():
        hn_ref[...] = jnp.zeros_like(hn_ref)
        lg_ref[...] = jnp.zeros_like(lg_ref)


def _outproj(x2d, o_s, o_d, o_m, w_o, ln_g, rw_hi, rw_lo, rb, t_all, row0, prev=None):
    t, d = x2d.shape
    tm = min(ROW_TILE, t)
    blk0 = row0 // tm
    n_own = t // tm
    n_steps = n_own if prev is not None else t_all // tm
    row = lambda i: (jnp.minimum(i, n_own - 1), 0)
    row_off = lambda i: (i + blk0, 0)
    const = lambda i: (0, 0)
    in_specs = [pl.BlockSpec((tm, d), row),
                pl.BlockSpec((tm, o_s.shape[1]), row),
                pl.BlockSpec((tm, o_d.shape[1]), row),
                pl.BlockSpec((tm, o_m.shape[1]), row),
                pl.BlockSpec((d, d), const),
                pl.BlockSpec((1, d), const),
                pl.BlockSpec((d, LANES), const),
                pl.BlockSpec((d, LANES), const),
                pl.BlockSpec((1, LANES), const)]
    args = [x2d, o_s, o_d, o_m, w_o, ln_g.reshape(1, d), rw_hi, rw_lo, rb]
    aliases = {}
    if prev is not None:
        in_specs += [pl.BlockSpec(memory_space=pl.ANY), pl.BlockSpec(memory_space=pl.ANY)]
        aliases = {len(args): 1, len(args) + 1: 2}
        args += list(prev)
    return pl.pallas_call(
        functools.partial(_outproj_kernel, n_own=n_own),
        grid=(n_steps,),
        in_specs=in_specs,
        out_specs=[pl.BlockSpec((tm, d), row),
                   pl.BlockSpec((tm, d), row_off),
                   pl.BlockSpec((tm, LANES), row_off)],
        out_shape=[jax.ShapeDtypeStruct((t, d), F32),
                   jax.ShapeDtypeStruct((t_all, d), BF16),
                   jax.ShapeDtypeStruct((t_all, LANES), F32)],
        input_output_aliases=aliases,
        compiler_params=_cparams("arbitrary"),
        name="outproj_router",
    )(*args)


def _route_kernel(lg_ref, pos_ref, post_ref, g_ref, cnt_ref):
    tm = lg_ref.shape[0]
    lane = lax.broadcasted_iota(I32, (tm, LANES), 1).astype(F32)
    l = jnp.where(lane < N_EXPERTS, lg_ref[...], -jnp.inf)
    vals, idxs = [], []
    for _k in range(TOP_K):
        m = jnp.max(l, axis=-1, keepdims=True)
        idx = jnp.min(jnp.where(l == m, lane, float(LANES)), axis=-1, keepdims=True)
        l = jnp.where(lane == idx, -jnp.inf, l)
        vals.append(m)
        idxs.append(idx)
    ex = [jnp.exp(v - vals[0]) for v in vals]
    den = ex[0] + ex[1] + ex[2] + ex[3]
    member = jnp.zeros((tm, LANES), F32)
    for idx in idxs:
        member = member + jnp.where(lane == idx, 1.0, 0.0)
    ri = lax.broadcasted_iota(I32, (tm, tm), 0)
    ci = lax.broadcasted_iota(I32, (tm, tm), 1)
    strict = jnp.where(ci < ri, 1.0, 0.0).astype(BF16)
    prefix = jnp.dot(strict, member.astype(BF16), preferred_element_type=F32)
    cnt = jnp.sum(member, axis=0, keepdims=True)
    cpad = jnp.ceil(cnt * (1.0 / RUN_ALIGN)) * float(RUN_ALIGN)
    c_hi = jnp.floor(cpad * (1.0 / BF16_EXACT_INT))
    c_lo = cpad - BF16_EXACT_INT * c_hi
    ej = lax.broadcasted_iota(I32, (LANES, LANES), 0)
    ee = lax.broadcasted_iota(I32, (LANES, LANES), 1)
    before = jnp.where(ej < ee, 1.0, 0.0).astype(BF16)
    bcast = lambda v: jnp.broadcast_to(v, (SUBLANES, LANES)).astype(BF16)
    off = (BF16_EXACT_INT * jnp.dot(bcast(c_hi), before, preferred_element_type=F32)
           + jnp.dot(bcast(c_lo), before, preferred_element_type=F32))[0:1]
    where_in_run = prefix + off
    p_out = jnp.zeros((tm, LANES), F32)
    g_out = jnp.zeros((tm, LANES), F32)
    for k in range(TOP_K):
        pos = jnp.sum(jnp.where(lane == idxs[k], where_in_run, 0.0), axis=-1, keepdims=True)
        p_out = jnp.where(lane == float(k), pos, p_out)
        g_out = jnp.where(lane == float(k), ex[k] / den, g_out)
    pos_ref[...] = p_out[:, :TOP_K]
    post_ref[...] = p_out.T[:SUBLANES, :]
    g_ref[...] = g_out[:, :TOP_K]
    cnt_ref[0] = cnt


def _route(logits):
    t = logits.shape[0]
    tm = MOE_BLK
    nb = t // tm
    return pl.pallas_call(
        _route_kernel,
        grid=(nb,),
        in_specs=[pl.BlockSpec((tm, LANES), lambda i: (i, 0))],
        out_specs=[pl.BlockSpec((tm, TOP_K), lambda i: (i, 0)),
                   pl.BlockSpec((SUBLANES, tm), lambda i: (0, i)),
                   pl.BlockSpec((tm, TOP_K), lambda i: (i, 0)),
                   pl.BlockSpec((1, 1, LANES), lambda i: (i, 0, 0))],
        out_shape=[jax.ShapeDtypeStruct((t, TOP_K), F32),
                   jax.ShapeDtypeStruct((SUBLANES, t), F32),
                   jax.ShapeDtypeStruct((t, TOP_K), F32),
                   jax.ShapeDtypeStruct((nb, 1, LANES), F32)],
        compiler_params=_cparams("parallel"),
        name="route",
    )(logits)


def _run_copies(n, max_rows, src_ref, src0, dst_ref, dst0, sem, wait):
    pos = 0
    bit = max_rows
    while bit >= RUN_ALIGN:
        take = (n & bit) != 0

        def go(pos=pos, bit=bit):
            cp = pltpu.make_async_copy(src_ref.at[pl.ds(pl.multiple_of(src0 + pos, RUN_ALIGN), bit)],
                                       dst_ref.at[pl.ds(pl.multiple_of(dst0 + pos, RUN_ALIGN), bit)], sem)
            cp.wait() if wait else cp.start()

        pl.when(take)(go)
        pos = pos + jnp.where(take, bit, 0)
        bit //= 2


RUN_SIZES = tuple(MOE_BLK >> i for i in range((MOE_BLK // RUN_ALIGN).bit_length()))


def _piece_copies(b, cnt_ref, loc_ref, glob_ref, local_ref, global_hbm, sem, to_global, wait):
    for c, rows in enumerate(RUN_SIZES):
        base = b * len(RUN_SIZES) + c

        def body(s, carry, rows=rows, base=base):
            j = base * N_EXPERTS + s
            loc = local_ref.at[pl.ds(pl.multiple_of(loc_ref[j], RUN_ALIGN), rows)]
            glob = global_hbm.at[pl.ds(pl.multiple_of(glob_ref[j], RUN_ALIGN), rows)]
            cp = pltpu.make_async_copy(loc, glob, sem) if to_global else pltpu.make_async_copy(glob, loc, sem)
            cp.wait() if wait else cp.start()
            return carry

        lax.fori_loop(0, cnt_ref[base], body, 0)


def _dispatch_kernel(cnt_ref, loc_ref, glob_ref, rows_ref, estart_ref, elen_ref, nused_ref,
                     hn_ref, post_ref, xs_hbm, buf_ref, zero_ref, sem, zsem):
    b = pl.program_id(0)
    nb = pl.num_programs(0)
    slot = b % 2
    tm = hn_ref.shape[0]
    x = hn_ref[...].astype(BF16)
    post = post_ref[...]
    P = PERM_CHUNK

    def sort_rows(c):
        r = (lax.broadcasted_iota(I32, (P, tm), 0) + c * P).astype(F32)
        sel = jnp.zeros((P, tm), F32)
        for k in range(TOP_K):
            sel = jnp.where(r == post[k:k + 1, :], 1.0, sel)
        buf_ref[slot, c * P:(c + 1) * P, :] = jnp.dot(sel.astype(BF16), x,
                                                      preferred_element_type=F32).astype(BF16)

    for c in range(BLK_ROWS // P):
        if c * P < tm * TOP_K:
            sort_rows(c)
        else:
            pl.when(rows_ref[b] > c * P)(functools.partial(sort_rows, c))

    def runs(blk, s, wait):
        _piece_copies(blk, cnt_ref, loc_ref, glob_ref, buf_ref.at[s], xs_hbm, sem.at[s], True, wait)

    runs(b, slot, False)

    @pl.when(b == 0)
    def _():
        zero_ref[...] = jnp.zeros_like(zero_ref)

        def tail(wait):
            def body(e, c):
                n = (MOE_TILE - elen_ref[e] % MOE_TILE) % MOE_TILE
                _run_copies(n, MOE_TILE // 2, zero_ref, 0, xs_hbm, estart_ref[e] + elen_ref[e], zsem, wait)
                return c
            lax.fori_loop(0, N_EXPERTS, body, 0)

            def free_tile(ti, c):
                for half in range(2):
                    _run_copies(MOE_TILE // 2, MOE_TILE // 2, zero_ref, 0, xs_hbm,
                                ti * MOE_TILE + half * (MOE_TILE // 2), zsem, wait)
                return c
            lax.fori_loop(nused_ref[0], xs_hbm.shape[0] // MOE_TILE, free_tile, 0)

        tail(False)
        tail(True)

    pl.when(b > 0)(lambda: runs(b - 1, 1 - slot, True))
    pl.when(b == nb - 1)(lambda: runs(b, slot, True))


def _dispatch(hn, post, piece_cnt, piece_loc, piece_glob, blk_rows, e_start, e_len, n_used, n_rows):
    t, d = hn.shape
    grid_spec = pltpu.PrefetchScalarGridSpec(
        num_scalar_prefetch=7,
        grid=(t // MOE_BLK,),
        in_specs=[pl.BlockSpec((MOE_BLK, d), lambda i, *_: (i, 0)),
                  pl.BlockSpec((SUBLANES, MOE_BLK), lambda i, *_: (0, i))],
        out_specs=pl.BlockSpec(memory_space=pl.ANY),
        scratch_shapes=[pltpu.VMEM((2, BLK_ROWS, d), BF16),
                        pltpu.VMEM((MOE_TILE, d), BF16),
                        pltpu.SemaphoreType.DMA((2,)),
                        pltpu.SemaphoreType.DMA(())],
    )
    return pl.pallas_call(
        _dispatch_kernel,
        grid_spec=grid_spec,
        out_shape=jax.ShapeDtypeStruct((n_rows, d), BF16),
        compiler_params=_cparams("arbitrary"),
        name="dispatch",
    )(piece_cnt, piece_loc, piece_glob, blk_rows, e_start, e_len, n_used, hn, post)


def _expert_kernel(te_ref, nu_ref, nxt_ref, slot_ref, x_ref, w1_hbm, b1_ref, w2_hbm, b2_ref, y_ref,
                   w1f_ref, w2f_ref, w1b_ref, w2b_ref, sem):
    i = pl.program_id(0)
    live = i < nu_ref[0]
    e = te_ref[i]
    s = slot_ref[e]

    def weight_copies(expert, slot):
        return (pltpu.make_async_copy(w1_hbm.at[expert], w1f_ref.at[slot], sem.at[0, slot]),
                pltpu.make_async_copy(w2_hbm.at[expert], w2f_ref.at[slot], sem.at[1, slot]))

    @pl.when(live & (i == 0))
    def _():
        for cp in weight_copies(e, s):
            cp.start()

    @pl.when(live & ((i == 0) | (e != te_ref[jnp.maximum(i - 1, 0)])))
    def _():
        for cp in weight_copies(e, s):
            cp.wait()
        w1b_ref[...] = w1f_ref[s].astype(BF16)
        w2b_ref[...] = w2f_ref[s].astype(BF16)

        @pl.when(nxt_ref[e] >= 0)
        def _():
            for cp in weight_copies(nxt_ref[e], 1 - s):
                cp.start()

    @pl.when(live)
    def _():
        f = w2b_ref.shape[0]
        h = jnp.dot(x_ref[...], w1b_ref[...], preferred_element_type=F32) + b1_ref[0]
        glu = jnp.minimum(h[:, :f], SWIGLU_LIMIT)
        lin = jnp.clip(h[:, f:], -SWIGLU_LIMIT, SWIGLU_LIMIT)
        act = glu * jax.nn.sigmoid(SWIGLU_ALPHA * glu) * (lin + 1.0)
        y = jnp.dot(act.astype(BF16), w2b_ref[...], preferred_element_type=F32) + b2_ref[0]
        y_ref[...] = y.astype(BF16)

    @pl.when(i >= nu_ref[0])
    def _():
        y_ref[...] = jnp.zeros_like(y_ref)


def _experts(xs, tile_expert, n_used, next_expert, expert_slot, w1, b1, w2, b2):
    n_rows, d = xs.shape
    f2 = w1.shape[2]
    f = w2.shape[1]
    n_tiles = n_rows // MOE_TILE
    live = lambda i, te, nu, *_: (jnp.minimum(i, nu[0] - 1), 0)
    every = lambda i, *_: (i, 0)
    wsel = lambda i, te, *_: (te[i], 0, 0)
    grid_spec = pltpu.PrefetchScalarGridSpec(
        num_scalar_prefetch=4,
        grid=(n_tiles,),
        in_specs=[pl.BlockSpec((MOE_TILE, d), live),
                  pl.BlockSpec(memory_space=pl.ANY),
                  pl.BlockSpec((1, 1, f2), wsel),
                  pl.BlockSpec(memory_space=pl.ANY),
                  pl.BlockSpec((1, 1, d), wsel)],
        out_specs=pl.BlockSpec((MOE_TILE, d), every),
        scratch_shapes=[pltpu.VMEM((2, d, f2), F32), pltpu.VMEM((2, f, d), F32),
                        pltpu.VMEM((d, f2), BF16), pltpu.VMEM((f, d), BF16),
                        pltpu.SemaphoreType.DMA((2, 2))],
    )
    return pl.pallas_call(
        _expert_kernel,
        grid_spec=grid_spec,
        out_shape=jax.ShapeDtypeStruct((n_rows, d), BF16),
        compiler_params=_cparams("arbitrary"),
        name="experts",
    )(tile_expert, n_used, next_expert, expert_slot, xs, w1, b1, w2, b2)


def _combine_kernel(cnt_ref, loc_ref, glob_ref, rows_ref, h_ref, pos_ref, g_ref, yb_hbm, y_ref, buf_ref, sem,
                    *, blk0):
    i = pl.program_id(0)
    n_steps = pl.num_programs(0)
    b = i + blk0
    slot = i % 2
    tm, d = h_ref.shape

    def runs(blk, s, wait):
        _piece_copies(blk, cnt_ref, loc_ref, glob_ref, buf_ref.at[s], yb_hbm, sem.at[s], False, wait)

    @pl.when(i == 0)
    def _():
        buf_ref[...] = jnp.zeros_like(buf_ref)
        runs(b, slot, False)

    pl.when(i + 1 < n_steps)(lambda: runs(b + 1, 1 - slot, False))
    runs(b, slot, True)
    pos = pos_ref[...]
    g = g_ref[...]
    P = PERM_CHUNK

    def weighted_rows(c):
        col = (lax.broadcasted_iota(I32, (tm, P), 1) + c * P).astype(F32)
        wgt = jnp.zeros((tm, P), F32)
        for k in range(TOP_K):
            wgt = jnp.where(col == pos[:, k:k + 1], g[:, k:k + 1], wgt)
        return jnp.dot(wgt.astype(BF16), buf_ref[slot, c * P:(c + 1) * P, :], preferred_element_type=F32)

    n_sure = tm * TOP_K // P
    y = h_ref[...]
    for c in range(n_sure):
        y = y + weighted_rows(c)
    y_ref[...] = y
    for c in range(n_sure, BLK_ROWS // P):
        @pl.when(rows_ref[b] > c * P)
        def _(c=c):
            y_ref[...] += weighted_rows(c)


def _combine(h, pos, gates, piece_cnt, piece_loc, piece_glob, blk_rows, yb, blk0):
    t, d = h.shape
    grid_spec = pltpu.PrefetchScalarGridSpec(
        num_scalar_prefetch=4,
        grid=(t // MOE_BLK,),
        in_specs=[pl.BlockSpec((MOE_BLK, d), lambda i, *_: (i, 0)),
                  pl.BlockSpec((MOE_BLK, TOP_K), lambda i, *_: (i + blk0, 0)),
                  pl.BlockSpec((MOE_BLK, TOP_K), lambda i, *_: (i + blk0, 0)),
                  pl.BlockSpec(memory_space=pl.ANY)],
        out_specs=pl.BlockSpec((MOE_BLK, d), lambda i, *_: (i, 0)),
        scratch_shapes=[pltpu.VMEM((2, BLK_ROWS, d), BF16),
                        pltpu.SemaphoreType.DMA((2,))],
    )
    return pl.pallas_call(
        functools.partial(_combine_kernel, blk0=blk0),
        grid_spec=grid_spec,
        out_shape=jax.ShapeDtypeStruct((t, d), F32),
        compiler_params=_cparams("arbitrary"),
        name="combine",
    )(piece_cnt, piece_loc, piece_glob, blk_rows, h, pos, gates, yb)


def _moe(hn_all, logits_all, h_parts, w1, b1, w2, b2):
    t_all = hn_all.shape[0]
    nb = t_all // MOE_BLK
    pos, post, gates, counts_f = _route(logits_all)
    cnt = counts_f.reshape(nb, LANES)[:, :N_EXPERTS].astype(I32)
    seg_len = (cnt + RUN_ALIGN - 1) // RUN_ALIGN * RUN_ALIGN
    before_e = jnp.arange(N_EXPERTS)[:, None] < jnp.arange(N_EXPERTS)[None, :]
    before_b = jnp.arange(nb)[None, :] < jnp.arange(nb)[:, None]
    seg_off = jnp.sum(jnp.where(before_e[None], seg_len[:, :, None], 0), axis=1)
    e_len = jnp.sum(seg_len, axis=0)
    e_tiles = (e_len + MOE_TILE - 1) // MOE_TILE
    tile_start = jnp.sum(jnp.where(before_e, e_tiles[:, None], 0), axis=0)
    tile_end = tile_start + e_tiles
    e_start = tile_start * MOE_TILE
    seg_dst = e_start[None, :] + jnp.sum(jnp.where(before_b[:, :, None], seg_len[None], 0), axis=1)
    max_rows = t_all * TOP_K + nb * N_EXPERTS * (RUN_ALIGN - 1) + N_EXPERTS * (MOE_TILE - RUN_ALIGN)
    n_tiles = -(-max_rows // MOE_TILE)
    n_rows = n_tiles * MOE_TILE
    n_used = tile_end[-1:].astype(I32)
    tile_expert = jnp.minimum(jnp.sum(tile_end[None, :] <= jnp.arange(n_tiles, dtype=I32)[:, None], axis=1),
                              N_EXPERTS - 1).astype(I32)
    sizes = jnp.array(RUN_SIZES, I32)[None, :, None]
    n_run = seg_len[:, None, :]
    has = (n_run & sizes) != 0
    piece_at = n_run & ~(2 * sizes - 1)
    rank = jnp.sum(jnp.where(before_e[None, None], has[:, :, :, None], False), axis=2)
    slot = jnp.arange(N_EXPERTS)
    put = has[..., None] & (rank[..., None] == slot)
    listed = lambda v: jnp.sum(jnp.where(put, v[..., None], 0), axis=2)
    piece_loc = listed(seg_off[:, None, :] + piece_at)
    piece_glob = listed(seg_dst[:, None, :] + piece_at)
    piece_cnt = jnp.sum(has, axis=2)
    flat = lambda a: a.reshape(-1).astype(I32)
    tables = (flat(piece_cnt), flat(piece_loc), flat(piece_glob), flat(jnp.sum(seg_len, axis=1)))
    xs = _dispatch(hn_all, post, *tables, flat(e_start), flat(e_len), n_used, n_rows)
    e_ids = jnp.arange(N_EXPERTS)
    used = e_tiles > 0
    next_expert = jnp.min(jnp.where(before_e & used[None, :], e_ids[None, :], N_EXPERTS), axis=1)
    next_expert = jnp.where(next_expert == N_EXPERTS, -1, next_expert)
    expert_slot = jnp.sum(jnp.where(before_e & used[:, None], 1, 0), axis=0) % 2
    yb = _experts(xs, tile_expert, n_used, flat(next_expert), flat(expert_slot), w1, b1, w2, b2)
    outs = []
    row = 0
    for h in h_parts:
        outs.append(_combine(h, pos, gates, *tables, yb, row // MOE_BLK))
        row += h.shape[0]
    return outs


GDN_ROWS = 4 * GDN_CHUNK
CONV_HALO = SUBLANES
NEUMANN_SPLIT = 2
AB_LANES = 2 * N_GDN_HEADS
SOLVE_ROWS = 2 * GDN_CHUNK


def _softplus(x):
    return jnp.maximum(x, 0.0) + jnp.log1p(jnp.exp(-jnp.abs(x)))


def _gdn_prompt_kernel(u_ref, ab_ref, gate_ref, cw_ref, alog_ref, dtb_ref, ng_ref, o_ref, s_ref, ubuf_ref):
    step = pl.program_id(0)
    NB = u_ref.shape[0]
    R = GDN_ROWS
    C = GDN_CHUNK
    NC = R // C

    @pl.when(step == 0)
    def _():
        ubuf_ref[:, 0:CONV_HALO, :] = jnp.zeros((NB, CONV_HALO, ubuf_ref.shape[2]), F32)
        s_ref[...] = jnp.zeros_like(s_ref)

    ri = lax.broadcasted_iota(I32, (R, R), 0)
    ci = lax.broadcasted_iota(I32, (R, R), 1)
    shift = C.bit_length() - 1
    same = lax.shift_right_logical(ri, shift) == lax.shift_right_logical(ci, shift)
    incl = same & (ci <= ri)
    strict = same & (ci < ri)
    tri = jnp.where(incl, 1.0, 0.0).astype(BF16)
    blk = jnp.where(same, 1.0, 0.0).astype(BF16)
    cw = cw_ref[...]
    ng = ng_ref[...]

    lane = lax.broadcasted_iota(I32, (R, LANES), 1)
    lane1 = lax.broadcasted_iota(I32, (1, LANES), 1)
    ab = ab_ref[0]
    alog = alog_ref[...]
    dtb = dtb_ref[...]
    for b in range(1, NB):
        own = (lane >= b * AB_LANES) & (lane < (b + 1) * AB_LANES)
        own1 = (lane1 >= b * AB_LANES) & (lane1 < (b + 1) * AB_LANES)
        ab = jnp.where(own, pltpu.roll(ab_ref[b], b * AB_LANES, 1), ab)
        alog = jnp.where(own1, pltpu.roll(alog_ref[...], b * AB_LANES, 1), alog)
        dtb = jnp.where(own1, pltpu.roll(dtb_ref[...], b * AB_LANES, 1), dtb)
    g_t = -jnp.exp(alog) * _softplus(ab + dtb)
    beta_t = jax.nn.sigmoid(ab)
    both = jnp.concatenate([tri, blk], axis=0)
    sums = sum(jnp.dot(both, p, preferred_element_type=F32) for p in _split3(g_t))
    gcum, gtot = sums[:R], sums[R:]
    gcum_t = gcum.T

    def conv_and_norms(b):
        u = u_ref[b]
        ubuf_ref[b, CONV_HALO:CONV_HALO + R, :] = u
        y = u * cw[GDN_CONV - 1:GDN_CONV, :]
        for j in range(1, GDN_CONV):
            y = y + ubuf_ref[b, CONV_HALO - j:CONV_HALO - j + R, :] * cw[GDN_CONV - 1 - j:GDN_CONV - j, :]
        ubuf_ref[b, 0:CONV_HALO, :] = u[R - CONV_HALO:, :]
        qkv = y * jax.nn.sigmoid(y)
        qk_n = []
        for t in range(2 * GDN_QK_COLS // LANES):
            x = qkv[:, t * LANES:(t + 1) * LANES]
            x = x * lax.rsqrt(_pair_sumsq(x) + EPS)
            qk_n.append(x * (GDN_DK ** -0.5) if t < GDN_QK_COLS // LANES else x)
        return qkv, jnp.concatenate(qk_n, axis=-1)

    rhs_tiles = {}

    def make_chain(b, h, qkv, qk_n):
        q = qk_n[:, h * GDN_DK:(h + 1) * GDN_DK]
        k = qk_n[:, GDN_QK_COLS + h * GDN_DK:GDN_QK_COLS + (h + 1) * GDN_DK]
        col = b * AB_LANES + h
        gc = gcum[:, col:col + 1]
        gt = gtot[:, col:col + 1]
        beta = beta_t[:, col + N_GDN_HEADS:col + N_GDN_HEADS + 1]
        pair = (b, h // 2)
        if pair not in rhs_tiles:
            c0 = b * AB_LANES + 2 * (h // 2)
            first = _first_half((R, LANES))
            per_lane = lambda m, off: jnp.where(first, m[:, c0 + off:c0 + off + 1], m[:, c0 + off + 1:c0 + off + 2])
            beta_l = per_lane(beta_t, N_GDN_HEADS)
            t0 = (h // 2) * LANES
            gc_l, gt_l = per_lane(gcum, 0), per_lane(gtot, 0)
            e_gc = jnp.exp(gc_l)
            k_tile = qk_n[:, GDN_QK_COLS + t0:GDN_QK_COLS + t0 + LANES]
            vb = qkv[:, 2 * GDN_QK_COLS + t0:2 * GDN_QK_COLS + t0 + LANES] * beta_l
            kb = k_tile * (beta_l * e_gc)
            rhs_tiles[pair] = (jnp.where(first, vb, pltpu.roll(kb, GDN_DK, 1)),
                               jnp.where(first, pltpu.roll(vb, GDN_DV, 1), kb),
                               qk_n[:, t0:t0 + LANES] * e_gc,
                               k_tile * jnp.exp(gt_l - gc_l))
        r = rhs_tiles[pair][h % 2]
        half = slice((h % 2) * GDN_DK, (h % 2 + 1) * GDN_DK)
        q_dec = rhs_tiles[pair][2][:, half]
        k_dec = rhs_tiles[pair][3][:, half]
        a_blocks, qk_blocks, r_blocks = [], [], []
        for sb in range(R // SOLVE_ROWS):
            rows = slice(sb * SOLVE_ROWS, (sb + 1) * SOLVE_ROWS)
            decay = jnp.exp(jnp.where(incl[:SOLVE_ROWS, :SOLVE_ROWS], gc[rows] - gcum_t[col:col + 1, rows], -jnp.inf))
            a_blocks.append(jnp.where(strict[:SOLVE_ROWS, :SOLVE_ROWS],
                                      beta[rows] * _bdot_nt(k[rows], k[rows]) * decay, 0.0))
            qk_blocks.append(_bdot_nt(q[rows], k[rows]) * decay)
            r_blocks.append(r[rows])
        return dict(b=b, h=h, a=a_blocks, qk=qk_blocks, r=r_blocks,
                    q_dec=q_dec, k_dec=k_dec, g_last=jnp.exp(gt))

    dot = lambda x, y: jnp.dot(x, y, preferred_element_type=F32)
    n_levels = C.bit_length() - 1

    def neumann_level(chains, j):
        for ch in chains:
            for sb in range(len(ch['a'])):
                a, r = ch['a'][sb], ch['r'][sb]
                if j < NEUMANN_SPLIT:
                    a_hi, a_lo = _split2(a)
                    r_hi, r_lo = _split2(r)
                    upd = dot(a_hi, r_hi) + dot(a_hi, r_lo) + dot(a_lo, r_hi)
                else:
                    a_hi = a.astype(BF16)
                    upd = dot(a_hi, r.astype(BF16))
                ch['r'][sb] = r - upd if j == 0 else r + upd
                if j + 1 < n_levels:
                    sq = dot(a_hi, a_hi)
                    if j + 1 < NEUMANN_SPLIT:
                        sq = sq + dot(a_hi, a_lo) + dot(a_lo, a_hi)
                    ch['a'][sb] = sq

    def chunk_begin(chains):
        for ch in chains:
            r = jnp.concatenate(ch['r'], axis=0)
            ch['u'], ch['w'] = r[:, :GDN_DV], r[:, GDN_DV:]
            ch['S'] = s_ref[ch['b'], ch['h']]
            ch['k_dec_t'] = ch['k_dec'].T
            ch['outs'] = []

    def chunk_step(chains, c):
        sl = slice(c * C, (c + 1) * C)
        per = SOLVE_ROWS // C
        loc = slice((c % per) * C, (c % per + 1) * C)
        from_state = [(_bdot(ch['w'][sl], ch['S']), _bdot(ch['q_dec'][sl], ch['S'])) for ch in chains]
        for ch, (w_s, q_s) in zip(chains, from_state):
            v_new = ch['u'][sl] - w_s
            ch['outs'].append(q_s + _bdot(ch['qk'][c // per][loc, loc], v_new))
            ch['S'] = ch['S'] * ch['g_last'][c * C:c * C + 1, :] + _bdot(ch['k_dec_t'][:, sl], v_new)

    def chunk_end(chains):
        for ch in chains:
            b, h = ch['b'], ch['h']
            s_ref[b, h] = ch['S']
            o = jnp.concatenate(ch['outs'], axis=0)
            o = o * lax.rsqrt(jnp.mean(o * o, axis=-1, keepdims=True) + EPS) * ng
            gh = gate_ref[b, :, h * GDN_DV:(h + 1) * GDN_DV]
            o_ref[b, :, h * GDN_DV:(h + 1) * GDN_DV] = (o * (gh * jax.nn.sigmoid(gh))).astype(o_ref.dtype)

    chains = []
    for b in range(NB):
        qkv, qk_n = conv_and_norms(b)
        chains += [make_chain(b, h, qkv, qk_n) for h in range(N_GDN_HEADS)]
    for j in range(n_levels):
        neumann_level(chains, j)
    chunk_begin(chains)
    for c in range(NC):
        chunk_step(chains, c)
    chunk_end(chains)


def _gdn_prompt(z3, conv_w, a_log, dt_bias, norm_g):
    B, S, _ = z3.shape
    R = GDN_ROWS
    lanes4 = lambda a: jnp.pad(a, (0, LANES - a.shape[0])).reshape(1, LANES)
    return pl.pallas_call(
        _gdn_prompt_kernel,
        grid=(S // R,),
        in_specs=[pl.BlockSpec((B, R, GDN_CONV_CH), lambda s: (0, s, Z_GDN // GDN_CONV_CH)),
                  pl.BlockSpec((B, R, LANES), lambda s: (0, s, Z_AB // LANES)),
                  pl.BlockSpec((B, R, GDN_V_COLS), lambda s: (0, s, Z_GATE // GDN_V_COLS)),
                  pl.BlockSpec((GDN_CONV, GDN_CONV_CH), lambda s: (0, 0)),
                  pl.BlockSpec((1, LANES), lambda s: (0, 0)),
                  pl.BlockSpec((1, LANES), lambda s: (0, 0)),
                  pl.BlockSpec((1, GDN_DV), lambda s: (0, 0))],
        out_specs=[pl.BlockSpec((B, R, GDN_V_COLS), lambda s: (0, s, 0)),
                   pl.BlockSpec((B, N_GDN_HEADS, GDN_DK, GDN_DV), lambda s: (0, 0, 0, 0))],
        out_shape=[jax.ShapeDtypeStruct((B, S, GDN_V_COLS), BF16),
                   jax.ShapeDtypeStruct((B, N_GDN_HEADS, GDN_DK, GDN_DV), F32)],
        scratch_shapes=[pltpu.VMEM((B, CONV_HALO + R, GDN_CONV_CH), F32)],
        compiler_params=_cparams("arbitrary"),
        name="gdn_prompt",
    )(z3, z3, z3, conv_w, lanes4(a_log), lanes4(dt_bias), norm_g.reshape(1, GDN_DV))


def _pair_sumsq(x):
    li = lax.broadcasted_iota(I32, (LANES, LANES), 0) // HEAD_DIM
    lj = lax.broadcasted_iota(I32, (LANES, LANES), 1) // HEAD_DIM
    same = jnp.where(li == lj, 1.0, 0.0).astype(BF16)
    hi, lo = _split2(x * x)
    return jnp.dot(hi, same, preferred_element_type=F32) + jnp.dot(lo, same, preferred_element_type=F32)


def _pair_rms(x, g):
    return x * lax.rsqrt(_pair_sumsq(x) * (1.0 / HEAD_DIM) + EPS) * g


def _first_half(shape):
    return lax.broadcasted_iota(I32, shape, 1) < HEAD_DIM


LOG2E = 1.4426950408889634


SWA_BLOCKS = 4


def _swa_prompt_kernel(sink_ref, q_ref, kc_ref, kp_ref, vc_ref, vp_ref, qg_ref, kg_ref, o_ref, kn_ref,
                       bias_ref):
    first = (pl.program_id(0) == 0) & (pl.program_id(1) == 0)
    n = pl.program_id(1)
    W = WINDOW
    NQ = SWA_BLOCKS

    @pl.when(first)
    def _():
        qi = lax.broadcasted_iota(I32, (W, 2 * W), 0)
        kj = lax.broadcasted_iota(I32, (W, 2 * W), 1)
        dist = qi + W - kj
        band = (dist >= 0) & (dist < W)
        distf = dist.astype(F32)
        for has_prev in range(2):
            mask = jnp.where(band & ((has_prev == 1) | (kj >= W)), 0.0, -jnp.inf)
            for head in range(N_SWA_HEADS):
                slope = 2.0 ** (-(8.0 / N_SWA_HEADS) * (head + 1))
                bias_ref[has_prev, head] = mask - (slope * LOG2E) * distf

    kg = kg_ref[...]
    qg = qg_ref[...]
    kc = _pair_rms(kc_ref[0], kg)
    kn_ref[0] = kc[(NQ - 1) * W:]
    k3 = jnp.concatenate([_pair_rms(kp_ref[0], kg), kc], axis=0)
    v3 = jnp.concatenate([vp_ref[0], vc_ref[0]], axis=0)
    fh = _first_half(k3.shape)
    k3r = pltpu.roll(k3, HEAD_DIM, 1)
    v3r = pltpu.roll(v3, HEAD_DIM, 1)
    kdup = (jnp.where(fh, k3, k3r).astype(BF16), jnp.where(fh, k3r, k3).astype(BF16))
    vdup = (jnp.where(fh, v3, v3r).astype(BF16), jnp.where(fh, v3r, v3).astype(BF16))
    fq = _first_half((W, LANES))
    kv_of = lambda head: head // SWA_GROUP
    probs = [(j, head) for j in range(NQ) for head in range(N_SWA_HEADS)]
    keys = lambda j: slice(j * W, (j + 2) * W)
    qts = [[_pair_rms(q_ref[0, j * W:(j + 1) * W, t * LANES:(t + 1) * LANES], qg) * (ATTN_SCALE * LOG2E)
            for t in range(SWA_Q_COLS // LANES)] for j in range(NQ)]
    qms = [jnp.where(fq == (head % 2 == 0), qts[j][head // 2], 0.0).astype(BF16) for j, head in probs]
    table = [jnp.where(n > 0, 1, 0)] + [1] * (NQ - 1)
    ss = [_bdot_nt(qms[i], kdup[kv_of(head)][keys(j)]) + bias_ref[table[j], head]
          for i, (j, head) in enumerate(probs)]
    sinks = [sink_ref[head] * LOG2E for head in range(N_SWA_HEADS)]
    ms = [jnp.maximum(jnp.max(ss[i], axis=-1, keepdims=True), sinks[head]) for i, (j, head) in enumerate(probs)]
    ps = [jnp.exp2(ss[i] - ms[i]) for i in range(len(probs))]
    dens = [jnp.sum(ps[i], axis=-1, keepdims=True) + jnp.exp2(sinks[head] - ms[i])
            for i, (j, head) in enumerate(probs)]
    outs = [_bdot(ps[i], vdup[kv_of(head)][keys(j)]) / dens[i] for i, (j, head) in enumerate(probs)]
    for j in range(NQ):
        for t in range(SWA_Q_COLS // LANES):
            o_ref[0, j * W:(j + 1) * W, t * LANES:(t + 1) * LANES] = jnp.where(
                fq, outs[j * N_SWA_HEADS + 2 * t], outs[j * N_SWA_HEADS + 2 * t + 1]).astype(o_ref.dtype)


def _swa_prompt(z3, q_g, k_g, sinks):
    B, S, _ = z3.shape
    W = WINDOW
    NQ = SWA_BLOCKS
    twice = lambda g: jnp.concatenate([g, g]).reshape(1, LANES)
    kcol, vcol = Z_K // LANES, Z_V // LANES
    prev = lambda col: pl.BlockSpec((1, W, LANES), lambda b, n: (b, jnp.maximum(NQ * n - 1, 0), col))
    grid_spec = pltpu.PrefetchScalarGridSpec(
        num_scalar_prefetch=0,
        grid=(B, S // (NQ * W)),
        in_specs=[pl.BlockSpec(memory_space=pltpu.SMEM),
                  pl.BlockSpec((1, NQ * W, SWA_Q_COLS), lambda b, n: (b, n, 0)),
                  pl.BlockSpec((1, NQ * W, LANES), lambda b, n: (b, n, kcol)),
                  prev(kcol),
                  pl.BlockSpec((1, NQ * W, LANES), lambda b, n: (b, n, vcol)),
                  prev(vcol),
                  pl.BlockSpec((1, LANES), lambda b, n: (0, 0)),
                  pl.BlockSpec((1, LANES), lambda b, n: (0, 0))],
        out_specs=[pl.BlockSpec((1, NQ * W, SWA_Q_COLS), lambda b, n: (b, n, 0)),
                   pl.BlockSpec((1, W, LANES), lambda b, n: (b, 0, 0))],
        scratch_shapes=[pltpu.VMEM((2, N_SWA_HEADS, W, 2 * W), F32)],
    )
    return pl.pallas_call(
        _swa_prompt_kernel,
        grid_spec=grid_spec,
        out_shape=[jax.ShapeDtypeStruct((B, S, SWA_Q_COLS), BF16),
                   jax.ShapeDtypeStruct((B, W, LANES), F32)],
        compiler_params=_cparams("arbitrary", "arbitrary"),
        name="swa_prompt",
    )(sinks, z3, z3, z3, z3, z3, twice(q_g), twice(k_g))


def _mem_kv_kernel(m_ref, g_ref, w_ref, kg_ref, k_ref, v_ref):
    n = _rms_rows(m_ref[...], g_ref[...])
    kv = jnp.dot(n.astype(BF16), w_ref[...], preferred_element_type=F32)
    kg = kg_ref[...]
    for t in range(MEM_Q_COLS // LANES):
        k_ref[:, t * LANES:(t + 1) * LANES] = _pair_rms(kv[:, t * LANES:(t + 1) * LANES], kg)
    v_ref[...] = kv[:, MEM_Q_COLS:]


def _mem_kv(mem2d, ln_g, w_kv, k_g):
    r, d = mem2d.shape
    twice = jnp.concatenate([k_g, k_g]).reshape(1, LANES)
    full = lambda shape: pl.BlockSpec(shape, lambda i: (0,) * len(shape))
    return pl.pallas_call(
        _mem_kv_kernel,
        grid=(1,),
        in_specs=[full((r, d)), full((1, d)), full((d, 2 * MEM_Q_COLS)), full((1, LANES))],
        out_specs=[full((r, MEM_Q_COLS)), full((r, MEM_Q_COLS))],
        out_shape=[jax.ShapeDtypeStruct((r, MEM_Q_COLS), F32), jax.ShapeDtypeStruct((r, MEM_Q_COLS), F32)],
        compiler_params=_cparams("arbitrary"),
        name="mem_kv",
    )(mem2d, ln_g.reshape(1, d), w_kv.astype(BF16), twice)


def _mem_attn_kernel(q_ref, k_ref, v_ref, qg_ref, o_ref):
    qg = qg_ref[...]
    rows = q_ref.shape[1]
    fq = _first_half((rows, LANES))
    heads = range(N_MEM_HEADS)
    tile = lambda t: slice(t * LANES, (t + 1) * LANES)
    qts = [_pair_rms(q_ref[0, :, tile(t)], qg) * (ATTN_SCALE * LOG2E) for t in range(MEM_Q_COLS // LANES)]
    kts = [k_ref[0, :, tile(t)].astype(BF16) for t in range(MEM_Q_COLS // LANES)]
    vts = [v_ref[0, :, tile(t)].astype(BF16) for t in range(MEM_Q_COLS // LANES)]
    ss = [_bdot_nt(jnp.where(fq == (h % 2 == 0), qts[h // 2], 0.0), kts[h // 2]) for h in heads]
    ps = [jnp.exp2(s - jnp.max(s, axis=-1, keepdims=True)) for s in ss]
    outs = [_bdot(ps[h], vts[h // 2]) / jnp.sum(ps[h], axis=-1, keepdims=True) for h in heads]
    for t in range(MEM_Q_COLS // LANES):
        o_ref[0, :, tile(t)] = jnp.where(fq, outs[2 * t], outs[2 * t + 1]).astype(o_ref.dtype)


MEM_Q_TILE = 512


def _mem_attn_prompt(z3, mem_k, mem_v, q_g):
    B, S, _ = z3.shape
    M = mem_k.shape[1]
    tq = MEM_Q_TILE
    twice = jnp.concatenate([q_g, q_g]).reshape(1, LANES)
    return pl.pallas_call(
        _mem_attn_kernel,
        grid=(B, S // tq),
        in_specs=[pl.BlockSpec((1, tq, MEM_Q_COLS), lambda b, i: (b, i, Z_QM // MEM_Q_COLS)),
                  pl.BlockSpec((1, M, MEM_Q_COLS), lambda b, i: (b, 0, 0)),
                  pl.BlockSpec((1, M, MEM_Q_COLS), lambda b, i: (b, 0, 0)),
                  pl.BlockSpec((1, LANES), lambda b, i: (0, 0))],
        out_specs=pl.BlockSpec((1, tq, MEM_Q_COLS), lambda b, i: (b, i, 0)),
        out_shape=jax.ShapeDtypeStruct((B, S, MEM_Q_COLS), BF16),
        compiler_params=_cparams("parallel", "parallel"),
        name="mem_attn_prompt",
    )(z3, mem_k, mem_v, twice)


PAIR = 2


def _swa_sample_kernel(sink_ref, q_ref, k_ref, v_ref, ck_ref, cv_ref, qg_ref, kg_ref, o_ref, nk_ref, nv_ref, *, L):
    n_seq = ck_ref.shape[0]
    Wb = ck_ref.shape[2]
    rows8 = SUBLANES
    nh = N_SWA_HEADS
    kn = _pair_rms(k_ref[...], kg_ref[...])
    qg = qg_ref[...]
    R = nh * rows8
    row = lax.broadcasted_iota(I32, (R, 1), 0)
    head = row // rows8
    seq_in_pair = (row % rows8) // L
    step = (row % L).astype(F32)
    slope = jnp.exp2(-(8.0 / N_SWA_HEADS) * (head.astype(F32) + 1.0))
    sink = jnp.zeros((R, 1), F32)
    for h in range(nh):
        sink = jnp.where(head == h, sink_ref[h], sink)
    key = lax.broadcasted_iota(I32, (R, Wb), 1).astype(F32)
    dist_c = float(Wb) + step - key
    bias_c = jnp.where(dist_c < float(WINDOW), 0.0, -jnp.inf)
    col = lax.broadcasted_iota(I32, (R, rows8), 1)
    dist_n = step - (col % L).astype(F32)
    bias_n = jnp.where((dist_n >= 0.0) & ((col // L) == seq_in_pair), 0.0, -jnp.inf)
    fh8 = _first_half((rows8, LANES))
    fhR = _first_half((R, LANES))
    kv_first = head < SWA_GROUP
    pairs = range(n_seq // PAIR)
    bias_c = bias_c - slope * dist_c
    bias_n = bias_n - slope * dist_n

    def stacked_queries(pr):
        r0 = pr * rows8
        pieces = []
        for t in range(SWA_Q_COLS // LANES):
            qt = _pair_rms(q_ref[r0:r0 + rows8, t * LANES:(t + 1) * LANES], qg) * ATTN_SCALE
            qr = pltpu.roll(qt, HEAD_DIM, 1)
            kv = t // (SWA_GROUP // 2)
            for half in range(2):
                src = qt if half == kv else qr
                pieces.append(jnp.where(fh8 == (kv == 0), src, 0.0))
        return jnp.concatenate(pieces, axis=0).astype(BF16)

    qs = [stacked_queries(pr) for pr in pairs]
    s_c = [jnp.where(seq_in_pair == 0, _bdot(qs[pr], ck_ref[pr * PAIR]), _bdot(qs[pr], ck_ref[pr * PAIR + 1]))
           + bias_c for pr in pairs]
    s_n = [_bdot_nt(qs[pr], kn[pr * rows8:(pr + 1) * rows8]) + bias_n for pr in pairs]
    m = [jnp.maximum(jnp.maximum(jnp.max(s_c[pr], axis=-1, keepdims=True),
                                 jnp.max(s_n[pr], axis=-1, keepdims=True)), sink) for pr in pairs]
    p_c = [jnp.exp(s_c[pr] - m[pr]) for pr in pairs]
    p_n = [jnp.exp(s_n[pr] - m[pr]) for pr in pairs]
    den = [jnp.sum(p_c[pr], axis=-1, keepdims=True) + jnp.sum(p_n[pr], axis=-1, keepdims=True)
           + jnp.exp(sink - m[pr]) for pr in pairs]
    outs = [(_bdot(p_n[pr], v_ref[pr * rows8:(pr + 1) * rows8, :])
             + _bdot_nt(jnp.where(seq_in_pair == 0, p_c[pr], 0.0), cv_ref[pr * PAIR])
             + _bdot_nt(jnp.where(seq_in_pair == 1, p_c[pr], 0.0), cv_ref[pr * PAIR + 1])) / den[pr] for pr in pairs]

    pos_r = lax.broadcasted_iota(I32, (Wb, rows8), 0)
    new_c = lax.broadcasted_iota(I32, (Wb, rows8), 1)
    tail = lax.broadcasted_iota(I32, (SWA_KV_COLS, Wb), 1) >= Wb - L

    def shifted(old, new8, j):
        place = jnp.where((pos_r == Wb - L + new_c % L) & (new_c // L == j), 1.0, 0.0).astype(BF16)
        rows_at_tail = sum(jnp.dot(place, part, preferred_element_type=F32) for part in _split3(new8))
        return jnp.where(tail, rows_at_tail.T, pltpu.roll(old, Wb - L, 1))

    for pr in pairs:
        for j in range(PAIR):
            s = pr * PAIR + j
            nk_ref[s] = shifted(ck_ref[s], kn[pr * rows8:(pr + 1) * rows8], j)
            nv_ref[s] = shifted(cv_ref[s], v_ref[pr * rows8:(pr + 1) * rows8, :], j)
    for pr in pairs:
        r0 = pr * rows8
        o = jnp.where(fhR == kv_first, outs[pr], 0.0)
        o_r = pltpu.roll(o, HEAD_DIM, 1)
        for t in range(SWA_Q_COLS // LANES):
            kv = t // (SWA_GROUP // 2)
            halves = []
            for half in range(2):
                h = 2 * t + half
                src = o if half == kv else o_r
                halves.append(src[h * rows8:(h + 1) * rows8])
            o_ref[r0:r0 + rows8, t * LANES:(t + 1) * LANES] = jnp.where(fh8, halves[0], halves[1])


SAMPLE_SEQS = 8


def _swa_sample(z_s, cache_k, cache_v, q_g, k_g, sinks, L):
    t = z_s.shape[0]
    DB, Wb, KV, HD = cache_k.shape
    ns = SAMPLE_SEQS
    rows = ns * L
    twice = lambda g: jnp.concatenate([g, g]).reshape(1, LANES)
    fm = lambda c: jnp.transpose(c, (0, 2, 3, 1)).reshape(DB, KV * HD, Wb)
    back = lambda c: jnp.transpose(c.reshape(DB, KV, HD, Wb), (0, 3, 1, 2))
    cache = pl.BlockSpec((ns, KV * HD, Wb), lambda i: (i, 0, 0))
    o, nk, nv = pl.pallas_call(
        functools.partial(_swa_sample_kernel, L=L),
        grid=(DB // ns,),
        in_specs=[pl.BlockSpec(memory_space=pltpu.SMEM),
                  pl.BlockSpec((rows, SWA_Q_COLS), lambda i: (i, 0)),
                  pl.BlockSpec((rows, LANES), lambda i: (i, Z_K // LANES)),
                  pl.BlockSpec((rows, LANES), lambda i: (i, Z_V // LANES)),
                  cache, cache,
                  pl.BlockSpec((1, LANES), lambda i: (0, 0)),
                  pl.BlockSpec((1, LANES), lambda i: (0, 0))],
        out_specs=[pl.BlockSpec((rows, SWA_Q_COLS), lambda i: (i, 0)), cache, cache],
        out_shape=[jax.ShapeDtypeStruct((t, SWA_Q_COLS), F32),
                   jax.ShapeDtypeStruct((DB, KV * HD, Wb), F32),
                   jax.ShapeDtypeStruct((DB, KV * HD, Wb), F32)],
        compiler_params=_cparams("parallel"),
        name="swa_sample",
    )(sinks, z_s, z_s, z_s, fm(cache_k), fm(cache_v), twice(q_g), twice(k_g))
    return o, back(nk), back(nv)


def _mem_sample_kernel(q_ref, mk_ref, mv_ref, qg_ref, o_ref, *, L):
    n_seq = mk_ref.shape[0]
    rows8 = SUBLANES
    nh = N_MEM_HEADS
    qg = qg_ref[...]
    R = nh * rows8
    row = lax.broadcasted_iota(I32, (R, 1), 0)
    seq_in_pair = (row % rows8) // L
    lane_head8 = lax.broadcasted_iota(I32, (rows8, MEM_Q_COLS), 1) // HEAD_DIM
    pairs = range(n_seq // PAIR)

    def stacked_queries(pr):
        qn = jnp.concatenate([_pair_rms(q_ref[pr * rows8:(pr + 1) * rows8, t * LANES:(t + 1) * LANES], qg)
                              for t in range(MEM_Q_COLS // LANES)], axis=-1) * ATTN_SCALE
        return jnp.concatenate([jnp.where(lane_head8 == h, qn, 0.0) for h in range(nh)], axis=0).astype(BF16)

    qs = [stacked_queries(pr) for pr in pairs]
    ss = [jnp.where(seq_in_pair == 0, _bdot(qs[pr], mk_ref[pr * PAIR]), _bdot(qs[pr], mk_ref[pr * PAIR + 1]))
          for pr in pairs]
    ps = [jnp.exp(s - jnp.max(s, axis=-1, keepdims=True)) for s in ss]
    outs = [(_bdot_nt(jnp.where(seq_in_pair == 0, ps[pr], 0.0), mv_ref[pr * PAIR])
             + _bdot_nt(jnp.where(seq_in_pair == 1, ps[pr], 0.0), mv_ref[pr * PAIR + 1]))
            / jnp.sum(ps[pr], axis=-1, keepdims=True) for pr in pairs]
    for pr in pairs:
        o = jnp.zeros((rows8, MEM_Q_COLS), F32)
        for h in range(nh):
            o = jnp.where(lane_head8 == h, outs[pr][h * rows8:(h + 1) * rows8], o)
        o_ref[pr * rows8:(pr + 1) * rows8, :] = o


def _mem_attn_sample(z_s, mem_k, mem_v, q_g, L):
    t = z_s.shape[0]
    DB, M, H, HD = mem_k.shape
    ns = SAMPLE_SEQS
    rows = ns * L
    twice = jnp.concatenate([q_g, q_g]).reshape(1, LANES)
    fm = lambda c: jnp.transpose(c, (0, 2, 3, 1)).reshape(DB, H * HD, M)
    cache = pl.BlockSpec((ns, H * HD, M), lambda i: (i, 0, 0))
    return pl.pallas_call(
        functools.partial(_mem_sample_kernel, L=L),
        grid=(DB // ns,),
        in_specs=[pl.BlockSpec((rows, MEM_Q_COLS), lambda i: (i, Z_QM // MEM_Q_COLS)),
                  cache, cache,
                  pl.BlockSpec((1, LANES), lambda i: (0, 0))],
        out_specs=pl.BlockSpec((rows, MEM_Q_COLS), lambda i: (i, 0)),
        out_shape=jax.ShapeDtypeStruct((t, MEM_Q_COLS), F32),
        compiler_params=_cparams("parallel"),
        name="mem_attn_sample",
    )(z_s, fm(mem_k), fm(mem_v), twice)


def _gdn_sample_kernel(uq_ref, uk_ref, uv_ref, bq_ref, bk_ref, bv_ref, wq_ref, wk_ref, wv_ref,
                       ab_ref, gate_ref, alog_ref, dtb_ref, ng_ref, s_in_ref, o_ref, s_ref, kq_ref):
    h = pl.program_id(0)
    L = uq_ref.shape[0]
    nbuf = bq_ref.shape[0]
    DK = GDN_DK

    def conv(u_ref, b_ref, w_ref, t):
        up = [b_ref[i] for i in range(nbuf)] + [u_ref[i] for i in range(L)]
        y = up[t] * w_ref[0]
        for i in range(1, GDN_CONV):
            y = y + up[t + i] * w_ref[i]
        return y * jax.nn.sigmoid(y)

    s_ref[...] = s_in_ref[...]
    ng = ng_ref[...]
    hsel = lax.broadcasted_iota(I32, (SUBLANES, 1), 0)
    pick = lambda m, r: jnp.sum(jnp.where(hsel == r, m, 0.0), axis=0, keepdims=True)
    alog = pick(alog_ref[...], h)
    dtb = pick(dtb_ref[...], h)
    for t in range(L):
        q = conv(uq_ref, bq_ref, wq_ref, t)
        k = conv(uk_ref, bk_ref, wk_ref, t)
        v = conv(uv_ref, bv_ref, wv_ref, t)
        q = q * lax.rsqrt(jnp.sum(q * q, axis=0, keepdims=True) + EPS) * (GDN_DK ** -0.5)
        k = k * lax.rsqrt(jnp.sum(k * k, axis=0, keepdims=True) + EPS)
        ab = ab_ref[t]
        a = pick(ab, h)
        bb = pick(ab, h + N_GDN_HEADS)
        decay = jnp.exp(-jnp.exp(alog) * _softplus(a + dtb))
        beta = jax.nn.sigmoid(bb)
        kq_ref[0] = k
        kq_ref[1] = q

        def decay_and_project(dk, acc):
            s = s_ref[0, dk] * decay
            s_ref[0, dk] = s
            return acc + s * kq_ref[0, pl.ds(dk, 1), :]

        sk = lax.fori_loop(0, DK, decay_and_project, jnp.zeros_like(v), unroll=8)
        u = beta * (v - sk)

        def update_and_read(dk, acc):
            s = s_ref[0, dk] + kq_ref[0, pl.ds(dk, 1), :] * u
            s_ref[0, dk] = s
            return acc + s * kq_ref[1, pl.ds(dk, 1), :]

        o = lax.fori_loop(0, DK, update_and_read, jnp.zeros_like(v), unroll=8)
        o = o * lax.rsqrt(jnp.mean(o * o, axis=0, keepdims=True) + EPS) * ng
        g = gate_ref[t]
        o_ref[t] = o * (g * jax.nn.sigmoid(g))


def _gdn_sample(z_s, conv_buf, state, conv_w, a_log, dt_bias, norm_g, DB, L):
    H = N_GDN_HEADS
    z3 = z_s.reshape(DB, L, Z_COLS)
    u_t = jnp.transpose(z3[:, :, Z_GDN:Z_GATE], (1, 2, 0))
    gate_t = jnp.transpose(z3[:, :, Z_GATE:Z_QM], (1, 2, 0))
    ab_t = jnp.transpose(z3[:, :, Z_AB:Z_AB + SUBLANES], (1, 2, 0))
    buf_t = jnp.transpose(conv_buf, (1, 2, 0))
    s_t = jnp.transpose(state, (1, 2, 3, 0))
    w_col = conv_w.reshape(GDN_CONV, GDN_CONV_CH, 1)
    col8 = lambda a: jnp.pad(a, (0, SUBLANES - a.shape[0])).reshape(SUBLANES, 1)
    nbuf = conv_buf.shape[1]
    part = lambda n, j: pl.BlockSpec((n, GDN_DK, DB), lambda h: (0, j * H + h, 0))
    wpart = lambda j: pl.BlockSpec((GDN_CONV, GDN_DK, 1), lambda h: (0, j * H + h, 0))
    whole = lambda shape: pl.BlockSpec(shape, lambda h: (0,) * len(shape))
    o_t, s_new = pl.pallas_call(
        _gdn_sample_kernel,
        grid=(H,),
        in_specs=[part(L, 0), part(L, 1), part(L, 2), part(nbuf, 0), part(nbuf, 1), part(nbuf, 2),
                  wpart(0), wpart(1), wpart(2),
                  whole((L, SUBLANES, DB)),
                  pl.BlockSpec((L, GDN_DV, DB), lambda h: (0, h, 0)),
                  whole((SUBLANES, 1)), whole((SUBLANES, 1)), whole((GDN_DV, 1)),
                  pl.BlockSpec((1, GDN_DK, GDN_DV, DB), lambda h: (h, 0, 0, 0))],
        out_specs=[pl.BlockSpec((L, GDN_DV, DB), lambda h: (0, h, 0)),
                   pl.BlockSpec((1, GDN_DK, GDN_DV, DB), lambda h: (h, 0, 0, 0))],
        out_shape=[jax.ShapeDtypeStruct((L, H * GDN_DV, DB), F32),
                   jax.ShapeDtypeStruct((H, GDN_DK, GDN_DV, DB), F32)],
        scratch_shapes=[pltpu.VMEM((2, GDN_DK, DB), F32)],
        compiler_params=_cparams("parallel"),
        name="gdn_sample",
    )(u_t, u_t, u_t, buf_t, buf_t, buf_t, w_col, w_col, w_col, ab_t, gate_t,
      col8(a_log), col8(dt_bias), norm_g.reshape(GDN_DV, 1), s_t)
    o = jnp.transpose(o_t, (2, 0, 1)).reshape(DB * L, H * GDN_DV)
    return o, jnp.transpose(s_new, (3, 0, 1, 2))


def kernel(x_prompt, x_sample, cache_swa_k, cache_swa_v, state_gdn, state_gdn_conv, cache_mem_k, cache_mem_v,
           mem_prompt, ln1_g, w_in, swa_q_norm, swa_k_norm, swa_sinks, gdn_conv_w, gdn_a_log, gdn_dt_bias,
           gdn_norm_g, mem_ln_g, w_mem_kv, mem_q_norm, mem_k_norm, w_o, ln2_g, router_w, router_b,
           moe_w1, moe_b1, moe_w2, moe_b2):
    B, S, D = x_prompt.shape
    DB, DL, _ = x_sample.shape
    depth = ln1_g.shape[0]
    assert depth == 1
    l = 0
    tp, ts = B * S, DB * DL
    t_all = tp + ts
    n_ab = 2 * N_GDN_HEADS
    c_ab = SWA_Q_COLS + 2 * SWA_KV_COLS + GDN_CONV_CH
    w = w_in[l]
    w_z = jnp.concatenate([w[:, :c_ab], w[:, c_ab + n_ab:], w[:, c_ab:c_ab + n_ab],
                           jnp.zeros((D, LANES - n_ab), F32)], axis=1).astype(BF16)
    rw = jnp.pad(router_w[l], ((0, 0), (0, LANES - N_EXPERTS)))
    rw_hi = rw.astype(BF16)
    rw_lo = (rw - rw_hi.astype(F32)).astype(BF16)
    rb = jnp.pad(router_b[l], (0, LANES - N_EXPERTS)).reshape(1, LANES)
    wo = w_o[l].astype(BF16)
    w1 = moe_w1[l]
    w2 = moe_w2[l]
    b1 = moe_b1[l].reshape(N_EXPERTS, 1, -1)
    b2 = moe_b2[l].reshape(N_EXPERTS, 1, -1)
    p = {'q_norm': swa_q_norm[l], 'k_norm': swa_k_norm[l], 'sinks': swa_sinks[l], 'conv_w': gdn_conv_w[l],
         'a_log': gdn_a_log[l], 'dt_bias': gdn_dt_bias[l], 'gdn_norm': gdn_norm_g[l], 'mem_q_norm': mem_q_norm[l]}

    xp = x_prompt.reshape(tp, D)
    xs = x_sample.reshape(ts, D)
    z_p = _inproj(xp, ln1_g[l], w_z)
    z_s = _inproj(xs, ln1_g[l], w_z)

    M = mem_prompt.shape[1]
    z_p3 = z_p.reshape(B, S, Z_COLS)
    mk2, mv2 = _mem_kv(mem_prompt.reshape(B * M, D), mem_ln_g[l], w_mem_kv[l], mem_k_norm[l])
    mk = mk2.reshape(B, M, N_MEM_HEADS, HEAD_DIM)
    mv = mv2.reshape(B, M, N_MEM_HEADS, HEAD_DIM)
    os_p, pk = _swa_prompt(z_p3, p['q_norm'], p['k_norm'], p['sinks'])
    od_p, ps = _gdn_prompt(z_p3, p['conv_w'], p['a_log'], p['dt_bias'], p['gdn_norm'])
    om_p = _mem_attn_prompt(z_p3, mk2.reshape(B, M, MEM_Q_COLS), mv2.reshape(B, M, MEM_Q_COLS), p['mem_q_norm'])
    os_p, od_p, om_p = os_p.reshape(tp, -1), od_p.reshape(tp, -1), om_p.reshape(tp, -1)
    pk = pk.reshape(B, WINDOW, N_SWA_KV, HEAD_DIM)
    pv = z_p3[:, S - WINDOW:, Z_V:Z_GDN].reshape(B, WINDOW, N_SWA_KV, HEAD_DIM)
    pc = z_p3[:, S - (GDN_CONV - 1):, Z_GDN:Z_GATE]
    os_s, sk, sv = _swa_sample(z_s, cache_swa_k[l], cache_swa_v[l], p['q_norm'], p['k_norm'], p['sinks'], DL)
    od_s, ss = _gdn_sample(z_s, state_gdn_conv[l], state_gdn[l], p['conv_w'], p['a_log'], p['dt_bias'],
                           p['gdn_norm'], DB, DL)
    om_s = _mem_attn_sample(z_s, cache_mem_k[l], cache_mem_v[l], p['mem_q_norm'], DL)
    z_s3 = z_s.reshape(DB, DL, Z_COLS)
    sc = jnp.concatenate([state_gdn_conv[l], z_s3[:, :, Z_GDN:Z_GATE]], axis=1)[:, DL:]

    h_p, hn_all, lg_all = _outproj(xp, os_p, od_p, om_p, wo, ln2_g[l], rw_hi, rw_lo, rb, t_all, 0)
    h_s, hn_all, lg_all = _outproj(xs, os_s, od_s, om_s, wo, ln2_g[l], rw_hi, rw_lo, rb, t_all, tp,
                                   prev=(hn_all, lg_all))
    y_p, y_s = _moe(hn_all, lg_all, [h_p, h_s], w1, b1, w2, b2)
    return (y_p.reshape(B, S, D), y_s.reshape(DB, DL, D), pk[None], pv[None], ps[None], pc[None], mk[None],
            mv[None], sk[None], sv[None], ss[None], sc[None])
```

```python
import functools

import jax
import jax.numpy as jnp
from jax import lax
from jax.experimental import pallas as pl
from jax.experimental.pallas import tpu as pltpu

F32 = jnp.float32
BF16 = jnp.bfloat16
I32 = jnp.int32

HEAD_DIM = 64
N_SWA_HEADS = 8
N_SWA_KV = 2
SWA_GROUP = N_SWA_HEADS // N_SWA_KV
WINDOW = 128
N_GDN_HEADS = 4
GDN_DK = 64
GDN_DV = 64
GDN_CONV = 4
GDN_CHUNK = 64
N_MEM_HEADS = 4
N_EXPERTS = 32
TOP_K = 4
SWIGLU_ALPHA = 1.702
SWIGLU_LIMIT = 7.0
EPS = 1e-6
ATTN_SCALE = HEAD_DIM ** -0.5

SWA_Q_COLS = N_SWA_HEADS * HEAD_DIM
SWA_KV_COLS = N_SWA_KV * HEAD_DIM
GDN_QK_COLS = N_GDN_HEADS * GDN_DK
GDN_V_COLS = N_GDN_HEADS * GDN_DV
GDN_CONV_CH = 2 * GDN_QK_COLS + GDN_V_COLS
MEM_Q_COLS = N_MEM_HEADS * HEAD_DIM

LANES = 128
SUBLANES = 8
BF16_EXACT_INT = 256.0
VMEM_LIMIT = 56 * 1024 * 1024

Z_Q = 0
Z_K = Z_Q + SWA_Q_COLS
Z_V = Z_K + SWA_KV_COLS
Z_GDN = Z_V + SWA_KV_COLS
Z_GATE = Z_GDN + GDN_CONV_CH
Z_QM = Z_GATE + GDN_V_COLS
Z_AB = Z_QM + MEM_Q_COLS
Z_COLS = Z_AB + LANES

INPROJ_TILE = 1024
ROW_TILE = 512
MOE_TILE = 512
MOE_BLK = 512
PERM_CHUNK = 256
RUN_ALIGN = 16
BLK_ROWS = -(-(MOE_BLK * TOP_K + N_EXPERTS * (RUN_ALIGN - 1)) // PERM_CHUNK) * PERM_CHUNK


def _cparams(*sem):
    return pltpu.CompilerParams(dimension_semantics=sem, vmem_limit_bytes=VMEM_LIMIT)


def _bdot(a, b):
    return jnp.dot(a.astype(BF16), b.astype(BF16), preferred_element_type=F32)


def _bdot_nt(a, b):
    return lax.dot_general(a.astype(BF16), b.astype(BF16), (((1,), (1,)), ((), ())),
                           preferred_element_type=F32)


def _split2(x):
    hi = x.astype(BF16)
    lo = (x - hi.astype(F32)).astype(BF16)
    return hi, lo


def _split3(x):
    hi = x.astype(BF16)
    r = x - hi.astype(F32)
    mid = r.astype(BF16)
    lo = (r - mid.astype(F32)).astype(BF16)
    return hi, mid, lo


def _rms_rows(x, g):
    ms = jnp.mean(x * x, axis=-1, keepdims=True)
    return x * lax.rsqrt(ms + EPS) * g


def _inproj_kernel(x_ref, g_ref, w_ref, z_ref):
    n = _rms_rows(x_ref[...], g_ref[...])
    z_ref[...] = jnp.dot(n.astype(BF16), w_ref[...], preferred_element_type=F32)


def _inproj(x2d, ln_g, w_z):
    t, d = x2d.shape
    tm = min(INPROJ_TILE, t)
    return pl.pallas_call(
        _inproj_kernel,
        grid=(t // tm,),
        in_specs=[pl.BlockSpec((tm, d), lambda i: (i, 0)),
                  pl.BlockSpec((1, d), lambda i: (0, 0)),
                  pl.BlockSpec((d, Z_COLS), lambda i: (0, 0))],
        out_specs=pl.BlockSpec((tm, Z_COLS), lambda i: (i, 0)),
        out_shape=jax.ShapeDtypeStruct((t, Z_COLS), F32),
        compiler_params=_cparams("parallel"),
        name="inproj",
    )(x2d, ln_g.reshape(1, d), w_z)


def _outproj_kernel(x_ref, os_ref, od_ref, om_ref, wo_ref, g_ref, rwh_ref, rwl_ref, rb_ref,
                    *refs, n_own):
    h_ref, hn_ref, lg_ref = refs[-3:]
    i = pl.program_id(0)

    @pl.when(i < n_own)
    def _():
        n_s = os_ref.shape[1]
        n_d = od_ref.shape[1]
        tm = x_ref.shape[0]
        halves = [slice(k * (tm // 2), (k + 1) * (tm // 2)) for k in range(2)]
        hs = []
        for rows in halves:
            h = x_ref[rows, :]
            h = h + jnp.dot(os_ref[rows, :].astype(BF16), wo_ref[0:n_s, :], preferred_element_type=F32)
            h = h + jnp.dot(od_ref[rows, :].astype(BF16), wo_ref[n_s:n_s + n_d, :], preferred_element_type=F32)
            h = h + jnp.dot(om_ref[rows, :].astype(BF16), wo_ref[n_s + n_d:, :], preferred_element_type=F32)
            hs.append(h)
        for rows, h in zip(halves, hs):
            h_ref[rows, :] = h
            hn = _rms_rows(h, g_ref[...])
            hn_ref[rows, :] = hn.astype(BF16)
            hi, lo = _split2(hn)
            lg = (jnp.dot(hi, rwh_ref[...], preferred_element_type=F32)
                  + jnp.dot(lo, rwh_ref[...], preferred_element_type=F32)
                  + jnp.dot(hi, rwl_ref[...], preferred_element_type=F32))
            lg_ref[rows, :] = lg + rb_ref[...]

    @pl.when(i >= n_own)
    def _():
        hn_ref[...] = jnp.zeros_like(hn_ref)
        lg_ref[...] = jnp.zeros_like(lg_ref)


def _outproj(x2d, o_s, o_d, o_m, w_o, ln_g, rw_hi, rw_lo, rb, t_all, row0, prev=None):
    t, d = x2d.shape
    tm = min(ROW_TILE, t)
    blk0 = row0 // tm
    n_own = t // tm
    n_steps = n_own if prev is not None else t_all // tm
    row = lambda i: (jnp.minimum(i, n_own - 1), 0)
    row_off = lambda i: (i + blk0, 0)
    const = lambda i: (0, 0)
    in_specs = [pl.BlockSpec((tm, d), row),
                pl.BlockSpec((tm, o_s.shape[1]), row),
                pl.BlockSpec((tm, o_d.shape[1]), row),
                pl.BlockSpec((tm, o_m.shape[1]), row),
                pl.BlockSpec((d, d), const),
                pl.BlockSpec((1, d), const),
                pl.BlockSpec((d, LANES), const),
                pl.BlockSpec((d, LANES), const),
                pl.BlockSpec((1, LANES), const)]
    args = [x2d, o_s, o_d, o_m, w_o, ln_g.reshape(1, d), rw_hi, rw_lo, rb]
    aliases = {}
    if prev is not None:
        in_specs += [pl.BlockSpec(memory_space=pl.ANY), pl.BlockSpec(memory_space=pl.ANY)]
        aliases = {len(args): 1, len(args) + 1: 2}
        args += list(prev)
    return pl.pallas_call(
        functools.partial(_outproj_kernel, n_own=n_own),
        grid=(n_steps,),
        in_specs=in_specs,
        out_specs=[pl.BlockSpec((tm, d), row),
                   pl.BlockSpec((tm, d), row_off),
                   pl.BlockSpec((tm, LANES), row_off)],
        out_shape=[jax.ShapeDtypeStruct((t, d), F32),
                   jax.ShapeDtypeStruct((t_all, d), BF16),
                   jax.ShapeDtypeStruct((t_all, LANES), F32)],
        input_output_aliases=aliases,
        compiler_params=_cparams("arbitrary"),
        name="outproj_router",
    )(*args)


def _route_kernel(lg_ref, pos_ref, post_ref, g_ref, cnt_ref):
    tm = lg_ref.shape[0]
    lane = lax.broadcasted_iota(I32, (tm, LANES), 1).astype(F32)
    l = jnp.where(lane < N_EXPERTS, lg_ref[...], -jnp.inf)
    vals, idxs = [], []
    for _k in range(TOP_K):
        m = jnp.max(l, axis=-1, keepdims=True)
        idx = jnp.min(jnp.where(l == m, lane, float(LANES)), axis=-1, keepdims=True)
        l = jnp.where(lane == idx, -jnp.inf, l)
        vals.append(m)
        idxs.append(idx)
    ex = [jnp.exp(v - vals[0]) for v in vals]
    den = ex[0] + ex[1] + ex[2] + ex[3]
    member = jnp.zeros((tm, LANES), F32)
    for idx in idxs:
        member = member + jnp.where(lane == idx, 1.0, 0.0)
    ri = lax.broadcasted_iota(I32, (tm, tm), 0)
    ci = lax.broadcasted_iota(I32, (tm, tm), 1)
    strict = jnp.where(ci < ri, 1.0, 0.0).astype(BF16)
    prefix = jnp.dot(strict, member.astype(BF16), preferred_element_type=F32)
    cnt = jnp.sum(member, axis=0, keepdims=True)
    cpad = jnp.ceil(cnt * (1.0 / RUN_ALIGN)) * float(RUN_ALIGN)
    c_hi = jnp.floor(cpad * (1.0 / BF16_EXACT_INT))
    c_lo = cpad - BF16_EXACT_INT * c_hi
    ej = lax.broadcasted_iota(I32, (LANES, LANES), 0)
    ee = lax.broadcasted_iota(I32, (LANES, LANES), 1)
    before = jnp.where(ej < ee, 1.0, 0.0).astype(BF16)
    bcast = lambda v: jnp.broadcast_to(v, (SUBLANES, LANES)).astype(BF16)
    off = (BF16_EXACT_INT * jnp.dot(bcast(c_hi), before, preferred_element_type=F32)
           + jnp.dot(bcast(c_lo), before, preferred_element_type=F32))[0:1]
    where_in_run = prefix + off
    p_out = jnp.zeros((tm, LANES), F32)
    g_out = jnp.zeros((tm, LANES), F32)
    for k in range(TOP_K):
        pos = jnp.sum(jnp.where(lane == idxs[k], where_in_run, 0.0), axis=-1, keepdims=True)
        p_out = jnp.where(lane == float(k), pos, p_out)
        g_out = jnp.where(lane == float(k), ex[k] / den, g_out)
    pos_ref[...] = p_out[:, :TOP_K]
    post_ref[...] = p_out.T[:SUBLANES, :]
    g_ref[...] = g_out[:, :TOP_K]
    cnt_ref[0] = cnt


def _route(logits):
    t = logits.shape[0]
    tm = MOE_BLK
    nb = t // tm
    return pl.pallas_call(
        _route_kernel,
        grid=(nb,),
        in_specs=[pl.BlockSpec((tm, LANES), lambda i: (i, 0))],
        out_specs=[pl.BlockSpec((tm, TOP_K), lambda i: (i, 0)),
                   pl.BlockSpec((SUBLANES, tm), lambda i: (0, i)),
                   pl.BlockSpec((tm, TOP_K), lambda i: (i, 0)),
                   pl.BlockSpec((1, 1, LANES), lambda i: (i, 0, 0))],
        out_shape=[jax.ShapeDtypeStruct((t, TOP_K), F32),
                   jax.ShapeDtypeStruct((SUBLANES, t), F32),
                   jax.ShapeDtypeStruct((t, TOP_K), F32),
                   jax.ShapeDtypeStruct((nb, 1, LANES), F32)],
        compiler_params=_cparams("parallel"),
        name="route",
    )(logits)


def _run_copies(n, max_rows, src_ref, src0, dst_ref, dst0, sem, wait):
    pos = 0
    bit = max_rows
    while bit >= RUN_ALIGN:
        take = (n & bit) != 0

        def go(pos=pos, bit=bit):
            cp = pltpu.make_async_copy(src_ref.at[pl.ds(pl.multiple_of(src0 + pos, RUN_ALIGN), bit)],
                                       dst_ref.at[pl.ds(pl.multiple_of(dst0 + pos, RUN_ALIGN), bit)], sem)
            cp.wait() if wait else cp.start()

        pl.when(take)(go)
        pos = pos + jnp.where(take, bit, 0)
        bit //= 2


RUN_SIZES = tuple(MOE_BLK >> i for i in range((MOE_BLK // RUN_ALIGN).bit_length()))


def _piece_copies(b, cnt_ref, loc_ref, glob_ref, local_ref, global_hbm, sem, to_global, wait):
    for c, rows in enumerate(RUN_SIZES):
        base = b * len(RUN_SIZES) + c

        def body(s, carry, rows=rows, base=base):
            j = base * N_EXPERTS + s
            loc = local_ref.at[pl.ds(pl.multiple_of(loc_ref[j], RUN_ALIGN), rows)]
            glob = global_hbm.at[pl.ds(pl.multiple_of(glob_ref[j], RUN_ALIGN), rows)]
            cp = pltpu.make_async_copy(loc, glob, sem) if to_global else pltpu.make_async_copy(glob, loc, sem)
            cp.wait() if wait else cp.start()
            return carry

        lax.fori_loop(0, cnt_ref[base], body, 0)


def _dispatch_kernel(cnt_ref, loc_ref, glob_ref, rows_ref, estart_ref, elen_ref, nused_ref,
                     hn_ref, post_ref, xs_hbm, buf_ref, zero_ref, sem, zsem):
    b = pl.program_id(0)
    nb = pl.num_programs(0)
    slot = b % 2
    tm = hn_ref.shape[0]
    x = hn_ref[...].astype(BF16)
    post = post_ref[...]
    P = PERM_CHUNK

    def sort_rows(c):
        r = (lax.broadcasted_iota(I32, (P, tm), 0) + c * P).astype(F32)
        sel = jnp.zeros((P, tm), F32)
        for k in range(TOP_K):
            sel = jnp.where(r == post[k:k + 1, :], 1.0, sel)
        buf_ref[slot, c * P:(c + 1) * P, :] = jnp.dot(sel.astype(BF16), x,
                                                      preferred_element_type=F32).astype(BF16)

    for c in range(BLK_ROWS // P):
        if c * P < tm * TOP_K:
            sort_rows(c)
        else:
            pl.when(rows_ref[b] > c * P)(functools.partial(sort_rows, c))

    def runs(blk, s, wait):
        _piece_copies(blk, cnt_ref, loc_ref, glob_ref, buf_ref.at[s], xs_hbm, sem.at[s], True, wait)

    runs(b, slot, False)

    @pl.when(b == 0)
    def _():
        zero_ref[...] = jnp.zeros_like(zero_ref)

        def tail(wait):
            def body(e, c):
                n = (MOE_TILE - elen_ref[e] % MOE_TILE) % MOE_TILE
                _run_copies(n, MOE_TILE // 2, zero_ref, 0, xs_hbm, estart_ref[e] + elen_ref[e], zsem, wait)
                return c
            lax.fori_loop(0, N_EXPERTS, body, 0)

            def free_tile(ti, c):
                for half in range(2):
                    _run_copies(MOE_TILE // 2, MOE_TILE // 2, zero_ref, 0, xs_hbm,
                                ti * MOE_TILE + half * (MOE_TILE // 2), zsem, wait)
                return c
            lax.fori_loop(nused_ref[0], xs_hbm.shape[0] // MOE_TILE, free_tile, 0)

        tail(False)
        tail(True)

    pl.when(b > 0)(lambda: runs(b - 1, 1 - slot, True))
    pl.when(b == nb - 1)(lambda: runs(b, slot, True))


def _dispatch(hn, post, piece_cnt, piece_loc, piece_glob, blk_rows, e_start, e_len, n_used, n_rows):
    t, d = hn.shape
    grid_spec = pltpu.PrefetchScalarGridSpec(
        num_scalar_prefetch=7,
        grid=(t // MOE_BLK,),
        in_specs=[pl.BlockSpec((MOE_BLK, d), lambda i, *_: (i, 0)),
                  pl.BlockSpec((SUBLANES, MOE_BLK), lambda i, *_: (0, i))],
        out_specs=pl.BlockSpec(memory_space=pl.ANY),
        scratch_shapes=[pltpu.VMEM((2, BLK_ROWS, d), BF16),
                        pltpu.VMEM((MOE_TILE, d), BF16),
                        pltpu.SemaphoreType.DMA((2,)),
                        pltpu.SemaphoreType.DMA(())],
    )
    return pl.pallas_call(
        _dispatch_kernel,
        grid_spec=grid_spec,
        out_shape=jax.ShapeDtypeStruct((n_rows, d), BF16),
        compiler_params=_cparams("arbitrary"),
        name="dispatch",
    )(piece_cnt, piece_loc, piece_glob, blk_rows, e_start, e_len, n_used, hn, post)


def _expert_kernel(te_ref, nu_ref, nxt_ref, slot_ref, x_ref, w1_hbm, b1_ref, w2_hbm, b2_ref, y_ref,
                   w1f_ref, w2f_ref, w1b_ref, w2b_ref, sem):
    i = pl.program_id(0)
    live = i < nu_ref[0]
    e = te_ref[i]
    s = slot_ref[e]

    def weight_copies(expert, slot):
        return (pltpu.make_async_copy(w1_hbm.at[expert], w1f_ref.at[slot], sem.at[0, slot]),
                pltpu.make_async_copy(w2_hbm.at[expert], w2f_ref.at[slot], sem.at[1, slot]))

    @pl.when(live & (i == 0))
    def _():
        for cp in weight_copies(e, s):
            cp.start()

    @pl.when(live & ((i == 0) | (e != te_ref[jnp.maximum(i - 1, 0)])))
    def _():
        for cp in weight_copies(e, s):
            cp.wait()
        w1b_ref[...] = w1f_ref[s].astype(BF16)
        w2b_ref[...] = w2f_ref[s].astype(BF16)

        @pl.when(nxt_ref[e] >= 0)
        def _():
            for cp in weight_copies(nxt_ref[e], 1 - s):
                cp.start()

    @pl.when(live)
    def _():
        f = w2b_ref.shape[0]
        h = jnp.dot(x_ref[...], w1b_ref[...], preferred_element_type=F32) + b1_ref[0]
        glu = jnp.minimum(h[:, :f], SWIGLU_LIMIT)
        lin = jnp.clip(h[:, f:], -SWIGLU_LIMIT, SWIGLU_LIMIT)
        act = glu * jax.nn.sigmoid(SWIGLU_ALPHA * glu) * (lin + 1.0)
        y = jnp.dot(act.astype(BF16), w2b_ref[...], preferred_element_type=F32) + b2_ref[0]
        y_ref[...] = y.astype(BF16)

    @pl.when(i >= nu_ref[0])
    def _():
        y_ref[...] = jnp.zeros_like(y_ref)


def _experts(xs, tile_expert, n_used, next_expert, expert_slot, w1, b1, w2, b2):
    n_rows, d = xs.shape
    f2 = w1.shape[2]
    f = w2.shape[1]
    n_tiles = n_rows // MOE_TILE
    live = lambda i, te, nu, *_: (jnp.minimum(i, nu[0] - 1), 0)
    every = lambda i, *_: (i, 0)
    wsel = lambda i, te, *_: (te[i], 0, 0)
    grid_spec = pltpu.PrefetchScalarGridSpec(
        num_scalar_prefetch=4,
        grid=(n_tiles,),
        in_specs=[pl.BlockSpec((MOE_TILE, d), live),
                  pl.BlockSpec(memory_space=pl.ANY),
                  pl.BlockSpec((1, 1, f2), wsel),
                  pl.BlockSpec(memory_space=pl.ANY),
                  pl.BlockSpec((1, 1, d), wsel)],
        out_specs=pl.BlockSpec((MOE_TILE, d), every),
        scratch_shapes=[pltpu.VMEM((2, d, f2), F32), pltpu.VMEM((2, f, d), F32),
                        pltpu.VMEM((d, f2), BF16), pltpu.VMEM((f, d), BF16),
                        pltpu.SemaphoreType.DMA((2, 2))],
    )
    return pl.pallas_call(
        _expert_kernel,
        grid_spec=grid_spec,
        out_shape=jax.ShapeDtypeStruct((n_rows, d), BF16),
        compiler_params=_cparams("arbitrary"),
        name="experts",
    )(tile_expert, n_used, next_expert, expert_slot, xs, w1, b1, w2, b2)


def _combine_kernel(cnt_ref, loc_ref, glob_ref, rows_ref, h_ref, pos_ref, g_ref, yb_hbm, y_ref, buf_ref, sem,
                    *, blk0):
    i = pl.program_id(0)
    n_steps = pl.num_programs(0)
    b = i + blk0
    slot = i % 2
    tm, d = h_ref.shape

    def runs(blk, s, wait):
        _piece_copies(blk, cnt_ref, loc_ref, glob_ref, buf_ref.at[s], yb_hbm, sem.at[s], False, wait)

    @pl.when(i == 0)
    def _():
        buf_ref[...] = jnp.zeros_like(buf_ref)
        runs(b, slot, False)

    pl.when(i + 1 < n_steps)(lambda: runs(b + 1, 1 - slot, False))
    runs(b, slot, True)
    pos = pos_ref[...]
    g = g_ref[...]
    P = PERM_CHUNK

    def weighted_rows(c):
        col = (lax.broadcasted_iota(I32, (tm, P), 1) + c * P).astype(F32)
        wgt = jnp.zeros((tm, P), F32)
        for k in range(TOP_K):
            wgt = jnp.where(col == pos[:, k:k + 1], g[:, k:k + 1], wgt)
        return jnp.dot(wgt.astype(BF16), buf_ref[slot, c * P:(c + 1) * P, :], preferred_element_type=F32)

    n_sure = tm * TOP_K // P
    y = h_ref[...]
    for c in range(n_sure):
        y = y + weighted_rows(c)
    y_ref[...] = y
    for c in range(n_sure, BLK_ROWS // P):
        @pl.when(rows_ref[b] > c * P)
        def _(c=c):
            y_ref[...] += weighted_rows(c)


def _combine(h, pos, gates, piece_cnt, piece_loc, piece_glob, blk_rows, yb, blk0):
    t, d = h.shape
    grid_spec = pltpu.PrefetchScalarGridSpec(
        num_scalar_prefetch=4,
        grid=(t // MOE_BLK,),
        in_specs=[pl.BlockSpec((MOE_BLK, d), lambda i, *_: (i, 0)),
                  pl.BlockSpec((MOE_BLK, TOP_K), lambda i, *_: (i + blk0, 0)),
                  pl.BlockSpec((MOE_BLK, TOP_K), lambda i, *_: (i + blk0, 0)),
                  pl.BlockSpec(memory_space=pl.ANY)],
        out_specs=pl.BlockSpec((MOE_BLK, d), lambda i, *_: (i, 0)),
        scratch_shapes=[pltpu.VMEM((2, BLK_ROWS, d), BF16),
                        pltpu.SemaphoreType.DMA((2,))],
    )
    return pl.pallas_call(
        functools.partial(_combine_kernel, blk0=blk0),
        grid_spec=grid_spec,
        out_shape=jax.ShapeDtypeStruct((t, d), F32),
        compiler_params=_cparams("arbitrary"),
        name="combine",
    )(piece_cnt, piece_loc, piece_glob, blk_rows, h, pos, gates, yb)


def _moe(hn_all, logits_all, h_parts, w1, b1, w2, b2):
    t_all = hn_all.shape[0]
    nb = t_all // MOE_BLK
    pos, post, gates, counts_f = _route(logits_all)
    cnt = counts_f.reshape(nb, LANES)[:, :N_EXPERTS].astype(I32)
    seg_len = (cnt + RUN_ALIGN - 1) // RUN_ALIGN * RUN_ALIGN
    before_e = jnp.arange(N_EXPERTS)[:, None] < jnp.arange(N_EXPERTS)[None, :]
    before_b = jnp.arange(nb)[None, :] < jnp.arange(nb)[:, None]
    seg_off = jnp.sum(jnp.where(before_e[None], seg_len[:, :, None], 0), axis=1)
    e_len = jnp.sum(seg_len, axis=0)
    e_tiles = (e_len + MOE_TILE - 1) // MOE_TILE
    tile_start = jnp.sum(jnp.where(before_e, e_tiles[:, None], 0), axis=0)
    tile_end = tile_start + e_tiles
    e_start = tile_start * MOE_TILE
    seg_dst = e_start[None, :] + jnp.sum(jnp.where(before_b[:, :, None], seg_len[None], 0), axis=1)
    max_rows = t_all * TOP_K + nb * N_EXPERTS * (RUN_ALIGN - 1) + N_EXPERTS * (MOE_TILE - RUN_ALIGN)
    n_tiles = -(-max_rows // MOE_TILE)
    n_rows = n_tiles * MOE_TILE
    n_used = tile_end[-1:].astype(I32)
    tile_expert = jnp.minimum(jnp.sum(tile_end[None, :] <= jnp.arange(n_tiles, dtype=I32)[:, None], axis=1),
                              N_EXPERTS - 1).astype(I32)
    sizes = jnp.array(RUN_SIZES, I32)[None, :, None]
    n_run = seg_len[:, None, :]
    has = (n_run & sizes) != 0
    piece_at = n_run & ~(2 * sizes - 1)
    rank = jnp.sum(jnp.where(before_e[None, None], has[:, :, :, None], False), axis=2)
    slot = jnp.arange(N_EXPERTS)
    put = has[..., None] & (rank[..., None] == slot)
    listed = lambda v: jnp.sum(jnp.where(put, v[..., None], 0), axis=2)
    piece_loc = listed(seg_off[:, None, :] + piece_at)
    piece_glob = listed(seg_dst[:, None, :] + piece_at)
    piece_cnt = jnp.sum(has, axis=2)
    flat = lambda a: a.reshape(-1).astype(I32)
    tables = (flat(piece_cnt), flat(piece_loc), flat(piece_glob), flat(jnp.sum(seg_len, axis=1)))
    xs = _dispatch(hn_all, post, *tables, flat(e_start), flat(e_len), n_used, n_rows)
    e_ids = jnp.arange(N_EXPERTS)
    used = e_tiles > 0
    next_expert = jnp.min(jnp.where(before_e & used[None, :], e_ids[None, :], N_EXPERTS), axis=1)
    next_expert = jnp.where(next_expert == N_EXPERTS, -1, next_expert)
    expert_slot = jnp.sum(jnp.where(before_e & used[:, None], 1, 0), axis=0) % 2
    yb = _experts(xs, tile_expert, n_used, flat(next_expert), flat(expert_slot), w1, b1, w2, b2)
    outs = []
    row = 0
    for h in h_parts:
        outs.append(_combine(h, pos, gates, *tables, yb, row // MOE_BLK))
        row += h.shape[0]
    return outs


GDN_ROWS = 4 * GDN_CHUNK
CONV_HALO = SUBLANES
NEUMANN_SPLIT = 2
AB_LANES = 2 * N_GDN_HEADS
SOLVE_ROWS = 2 * GDN_CHUNK


def _softplus(x):
    return jnp.maximum(x, 0.0) + jnp.log1p(jnp.exp(-jnp.abs(x)))


def _gdn_prompt_kernel(u_ref, ab_ref, gate_ref, cw_ref, alog_ref, dtb_ref, ng_ref, o_ref, s_ref, ubuf_ref):
    step = pl.program_id(0)
    NB = u_ref.shape[0]
    R = GDN_ROWS
    C = GDN_CHUNK
    NC = R // C

    @pl.when(step == 0)
    def _():
        ubuf_ref[:, 0:CONV_HALO, :] = jnp.zeros((NB, CONV_HALO, ubuf_ref.shape[2]), F32)
        s_ref[...] = jnp.zeros_like(s_ref)

    ri = lax.broadcasted_iota(I32, (R, R), 0)
    ci = lax.broadcasted_iota(I32, (R, R), 1)
    shift = C.bit_length() - 1
    same = lax.shift_right_logical(ri, shift) == lax.shift_right_logical(ci, shift)
    incl = same & (ci <= ri)
    strict = same & (ci < ri)
    tri = jnp.where(incl, 1.0, 0.0).astype(BF16)
    blk = jnp.where(same, 1.0, 0.0).astype(BF16)
    cw = cw_ref[...]
    ng = ng_ref[...]

    lane = lax.broadcasted_iota(I32, (R, LANES), 1)
    lane1 = lax.broadcasted_iota(I32, (1, LANES), 1)
    ab = ab_ref[0]
    alog = alog_ref[...]
    dtb = dtb_ref[...]
    for b in range(1, NB):
        own = (lane >= b * AB_LANES) & (lane < (b + 1) * AB_LANES)
        own1 = (lane1 >= b * AB_LANES) & (lane1 < (b + 1) * AB_LANES)
        ab = jnp.where(own, pltpu.roll(ab_ref[b], b * AB_LANES, 1), ab)
        alog = jnp.where(own1, pltpu.roll(alog_ref[...], b * AB_LANES, 1), alog)
        dtb = jnp.where(own1, pltpu.roll(dtb_ref[...], b * AB_LANES, 1), dtb)
    g_t = -jnp.exp(alog) * _softplus(ab + dtb)
    beta_t = jax.nn.sigmoid(ab)
    both = jnp.concatenate([tri, blk], axis=0)
    sums = sum(jnp.dot(both, p, preferred_element_type=F32) for p in _split3(g_t))
    gcum, gtot = sums[:R], sums[R:]
    gcum_t = gcum.T

    def conv_and_norms(b):
        u = u_ref[b]
        ubuf_ref[b, CONV_HALO:CONV_HALO + R, :] = u
        y = u * cw[GDN_CONV - 1:GDN_CONV, :]
        for j in range(1, GDN_CONV):
            y = y + ubuf_ref[b, CONV_HALO - j:CONV_HALO - j + R, :] * cw[GDN_CONV - 1 - j:GDN_CONV - j, :]
        ubuf_ref[b, 0:CONV_HALO, :] = u[R - CONV_HALO:, :]
        qkv = y * jax.nn.sigmoid(y)
        qk_n = []
        for t in range(2 * GDN_QK_COLS // LANES):
            x = qkv[:, t * LANES:(t + 1) * LANES]
            x = x * lax.rsqrt(_pair_sumsq(x) + EPS)
            qk_n.append(x * (GDN_DK ** -0.5) if t < GDN_QK_COLS // LANES else x)
        return qkv, jnp.concatenate(qk_n, axis=-1)

    rhs_tiles = {}

    def make_chain(b, h, qkv, qk_n):
        q = qk_n[:, h * GDN_DK:(h + 1) * GDN_DK]
        k = qk_n[:, GDN_QK_COLS + h * GDN_DK:GDN_QK_COLS + (h + 1) * GDN_DK]
        col = b * AB_LANES + h
        gc = gcum[:, col:col + 1]
        gt = gtot[:, col:col + 1]
        beta = beta_t[:, col + N_GDN_HEADS:col + N_GDN_HEADS + 1]
        pair = (b, h // 2)
        if pair not in rhs_tiles:
            c0 = b * AB_LANES + 2 * (h // 2)
            first = _first_half((R, LANES))
            per_lane = lambda m, off: jnp.where(first, m[:, c0 + off:c0 + off + 1], m[:, c0 + off + 1:c0 + off + 2])
            beta_l = per_lane(beta_t, N_GDN_HEADS)
            t0 = (h // 2) * LANES
            gc_l, gt_l = per_lane(gcum, 0), per_lane(gtot, 0)
            e_gc = jnp.exp(gc_l)
            k_tile = qk_n[:, GDN_QK_COLS + t0:GDN_QK_COLS + t0 + LANES]
            vb = qkv[:, 2 * GDN_QK_COLS + t0:2 * GDN_QK_COLS + t0 + LANES] * beta_l
            kb = k_tile * (beta_l * e_gc)
            rhs_tiles[pair] = (jnp.where(first, vb, pltpu.roll(kb, GDN_DK, 1)),
                               jnp.where(first, pltpu.roll(vb, GDN_DV, 1), kb),
                               qk_n[:, t0:t0 + LANES] * e_gc,
                               k_tile * jnp.exp(gt_l - gc_l))
        r = rhs_tiles[pair][h % 2]
        half = slice((h % 2) * GDN_DK, (h % 2 + 1) * GDN_DK)
        q_dec = rhs_tiles[pair][2][:, half]
        k_dec = rhs_tiles[pair][3][:, half]
        a_blocks, qk_blocks, r_blocks = [], [], []
        for sb in range(R // SOLVE_ROWS):
            rows = slice(sb * SOLVE_ROWS, (sb + 1) * SOLVE_ROWS)
            decay = jnp.exp(jnp.where(incl[:SOLVE_ROWS, :SOLVE_ROWS], gc[rows] - gcum_t[col:col + 1, rows], -jnp.inf))
            a_blocks.append(jnp.where(strict[:SOLVE_ROWS, :SOLVE_ROWS],
                                      beta[rows] * _bdot_nt(k[rows], k[rows]) * decay, 0.0))
            qk_blocks.append(_bdot_nt(q[rows], k[rows]) * decay)
            r_blocks.append(r[rows])
        return dict(b=b, h=h, a=a_blocks, qk=qk_blocks, r=r_blocks,
                    q_dec=q_dec, k_dec=k_dec, g_last=jnp.exp(gt))

    dot = lambda x, y: jnp.dot(x, y, preferred_element_type=F32)
    n_levels = C.bit_length() - 1

    def neumann_level(chains, j):
        blocks = [(ch, sb) for ch in chains for sb in range(len(ch['a']))]
        pieces = []
        for ch, sb in blocks:
            a = ch['a'][sb]
            a_hi, a_lo = _split2(a) if j < NEUMANN_SPLIT else (a.astype(BF16), None)
            pieces.append((a_hi, a_lo))
            if j + 1 < n_levels:
                sq = dot(a_hi, a_hi)
                if j + 1 < NEUMANN_SPLIT:
                    sq = sq + dot(a_hi, a_lo) + dot(a_lo, a_hi)
                ch['a'][sb] = sq
        for (ch, sb), (a_hi, a_lo) in zip(blocks, pieces):
            r = ch['r'][sb]
            if j < NEUMANN_SPLIT:
                r_hi, r_lo = _split2(r)
                upd = dot(a_hi, r_hi) + dot(a_hi, r_lo) + dot(a_lo, r_hi)
            else:
                upd = dot(a_hi, r.astype(BF16))
            ch['r'][sb] = r - upd if j == 0 else r + upd

    def chunk_begin(chains):
        for ch in chains:
            r = jnp.concatenate(ch['r'], axis=0)
            ch['u'], ch['w'] = r[:, :GDN_DV], r[:, GDN_DV:]
            ch['S'] = s_ref[ch['b'], ch['h']]
            ch['k_dec_t'] = ch['k_dec'].T
            ch['outs'] = []

    def chunk_step(chains, c):
        sl = slice(c * C, (c + 1) * C)
        per = SOLVE_ROWS // C
        loc = slice((c % per) * C, (c % per + 1) * C)
        from_state = [(_bdot(ch['w'][sl], ch['S']), _bdot(ch['q_dec'][sl], ch['S'])) for ch in chains]
        for ch, (w_s, q_s) in zip(chains, from_state):
            v_new = ch['u'][sl] - w_s
            ch['outs'].append(q_s + _bdot(ch['qk'][c // per][loc, loc], v_new))
            ch['S'] = ch['S'] * ch['g_last'][c * C:c * C + 1, :] + _bdot(ch['k_dec_t'][:, sl], v_new)

    def chunk_end(chains):
        for ch in chains:
            b, h = ch['b'], ch['h']
            s_ref[b, h] = ch['S']
            o = jnp.concatenate(ch['outs'], axis=0)
            o = o * lax.rsqrt(jnp.mean(o * o, axis=-1, keepdims=True) + EPS) * ng
            gh = gate_ref[b, :, h * GDN_DV:(h + 1) * GDN_DV]
            o_ref[b, :, h * GDN_DV:(h + 1) * GDN_DV] = (o * (gh * jax.nn.sigmoid(gh))).astype(o_ref.dtype)

    chains = []
    for b in range(NB):
        qkv, qk_n = conv_and_norms(b)
        chains += [make_chain(b, h, qkv, qk_n) for h in range(N_GDN_HEADS)]
    for j in range(n_levels):
        neumann_level(chains, j)
    chunk_begin(chains)
    for c in range(NC):
        chunk_step(chains, c)
    chunk_end(chains)


def _gdn_prompt(z3, conv_w, a_log, dt_bias, norm_g):
    B, S, _ = z3.shape
    R = GDN_ROWS
    lanes4 = lambda a: jnp.pad(a, (0, LANES - a.shape[0])).reshape(1, LANES)
    return pl.pallas_call(
        _gdn_prompt_kernel,
        grid=(S // R,),
        in_specs=[pl.BlockSpec((B, R, GDN_CONV_CH), lambda s: (0, s, Z_GDN // GDN_CONV_CH)),
                  pl.BlockSpec((B, R, LANES), lambda s: (0, s, Z_AB // LANES)),
                  pl.BlockSpec((B, R, GDN_V_COLS), lambda s: (0, s, Z_GATE // GDN_V_COLS)),
                  pl.BlockSpec((GDN_CONV, GDN_CONV_CH), lambda s: (0, 0)),
                  pl.BlockSpec((1, LANES), lambda s: (0, 0)),
                  pl.BlockSpec((1, LANES), lambda s: (0, 0)),
                  pl.BlockSpec((1, GDN_DV), lambda s: (0, 0))],
        out_specs=[pl.BlockSpec((B, R, GDN_V_COLS), lambda s: (0, s, 0)),
                   pl.BlockSpec((B, N_GDN_HEADS, GDN_DK, GDN_DV), lambda s: (0, 0, 0, 0))],
        out_shape=[jax.ShapeDtypeStruct((B, S, GDN_V_COLS), BF16),
                   jax.ShapeDtypeStruct((B, N_GDN_HEADS, GDN_DK, GDN_DV), F32)],
        scratch_shapes=[pltpu.VMEM((B, CONV_HALO + R, GDN_CONV_CH), F32)],
        compiler_params=_cparams("arbitrary"),
        name="gdn_prompt",
    )(z3, z3, z3, conv_w, lanes4(a_log), lanes4(dt_bias), norm_g.reshape(1, GDN_DV))


def _pair_sumsq(x):
    li = lax.broadcasted_iota(I32, (LANES, LANES), 0) // HEAD_DIM
    lj = lax.broadcasted_iota(I32, (LANES, LANES), 1) // HEAD_DIM
    same = jnp.where(li == lj, 1.0, 0.0).astype(BF16)
    hi, lo = _split2(x * x)
    return jnp.dot(hi, same, preferred_element_type=F32) + jnp.dot(lo, same, preferred_element_type=F32)


def _pair_rms(x, g):
    return x * lax.rsqrt(_pair_sumsq(x) * (1.0 / HEAD_DIM) + EPS) * g


def _first_half(shape):
    return lax.broadcasted_iota(I32, shape, 1) < HEAD_DIM


LOG2E = 1.4426950408889634


SWA_BLOCKS = 4


def _swa_prompt_kernel(sink_ref, q_ref, kc_ref, kp_ref, vc_ref, vp_ref, qg_ref, kg_ref, o_ref, kn_ref,
                       bias_ref):
    first = (pl.program_id(0) == 0) & (pl.program_id(1) == 0)
    n = pl.program_id(1)
    W = WINDOW
    NQ = SWA_BLOCKS

    @pl.when(first)
    def _():
        qi = lax.broadcasted_iota(I32, (W, 2 * W), 0)
        kj = lax.broadcasted_iota(I32, (W, 2 * W), 1)
        dist = qi + W - kj
        band = (dist >= 0) & (dist < W)
        distf = dist.astype(F32)
        for has_prev in range(2):
            mask = jnp.where(band & ((has_prev == 1) | (kj >= W)), 0.0, -jnp.inf)
            for head in range(N_SWA_HEADS):
                slope = 2.0 ** (-(8.0 / N_SWA_HEADS) * (head + 1))
                bias_ref[has_prev, head] = mask - (slope * LOG2E) * distf

    kg = kg_ref[...]
    qg = qg_ref[...]
    kc = _pair_rms(kc_ref[0], kg)
    kn_ref[0] = kc[(NQ - 1) * W:]
    k3 = jnp.concatenate([_pair_rms(kp_ref[0], kg), kc], axis=0)
    v3 = jnp.concatenate([vp_ref[0], vc_ref[0]], axis=0)
    fh = _first_half(k3.shape)
    k3r = pltpu.roll(k3, HEAD_DIM, 1)
    v3r = pltpu.roll(v3, HEAD_DIM, 1)
    kdup = (jnp.where(fh, k3, k3r).astype(BF16), jnp.where(fh, k3r, k3).astype(BF16))
    vdup = (jnp.where(fh, v3, v3r).astype(BF16), jnp.where(fh, v3r, v3).astype(BF16))
    fq = _first_half((W, LANES))
    kv_of = lambda head: head // SWA_GROUP
    probs = [(j, head) for j in range(NQ) for head in range(N_SWA_HEADS)]
    keys = lambda j: slice(j * W, (j + 2) * W)
    qts = [[_pair_rms(q_ref[0, j * W:(j + 1) * W, t * LANES:(t + 1) * LANES], qg) * (ATTN_SCALE * LOG2E)
            for t in range(SWA_Q_COLS // LANES)] for j in range(NQ)]
    qms = [jnp.where(fq == (head % 2 == 0), qts[j][head // 2], 0.0).astype(BF16) for j, head in probs]
    table = [jnp.where(n > 0, 1, 0)] + [1] * (NQ - 1)
    ss = [_bdot_nt(qms[i], kdup[kv_of(head)][keys(j)]) + bias_ref[table[j], head]
          for i, (j, head) in enumerate(probs)]
    sinks = [sink_ref[head] * LOG2E for head in range(N_SWA_HEADS)]
    ms = [jnp.maximum(jnp.max(ss[i], axis=-1, keepdims=True), sinks[head]) for i, (j, head) in enumerate(probs)]
    ps = [jnp.exp2(ss[i] - ms[i]) for i in range(len(probs))]
    dens = [jnp.sum(ps[i], axis=-1, keepdims=True) + jnp.exp2(sinks[head] - ms[i])
            for i, (j, head) in enumerate(probs)]
    outs = [_bdot(ps[i], vdup[kv_of(head)][keys(j)]) / dens[i] for i, (j, head) in enumerate(probs)]
    for j in range(NQ):
        for t in range(SWA_Q_COLS // LANES):
            o_ref[0, j * W:(j + 1) * W, t * LANES:(t + 1) * LANES] = jnp.where(
                fq, outs[j * N_SWA_HEADS + 2 * t], outs[j * N_SWA_HEADS + 2 * t + 1]).astype(o_ref.dtype)


def _swa_prompt(z3, q_g, k_g, sinks):
    B, S, _ = z3.shape
    W = WINDOW
    NQ = SWA_BLOCKS
    twice = lambda g: jnp.concatenate([g, g]).reshape(1, LANES)
    kcol, vcol = Z_K // LANES, Z_V // LANES
    prev = lambda col: pl.BlockSpec((1, W, LANES), lambda b, n: (b, jnp.maximum(NQ * n - 1, 0), col))
    grid_spec = pltpu.PrefetchScalarGridSpec(
        num_scalar_prefetch=0,
        grid=(B, S // (NQ * W)),
        in_specs=[pl.BlockSpec(memory_space=pltpu.SMEM),
                  pl.BlockSpec((1, NQ * W, SWA_Q_COLS), lambda b, n: (b, n, 0)),
                  pl.BlockSpec((1, NQ * W, LANES), lambda b, n: (b, n, kcol)),
                  prev(kcol),
                  pl.BlockSpec((1, NQ * W, LANES), lambda b, n: (b, n, vcol)),
                  prev(vcol),
                  pl.BlockSpec((1, LANES), lambda b, n: (0, 0)),
                  pl.BlockSpec((1, LANES), lambda b, n: (0, 0))],
        out_specs=[pl.BlockSpec((1, NQ * W, SWA_Q_COLS), lambda b, n: (b, n, 0)),
                   pl.BlockSpec((1, W, LANES), lambda b, n: (b, 0, 0))],
        scratch_shapes=[pltpu.VMEM((2, N_SWA_HEADS, W, 2 * W), F32)],
    )
    return pl.pallas_call(
        _swa_prompt_kernel,
        grid_spec=grid_spec,
        out_shape=[jax.ShapeDtypeStruct((B, S, SWA_Q_COLS), BF16),
                   jax.ShapeDtypeStruct((B, W, LANES), F32)],
        compiler_params=_cparams("arbitrary", "arbitrary"),
        name="swa_prompt",
    )(sinks, z3, z3, z3, z3, z3, twice(q_g), twice(k_g))


def _mem_kv_kernel(m_ref, g_ref, w_ref, kg_ref, k_ref, v_ref):
    n = _rms_rows(m_ref[...], g_ref[...])
    kv = jnp.dot(n.astype(BF16), w_ref[...], preferred_element_type=F32)
    kg = kg_ref[...]
    for t in range(MEM_Q_COLS // LANES):
        k_ref[:, t * LANES:(t + 1) * LANES] = _pair_rms(kv[:, t * LANES:(t + 1) * LANES], kg)
    v_ref[...] = kv[:, MEM_Q_COLS:]


def _mem_kv(mem2d, ln_g, w_kv, k_g):
    r, d = mem2d.shape
    twice = jnp.concatenate([k_g, k_g]).reshape(1, LANES)
    full = lambda shape: pl.BlockSpec(shape, lambda i: (0,) * len(shape))
    return pl.pallas_call(
        _mem_kv_kernel,
        grid=(1,),
        in_specs=[full((r, d)), full((1, d)), full((d, 2 * MEM_Q_COLS)), full((1, LANES))],
        out_specs=[full((r, MEM_Q_COLS)), full((r, MEM_Q_COLS))],
        out_shape=[jax.ShapeDtypeStruct((r, MEM_Q_COLS), F32), jax.ShapeDtypeStruct((r, MEM_Q_COLS), F32)],
        compiler_params=_cparams("arbitrary"),
        name="mem_kv",
    )(mem2d, ln_g.reshape(1, d), w_kv.astype(BF16), twice)


def _mem_attn_kernel(q_ref, k_ref, v_ref, qg_ref, o_ref):
    qg = qg_ref[...]
    rows = q_ref.shape[1]
    fq = _first_half((rows, LANES))
    heads = range(N_MEM_HEADS)
    tile = lambda t: slice(t * LANES, (t + 1) * LANES)
    qts = [_pair_rms(q_ref[0, :, tile(t)], qg) * (ATTN_SCALE * LOG2E) for t in range(MEM_Q_COLS // LANES)]
    kts = [k_ref[0, :, tile(t)].astype(BF16) for t in range(MEM_Q_COLS // LANES)]
    vts = [v_ref[0, :, tile(t)].astype(BF16) for t in range(MEM_Q_COLS // LANES)]
    ss = [_bdot_nt(jnp.where(fq == (h % 2 == 0), qts[h // 2], 0.0), kts[h // 2]) for h in heads]
    ps = [jnp.exp2(s - jnp.max(s, axis=-1, keepdims=True)) for s in ss]
    outs = [_bdot(ps[h], vts[h // 2]) / jnp.sum(ps[h], axis=-1, keepdims=True) for h in heads]
    for t in range(MEM_Q_COLS // LANES):
        o_ref[0, :, tile(t)] = jnp.where(fq, outs[2 * t], outs[2 * t + 1]).astype(o_ref.dtype)


MEM_Q_TILE = 512


def _mem_attn_prompt(z3, mem_k, mem_v, q_g):
    B, S, _ = z3.shape
    M = mem_k.shape[1]
    tq = MEM_Q_TILE
    twice = jnp.concatenate([q_g, q_g]).reshape(1, LANES)
    return pl.pallas_call(
        _mem_attn_kernel,
        grid=(B, S // tq),
        in_specs=[pl.BlockSpec((1, tq, MEM_Q_COLS), lambda b, i: (b, i, Z_QM // MEM_Q_COLS)),
                  pl.BlockSpec((1, M, MEM_Q_COLS), lambda b, i: (b, 0, 0)),
                  pl.BlockSpec((1, M, MEM_Q_COLS), lambda b, i: (b, 0, 0)),
                  pl.BlockSpec((1, LANES), lambda b, i: (0, 0))],
        out_specs=pl.BlockSpec((1, tq, MEM_Q_COLS), lambda b, i: (b, i, 0)),
        out_shape=jax.ShapeDtypeStruct((B, S, MEM_Q_COLS), BF16),
        compiler_params=_cparams("parallel", "parallel"),
        name="mem_attn_prompt",
    )(z3, mem_k, mem_v, twice)


PAIR = 2


def _swa_sample_kernel(sink_ref, q_ref, k_ref, v_ref, ck_ref, cv_ref, qg_ref, kg_ref, o_ref, nk_ref, nv_ref, *, L):
    n_seq = ck_ref.shape[0]
    Wb = ck_ref.shape[2]
    rows8 = SUBLANES
    nh = N_SWA_HEADS
    kn = _pair_rms(k_ref[...], kg_ref[...])
    qg = qg_ref[...]
    R = nh * rows8
    row = lax.broadcasted_iota(I32, (R, 1), 0)
    head = row // rows8
    seq_in_pair = (row % rows8) // L
    step = (row % L).astype(F32)
    slope = jnp.exp2(-(8.0 / N_SWA_HEADS) * (head.astype(F32) + 1.0))
    sink = jnp.zeros((R, 1), F32)
    for h in range(nh):
        sink = jnp.where(head == h, sink_ref[h], sink)
    key = lax.broadcasted_iota(I32, (R, Wb), 1).astype(F32)
    dist_c = float(Wb) + step - key
    bias_c = jnp.where(dist_c < float(WINDOW), 0.0, -jnp.inf)
    col = lax.broadcasted_iota(I32, (R, rows8), 1)
    dist_n = step - (col % L).astype(F32)
    bias_n = jnp.where((dist_n >= 0.0) & ((col // L) == seq_in_pair), 0.0, -jnp.inf)
    fh8 = _first_half((rows8, LANES))
    fhR = _first_half((R, LANES))
    kv_first = head < SWA_GROUP
    pairs = range(n_seq // PAIR)
    bias_c = bias_c - slope * dist_c
    bias_n = bias_n - slope * dist_n

    def stacked_queries(pr):
        r0 = pr * rows8
        pieces = []
        for t in range(SWA_Q_COLS // LANES):
            qt = _pair_rms(q_ref[r0:r0 + rows8, t * LANES:(t + 1) * LANES], qg) * ATTN_SCALE
            qr = pltpu.roll(qt, HEAD_DIM, 1)
            kv = t // (SWA_GROUP // 2)
            for half in range(2):
                src = qt if half == kv else qr
                pieces.append(jnp.where(fh8 == (kv == 0), src, 0.0))
        return jnp.concatenate(pieces, axis=0).astype(BF16)

    qs = [stacked_queries(pr) for pr in pairs]
    s_c = [jnp.where(seq_in_pair == 0, _bdot(qs[pr], ck_ref[pr * PAIR]), _bdot(qs[pr], ck_ref[pr * PAIR + 1]))
           + bias_c for pr in pairs]
    s_n = [_bdot_nt(qs[pr], kn[pr * rows8:(pr + 1) * rows8]) + bias_n for pr in pairs]
    m = [jnp.maximum(jnp.maximum(jnp.max(s_c[pr], axis=-1, keepdims=True),
                                 jnp.max(s_n[pr], axis=-1, keepdims=True)), sink) for pr in pairs]
    p_c = [jnp.exp(s_c[pr] - m[pr]) for pr in pairs]
    p_n = [jnp.exp(s_n[pr] - m[pr]) for pr in pairs]
    den = [jnp.sum(p_c[pr], axis=-1, keepdims=True) + jnp.sum(p_n[pr], axis=-1, keepdims=True)
           + jnp.exp(sink - m[pr]) for pr in pairs]
    outs = [(_bdot(p_n[pr], v_ref[pr * rows8:(pr + 1) * rows8, :])
             + _bdot_nt(jnp.where(seq_in_pair == 0, p_c[pr], 0.0), cv_ref[pr * PAIR])
             + _bdot_nt(jnp.where(seq_in_pair == 1, p_c[pr], 0.0), cv_ref[pr * PAIR + 1])) / den[pr] for pr in pairs]

    pos_r = lax.broadcasted_iota(I32, (Wb, rows8), 0)
    new_c = lax.broadcasted_iota(I32, (Wb, rows8), 1)
    tail = lax.broadcasted_iota(I32, (SWA_KV_COLS, Wb), 1) >= Wb - L

    def shifted(old, new8, j):
        place = jnp.where((pos_r == Wb - L + new_c % L) & (new_c // L == j), 1.0, 0.0).astype(BF16)
        rows_at_tail = sum(jnp.dot(place, part, preferred_element_type=F32) for part in _split3(new8))
        return jnp.where(tail, rows_at_tail.T, pltpu.roll(old, Wb - L, 1))

    for pr in pairs:
        for j in range(PAIR):
            s = pr * PAIR + j
            nk_ref[s] = shifted(ck_ref[s], kn[pr * rows8:(pr + 1) * rows8], j)
            nv_ref[s] = shifted(cv_ref[s], v_ref[pr * rows8:(pr + 1) * rows8, :], j)
    for pr in pairs:
        r0 = pr * rows8
        o = jnp.where(fhR == kv_first, outs[pr], 0.0)
        o_r = pltpu.roll(o, HEAD_DIM, 1)
        for t in range(SWA_Q_COLS // LANES):
            kv = t // (SWA_GROUP // 2)
            halves = []
            for half in range(2):
                h = 2 * t + half
                src = o if half == kv else o_r
                halves.append(src[h * rows8:(h + 1) * rows8])
            o_ref[r0:r0 + rows8, t * LANES:(t + 1) * LANES] = jnp.where(fh8, halves[0], halves[1])


SAMPLE_SEQS = 8


def _swa_sample(z_s, cache_k, cache_v, q_g, k_g, sinks, L):
    t = z_s.shape[0]
    DB, Wb, KV, HD = cache_k.shape
    ns = SAMPLE_SEQS
    rows = ns * L
    twice = lambda g: jnp.concatenate([g, g]).reshape(1, LANES)
    fm = lambda c: jnp.transpose(c, (0, 2, 3, 1)).reshape(DB, KV * HD, Wb)
    back = lambda c: jnp.transpose(c.reshape(DB, KV, HD, Wb), (0, 3, 1, 2))
    cache = pl.BlockSpec((ns, KV * HD, Wb), lambda i: (i, 0, 0))
    o, nk, nv = pl.pallas_call(
        functools.partial(_swa_sample_kernel, L=L),
        grid=(DB // ns,),
        in_specs=[pl.BlockSpec(memory_space=pltpu.SMEM),
                  pl.BlockSpec((rows, SWA_Q_COLS), lambda i: (i, 0)),
                  pl.BlockSpec((rows, LANES), lambda i: (i, Z_K // LANES)),
                  pl.BlockSpec((rows, LANES), lambda i: (i, Z_V // LANES)),
                  cache, cache,
                  pl.BlockSpec((1, LANES), lambda i: (0, 0)),
                  pl.BlockSpec((1, LANES), lambda i: (0, 0))],
        out_specs=[pl.BlockSpec((rows, SWA_Q_COLS), lambda i: (i, 0)), cache, cache],
        out_shape=[jax.ShapeDtypeStruct((t, SWA_Q_COLS), F32),
                   jax.ShapeDtypeStruct((DB, KV * HD, Wb), F32),
                   jax.ShapeDtypeStruct((DB, KV * HD, Wb), F32)],
        compiler_params=_cparams("parallel"),
        name="swa_sample",
    )(sinks, z_s, z_s, z_s, fm(cache_k), fm(cache_v), twice(q_g), twice(k_g))
    return o, back(nk), back(nv)


def _mem_sample_kernel(q_ref, mk_ref, mv_ref, qg_ref, o_ref, *, L):
    n_seq = mk_ref.shape[0]
    rows8 = SUBLANES
    nh = N_MEM_HEADS
    qg = qg_ref[...]
    R = nh * rows8
    row = lax.broadcasted_iota(I32, (R, 1), 0)
    seq_in_pair = (row % rows8) // L
    lane_head8 = lax.broadcasted_iota(I32, (rows8, MEM_Q_COLS), 1) // HEAD_DIM
    pairs = range(n_seq // PAIR)

    def stacked_queries(pr):
        qn = jnp.concatenate([_pair_rms(q_ref[pr * rows8:(pr + 1) * rows8, t * LANES:(t + 1) * LANES], qg)
                              for t in range(MEM_Q_COLS // LANES)], axis=-1) * ATTN_SCALE
        return jnp.concatenate([jnp.where(lane_head8 == h, qn, 0.0) for h in range(nh)], axis=0).astype(BF16)

    qs = [stacked_queries(pr) for pr in pairs]
    ss = [jnp.where(seq_in_pair == 0, _bdot(qs[pr], mk_ref[pr * PAIR]), _bdot(qs[pr], mk_ref[pr * PAIR + 1]))
          for pr in pairs]
    ps = [jnp.exp(s - jnp.max(s, axis=-1, keepdims=True)) for s in ss]
    outs = [(_bdot_nt(jnp.where(seq_in_pair == 0, ps[pr], 0.0), mv_ref[pr * PAIR])
             + _bdot_nt(jnp.where(seq_in_pair == 1, ps[pr], 0.0), mv_ref[pr * PAIR + 1]))
            / jnp.sum(ps[pr], axis=-1, keepdims=True) for pr in pairs]
    for pr in pairs:
        o = jnp.zeros((rows8, MEM_Q_COLS), F32)
        for h in range(nh):
            o = jnp.where(lane_head8 == h, outs[pr][h * rows8:(h + 1) * rows8], o)
        o_ref[pr * rows8:(pr + 1) * rows8, :] = o


def _mem_attn_sample(z_s, mem_k, mem_v, q_g, L):
    t = z_s.shape[0]
    DB, M, H, HD = mem_k.shape
    ns = SAMPLE_SEQS
    rows = ns * L
    twice = jnp.concatenate([q_g, q_g]).reshape(1, LANES)
    fm = lambda c: jnp.transpose(c, (0, 2, 3, 1)).reshape(DB, H * HD, M)
    cache = pl.BlockSpec((ns, H * HD, M), lambda i: (i, 0, 0))
    return pl.pallas_call(
        functools.partial(_mem_sample_kernel, L=L),
        grid=(DB // ns,),
        in_specs=[pl.BlockSpec((rows, MEM_Q_COLS), lambda i: (i, Z_QM // MEM_Q_COLS)),
                  cache, cache,
                  pl.BlockSpec((1, LANES), lambda i: (0, 0))],
        out_specs=pl.BlockSpec((rows, MEM_Q_COLS), lambda i: (i, 0)),
        out_shape=jax.ShapeDtypeStruct((t, MEM_Q_COLS), F32),
        compiler_params=_cparams("parallel"),
        name="mem_attn_sample",
    )(z_s, fm(mem_k), fm(mem_v), twice)


def _gdn_sample_kernel(uq_ref, uk_ref, uv_ref, bq_ref, bk_ref, bv_ref, wq_ref, wk_ref, wv_ref,
                       ab_ref, gate_ref, alog_ref, dtb_ref, ng_ref, s_in_ref, o_ref, s_ref, kq_ref):
    h = pl.program_id(0)
    L = uq_ref.shape[0]
    nbuf = bq_ref.shape[0]
    DK = GDN_DK

    def conv(u_ref, b_ref, w_ref, t):
        up = [b_ref[i] for i in range(nbuf)] + [u_ref[i] for i in range(L)]
        y = up[t] * w_ref[0]
        for i in range(1, GDN_CONV):
            y = y + up[t + i] * w_ref[i]
        return y * jax.nn.sigmoid(y)

    s_ref[...] = s_in_ref[...]
    ng = ng_ref[...]
    hsel = lax.broadcasted_iota(I32, (SUBLANES, 1), 0)
    pick = lambda m, r: jnp.sum(jnp.where(hsel == r, m, 0.0), axis=0, keepdims=True)
    alog = pick(alog_ref[...], h)
    dtb = pick(dtb_ref[...], h)
    for t in range(L):
        q = conv(uq_ref, bq_ref, wq_ref, t)
        k = conv(uk_ref, bk_ref, wk_ref, t)
        v = conv(uv_ref, bv_ref, wv_ref, t)
        q = q * lax.rsqrt(jnp.sum(q * q, axis=0, keepdims=True) + EPS) * (GDN_DK ** -0.5)
        k = k * lax.rsqrt(jnp.sum(k * k, axis=0, keepdims=True) + EPS)
        ab = ab_ref[t]
        a = pick(ab, h)
        bb = pick(ab, h + N_GDN_HEADS)
        decay = jnp.exp(-jnp.exp(alog) * _softplus(a + dtb))
        beta = jax.nn.sigmoid(bb)
        kq_ref[0] = k
        kq_ref[1] = q

        def decay_and_project(dk, acc):
            s = s_ref[0, dk] * decay
            s_ref[0, dk] = s
            return acc + s * kq_ref[0, pl.ds(dk, 1), :]

        sk = lax.fori_loop(0, DK, decay_and_project, jnp.zeros_like(v), unroll=8)
        u = beta * (v - sk)

        def update_and_read(dk, acc):
            s = s_ref[0, dk] + kq_ref[0, pl.ds(dk, 1), :] * u
            s_ref[0, dk] = s
            return acc + s * kq_ref[1, pl.ds(dk, 1), :]

        o = lax.fori_loop(0, DK, update_and_read, jnp.zeros_like(v), unroll=8)
        o = o * lax.rsqrt(jnp.mean(o * o, axis=0, keepdims=True) + EPS) * ng
        g = gate_ref[t]
        o_ref[t] = o * (g * jax.nn.sigmoid(g))


def _gdn_sample(z_s, conv_buf, state, conv_w, a_log, dt_bias, norm_g, DB, L):
    H = N_GDN_HEADS
    z3 = z_s.reshape(DB, L, Z_COLS)
    u_t = jnp.transpose(z3[:, :, Z_GDN:Z_GATE], (1, 2, 0))
    gate_t = jnp.transpose(z3[:, :, Z_GATE:Z_QM], (1, 2, 0))
    ab_t = jnp.transpose(z3[:, :, Z_AB:Z_AB + SUBLANES], (1, 2, 0))
    buf_t = jnp.transpose(conv_buf, (1, 2, 0))
    s_t = jnp.transpose(state, (1, 2, 3, 0))
    w_col = conv_w.reshape(GDN_CONV, GDN_CONV_CH, 1)
    col8 = lambda a: jnp.pad(a, (0, SUBLANES - a.shape[0])).reshape(SUBLANES, 1)
    nbuf = conv_buf.shape[1]
    part = lambda n, j: pl.BlockSpec((n, GDN_DK, DB), lambda h: (0, j * H + h, 0))
    wpart = lambda j: pl.BlockSpec((GDN_CONV, GDN_DK, 1), lambda h: (0, j * H + h, 0))
    whole = lambda shape: pl.BlockSpec(shape, lambda h: (0,) * len(shape))
    o_t, s_new = pl.pallas_call(
        _gdn_sample_kernel,
        grid=(H,),
        in_specs=[part(L, 0), part(L, 1), part(L, 2), part(nbuf, 0), part(nbuf, 1), part(nbuf, 2),
                  wpart(0), wpart(1), wpart(2),
                  whole((L, SUBLANES, DB)),
                  pl.BlockSpec((L, GDN_DV, DB), lambda h: (0, h, 0)),
                  whole((SUBLANES, 1)), whole((SUBLANES, 1)), whole((GDN_DV, 1)),
                  pl.BlockSpec((1, GDN_DK, GDN_DV, DB), lambda h: (h, 0, 0, 0))],
        out_specs=[pl.BlockSpec((L, GDN_DV, DB), lambda h: (0, h, 0)),
                   pl.BlockSpec((1, GDN_DK, GDN_DV, DB), lambda h: (h, 0, 0, 0))],
        out_shape=[jax.ShapeDtypeStruct((L, H * GDN_DV, DB), F32),
                   jax.ShapeDtypeStruct((H, GDN_DK, GDN_DV, DB), F32)],
        scratch_shapes=[pltpu.VMEM((2, GDN_DK, DB), F32)],
        compiler_params=_cparams("parallel"),
        name="gdn_sample",
    )(u_t, u_t, u_t, buf_t, buf_t, buf_t, w_col, w_col, w_col, ab_t, gate_t,
      col8(a_log), col8(dt_bias), norm_g.reshape(GDN_DV, 1), s_t)
    o = jnp.transpose(o_t, (2, 0, 1)).reshape(DB * L, H * GDN_DV)
    return o, jnp.transpose(s_new, (3, 0, 1, 2))


def kernel(x_prompt, x_sample, cache_swa_k, cache_swa_v, state_gdn, state_gdn_conv, cache_mem_k, cache_mem_v,
           mem_prompt, ln1_g, w_in, swa_q_norm, swa_k_norm, swa_sinks, gdn_conv_w, gdn_a_log, gdn_dt_bias,
           gdn_norm_g, mem_ln_g, w_mem_kv, mem_q_norm, mem_k_norm, w_o, ln2_g, router_w, router_b,
           moe_w1, moe_b1, moe_w2, moe_b2):
    B, S, D = x_prompt.shape
    DB, DL, _ = x_sample.shape
    depth = ln1_g.shape[0]
    assert depth == 1
    l = 0
    tp, ts = B * S, DB * DL
    t_all = tp + ts
    n_ab = 2 * N_GDN_HEADS
    c_ab = SWA_Q_COLS + 2 * SWA_KV_COLS + GDN_CONV_CH
    w = w_in[l]
    w_z = jnp.concatenate([w[:, :c_ab], w[:, c_ab + n_ab:], w[:, c_ab:c_ab + n_ab],
                           jnp.zeros((D, LANES - n_ab), F32)], axis=1).astype(BF16)
    rw = jnp.pad(router_w[l], ((0, 0), (0, LANES - N_EXPERTS)))
    rw_hi = rw.astype(BF16)
    rw_lo = (rw - rw_hi.astype(F32)).astype(BF16)
    rb = jnp.pad(router_b[l], (0, LANES - N_EXPERTS)).reshape(1, LANES)
    wo = w_o[l].astype(BF16)
    w1 = moe_w1[l]
    w2 = moe_w2[l]
    b1 = moe_b1[l].reshape(N_EXPERTS, 1, -1)
    b2 = moe_b2[l].reshape(N_EXPERTS, 1, -1)
    p = {'q_norm': swa_q_norm[l], 'k_norm': swa_k_norm[l], 'sinks': swa_sinks[l], 'conv_w': gdn_conv_w[l],
         'a_log': gdn_a_log[l], 'dt_bias': gdn_dt_bias[l], 'gdn_norm': gdn_norm_g[l], 'mem_q_norm': mem_q_norm[l]}

    xp = x_prompt.reshape(tp, D)
    xs = x_sample.reshape(ts, D)
    z_p = _inproj(xp, ln1_g[l], w_z)
    z_s = _inproj(xs, ln1_g[l], w_z)

    M = mem_prompt.shape[1]
    z_p3 = z_p.reshape(B, S, Z_COLS)
    mk2, mv2 = _mem_kv(mem_prompt.reshape(B * M, D), mem_ln_g[l], w_mem_kv[l], mem_k_norm[l])
    mk = mk2.reshape(B, M, N_MEM_HEADS, HEAD_DIM)
    mv = mv2.reshape(B, M, N_MEM_HEADS, HEAD_DIM)
    os_p, pk = _swa_prompt(z_p3, p['q_norm'], p['k_norm'], p['sinks'])
    od_p, ps = _gdn_prompt(z_p3, p['conv_w'], p['a_log'], p['dt_bias'], p['gdn_norm'])
    om_p = _mem_attn_prompt(z_p3, mk2.reshape(B, M, MEM_Q_COLS), mv2.reshape(B, M, MEM_Q_COLS), p['mem_q_norm'])
    os_p, od_p, om_p = os_p.reshape(tp, -1), od_p.reshape(tp, -1), om_p.reshape(tp, -1)
    pk = pk.reshape(B, WINDOW, N_SWA_KV, HEAD_DIM)
    pv = z_p3[:, S - WINDOW:, Z_V:Z_GDN].reshape(B, WINDOW, N_SWA_KV, HEAD_DIM)
    pc = z_p3[:, S - (GDN_CONV - 1):, Z_GDN:Z_GATE]
    os_s, sk, sv = _swa_sample(z_s, cache_swa_k[l], cache_swa_v[l], p['q_norm'], p['k_norm'], p['sinks'], DL)
    od_s, ss = _gdn_sample(z_s, state_gdn_conv[l], state_gdn[l], p['conv_w'], p['a_log'], p['dt_bias'],
                           p['gdn_norm'], DB, DL)
    om_s = _mem_attn_sample(z_s, cache_mem_k[l], cache_mem_v[l], p['mem_q_norm'], DL)
    z_s3 = z_s.reshape(DB, DL, Z_COLS)
    sc = jnp.concatenate([state_gdn_conv[l], z_s3[:, :, Z_GDN:Z_GATE]], axis=1)[:, DL:]

    h_p, hn_all, lg_all = _outproj(xp, os_p, od_p, om_p, wo, ln2_g[l], rw_hi, rw_lo, rb, t_all, 0)
    h_s, hn_all, lg_all = _outproj(xs, os_s, od_s, om_s, wo, ln2_g[l], rw_hi, rw_lo, rb, t_all, tp,
                                   prev=(hn_all, lg_all))
    y_p, y_s = _moe(hn_all, lg_all, [h_p, h_s], w1, b1, w2, b2)
    return (y_p.reshape(B, S, D), y_s.reshape(DB, DL, D), pk[None], pv[None], ps[None], pc[None], mk[None],
            mv[None], sk[None], sv[None], ss[None], sc[None])
```

```python
import functools

import jax
import jax.numpy as jnp
from jax import lax
from jax.experimental import pallas as pl
from jax.experimental.pallas import tpu as pltpu

F32 = jnp.float32
BF16 = jnp.bfloat16
I32 = jnp.int32

HEAD_DIM = 64
N_SWA_HEADS = 8
N_SWA_KV = 2
SWA_GROUP = N_SWA_HEADS // N_SWA_KV
WINDOW = 128
N_GDN_HEADS = 4
GDN_DK = 64
GDN_DV = 64
GDN_CONV = 4
GDN_CHUNK = 64
N_MEM_HEADS = 4
N_EXPERTS = 32
TOP_K = 4
SWIGLU_ALPHA = 1.702
SWIGLU_LIMIT = 7.0
EPS = 1e-6
ATTN_SCALE = HEAD_DIM ** -0.5

SWA_Q_COLS = N_SWA_HEADS * HEAD_DIM
SWA_KV_COLS = N_SWA_KV * HEAD_DIM
GDN_QK_COLS = N_GDN_HEADS * GDN_DK
GDN_V_COLS = N_GDN_HEADS * GDN_DV
GDN_CONV_CH = 2 * GDN_QK_COLS + GDN_V_COLS
MEM_Q_COLS = N_MEM_HEADS * HEAD_DIM

LANES = 128
SUBLANES = 8
BF16_EXACT_INT = 256.0
VMEM_LIMIT = 56 * 1024 * 1024

Z_Q = 0
Z_K = Z_Q + SWA_Q_COLS
Z_V = Z_K + SWA_KV_COLS
Z_GDN = Z_V + SWA_KV_COLS
Z_GATE = Z_GDN + GDN_CONV_CH
Z_QM = Z_GATE + GDN_V_COLS
Z_AB = Z_QM + MEM_Q_COLS
Z_COLS = Z_AB + LANES

INPROJ_TILE = 1024
ROW_TILE = 512
MOE_TILE = 512
MOE_BLK = 512
PERM_CHUNK = 256
RUN_ALIGN = 16
BLK_ROWS = -(-(MOE_BLK * TOP_K + N_EXPERTS * (RUN_ALIGN - 1)) // PERM_CHUNK) * PERM_CHUNK


def _cparams(*sem):
    return pltpu.CompilerParams(dimension_semantics=sem, vmem_limit_bytes=VMEM_LIMIT)


def _bdot(a, b):
    return jnp.dot(a.astype(BF16), b.astype(BF16), preferred_element_type=F32)


def _bdot_nt(a, b):
    return lax.dot_general(a.astype(BF16), b.astype(BF16), (((1,), (1,)), ((), ())),
                           preferred_element_type=F32)


def _split2(x):
    hi = x.astype(BF16)
    lo = (x - hi.astype(F32)).astype(BF16)
    return hi, lo


def _split3(x):
    hi = x.astype(BF16)
    r = x - hi.astype(F32)
    mid = r.astype(BF16)
    lo = (r - mid.astype(F32)).astype(BF16)
    return hi, mid, lo


def _rms_rows(x, g):
    ms = jnp.mean(x * x, axis=-1, keepdims=True)
    return x * lax.rsqrt(ms + EPS) * g


def _inproj_kernel(x_ref, g_ref, w_ref, z_ref):
    n = _rms_rows(x_ref[...], g_ref[...])
    z_ref[...] = jnp.dot(n.astype(BF16), w_ref[...], preferred_element_type=F32)


def _inproj(x2d, ln_g, w_z):
    t, d = x2d.shape
    tm = min(INPROJ_TILE, t)
    return pl.pallas_call(
        _inproj_kernel,
        grid=(t // tm,),
        in_specs=[pl.BlockSpec((tm, d), lambda i: (i, 0)),
                  pl.BlockSpec((1, d), lambda i: (0, 0)),
                  pl.BlockSpec((d, Z_COLS), lambda i: (0, 0))],
        out_specs=pl.BlockSpec((tm, Z_COLS), lambda i: (i, 0)),
        out_shape=jax.ShapeDtypeStruct((t, Z_COLS), F32),
        compiler_params=_cparams("parallel"),
        name="inproj",
    )(x2d, ln_g.reshape(1, d), w_z)


def _outproj_kernel(x_ref, os_ref, od_ref, om_ref, wo_ref, g_ref, rwh_ref, rwl_ref, rb_ref,
                    *refs, n_own):
    h_ref, hn_ref, lg_ref = refs[-3:]
    i = pl.program_id(0)

    @pl.when(i < n_own)
    def _():
        n_s = os_ref.shape[1]
        n_d = od_ref.shape[1]
        tm = x_ref.shape[0]
        halves = [slice(k * (tm // 2), (k + 1) * (tm // 2)) for k in range(2)]
        hs = []
        for rows in halves:
            h = x_ref[rows, :]
            h = h + jnp.dot(os_ref[rows, :].astype(BF16), wo_ref[0:n_s, :], preferred_element_type=F32)
            h = h + jnp.dot(od_ref[rows, :].astype(BF16), wo_ref[n_s:n_s + n_d, :], preferred_element_type=F32)
            h = h + jnp.dot(om_ref[rows, :].astype(BF16), wo_ref[n_s + n_d:, :], preferred_element_type=F32)
            hs.append(h)
        for rows, h in zip(halves, hs):
            h_ref[rows, :] = h
            hn = _rms_rows(h, g_ref[...])
            hn_ref[rows, :] = hn.astype(BF16)
            hi, lo = _split2(hn)
            lg = (jnp.dot(hi, rwh_ref[...], preferred_element_type=F32)
                  + jnp.dot(lo, rwh_ref[...], preferred_element_type=F32)
                  + jnp.dot(hi, rwl_ref[...], preferred_element_type=F32))
            lg_ref[rows, :] = lg + rb_ref[...]

    @pl.when(i >= n_own)
    def _():
        hn_ref[...] = jnp.zeros_like(hn_ref)
        lg_ref[...] = jnp.zeros_like(lg_ref)


def _outproj(x2d, o_s, o_d, o_m, w_o, ln_g, rw_hi, rw_lo, rb, t_all, row0, prev=None):
    t, d = x2d.shape
    tm = min(ROW_TILE, t)
    blk0 = row0 // tm
    n_own = t // tm
    n_steps = n_own if prev is not None else t_all // tm
    row = lambda i: (jnp.minimum(i, n_own - 1), 0)
    row_off = lambda i: (i + blk0, 0)
    const = lambda i: (0, 0)
    in_specs = [pl.BlockSpec((tm, d), row),
                pl.BlockSpec((tm, o_s.shape[1]), row),
                pl.BlockSpec((tm, o_d.shape[1]), row),
                pl.BlockSpec((tm, o_m.shape[1]), row),
                pl.BlockSpec((d, d), const),
                pl.BlockSpec((1, d), const),
                pl.BlockSpec((d, LANES), const),
                pl.BlockSpec((d, LANES), const),
                pl.BlockSpec((1, LANES), const)]
    args = [x2d, o_s, o_d, o_m, w_o, ln_g.reshape(1, d), rw_hi, rw_lo, rb]
    aliases = {}
    if prev is not None:
        in_specs += [pl.BlockSpec(memory_space=pl.ANY), pl.BlockSpec(memory_space=pl.ANY)]
        aliases = {len(args): 1, len(args) + 1: 2}
        args += list(prev)
    return pl.pallas_call(
        functools.partial(_outproj_kernel, n_own=n_own),
        grid=(n_steps,),
        in_specs=in_specs,
        out_specs=[pl.BlockSpec((tm, d), row),
                   pl.BlockSpec((tm, d), row_off),
                   pl.BlockSpec((tm, LANES), row_off)],
        out_shape=[jax.ShapeDtypeStruct((t, d), F32),
                   jax.ShapeDtypeStruct((t_all, d), BF16),
                   jax.ShapeDtypeStruct((t_all, LANES), F32)],
        input_output_aliases=aliases,
        compiler_params=_cparams("arbitrary"),
        name="outproj_router",
    )(*args)


def _route_kernel(lg_ref, pos_ref, post_ref, g_ref, cnt_ref):
    tm = lg_ref.shape[0]
    lane = lax.broadcasted_iota(I32, (tm, LANES), 1).astype(F32)
    l = jnp.where(lane < N_EXPERTS, lg_ref[...], -jnp.inf)
    vals, idxs = [], []
    for _k in range(TOP_K):
        m = jnp.max(l, axis=-1, keepdims=True)
        idx = jnp.min(jnp.where(l == m, lane, float(LANES)), axis=-1, keepdims=True)
        l = jnp.where(lane == idx, -jnp.inf, l)
        vals.append(m)
        idxs.append(idx)
    ex = [jnp.exp(v - vals[0]) for v in vals]
    den = ex[0] + ex[1] + ex[2] + ex[3]
    member = jnp.zeros((tm, LANES), F32)
    for idx in idxs:
        member = member + jnp.where(lane == idx, 1.0, 0.0)
    ri = lax.broadcasted_iota(I32, (tm, tm), 0)
    ci = lax.broadcasted_iota(I32, (tm, tm), 1)
    strict = jnp.where(ci < ri, 1.0, 0.0).astype(BF16)
    prefix = jnp.dot(strict, member.astype(BF16), preferred_element_type=F32)
    cnt = jnp.sum(member, axis=0, keepdims=True)
    cpad = jnp.ceil(cnt * (1.0 / RUN_ALIGN)) * float(RUN_ALIGN)
    c_hi = jnp.floor(cpad * (1.0 / BF16_EXACT_INT))
    c_lo = cpad - BF16_EXACT_INT * c_hi
    ej = lax.broadcasted_iota(I32, (LANES, LANES), 0)
    ee = lax.broadcasted_iota(I32, (LANES, LANES), 1)
    before = jnp.where(ej < ee, 1.0, 0.0).astype(BF16)
    bcast = lambda v: jnp.broadcast_to(v, (SUBLANES, LANES)).astype(BF16)
    off = (BF16_EXACT_INT * jnp.dot(bcast(c_hi), before, preferred_element_type=F32)
           + jnp.dot(bcast(c_lo), before, preferred_element_type=F32))[0:1]
    where_in_run = prefix + off
    p_out = jnp.zeros((tm, LANES), F32)
    g_out = jnp.zeros((tm, LANES), F32)
    for k in range(TOP_K):
        pos = jnp.sum(jnp.where(lane == idxs[k], where_in_run, 0.0), axis=-1, keepdims=True)
        p_out = jnp.where(lane == float(k), pos, p_out)
        g_out = jnp.where(lane == float(k), ex[k] / den, g_out)
    pos_ref[...] = p_out[:, :TOP_K]
    post_ref[...] = p_out.T[:SUBLANES, :]
    g_ref[...] = g_out[:, :TOP_K]
    cnt_ref[0] = cnt


def _route(logits):
    t = logits.shape[0]
    tm = MOE_BLK
    nb = t // tm
    return pl.pallas_call(
        _route_kernel,
        grid=(nb,),
        in_specs=[pl.BlockSpec((tm, LANES), lambda i: (i, 0))],
        out_specs=[pl.BlockSpec((tm, TOP_K), lambda i: (i, 0)),
                   pl.BlockSpec((SUBLANES, tm), lambda i: (0, i)),
                   pl.BlockSpec((tm, TOP_K), lambda i: (i, 0)),
                   pl.BlockSpec((1, 1, LANES), lambda i: (i, 0, 0))],
        out_shape=[jax.ShapeDtypeStruct((t, TOP_K), F32),
                   jax.ShapeDtypeStruct((SUBLANES, t), F32),
                   jax.ShapeDtypeStruct((t, TOP_K), F32),
                   jax.ShapeDtypeStruct((nb, 1, LANES), F32)],
        compiler_params=_cparams("parallel"),
        name="route",
    )(logits)


def _run_copies(n, max_rows, src_ref, src0, dst_ref, dst0, sem, wait):
    pos = 0
    bit = max_rows
    while bit >= RUN_ALIGN:
        take = (n & bit) != 0

        def go(pos=pos, bit=bit):
            cp = pltpu.make_async_copy(src_ref.at[pl.ds(pl.multiple_of(src0 + pos, RUN_ALIGN), bit)],
                                       dst_ref.at[pl.ds(pl.multiple_of(dst0 + pos, RUN_ALIGN), bit)], sem)
            cp.wait() if wait else cp.start()

        pl.when(take)(go)
        pos = pos + jnp.where(take, bit, 0)
        bit //= 2


RUN_SIZES = tuple(MOE_BLK >> i for i in range((MOE_BLK // RUN_ALIGN).bit_length()))


def _piece_copies(b, cnt_ref, loc_ref, glob_ref, local_ref, global_hbm, sem, to_global, wait):
    for c, rows in enumerate(RUN_SIZES):
        base = b * len(RUN_SIZES) + c

        def body(s, carry, rows=rows, base=base, c=c):
            j = base * N_EXPERTS + s
            loc = local_ref.at[pl.ds(pl.multiple_of(loc_ref[j], RUN_ALIGN), rows)]
            glob = global_hbm.at[pl.ds(pl.multiple_of(glob_ref[j], RUN_ALIGN), rows)]
            cp = pltpu.make_async_copy(loc, glob, sem) if to_global else pltpu.make_async_copy(glob, loc, sem)
            cp.wait() if wait else cp.start(priority=c % 2)
            return carry

        lax.fori_loop(0, cnt_ref[base], body, 0)


def _dispatch_kernel(cnt_ref, loc_ref, glob_ref, rows_ref, estart_ref, elen_ref, nused_ref,
                     hn_ref, post_ref, xs_hbm, buf_ref, zero_ref, sem, zsem):
    b = pl.program_id(0)
    nb = pl.num_programs(0)
    slot = b % 2
    tm = hn_ref.shape[0]
    x = hn_ref[...].astype(BF16)
    post = post_ref[...]
    P = PERM_CHUNK

    def sort_rows(c):
        r = (lax.broadcasted_iota(I32, (P, tm), 0) + c * P).astype(F32)
        sel = jnp.zeros((P, tm), F32)
        for k in range(TOP_K):
            sel = jnp.where(r == post[k:k + 1, :], 1.0, sel)
        buf_ref[slot, c * P:(c + 1) * P, :] = jnp.dot(sel.astype(BF16), x,
                                                      preferred_element_type=F32).astype(BF16)

    for c in range(BLK_ROWS // P):
        if c * P < tm * TOP_K:
            sort_rows(c)
        else:
            pl.when(rows_ref[b] > c * P)(functools.partial(sort_rows, c))

    def runs(blk, s, wait):
        _piece_copies(blk, cnt_ref, loc_ref, glob_ref, buf_ref.at[s], xs_hbm, sem.at[s], True, wait)

    runs(b, slot, False)

    @pl.when(b == 0)
    def _():
        zero_ref[...] = jnp.zeros_like(zero_ref)

        def tail(wait):
            def body(e, c):
                n = (MOE_TILE - elen_ref[e] % MOE_TILE) % MOE_TILE
                _run_copies(n, MOE_TILE // 2, zero_ref, 0, xs_hbm, estart_ref[e] + elen_ref[e], zsem, wait)
                return c
            lax.fori_loop(0, N_EXPERTS, body, 0)

            def free_tile(ti, c):
                for half in range(2):
                    _run_copies(MOE_TILE // 2, MOE_TILE // 2, zero_ref, 0, xs_hbm,
                                ti * MOE_TILE + half * (MOE_TILE // 2), zsem, wait)
                return c
            lax.fori_loop(nused_ref[0], xs_hbm.shape[0] // MOE_TILE, free_tile, 0)

        tail(False)
        tail(True)

    pl.when(b > 0)(lambda: runs(b - 1, 1 - slot, True))
    pl.when(b == nb - 1)(lambda: runs(b, slot, True))


def _dispatch(hn, post, piece_cnt, piece_loc, piece_glob, blk_rows, e_start, e_len, n_used, n_rows):
    t, d = hn.shape
    grid_spec = pltpu.PrefetchScalarGridSpec(
        num_scalar_prefetch=7,
        grid=(t // MOE_BLK,),
        in_specs=[pl.BlockSpec((MOE_BLK, d), lambda i, *_: (i, 0)),
                  pl.BlockSpec((SUBLANES, MOE_BLK), lambda i, *_: (0, i))],
        out_specs=pl.BlockSpec(memory_space=pl.ANY),
        scratch_shapes=[pltpu.VMEM((2, BLK_ROWS, d), BF16),
                        pltpu.VMEM((MOE_TILE, d), BF16),
                        pltpu.SemaphoreType.DMA((2,)),
                        pltpu.SemaphoreType.DMA(())],
    )
    return pl.pallas_call(
        _dispatch_kernel,
        grid_spec=grid_spec,
        out_shape=jax.ShapeDtypeStruct((n_rows, d), BF16),
        compiler_params=_cparams("arbitrary"),
        name="dispatch",
    )(piece_cnt, piece_loc, piece_glob, blk_rows, e_start, e_len, n_used, hn, post)


def _expert_kernel(te_ref, nu_ref, nxt_ref, slot_ref, x_ref, w1_hbm, b1_ref, w2_hbm, b2_ref, y_ref,
                   w1f_ref, w2f_ref, w1b_ref, w2b_ref, sem):
    i = pl.program_id(0)
    live = i < nu_ref[0]
    e = te_ref[i]
    s = slot_ref[e]

    def weight_copies(expert, slot):
        return (pltpu.make_async_copy(w1_hbm.at[expert], w1f_ref.at[slot], sem.at[0, slot]),
                pltpu.make_async_copy(w2_hbm.at[expert], w2f_ref.at[slot], sem.at[1, slot]))

    @pl.when(live & (i == 0))
    def _():
        for cp in weight_copies(e, s):
            cp.start()

    @pl.when(live & ((i == 0) | (e != te_ref[jnp.maximum(i - 1, 0)])))
    def _():
        for cp in weight_copies(e, s):
            cp.wait()
        w1b_ref[...] = w1f_ref[s].astype(BF16)
        w2b_ref[...] = w2f_ref[s].astype(BF16)

        @pl.when(nxt_ref[e] >= 0)
        def _():
            for cp in weight_copies(nxt_ref[e], 1 - s):
                cp.start()

    @pl.when(live)
    def _():
        f = w2b_ref.shape[0]
        h = jnp.dot(x_ref[...], w1b_ref[...], preferred_element_type=F32) + b1_ref[0]
        glu = jnp.minimum(h[:, :f], SWIGLU_LIMIT)
        lin = jnp.clip(h[:, f:], -SWIGLU_LIMIT, SWIGLU_LIMIT)
        act = glu * jax.nn.sigmoid(SWIGLU_ALPHA * glu) * (lin + 1.0)
        y = jnp.dot(act.astype(BF16), w2b_ref[...], preferred_element_type=F32) + b2_ref[0]
        y_ref[...] = y.astype(BF16)

    @pl.when(i >= nu_ref[0])
    def _():
        y_ref[...] = jnp.zeros_like(y_ref)


def _experts(xs, tile_expert, n_used, next_expert, expert_slot, w1, b1, w2, b2):
    n_rows, d = xs.shape
    f2 = w1.shape[2]
    f = w2.shape[1]
    n_tiles = n_rows // MOE_TILE
    live = lambda i, te, nu, *_: (jnp.minimum(i, nu[0] - 1), 0)
    every = lambda i, *_: (i, 0)
    wsel = lambda i, te, *_: (te[i], 0, 0)
    grid_spec = pltpu.PrefetchScalarGridSpec(
        num_scalar_prefetch=4,
        grid=(n_tiles,),
        in_specs=[pl.BlockSpec((MOE_TILE, d), live),
                  pl.BlockSpec(memory_space=pl.ANY),
                  pl.BlockSpec((1, 1, f2), wsel),
                  pl.BlockSpec(memory_space=pl.ANY),
                  pl.BlockSpec((1, 1, d), wsel)],
        out_specs=pl.BlockSpec((MOE_TILE, d), every),
        scratch_shapes=[pltpu.VMEM((2, d, f2), F32), pltpu.VMEM((2, f, d), F32),
                        pltpu.VMEM((d, f2), BF16), pltpu.VMEM((f, d), BF16),
                        pltpu.SemaphoreType.DMA((2, 2))],
    )
    return pl.pallas_call(
        _expert_kernel,
        grid_spec=grid_spec,
        out_shape=jax.ShapeDtypeStruct((n_rows, d), BF16),
        compiler_params=_cparams("arbitrary"),
        name="experts",
    )(tile_expert, n_used, next_expert, expert_slot, xs, w1, b1, w2, b2)


def _combine_kernel(cnt_ref, loc_ref, glob_ref, rows_ref, h_ref, pos_ref, g_ref, yb_hbm, y_ref, buf_ref, sem,
                    *, blk0):
    i = pl.program_id(0)
    n_steps = pl.num_programs(0)
    b = i + blk0
    slot = i % 2
    tm, d = h_ref.shape

    def runs(blk, s, wait):
        _piece_copies(blk, cnt_ref, loc_ref, glob_ref, buf_ref.at[s], yb_hbm, sem.at[s], False, wait)

    @pl.when(i == 0)
    def _():
        buf_ref[...] = jnp.zeros_like(buf_ref)
        runs(b, slot, False)

    pl.when(i + 1 < n_steps)(lambda: runs(b + 1, 1 - slot, False))
    runs(b, slot, True)
    pos = pos_ref[...]
    g = g_ref[...]
    P = PERM_CHUNK

    def weighted_rows(c):
        col = (lax.broadcasted_iota(I32, (tm, P), 1) + c * P).astype(F32)
        wgt = jnp.zeros((tm, P), F32)
        for k in range(TOP_K):
            wgt = jnp.where(col == pos[:, k:k + 1], g[:, k:k + 1], wgt)
        return jnp.dot(wgt.astype(BF16), buf_ref[slot, c * P:(c + 1) * P, :], preferred_element_type=F32)

    n_sure = tm * TOP_K // P
    y = h_ref[...]
    for c in range(n_sure):
        y = y + weighted_rows(c)
    y_ref[...] = y
    for c in range(n_sure, BLK_ROWS // P):
        @pl.when(rows_ref[b] > c * P)
        def _(c=c):
            y_ref[...] += weighted_rows(c)


def _combine(h, pos, gates, piece_cnt, piece_loc, piece_glob, blk_rows, yb, blk0):
    t, d = h.shape
    grid_spec = pltpu.PrefetchScalarGridSpec(
        num_scalar_prefetch=4,
        grid=(t // MOE_BLK,),
        in_specs=[pl.BlockSpec((MOE_BLK, d), lambda i, *_: (i, 0)),
                  pl.BlockSpec((MOE_BLK, TOP_K), lambda i, *_: (i + blk0, 0)),
                  pl.BlockSpec((MOE_BLK, TOP_K), lambda i, *_: (i + blk0, 0)),
                  pl.BlockSpec(memory_space=pl.ANY)],
        out_specs=pl.BlockSpec((MOE_BLK, d), lambda i, *_: (i, 0)),
        scratch_shapes=[pltpu.VMEM((2, BLK_ROWS, d), BF16),
                        pltpu.SemaphoreType.DMA((2,))],
    )
    return pl.pallas_call(
        functools.partial(_combine_kernel, blk0=blk0),
        grid_spec=grid_spec,
        out_shape=jax.ShapeDtypeStruct((t, d), F32),
        compiler_params=_cparams("arbitrary"),
        name="combine",
    )(piece_cnt, piece_loc, piece_glob, blk_rows, h, pos, gates, yb)


def _moe(hn_all, logits_all, h_parts, w1, b1, w2, b2):
    t_all = hn_all.shape[0]
    nb = t_all // MOE_BLK
    pos, post, gates, counts_f = _route(logits_all)
    cnt = counts_f.reshape(nb, LANES)[:, :N_EXPERTS].astype(I32)
    seg_len = (cnt + RUN_ALIGN - 1) // RUN_ALIGN * RUN_ALIGN
    before_e = jnp.arange(N_EXPERTS)[:, None] < jnp.arange(N_EXPERTS)[None, :]
    before_b = jnp.arange(nb)[None, :] < jnp.arange(nb)[:, None]
    seg_off = jnp.sum(jnp.where(before_e[None], seg_len[:, :, None], 0), axis=1)
    e_len = jnp.sum(seg_len, axis=0)
    e_tiles = (e_len + MOE_TILE - 1) // MOE_TILE
    tile_start = jnp.sum(jnp.where(before_e, e_tiles[:, None], 0), axis=0)
    tile_end = tile_start + e_tiles
    e_start = tile_start * MOE_TILE
    seg_dst = e_start[None, :] + jnp.sum(jnp.where(before_b[:, :, None], seg_len[None], 0), axis=1)
    max_rows = t_all * TOP_K + nb * N_EXPERTS * (RUN_ALIGN - 1) + N_EXPERTS * (MOE_TILE - RUN_ALIGN)
    n_tiles = -(-max_rows // MOE_TILE)
    n_rows = n_tiles * MOE_TILE
    n_used = tile_end[-1:].astype(I32)
    tile_expert = jnp.minimum(jnp.sum(tile_end[None, :] <= jnp.arange(n_tiles, dtype=I32)[:, None], axis=1),
                              N_EXPERTS - 1).astype(I32)
    sizes = jnp.array(RUN_SIZES, I32)[None, :, None]
    n_run = seg_len[:, None, :]
    has = (n_run & sizes) != 0
    piece_at = n_run & ~(2 * sizes - 1)
    rank = jnp.sum(jnp.where(before_e[None, None], has[:, :, :, None], False), axis=2)
    slot = jnp.arange(N_EXPERTS)
    put = has[..., None] & (rank[..., None] == slot)
    listed = lambda v: jnp.sum(jnp.where(put, v[..., None], 0), axis=2)
    piece_loc = listed(seg_off[:, None, :] + piece_at)
    piece_glob = listed(seg_dst[:, None, :] + piece_at)
    piece_cnt = jnp.sum(has, axis=2)
    flat = lambda a: a.reshape(-1).astype(I32)
    tables = (flat(piece_cnt), flat(piece_loc), flat(piece_glob), flat(jnp.sum(seg_len, axis=1)))
    xs = _dispatch(hn_all, post, *tables, flat(e_start), flat(e_len), n_used, n_rows)
    e_ids = jnp.arange(N_EXPERTS)
    used = e_tiles > 0
    next_expert = jnp.min(jnp.where(before_e & used[None, :], e_ids[None, :], N_EXPERTS), axis=1)
    next_expert = jnp.where(next_expert == N_EXPERTS, -1, next_expert)
    expert_slot = jnp.sum(jnp.where(before_e & used[:, None], 1, 0), axis=0) % 2
    yb = _experts(xs, tile_expert, n_used, flat(next_expert), flat(expert_slot), w1, b1, w2, b2)
    outs = []
    row = 0
    for h in h_parts:
        outs.append(_combine(h, pos, gates, *tables, yb, row // MOE_BLK))
        row += h.shape[0]
    return outs


GDN_ROWS = 4 * GDN_CHUNK
CONV_HALO = SUBLANES
NEUMANN_SPLIT = 2
AB_LANES = 2 * N_GDN_HEADS
SOLVE_ROWS = 2 * GDN_CHUNK


def _softplus(x):
    return jnp.maximum(x, 0.0) + jnp.log1p(jnp.exp(-jnp.abs(x)))


def _gdn_prompt_kernel(u_ref, ab_ref, gate_ref, cw_ref, alog_ref, dtb_ref, ng_ref, o_ref, s_ref, ubuf_ref):
    step = pl.program_id(0)
    NB = u_ref.shape[0]
    R = GDN_ROWS
    C = GDN_CHUNK
    NC = R // C

    @pl.when(step == 0)
    def _():
        ubuf_ref[:, 0:CONV_HALO, :] = jnp.zeros((NB, CONV_HALO, ubuf_ref.shape[2]), F32)
        s_ref[...] = jnp.zeros_like(s_ref)

    ri = lax.broadcasted_iota(I32, (R, R), 0)
    ci = lax.broadcasted_iota(I32, (R, R), 1)
    shift = C.bit_length() - 1
    same = lax.shift_right_logical(ri, shift) == lax.shift_right_logical(ci, shift)
    incl = same & (ci <= ri)
    strict = same & (ci < ri)
    tri = jnp.where(incl, 1.0, 0.0).astype(BF16)
    blk = jnp.where(same, 1.0, 0.0).astype(BF16)
    cw = cw_ref[...]
    ng = ng_ref[...]

    lane = lax.broadcasted_iota(I32, (R, LANES), 1)
    lane1 = lax.broadcasted_iota(I32, (1, LANES), 1)
    ab = ab_ref[0]
    alog = alog_ref[...]
    dtb = dtb_ref[...]
    for b in range(1, NB):
        own = (lane >= b * AB_LANES) & (lane < (b + 1) * AB_LANES)
        own1 = (lane1 >= b * AB_LANES) & (lane1 < (b + 1) * AB_LANES)
        ab = jnp.where(own, pltpu.roll(ab_ref[b], b * AB_LANES, 1), ab)
        alog = jnp.where(own1, pltpu.roll(alog_ref[...], b * AB_LANES, 1), alog)
        dtb = jnp.where(own1, pltpu.roll(dtb_ref[...], b * AB_LANES, 1), dtb)
    g_t = -jnp.exp(alog) * _softplus(ab + dtb)
    beta_t = jax.nn.sigmoid(ab)
    both = jnp.concatenate([tri, blk], axis=0)
    sums = sum(jnp.dot(both, p, preferred_element_type=F32) for p in _split3(g_t))
    gcum, gtot = sums[:R], sums[R:]
    gcum_t = gcum.T

    def conv_and_norms(b):
        u = u_ref[b]
        ubuf_ref[b, CONV_HALO:CONV_HALO + R, :] = u
        y = u * cw[GDN_CONV - 1:GDN_CONV, :]
        for j in range(1, GDN_CONV):
            y = y + ubuf_ref[b, CONV_HALO - j:CONV_HALO - j + R, :] * cw[GDN_CONV - 1 - j:GDN_CONV - j, :]
        ubuf_ref[b, 0:CONV_HALO, :] = u[R - CONV_HALO:, :]
        qkv = y * jax.nn.sigmoid(y)
        qk_n = []
        for t in range(2 * GDN_QK_COLS // LANES):
            x = qkv[:, t * LANES:(t + 1) * LANES]
            x = x * lax.rsqrt(_pair_sumsq(x) + EPS)
            qk_n.append(x * (GDN_DK ** -0.5) if t < GDN_QK_COLS // LANES else x)
        return qkv, jnp.concatenate(qk_n, axis=-1)

    rhs_tiles = {}

    def make_chain(b, h, qkv, qk_n):
        q = qk_n[:, h * GDN_DK:(h + 1) * GDN_DK]
        k = qk_n[:, GDN_QK_COLS + h * GDN_DK:GDN_QK_COLS + (h + 1) * GDN_DK]
        col = b * AB_LANES + h
        gc = gcum[:, col:col + 1]
        gt = gtot[:, col:col + 1]
        beta = beta_t[:, col + N_GDN_HEADS:col + N_GDN_HEADS + 1]
        pair = (b, h // 2)
        if pair not in rhs_tiles:
            c0 = b * AB_LANES + 2 * (h // 2)
            first = _first_half((R, LANES))
            per_lane = lambda m, off: jnp.where(first, m[:, c0 + off:c0 + off + 1], m[:, c0 + off + 1:c0 + off + 2])
            beta_l = per_lane(beta_t, N_GDN_HEADS)
            t0 = (h // 2) * LANES
            gc_l, gt_l = per_lane(gcum, 0), per_lane(gtot, 0)
            e_gc = jnp.exp(gc_l)
            k_tile = qk_n[:, GDN_QK_COLS + t0:GDN_QK_COLS + t0 + LANES]
            vb = qkv[:, 2 * GDN_QK_COLS + t0:2 * GDN_QK_COLS + t0 + LANES] * beta_l
            kb = k_tile * (beta_l * e_gc)
            rhs_tiles[pair] = (jnp.where(first, vb, pltpu.roll(kb, GDN_DK, 1)),
                               jnp.where(first, pltpu.roll(vb, GDN_DV, 1), kb),
                               qk_n[:, t0:t0 + LANES] * e_gc,
                               k_tile * jnp.exp(gt_l - gc_l))
        r = rhs_tiles[pair][h % 2]
        half = slice((h % 2) * GDN_DK, (h % 2 + 1) * GDN_DK)
        q_dec = rhs_tiles[pair][2][:, half]
        k_dec = rhs_tiles[pair][3][:, half]
        a_blocks, qk_blocks, r_blocks = [], [], []
        for sb in range(R // SOLVE_ROWS):
            rows = slice(sb * SOLVE_ROWS, (sb + 1) * SOLVE_ROWS)
            decay = jnp.exp(jnp.where(incl[:SOLVE_ROWS, :SOLVE_ROWS], gc[rows] - gcum_t[col:col + 1, rows], -jnp.inf))
            a_blocks.append(jnp.where(strict[:SOLVE_ROWS, :SOLVE_ROWS],
                                      beta[rows] * _bdot_nt(k[rows], k[rows]) * decay, 0.0))
            qk_blocks.append(_bdot_nt(q[rows], k[rows]) * decay)
            r_blocks.append(r[rows])
        return dict(b=b, h=h, a=a_blocks, qk=qk_blocks, r=r_blocks,
                    q_dec=q_dec, k_dec=k_dec, g_last=jnp.exp(gt))

    dot = lambda x, y: jnp.dot(x, y, preferred_element_type=F32)
    n_levels = C.bit_length() - 1

    def neumann_level(chains, j):
        blocks = [(ch, sb) for ch in chains for sb in range(len(ch['a']))]
        pieces = []
        for ch, sb in blocks:
            a = ch['a'][sb]
            a_hi, a_lo = _split2(a) if j < NEUMANN_SPLIT else (a.astype(BF16), None)
            pieces.append((a_hi, a_lo))
            if j + 1 < n_levels:
                sq = dot(a_hi, a_hi)
                if j + 1 < NEUMANN_SPLIT:
                    sq = sq + dot(a_hi, a_lo) + dot(a_lo, a_hi)
                ch['a'][sb] = sq
        for (ch, sb), (a_hi, a_lo) in zip(blocks, pieces):
            r = ch['r'][sb]
            if j < NEUMANN_SPLIT:
                r_hi, r_lo = _split2(r)
                upd = dot(a_hi, r_hi) + dot(a_hi, r_lo) + dot(a_lo, r_hi)
            else:
                upd = dot(a_hi, r.astype(BF16))
            ch['r'][sb] = r - upd if j == 0 else r + upd

    def chunk_begin(chains):
        for ch in chains:
            r = jnp.concatenate(ch['r'], axis=0)
            ch['u'], ch['w'] = r[:, :GDN_DV], r[:, GDN_DV:]
            ch['S'] = s_ref[ch['b'], ch['h']]
            ch['k_dec_t'] = ch['k_dec'].T
            ch['outs'] = []

    def chunk_step(chains, c):
        sl = slice(c * C, (c + 1) * C)
        per = SOLVE_ROWS // C
        loc = slice((c % per) * C, (c % per + 1) * C)
        from_state = [(_bdot(ch['w'][sl], ch['S']), _bdot(ch['q_dec'][sl], ch['S'])) for ch in chains]
        for ch, (w_s, q_s) in zip(chains, from_state):
            v_new = ch['u'][sl] - w_s
            ch['outs'].append(q_s + _bdot(ch['qk'][c // per][loc, loc], v_new))
            ch['S'] = ch['S'] * ch['g_last'][c * C:c * C + 1, :] + _bdot(ch['k_dec_t'][:, sl], v_new)

    def chunk_end(chains):
        for ch in chains:
            b, h = ch['b'], ch['h']
            s_ref[b, h] = ch['S']
            o = jnp.concatenate(ch['outs'], axis=0)
            o = o * lax.rsqrt(jnp.mean(o * o, axis=-1, keepdims=True) + EPS) * ng
            gh = gate_ref[b, :, h * GDN_DV:(h + 1) * GDN_DV]
            o_ref[b, :, h * GDN_DV:(h + 1) * GDN_DV] = (o * (gh * jax.nn.sigmoid(gh))).astype(o_ref.dtype)

    chains = []
    for b in range(NB):
        qkv, qk_n = conv_and_norms(b)
        chains += [make_chain(b, h, qkv, qk_n) for h in range(N_GDN_HEADS)]
    for j in range(n_levels):
        neumann_level(chains, j)
    chunk_begin(chains)
    for c in range(NC):
        chunk_step(chains, c)
    chunk_end(chains)


def _gdn_prompt(z3, conv_w, a_log, dt_bias, norm_g):
    B, S, _ = z3.shape
    R = GDN_ROWS
    lanes4 = lambda a: jnp.pad(a, (0, LANES - a.shape[0])).reshape(1, LANES)
    return pl.pallas_call(
        _gdn_prompt_kernel,
        grid=(S // R,),
        in_specs=[pl.BlockSpec((B, R, GDN_CONV_CH), lambda s: (0, s, Z_GDN // GDN_CONV_CH)),
                  pl.BlockSpec((B, R, LANES), lambda s: (0, s, Z_AB // LANES)),
                  pl.BlockSpec((B, R, GDN_V_COLS), lambda s: (0, s, Z_GATE // GDN_V_COLS)),
                  pl.BlockSpec((GDN_CONV, GDN_CONV_CH), lambda s: (0, 0)),
                  pl.BlockSpec((1, LANES), lambda s: (0, 0)),
                  pl.BlockSpec((1, LANES), lambda s: (0, 0)),
                  pl.BlockSpec((1, GDN_DV), lambda s: (0, 0))],
        out_specs=[pl.BlockSpec((B, R, GDN_V_COLS), lambda s: (0, s, 0)),
                   pl.BlockSpec((B, N_GDN_HEADS, GDN_DK, GDN_DV), lambda s: (0, 0, 0, 0))],
        out_shape=[jax.ShapeDtypeStruct((B, S, GDN_V_COLS), BF16),
                   jax.ShapeDtypeStruct((B, N_GDN_HEADS, GDN_DK, GDN_DV), F32)],
        scratch_shapes=[pltpu.VMEM((B, CONV_HALO + R, GDN_CONV_CH), F32)],
        compiler_params=_cparams("arbitrary"),
        name="gdn_prompt",
    )(z3, z3, z3, conv_w, lanes4(a_log), lanes4(dt_bias), norm_g.reshape(1, GDN_DV))


def _pair_sumsq(x):
    li = lax.broadcasted_iota(I32, (LANES, LANES), 0) // HEAD_DIM
    lj = lax.broadcasted_iota(I32, (LANES, LANES), 1) // HEAD_DIM
    same = jnp.where(li == lj, 1.0, 0.0).astype(BF16)
    hi, lo = _split2(x * x)
    return jnp.dot(hi, same, preferred_element_type=F32) + jnp.dot(lo, same, preferred_element_type=F32)


def _pair_rms(x, g):
    return x * lax.rsqrt(_pair_sumsq(x) * (1.0 / HEAD_DIM) + EPS) * g


def _first_half(shape):
    return lax.broadcasted_iota(I32, shape, 1) < HEAD_DIM


LOG2E = 1.4426950408889634


SWA_BLOCKS = 4


def _swa_prompt_kernel(sink_ref, q_ref, kc_ref, kp_ref, vc_ref, vp_ref, qg_ref, kg_ref, o_ref, kn_ref,
                       bias_ref):
    first = (pl.program_id(0) == 0) & (pl.program_id(1) == 0)
    n = pl.program_id(1)
    W = WINDOW
    NQ = SWA_BLOCKS

    @pl.when(first)
    def _():
        qi = lax.broadcasted_iota(I32, (W, 2 * W), 0)
        kj = lax.broadcasted_iota(I32, (W, 2 * W), 1)
        dist = qi + W - kj
        band = (dist >= 0) & (dist < W)
        distf = dist.astype(F32)
        for has_prev in range(2):
            mask = jnp.where(band & ((has_prev == 1) | (kj >= W)), 0.0, -jnp.inf)
            for head in range(N_SWA_HEADS):
                slope = 2.0 ** (-(8.0 / N_SWA_HEADS) * (head + 1))
                bias_ref[has_prev, head] = mask - (slope * LOG2E) * distf

    kg = kg_ref[...]
    qg = qg_ref[...]
    kc = _pair_rms(kc_ref[0], kg)
    kn_ref[0] = kc[(NQ - 1) * W:]
    k3 = jnp.concatenate([_pair_rms(kp_ref[0], kg), kc], axis=0)
    v3 = jnp.concatenate([vp_ref[0], vc_ref[0]], axis=0)
    fh = _first_half(k3.shape)
    k3r = pltpu.roll(k3, HEAD_DIM, 1)
    v3r = pltpu.roll(v3, HEAD_DIM, 1)
    kdup = (jnp.where(fh, k3, k3r).astype(BF16), jnp.where(fh, k3r, k3).astype(BF16))
    vdup = (jnp.where(fh, v3, v3r).astype(BF16), jnp.where(fh, v3r, v3).astype(BF16))
    fq = _first_half((W, LANES))
    kv_of = lambda head: head // SWA_GROUP
    probs = [(j, head) for j in range(NQ) for head in range(N_SWA_HEADS)]
    keys = lambda j: slice(j * W, (j + 2) * W)
    qts = [[_pair_rms(q_ref[0, j * W:(j + 1) * W, t * LANES:(t + 1) * LANES], qg) * (ATTN_SCALE * LOG2E)
            for t in range(SWA_Q_COLS // LANES)] for j in range(NQ)]
    qms = [jnp.where(fq == (head % 2 == 0), qts[j][head // 2], 0.0).astype(BF16) for j, head in probs]
    table = [jnp.where(n > 0, 1, 0)] + [1] * (NQ - 1)
    ss = [_bdot_nt(qms[i], kdup[kv_of(head)][keys(j)]) + bias_ref[table[j], head]
          for i, (j, head) in enumerate(probs)]
    sinks = [sink_ref[head] * LOG2E for head in range(N_SWA_HEADS)]
    ms = [jnp.maximum(jnp.max(ss[i], axis=-1, keepdims=True), sinks[head]) for i, (j, head) in enumerate(probs)]
    ps = [jnp.exp2(ss[i] - ms[i]) for i in range(len(probs))]
    dens = [jnp.sum(ps[i], axis=-1, keepdims=True) + jnp.exp2(sinks[head] - ms[i])
            for i, (j, head) in enumerate(probs)]
    outs = [_bdot(ps[i], vdup[kv_of(head)][keys(j)]) / dens[i] for i, (j, head) in enumerate(probs)]
    for j in range(NQ):
        for t in range(SWA_Q_COLS // LANES):
            o_ref[0, j * W:(j + 1) * W, t * LANES:(t + 1) * LANES] = jnp.where(
                fq, outs[j * N_SWA_HEADS + 2 * t], outs[j * N_SWA_HEADS + 2 * t + 1]).astype(o_ref.dtype)


def _swa_prompt(z3, q_g, k_g, sinks):
    B, S, _ = z3.shape
    W = WINDOW
    NQ = SWA_BLOCKS
    twice = lambda g: jnp.concatenate([g, g]).reshape(1, LANES)
    kcol, vcol = Z_K // LANES, Z_V // LANES
    prev = lambda col: pl.BlockSpec((1, W, LANES), lambda b, n: (b, jnp.maximum(NQ * n - 1, 0), col))
    grid_spec = pltpu.PrefetchScalarGridSpec(
        num_scalar_prefetch=0,
        grid=(B, S // (NQ * W)),
        in_specs=[pl.BlockSpec(memory_space=pltpu.SMEM),
                  pl.BlockSpec((1, NQ * W, SWA_Q_COLS), lambda b, n: (b, n, 0)),
                  pl.BlockSpec((1, NQ * W, LANES), lambda b, n: (b, n, kcol)),
                  prev(kcol),
                  pl.BlockSpec((1, NQ * W, LANES), lambda b, n: (b, n, vcol)),
                  prev(vcol),
                  pl.BlockSpec((1, LANES), lambda b, n: (0, 0)),
                  pl.BlockSpec((1, LANES), lambda b, n: (0, 0))],
        out_specs=[pl.BlockSpec((1, NQ * W, SWA_Q_COLS), lambda b, n: (b, n, 0)),
                   pl.BlockSpec((1, W, LANES), lambda b, n: (b, 0, 0))],
        scratch_shapes=[pltpu.VMEM((2, N_SWA_HEADS, W, 2 * W), F32)],
    )
    return pl.pallas_call(
        _swa_prompt_kernel,
        grid_spec=grid_spec,
        out_shape=[jax.ShapeDtypeStruct((B, S, SWA_Q_COLS), BF16),
                   jax.ShapeDtypeStruct((B, W, LANES), F32)],
        compiler_params=_cparams("arbitrary", "arbitrary"),
        name="swa_prompt",
    )(sinks, z3, z3, z3, z3, z3, twice(q_g), twice(k_g))


def _mem_kv_kernel(m_ref, g_ref, w_ref, kg_ref, k_ref, v_ref):
    n = _rms_rows(m_ref[...], g_ref[...])
    kv = jnp.dot(n.astype(BF16), w_ref[...], preferred_element_type=F32)
    kg = kg_ref[...]
    for t in range(MEM_Q_COLS // LANES):
        k_ref[:, t * LANES:(t + 1) * LANES] = _pair_rms(kv[:, t * LANES:(t + 1) * LANES], kg)
    v_ref[...] = kv[:, MEM_Q_COLS:]


def _mem_kv(mem2d, ln_g, w_kv, k_g):
    r, d = mem2d.shape
    twice = jnp.concatenate([k_g, k_g]).reshape(1, LANES)
    full = lambda shape: pl.BlockSpec(shape, lambda i: (0,) * len(shape))
    return pl.pallas_call(
        _mem_kv_kernel,
        grid=(1,),
        in_specs=[full((r, d)), full((1, d)), full((d, 2 * MEM_Q_COLS)), full((1, LANES))],
        out_specs=[full((r, MEM_Q_COLS)), full((r, MEM_Q_COLS))],
        out_shape=[jax.ShapeDtypeStruct((r, MEM_Q_COLS), F32), jax.ShapeDtypeStruct((r, MEM_Q_COLS), F32)],
        compiler_params=_cparams("arbitrary"),
        name="mem_kv",
    )(mem2d, ln_g.reshape(1, d), w_kv.astype(BF16), twice)


def _mem_attn_kernel(q_ref, k_ref, v_ref, qg_ref, o_ref):
    qg = qg_ref[...]
    rows = q_ref.shape[1]
    fq = _first_half((rows, LANES))
    heads = range(N_MEM_HEADS)
    tile = lambda t: slice(t * LANES, (t + 1) * LANES)
    qts = [_pair_rms(q_ref[0, :, tile(t)], qg) * (ATTN_SCALE * LOG2E) for t in range(MEM_Q_COLS // LANES)]
    kts = [k_ref[0, :, tile(t)].astype(BF16) for t in range(MEM_Q_COLS // LANES)]
    vts = [v_ref[0, :, tile(t)].astype(BF16) for t in range(MEM_Q_COLS // LANES)]
    ss = [_bdot_nt(jnp.where(fq == (h % 2 == 0), qts[h // 2], 0.0), kts[h // 2]) for h in heads]
    ps = [jnp.exp2(s - jnp.max(s, axis=-1, keepdims=True)) for s in ss]
    outs = [_bdot(ps[h], vts[h // 2]) / jnp.sum(ps[h], axis=-1, keepdims=True) for h in heads]
    for t in range(MEM_Q_COLS // LANES):
        o_ref[0, :, tile(t)] = jnp.where(fq, outs[2 * t], outs[2 * t + 1]).astype(o_ref.dtype)


MEM_Q_TILE = 512


def _mem_attn_prompt(z3, mem_k, mem_v, q_g):
    B, S, _ = z3.shape
    M = mem_k.shape[1]
    tq = MEM_Q_TILE
    twice = jnp.concatenate([q_g, q_g]).reshape(1, LANES)
    return pl.pallas_call(
        _mem_attn_kernel,
        grid=(B, S // tq),
        in_specs=[pl.BlockSpec((1, tq, MEM_Q_COLS), lambda b, i: (b, i, Z_QM // MEM_Q_COLS)),
                  pl.BlockSpec((1, M, MEM_Q_COLS), lambda b, i: (b, 0, 0)),
                  pl.BlockSpec((1, M, MEM_Q_COLS), lambda b, i: (b, 0, 0)),
                  pl.BlockSpec((1, LANES), lambda b, i: (0, 0))],
        out_specs=pl.BlockSpec((1, tq, MEM_Q_COLS), lambda b, i: (b, i, 0)),
        out_shape=jax.ShapeDtypeStruct((B, S, MEM_Q_COLS), BF16),
        compiler_params=_cparams("parallel", "parallel"),
        name="mem_attn_prompt",
    )(z3, mem_k, mem_v, twice)


PAIR = 2


def _swa_sample_kernel(sink_ref, q_ref, k_ref, v_ref, ck_ref, cv_ref, qg_ref, kg_ref, o_ref, nk_ref, nv_ref, *, L):
    n_seq = ck_ref.shape[0]
    Wb = ck_ref.shape[2]
    rows8 = SUBLANES
    nh = N_SWA_HEADS
    kn = _pair_rms(k_ref[...], kg_ref[...])
    qg = qg_ref[...]
    R = nh * rows8
    row = lax.broadcasted_iota(I32, (R, 1), 0)
    head = row // rows8
    seq_in_pair = (row % rows8) // L
    step = (row % L).astype(F32)
    slope = jnp.exp2(-(8.0 / N_SWA_HEADS) * (head.astype(F32) + 1.0))
    sink = jnp.zeros((R, 1), F32)
    for h in range(nh):
        sink = jnp.where(head == h, sink_ref[h], sink)
    key = lax.broadcasted_iota(I32, (R, Wb), 1).astype(F32)
    dist_c = float(Wb) + step - key
    bias_c = jnp.where(dist_c < float(WINDOW), 0.0, -jnp.inf)
    col = lax.broadcasted_iota(I32, (R, rows8), 1)
    dist_n = step - (col % L).astype(F32)
    bias_n = jnp.where((dist_n >= 0.0) & ((col // L) == seq_in_pair), 0.0, -jnp.inf)
    fh8 = _first_half((rows8, LANES))
    fhR = _first_half((R, LANES))
    kv_first = head < SWA_GROUP
    pairs = range(n_seq // PAIR)
    bias_c = bias_c - slope * dist_c
    bias_n = bias_n - slope * dist_n

    def stacked_queries(pr):
        r0 = pr * rows8
        pieces = []
        for t in range(SWA_Q_COLS // LANES):
            qt = _pair_rms(q_ref[r0:r0 + rows8, t * LANES:(t + 1) * LANES], qg) * ATTN_SCALE
            qr = pltpu.roll(qt, HEAD_DIM, 1)
            kv = t // (SWA_GROUP // 2)
            for half in range(2):
                src = qt if half == kv else qr
                pieces.append(jnp.where(fh8 == (kv == 0), src, 0.0))
        return jnp.concatenate(pieces, axis=0).astype(BF16)

    qs = [stacked_queries(pr) for pr in pairs]
    s_c = [jnp.where(seq_in_pair == 0, _bdot(qs[pr], ck_ref[pr * PAIR]), _bdot(qs[pr], ck_ref[pr * PAIR + 1]))
           + bias_c for pr in pairs]
    s_n = [_bdot_nt(qs[pr], kn[pr * rows8:(pr + 1) * rows8]) + bias_n for pr in pairs]
    m = [jnp.maximum(jnp.maximum(jnp.max(s_c[pr], axis=-1, keepdims=True),
                                 jnp.max(s_n[pr], axis=-1, keepdims=True)), sink) for pr in pairs]
    p_c = [jnp.exp(s_c[pr] - m[pr]) for pr in pairs]
    p_n = [jnp.exp(s_n[pr] - m[pr]) for pr in pairs]
    den = [jnp.sum(p_c[pr], axis=-1, keepdims=True) + jnp.sum(p_n[pr], axis=-1, keepdims=True)
           + jnp.exp(sink - m[pr]) for pr in pairs]
    outs = [(_bdot(p_n[pr], v_ref[pr * rows8:(pr + 1) * rows8, :])
             + _bdot_nt(jnp.where(seq_in_pair == 0, p_c[pr], 0.0), cv_ref[pr * PAIR])
             + _bdot_nt(jnp.where(seq_in_pair == 1, p_c[pr], 0.0), cv_ref[pr * PAIR + 1])) / den[pr] for pr in pairs]

    pos_r = lax.broadcasted_iota(I32, (Wb, rows8), 0)
    new_c = lax.broadcasted_iota(I32, (Wb, rows8), 1)
    tail = lax.broadcasted_iota(I32, (SWA_KV_COLS, Wb), 1) >= Wb - L

    def shifted(old, new8, j):
        place = jnp.where((pos_r == Wb - L + new_c % L) & (new_c // L == j), 1.0, 0.0).astype(BF16)
        rows_at_tail = sum(jnp.dot(place, part, preferred_element_type=F32) for part in _split3(new8))
        return jnp.where(tail, rows_at_tail.T, pltpu.roll(old, Wb - L, 1))

    for pr in pairs:
        for j in range(PAIR):
            s = pr * PAIR + j
            nk_ref[s] = shifted(ck_ref[s], kn[pr * rows8:(pr + 1) * rows8], j)
            nv_ref[s] = shifted(cv_ref[s], v_ref[pr * rows8:(pr + 1) * rows8, :], j)
    for pr in pairs:
        r0 = pr * rows8
        o = jnp.where(fhR == kv_first, outs[pr], 0.0)
        o_r = pltpu.roll(o, HEAD_DIM, 1)
        for t in range(SWA_Q_COLS // LANES):
            kv = t // (SWA_GROUP // 2)
            halves = []
            for half in range(2):
                h = 2 * t + half
                src = o if half == kv else o_r
                halves.append(src[h * rows8:(h + 1) * rows8])
            o_ref[r0:r0 + rows8, t * LANES:(t + 1) * LANES] = jnp.where(fh8, halves[0], halves[1])


SAMPLE_SEQS = 8


def _swa_sample(z_s, cache_k, cache_v, q_g, k_g, sinks, L):
    t = z_s.shape[0]
    DB, Wb, KV, HD = cache_k.shape
    ns = SAMPLE_SEQS
    rows = ns * L
    twice = lambda g: jnp.concatenate([g, g]).reshape(1, LANES)
    fm = lambda c: jnp.transpose(c, (0, 2, 3, 1)).reshape(DB, KV * HD, Wb)
    back = lambda c: jnp.transpose(c.reshape(DB, KV, HD, Wb), (0, 3, 1, 2))
    cache = pl.BlockSpec((ns, KV * HD, Wb), lambda i: (i, 0, 0))
    o, nk, nv = pl.pallas_call(
        functools.partial(_swa_sample_kernel, L=L),
        grid=(DB // ns,),
        in_specs=[pl.BlockSpec(memory_space=pltpu.SMEM),
                  pl.BlockSpec((rows, SWA_Q_COLS), lambda i: (i, 0)),
                  pl.BlockSpec((rows, LANES), lambda i: (i, Z_K // LANES)),
                  pl.BlockSpec((rows, LANES), lambda i: (i, Z_V // LANES)),
                  cache, cache,
                  pl.BlockSpec((1, LANES), lambda i: (0, 0)),
                  pl.BlockSpec((1, LANES), lambda i: (0, 0))],
        out_specs=[pl.BlockSpec((rows, SWA_Q_COLS), lambda i: (i, 0)), cache, cache],
        out_shape=[jax.ShapeDtypeStruct((t, SWA_Q_COLS), F32),
                   jax.ShapeDtypeStruct((DB, KV * HD, Wb), F32),
                   jax.ShapeDtypeStruct((DB, KV * HD, Wb), F32)],
        compiler_params=_cparams("parallel"),
        name="swa_sample",
    )(sinks, z_s, z_s, z_s, fm(cache_k), fm(cache_v), twice(q_g), twice(k_g))
    return o, back(nk), back(nv)


def _mem_sample_kernel(q_ref, mk_ref, mv_ref, qg_ref, o_ref, *, L):
    n_seq = mk_ref.shape[0]
    rows8 = SUBLANES
    nh = N_MEM_HEADS
    qg = qg_ref[...]
    R = nh * rows8
    row = lax.broadcasted_iota(I32, (R, 1), 0)
    seq_in_pair = (row % rows8) // L
    lane_head8 = lax.broadcasted_iota(I32, (rows8, MEM_Q_COLS), 1) // HEAD_DIM
    pairs = range(n_seq // PAIR)

    def stacked_queries(pr):
        qn = jnp.concatenate([_pair_rms(q_ref[pr * rows8:(pr + 1) * rows8, t * LANES:(t + 1) * LANES], qg)
                              for t in range(MEM_Q_COLS // LANES)], axis=-1) * ATTN_SCALE
        return jnp.concatenate([jnp.where(lane_head8 == h, qn, 0.0) for h in range(nh)], axis=0).astype(BF16)

    qs = [stacked_queries(pr) for pr in pairs]
    ss = [jnp.where(seq_in_pair == 0, _bdot(qs[pr], mk_ref[pr * PAIR]), _bdot(qs[pr], mk_ref[pr * PAIR + 1]))
          for pr in pairs]
    ps = [jnp.exp(s - jnp.max(s, axis=-1, keepdims=True)) for s in ss]
    outs = [(_bdot_nt(jnp.where(seq_in_pair == 0, ps[pr], 0.0), mv_ref[pr * PAIR])
             + _bdot_nt(jnp.where(seq_in_pair == 1, ps[pr], 0.0), mv_ref[pr * PAIR + 1]))
            / jnp.sum(ps[pr], axis=-1, keepdims=True) for pr in pairs]
    for pr in pairs:
        o = jnp.zeros((rows8, MEM_Q_COLS), F32)
        for h in range(nh):
            o = jnp.where(lane_head8 == h, outs[pr][h * rows8:(h + 1) * rows8], o)
        o_ref[pr * rows8:(pr + 1) * rows8, :] = o


def _mem_attn_sample(z_s, mem_k, mem_v, q_g, L):
    t = z_s.shape[0]
    DB, M, H, HD = mem_k.shape
    ns = SAMPLE_SEQS
    rows = ns * L
    twice = jnp.concatenate([q_g, q_g]).reshape(1, LANES)
    fm = lambda c: jnp.transpose(c, (0, 2, 3, 1)).reshape(DB, H * HD, M)
    cache = pl.BlockSpec((ns, H * HD, M), lambda i: (i, 0, 0))
    return pl.pallas_call(
        functools.partial(_mem_sample_kernel, L=L),
        grid=(DB // ns,),
        in_specs=[pl.BlockSpec((rows, MEM_Q_COLS), lambda i: (i, Z_QM // MEM_Q_COLS)),
                  cache, cache,
                  pl.BlockSpec((1, LANES), lambda i: (0, 0))],
        out_specs=pl.BlockSpec((rows, MEM_Q_COLS), lambda i: (i, 0)),
        out_shape=jax.ShapeDtypeStruct((t, MEM_Q_COLS), F32),
        compiler_params=_cparams("parallel"),
        name="mem_attn_sample",
    )(z_s, fm(mem_k), fm(mem_v), twice)


def _gdn_sample_kernel(uq_ref, uk_ref, uv_ref, bq_ref, bk_ref, bv_ref, wq_ref, wk_ref, wv_ref,
                       ab_ref, gate_ref, alog_ref, dtb_ref, ng_ref, s_in_ref, o_ref, s_ref, kq_ref):
    h = pl.program_id(0)
    L = uq_ref.shape[0]
    nbuf = bq_ref.shape[0]
    DK = GDN_DK

    def conv(u_ref, b_ref, w_ref, t):
        up = [b_ref[i] for i in range(nbuf)] + [u_ref[i] for i in range(L)]
        y = up[t] * w_ref[0]
        for i in range(1, GDN_CONV):
            y = y + up[t + i] * w_ref[i]
        return y * jax.nn.sigmoid(y)

    s_ref[...] = s_in_ref[...]
    ng = ng_ref[...]
    hsel = lax.broadcasted_iota(I32, (SUBLANES, 1), 0)
    pick = lambda m, r: jnp.sum(jnp.where(hsel == r, m, 0.0), axis=0, keepdims=True)
    alog = pick(alog_ref[...], h)
    dtb = pick(dtb_ref[...], h)
    for t in range(L):
        q = conv(uq_ref, bq_ref, wq_ref, t)
        k = conv(uk_ref, bk_ref, wk_ref, t)
        v = conv(uv_ref, bv_ref, wv_ref, t)
        q = q * lax.rsqrt(jnp.sum(q * q, axis=0, keepdims=True) + EPS) * (GDN_DK ** -0.5)
        k = k * lax.rsqrt(jnp.sum(k * k, axis=0, keepdims=True) + EPS)
        ab = ab_ref[t]
        a = pick(ab, h)
        bb = pick(ab, h + N_GDN_HEADS)
        decay = jnp.exp(-jnp.exp(alog) * _softplus(a + dtb))
        beta = jax.nn.sigmoid(bb)
        kq_ref[0] = k
        kq_ref[1] = q

        def decay_and_project(dk, acc):
            s = s_ref[0, dk] * decay
            s_ref[0, dk] = s
            return acc + s * kq_ref[0, pl.ds(dk, 1), :]

        sk = lax.fori_loop(0, DK, decay_and_project, jnp.zeros_like(v), unroll=8)
        u = beta * (v - sk)

        def update_and_read(dk, acc):
            s = s_ref[0, dk] + kq_ref[0, pl.ds(dk, 1), :] * u
            s_ref[0, dk] = s
            return acc + s * kq_ref[1, pl.ds(dk, 1), :]

        o = lax.fori_loop(0, DK, update_and_read, jnp.zeros_like(v), unroll=8)
        o = o * lax.rsqrt(jnp.mean(o * o, axis=0, keepdims=True) + EPS) * ng
        g = gate_ref[t]
        o_ref[t] = o * (g * jax.nn.sigmoid(g))


def _gdn_sample(z_s, conv_buf, state, conv_w, a_log, dt_bias, norm_g, DB, L):
    H = N_GDN_HEADS
    z3 = z_s.reshape(DB, L, Z_COLS)
    u_t = jnp.transpose(z3[:, :, Z_GDN:Z_GATE], (1, 2, 0))
    gate_t = jnp.transpose(z3[:, :, Z_GATE:Z_QM], (1, 2, 0))
    ab_t = jnp.transpose(z3[:, :, Z_AB:Z_AB + SUBLANES], (1, 2, 0))
    buf_t = jnp.transpose(conv_buf, (1, 2, 0))
    s_t = jnp.transpose(state, (1, 2, 3, 0))
    w_col = conv_w.reshape(GDN_CONV, GDN_CONV_CH, 1)
    col8 = lambda a: jnp.pad(a, (0, SUBLANES - a.shape[0])).reshape(SUBLANES, 1)
    nbuf = conv_buf.shape[1]
    part = lambda n, j: pl.BlockSpec((n, GDN_DK, DB), lambda h: (0, j * H + h, 0))
    wpart = lambda j: pl.BlockSpec((GDN_CONV, GDN_DK, 1), lambda h: (0, j * H + h, 0))
    whole = lambda shape: pl.BlockSpec(shape, lambda h: (0,) * len(shape))
    o_t, s_new = pl.pallas_call(
        _gdn_sample_kernel,
        grid=(H,),
        in_specs=[part(L, 0), part(L, 1), part(L, 2), part(nbuf, 0), part(nbuf, 1), part(nbuf, 2),
                  wpart(0), wpart(1), wpart(2),
                  whole((L, SUBLANES, DB)),
                  pl.BlockSpec((L, GDN_DV, DB), lambda h: (0, h, 0)),
                  whole((SUBLANES, 1)), whole((SUBLANES, 1)), whole((GDN_DV, 1)),
                  pl.BlockSpec((1, GDN_DK, GDN_DV, DB), lambda h: (h, 0, 0, 0))],
        out_specs=[pl.BlockSpec((L, GDN_DV, DB), lambda h: (0, h, 0)),
                   pl.BlockSpec((1, GDN_DK, GDN_DV, DB), lambda h: (h, 0, 0, 0))],
        out_shape=[jax.ShapeDtypeStruct((L, H * GDN_DV, DB), F32),
                   jax.ShapeDtypeStruct((H, GDN_DK, GDN_DV, DB), F32)],
        scratch_shapes=[pltpu.VMEM((2, GDN_DK, DB), F32)],
        compiler_params=_cparams("parallel"),
        name="gdn_sample",
    )(u_t, u_t, u_t, buf_t, buf_t, buf_t, w_col, w_col, w_col, ab_t, gate_t,
      col8(a_log), col8(dt_bias), norm_g.reshape(GDN_DV, 1), s_t)
    o = jnp.transpose(o_t, (2, 0, 1)).reshape(DB * L, H * GDN_DV)
    return o, jnp.transpose(s_new, (3, 0, 1, 2))


def kernel(x_prompt, x_sample, cache_swa_k, cache_swa_v, state_gdn, state_gdn_conv, cache_mem_k, cache_mem_v,
           mem_prompt, ln1_g, w_in, swa_q_norm, swa_k_norm, swa_sinks, gdn_conv_w, gdn_a_log, gdn_dt_bias,
           gdn_norm_g, mem_ln_g, w_mem_kv, mem_q_norm, mem_k_norm, w_o, ln2_g, router_w, router_b,
           moe_w1, moe_b1, moe_w2, moe_b2):
    B, S, D = x_prompt.shape
    DB, DL, _ = x_sample.shape
    depth = ln1_g.shape[0]
    assert depth == 1
    l = 0
    tp, ts = B * S, DB * DL
    t_all = tp + ts
    n_ab = 2 * N_GDN_HEADS
    c_ab = SWA_Q_COLS + 2 * SWA_KV_COLS + GDN_CONV_CH
    w = w_in[l]
    w_z = jnp.concatenate([w[:, :c_ab], w[:, c_ab + n_ab:], w[:, c_ab:c_ab + n_ab],
                           jnp.zeros((D, LANES - n_ab), F32)], axis=1).astype(BF16)
    rw = jnp.pad(router_w[l], ((0, 0), (0, LANES - N_EXPERTS)))
    rw_hi = rw.astype(BF16)
    rw_lo = (rw - rw_hi.astype(F32)).astype(BF16)
    rb = jnp.pad(router_b[l], (0, LANES - N_EXPERTS)).reshape(1, LANES)
    wo = w_o[l].astype(BF16)
    w1 = moe_w1[l]
    w2 = moe_w2[l]
    b1 = moe_b1[l].reshape(N_EXPERTS, 1, -1)
    b2 = moe_b2[l].reshape(N_EXPERTS, 1, -1)
    p = {'q_norm': swa_q_norm[l], 'k_norm': swa_k_norm[l], 'sinks': swa_sinks[l], 'conv_w': gdn_conv_w[l],
         'a_log': gdn_a_log[l], 'dt_bias': gdn_dt_bias[l], 'gdn_norm': gdn_norm_g[l], 'mem_q_norm': mem_q_norm[l]}

    xp = x_prompt.reshape(tp, D)
    xs = x_sample.reshape(ts, D)
    z_p = _inproj(xp, ln1_g[l], w_z)
    z_s = _inproj(xs, ln1_g[l], w_z)

    M = mem_prompt.shape[1]
    z_p3 = z_p.reshape(B, S, Z_COLS)
    mk2, mv2 = _mem_kv(mem_prompt.reshape(B * M, D), mem_ln_g[l], w_mem_kv[l], mem_k_norm[l])
    mk = mk2.reshape(B, M, N_MEM_HEADS, HEAD_DIM)
    mv = mv2.reshape(B, M, N_MEM_HEADS, HEAD_DIM)
    os_p, pk = _swa_prompt(z_p3, p['q_norm'], p['k_norm'], p['sinks'])
    od_p, ps = _gdn_prompt(z_p3, p['conv_w'], p['a_log'], p['dt_bias'], p['gdn_norm'])
    om_p = _mem_attn_prompt(z_p3, mk2.reshape(B, M, MEM_Q_COLS), mv2.reshape(B, M, MEM_Q_COLS), p['mem_q_norm'])
    os_p, od_p, om_p = os_p.reshape(tp, -1), od_p.reshape(tp, -1), om_p.reshape(tp, -1)
    pk = pk.reshape(B, WINDOW, N_SWA_KV, HEAD_DIM)
    pv = z_p3[:, S - WINDOW:, Z_V:Z_GDN].reshape(B, WINDOW, N_SWA_KV, HEAD_DIM)
    pc = z_p3[:, S - (GDN_CONV - 1):, Z_GDN:Z_GATE]
    os_s, sk, sv = _swa_sample(z_s, cache_swa_k[l], cache_swa_v[l], p['q_norm'], p['k_norm'], p['sinks'], DL)
    od_s, ss = _gdn_sample(z_s, state_gdn_conv[l], state_gdn[l], p['conv_w'], p['a_log'], p['dt_bias'],
                           p['gdn_norm'], DB, DL)
    om_s = _mem_attn_sample(z_s, cache_mem_k[l], cache_mem_v[l], p['mem_q_norm'], DL)
    z_s3 = z_s.reshape(DB, DL, Z_COLS)
    sc = jnp.concatenate([state_gdn_conv[l], z_s3[:, :, Z_GDN:Z_GATE]], axis=1)[:, DL:]

    h_p, hn_all, lg_all = _outproj(xp, os_p, od_p, om_p, wo, ln2_g[l], rw_hi, rw_lo, rb, t_all, 0)
    h_s, hn_all, lg_all = _outproj(xs, os_s, od_s, om_s, wo, ln2_g[l], rw_hi, rw_lo, rb, t_all, tp,
                                   prev=(hn_all, lg_all))
    y_p, y_s = _moe(hn_all, lg_all, [h_p, h_s], w1, b1, w2, b2)
    return (y_p.reshape(B, S, D), y_s.reshape(DB, DL, D), pk[None], pv[None], ps[None], pc[None], mk[None],
            mv[None], sk[None], sv[None], ss[None], sc[None])
```
